```python
import math
import jax, jax.numpy as jnp
from jax import lax
import numpy as np

D_MODEL = 1024
BATCH = 2
SEQ = 8192
DEPTH = 2
DEC_BATCH = 128
DEC_SEQ = 1
PAST_LEN = 8192
PAGE_SIZE = 128

CHUNK = 128
GM_GROUPS = 4
GM_WIDTH = 512
GM_GROUP_DIM = GM_WIDTH // GM_GROUPS
CONV_WIDTH = 512
CONV_K = 31
N_HEADS = 8
N_KV_HEADS = 2
HEAD_DIM = 64
Q_REP = N_HEADS // N_KV_HEADS
ATT_WIDTH = N_HEADS * HEAD_DIM
KV_WIDTH = N_KV_HEADS * HEAD_DIM
WINDOW = 128
ROPE_THETA = 500000.0
ROT_DIM = HEAD_DIM // 4
N_BRANCH = 3
OFF_GM_U = 0
OFF_GM_V = OFF_GM_U + GM_WIDTH
OFF_CV_A = OFF_GM_V + GM_WIDTH
OFF_CV_G = OFF_CV_A + CONV_WIDTH
OFF_Q = OFF_CV_G + CONV_WIDTH
OFF_K = OFF_Q + ATT_WIDTH
OFF_V = OFF_K + KV_WIDTH
OFF_GATE = OFF_V + KV_WIDTH
IN_WIDTH = OFF_GATE + N_BRANCH * D_MODEL
N_EXPERTS = 16
N_EXPERT_GROUPS = 4
EXPERTS_PER_GROUP = N_EXPERTS // N_EXPERT_GROUPS
TOP_K = 2
EXPERT_FF = 256
EPS = 1e-6

kernel_name = 'hybrid_gmlp_conformer_swa_moe_step'


def rms_norm(x, g):
    xf = x.astype(jnp.float32)
    y = xf * lax.rsqrt(jnp.mean(xf * xf, axis=-1, keepdims=True) + EPS)
    return (y * g.astype(jnp.float32)).astype(x.dtype)


def layer_norm(x, g, b):
    xf = x.astype(jnp.float32)
    mu = jnp.mean(xf, axis=-1, keepdims=True)
    xc = xf - mu
    var = jnp.mean(xc * xc, axis=-1, keepdims=True)
    y = xc * lax.rsqrt(var + EPS) * g.astype(jnp.float32) + b.astype(jnp.float32)
    return y.astype(x.dtype)


def partial_rope(x, pos):
    half = ROT_DIM // 2
    freqs = jnp.exp(-math.log(ROPE_THETA) * jnp.arange(half, dtype=jnp.float32) * (2.0 / ROT_DIM))
    ang = pos.astype(jnp.float32)[:, None] * freqs[None, :]
    cos = jnp.cos(ang)[:, None, :]
    sin = jnp.sin(ang)[:, None, :]
    xf = x.astype(jnp.float32)
    x1 = xf[..., :half]
    x2 = xf[..., half:ROT_DIM]
    out = jnp.concatenate([x1 * cos - x2 * sin, x2 * cos + x1 * sin, xf[..., ROT_DIM:]], axis=-1)
    return out.astype(x.dtype)


def ada_mod(c, w_ada, b_ada):
    mod = jax.nn.silu(c) @ w_ada + b_ada
    return jnp.split(mod[:, None, :], 6, axis=-1)


def modulate(x, g, shift, scale):
    return rms_norm(x, g) * (1.0 + scale) + shift


def gmlp_uv(z, g_norm):
    u = jax.nn.gelu(z[..., OFF_GM_U:OFF_GM_V])
    v = rms_norm(jax.nn.gelu(z[..., OFF_GM_V:OFF_CV_A]), g_norm)
    return u, v


def gmlp_spatial(u, v, ws, bs, n_pos):
    w = jnp.tril(ws[:, :n_pos, :n_pos])
    vg = v.reshape(v.shape[:-1] + (GM_GROUPS, GM_GROUP_DIM))
    mixed = jnp.einsum('gij,...jgd->...igd', w, vg) + bs[:, :n_pos].T[:, :, None]
    return u * mixed.reshape(u.shape)


def conformer_glu(z):
    return z[..., OFF_CV_A:OFF_CV_G] * jax.nn.sigmoid(z[..., OFF_CV_G:OFF_Q])


def conformer_conv(xp, w_dw, b_dw, ln_g, ln_b):
    y = lax.conv_general_dilated(xp, w_dw[:, None, :], window_strides=(1,), padding='VALID',
                                 dimension_numbers=('NWC', 'WIO', 'NWC'),
                                 feature_group_count=CONV_WIDTH)
    return jax.nn.silu(layer_norm(y + b_dw, ln_g, ln_b))


def attn_qkv(z, pos, qn_g, kn_g):
    lead = z.shape[:-1]
    q = z[..., OFF_Q:OFF_K].reshape(lead + (N_HEADS, HEAD_DIM))
    k = z[..., OFF_K:OFF_V].reshape(lead + (N_KV_HEADS, HEAD_DIM))
    v = z[..., OFF_V:OFF_GATE].reshape(lead + (N_KV_HEADS, HEAD_DIM))
    q = partial_rope(rms_norm(q, qn_g), pos)
    k = partial_rope(rms_norm(k, kn_g), pos)
    return q, k, v


def sink_attention(q, k, v, mask, sinks):
    s = jnp.einsum('...qgrd,...kgd->...grqk', q.astype(jnp.float32), k.astype(jnp.float32)) * (HEAD_DIM ** -0.5)
    s = jnp.where(mask, s, -jnp.inf)
    sk = sinks.astype(jnp.float32).reshape(N_KV_HEADS, Q_REP)[:, :, None, None]
    m = jnp.maximum(jnp.max(s, axis=-1, keepdims=True), sk)
    p = jnp.exp(s - m)
    den = jnp.sum(p, axis=-1, keepdims=True) + jnp.exp(sk - m)
    o = jnp.einsum('...grqk,...kgd->...qgrd', p / den, v.astype(jnp.float32))
    return o.astype(q.dtype)


def merge_branches(z, o_gm, o_cv, o_att, w_branch, w_out):
    gates = jax.nn.sigmoid(z[..., OFF_GATE:])
    g = gates.reshape(gates.shape[:-1] + (N_BRANCH, D_MODEL))
    merged = (g[..., 0, :] * (o_gm @ w_branch[0])
              + g[..., 1, :] * (o_cv @ w_branch[1])
              + g[..., 2, :] * (o_att @ w_branch[2]))
    return merged @ w_out


def prompt_mixers(h, lp):
    B, T, _ = h.shape
    z = h @ lp['w_in']
    pos = jnp.arange(T, dtype=jnp.int32)
    u, v = gmlp_uv(z, lp['gm_g'])
    nc = T // CHUNK
    o_gm = gmlp_spatial(u.reshape(B, nc, CHUNK, GM_WIDTH), v.reshape(B, nc, CHUNK, GM_WIDTH),
                        lp['gm_ws'], lp['gm_b'], CHUNK).reshape(B, T, GM_WIDTH)
    a = conformer_glu(z)
    ap = jnp.pad(a, ((0, 0), (CONV_K - 1, 0), (0, 0)))
    o_cv = conformer_conv(ap, lp['conv_dw'], lp['conv_b'], lp['conv_ln_g'], lp['conv_ln_b'])
    new_conv = ap[:, -(CONV_K - 1):]
    q, k, vv = attn_qkv(z, pos, lp['qn_g'], lp['kn_g'])
    nb = T // WINDOW
    qb = q.reshape(B, nb, WINDOW, N_KV_HEADS, Q_REP, HEAD_DIM)
    kb = k.reshape(B, nb, WINDOW, N_KV_HEADS, HEAD_DIM)
    vb = vv.reshape(B, nb, WINDOW, N_KV_HEADS, HEAD_DIM)
    kk = jnp.concatenate([jnp.concatenate([jnp.zeros_like(kb[:, :1]), kb[:, :-1]], axis=1), kb], axis=2)
    vk = jnp.concatenate([jnp.concatenate([jnp.zeros_like(vb[:, :1]), vb[:, :-1]], axis=1), vb], axis=2)
    i = jnp.arange(WINDOW)[:, None]
    j = jnp.arange(2 * WINDOW)[None, :]
    delta = WINDOW + i - j
    band = (delta >= 0) & (delta <= WINDOW)
    valid = (jnp.arange(nb) > 0)[:, None, None] | (j >= WINDOW)[None]
    mask = (band[None] & valid)[:, None, None]
    o_att = sink_attention(qb, kk, vk, mask, lp['sinks']).reshape(B, T, ATT_WIDTH)
    y = merge_branches(z, o_gm, o_cv, o_att, lp['w_branch'], lp['w_out'])
    return y, k[:, -WINDOW:], vv[:, -WINDOW:], new_conv


def sample_mixers(h, k_buf, v_buf, conv_buf, lp):
    DB, S, _ = h.shape
    z = h @ lp['w_in']
    pos = PAST_LEN + jnp.arange(S, dtype=jnp.int32)
    u, v = gmlp_uv(z, lp['gm_g'])
    o_gm = gmlp_spatial(u, v, lp['gm_ws'], lp['gm_b'], S)
    a = conformer_glu(z)
    ap = jnp.concatenate([conv_buf, a], axis=1)
    o_cv = conformer_conv(ap, lp['conv_dw'], lp['conv_b'], lp['conv_ln_g'], lp['conv_ln_b'])
    new_conv = ap[:, -(CONV_K - 1):]
    q, k, vv = attn_qkv(z, pos, lp['qn_g'], lp['kn_g'])
    kk = jnp.concatenate([k_buf, k], axis=1)
    vk = jnp.concatenate([v_buf, vv], axis=1)
    nbuf = k_buf.shape[1]
    kpos = PAST_LEN - nbuf + jnp.arange(nbuf + S, dtype=jnp.int32)
    delta = pos[:, None] - kpos[None, :]
    mask = (delta >= 0) & (delta <= WINDOW)
    o_att = sink_attention(q.reshape(DB, S, N_KV_HEADS, Q_REP, HEAD_DIM), kk, vk, mask,
                           lp['sinks']).reshape(DB, S, ATT_WIDTH)
    y = merge_branches(z, o_gm, o_cv, o_att, lp['w_branch'], lp['w_out'])
    return y, kk[:, -nbuf:], vk[:, -nbuf:], new_conv, v


def moe(h, router_w, router_b, w1, w3, w2):
    scores = jax.nn.sigmoid(h.astype(jnp.float32) @ router_w.astype(jnp.float32))
    biased = scores + router_b.astype(jnp.float32)
    grp = biased.reshape(scores.shape[:-1] + (N_EXPERT_GROUPS, EXPERTS_PER_GROUP))
    grp_score = jnp.sum(lax.top_k(grp, TOP_K)[0], axis=-1)
    sel = jnp.argmax(grp_score, axis=-1)
    in_group = (jnp.arange(N_EXPERTS) // EXPERTS_PER_GROUP) == sel[..., None]
    _, idx = lax.top_k(jnp.where(in_group, biased, -jnp.inf), TOP_K)
    wsel = jnp.take_along_axis(scores, idx, axis=-1)
    wsel = wsel / jnp.sum(wsel, axis=-1, keepdims=True)
    combine = jnp.sum(jax.nn.one_hot(idx, N_EXPERTS, dtype=jnp.float32) * wsel[..., None], axis=-2)
    a = jnp.einsum('...d,edf->...ef', h, w1)
    b = jnp.einsum('...d,edf->...ef', h, w3)
    act = jax.nn.silu(a) * b * combine[..., None].astype(h.dtype)
    return jnp.einsum('...ef,efd->...d', act, w2)


def setup_inputs(seed: int = 0) -> dict:
    key = jax.random.key(seed)
    ks = jax.random.split(key, 32)
    f32 = jnp.float32
    D = D_MODEL
    win_buf = min(WINDOW, PAST_LEN)

    def nrm(k, shape, s):
        return jax.random.normal(k, shape, f32) * s

    return {
        'x_prompt': nrm(ks[0], (BATCH, SEQ, D), 1.0),
        'x_sample': nrm(ks[1], (DEC_BATCH, DEC_SEQ, D), 1.0),
        'cache_win_k': nrm(ks[2], (DEPTH, DEC_BATCH, win_buf, N_KV_HEADS, HEAD_DIM), 1.0),
        'cache_win_v': nrm(ks[3], (DEPTH, DEC_BATCH, win_buf, N_KV_HEADS, HEAD_DIM), 1.0),
        'state_conv': nrm(ks[4], (DEPTH, DEC_BATCH, CONV_K - 1, CONV_WIDTH), 0.5),
        'c_prompt': nrm(ks[5], (BATCH, D), 1.0),
        'c_sample': nrm(ks[6], (DEC_BATCH, D), 1.0),
        'norm1_g': 1.0 + nrm(ks[7], (DEPTH, D), 0.02),
        'norm2_g': 1.0 + nrm(ks[8], (DEPTH, D), 0.02),
        'w_ada': nrm(ks[9], (DEPTH, D, 6 * D), 0.5 * D ** -0.5),
        'b_ada': nrm(ks[10], (DEPTH, 6 * D), 0.02),
        'w_in': nrm(ks[11], (DEPTH, D, IN_WIDTH), D ** -0.5),
        'gm_norm_g': 1.0 + nrm(ks[12], (DEPTH, GM_WIDTH), 0.02),
        'gm_ws': nrm(ks[13], (DEPTH, GM_GROUPS, CHUNK, CHUNK), CHUNK ** -0.5),
        'gm_b': 1.0 + nrm(ks[14], (DEPTH, GM_GROUPS, CHUNK), 0.1),
        'conv_dw': nrm(ks[15], (DEPTH, CONV_K, CONV_WIDTH), CONV_K ** -0.5),
        'conv_b': nrm(ks[16], (DEPTH, CONV_WIDTH), 0.02),
        'conv_ln_g': 1.0 + nrm(ks[17], (DEPTH, CONV_WIDTH), 0.02),
        'conv_ln_b': nrm(ks[18], (DEPTH, CONV_WIDTH), 0.02),
        'q_norm_g': 1.0 + nrm(ks[19], (DEPTH, HEAD_DIM), 0.02),
        'k_norm_g': 1.0 + nrm(ks[20], (DEPTH, HEAD_DIM), 0.02),
        'attn_sinks': nrm(ks[21], (DEPTH, N_HEADS), 0.5),
        'w_branch': nrm(ks[22], (DEPTH, N_BRANCH, GM_WIDTH, D), GM_WIDTH ** -0.5),
        'w_out': nrm(ks[23], (DEPTH, D, D), D ** -0.5),
        'router_w': nrm(ks[24], (D, N_EXPERTS), D ** -0.5),
        'router_b': nrm(ks[25], (N_EXPERTS,), 0.01),
        'moe_w1': nrm(ks[26], (DEPTH, N_EXPERTS, D, EXPERT_FF), D ** -0.5),
        'moe_w3': nrm(ks[27], (DEPTH, N_EXPERTS, D, EXPERT_FF), D ** -0.5),
        'moe_w2': nrm(ks[28], (DEPTH, N_EXPERTS, EXPERT_FF, D), EXPERT_FF ** -0.5),
    }


def reference(x_prompt, x_sample, cache_win_k, cache_win_v, state_conv, c_prompt, c_sample,
              norm1_g, norm2_g, w_ada, b_ada, w_in, gm_norm_g, gm_ws, gm_b, conv_dw, conv_b,
              conv_ln_g, conv_ln_b, q_norm_g, k_norm_g, attn_sinks, w_branch, w_out,
              router_w, router_b, moe_w1, moe_w3, moe_w2):
    xp = x_prompt
    xs = x_sample
    kp_l, vp_l, cp_l = [], [], []
    ks_l, vs_l, cs_l, gs_l = [], [], [], []
    for l in range(DEPTH):
        lp = {'w_in': w_in[l], 'gm_g': gm_norm_g[l], 'gm_ws': gm_ws[l], 'gm_b': gm_b[l],
              'conv_dw': conv_dw[l], 'conv_b': conv_b[l], 'conv_ln_g': conv_ln_g[l],
              'conv_ln_b': conv_ln_b[l], 'qn_g': q_norm_g[l], 'kn_g': k_norm_g[l],
              'sinks': attn_sinks[l], 'w_branch': w_branch[l], 'w_out': w_out[l]}
        sh1, sc1, gt1, sh2, sc2, gt2 = ada_mod(c_prompt, w_ada[l], b_ada[l])
        y, kp, vp, cp = prompt_mixers(modulate(xp, norm1_g[l], sh1, sc1), lp)
        xp = xp + gt1 * y
        xp = xp + gt2 * moe(modulate(xp, norm2_g[l], sh2, sc2), router_w, router_b,
                            moe_w1[l], moe_w3[l], moe_w2[l])
        kp_l.append(kp)
        vp_l.append(vp)
        cp_l.append(cp)
        sh1, sc1, gt1, sh2, sc2, gt2 = ada_mod(c_sample, w_ada[l], b_ada[l])
        y, kn, vn, cn, gv = sample_mixers(modulate(xs, norm1_g[l], sh1, sc1),
                                          cache_win_k[l], cache_win_v[l], state_conv[l], lp)
        xs = xs + gt1 * y
        xs = xs + gt2 * moe(modulate(xs, norm2_g[l], sh2, sc2), router_w, router_b,
                            moe_w1[l], moe_w3[l], moe_w2[l])
        ks_l.append(kn)
        vs_l.append(vn)
        cs_l.append(cn)
        gs_l.append(gv)
    return (xp, xs, jnp.stack(kp_l), jnp.stack(vp_l), jnp.stack(cp_l),
            jnp.stack(ks_l), jnp.stack(vs_l), jnp.stack(cs_l), jnp.stack(gs_l))
```

```python
import functools
import math

import jax
import jax.numpy as jnp
import numpy as np
from jax import lax
from jax.experimental import pallas as pl
from jax.experimental.pallas import tpu as pltpu

F32 = jnp.float32
BF = jnp.bfloat16

D_MODEL = 1024
DEPTH = 2
PAST_LEN = 8192
CHUNK = 128
GM_GROUPS = 4
GM_WIDTH = 512
CONV_WIDTH = 512
CONV_K = 31
N_HEADS = 8
N_KV_HEADS = 2
HEAD_DIM = 64
ATT_WIDTH = N_HEADS * HEAD_DIM
KV_WIDTH = N_KV_HEADS * HEAD_DIM
WINDOW = 128
ROPE_THETA = 500000.0
ROT_DIM = HEAD_DIM // 4
N_BRANCH = 3
N_EXPERTS = 16
EXPERTS_PER_GROUP = 4
EXPERT_FF = 256
EPS = 1e-6

OFF_GM_U = 0
OFF_GM_V = 512
OFF_CV_A = 1024
OFF_CV_G = 1536
OFF_Q = 2048
OFF_KD = 2560
OFF_VD = 2816
OFF_GATE = 3072
IN_WIDTH_X = OFF_GATE + N_BRANCH * D_MODEL
SRC_OFF_K = 2560
SRC_OFF_V = 2688
SRC_OFF_GATE = 2816

LANES = 128
HALO = 32
MIX_TILE = 512
MOE_TILE = 1024
ADA_ROWS = 136
ADA_BLOCK = 768
SAMPLE_BLOCK = 32
VMEM_LIMIT = 56 * 1024 * 1024
NEG_BIG = -1e30


def _dot(a, b):
    return jnp.dot(a, b, preferred_element_type=F32)


def _dot_nt(a, b):
    return lax.dot_general(a, b, (((1,), (1,)), ((), ())), preferred_element_type=F32)


def _sigmoid(x):
    return 0.5 * jnp.tanh(0.5 * x) + 0.5


def _silu(x):
    return x * _sigmoid(x)


def _gelu(x):
    return 0.5 * x * (1.0 + jnp.tanh(0.7978845608028654 * (x + 0.044715 * (x * x * x))))


def _split_bf16(x):
    hi = x.astype(BF)
    lo = (x - hi.astype(F32)).astype(BF)
    return hi, lo


def _modulate(x, g, shift, scale):
    ms = jnp.mean(x * x, axis=-1, keepdims=True)
    return (x * lax.rsqrt(ms + EPS) * g) * (1.0 + scale) + shift


def _head_norm(x, blockdiag, g):
    hi, lo = _split_bf16(x * x)
    ssum = _dot(hi, blockdiag) + _dot(lo, blockdiag)
    return x * lax.rsqrt(ssum * (1.0 / HEAD_DIM) + EPS) * g


def _rope(x, c, s):
    width = x.shape[-1]
    reps = width // LANES
    cc = jnp.concatenate([c] * reps, axis=-1)
    ss = jnp.concatenate([s] * reps, axis=-1)
    lane = lax.broadcasted_iota(jnp.int32, x.shape, x.ndim - 1) % HEAD_DIM
    partner = jnp.where(lane < ROT_DIM // 2,
                        pltpu.roll(x, width - ROT_DIM // 2, x.ndim - 1),
                        pltpu.roll(x, ROT_DIM // 2, x.ndim - 1))
    return x * cc + partner * ss


def _layer_norm_silu(y, g, b):
    mu = jnp.mean(y, axis=-1, keepdims=True)
    yc = y - mu
    var = jnp.mean(yc * yc, axis=-1, keepdims=True)
    return _silu(yc * lax.rsqrt(var + EPS) * g + b)


def _low_half(shape):
    return lax.broadcasted_iota(jnp.int32, shape, len(shape) - 1) % LANES < HEAD_DIM


def _ada_kernel(c_ref, w_ref, b_ref, o_ref):
    s = _silu(c_ref[...]).astype(BF)
    o_ref[...] = _dot(s, w_ref[...].astype(BF)) + b_ref[...]


def _ada_call(c_all, w_ada, b_ada):
    nb = (6 * D_MODEL) // ADA_BLOCK
    return pl.pallas_call(
        _ada_kernel,
        out_shape=jax.ShapeDtypeStruct((DEPTH, ADA_ROWS, 6 * D_MODEL), F32),
        grid=(DEPTH, nb),
        in_specs=[
            pl.BlockSpec((ADA_ROWS, D_MODEL), lambda l, j: (0, 0)),
            pl.BlockSpec((None, D_MODEL, ADA_BLOCK), lambda l, j: (l, 0, j)),
            pl.BlockSpec((None, 1, ADA_BLOCK), lambda l, j: (l, 0, j)),
        ],
        out_specs=pl.BlockSpec((None, ADA_ROWS, ADA_BLOCK), lambda l, j: (l, 0, j)),
        compiler_params=pltpu.CompilerParams(
            dimension_semantics=("arbitrary", "arbitrary")),
        name="ada_mod",
    )(c_all, w_ada, b_ada.reshape(DEPTH, 1, 6 * D_MODEL))


def _mixer_kernel(sinks_ref, x_ref, mod_ref, n1g_ref, win_ref, gmg_ref, gmws_ref, gmb_ref,
                  cdw_ref, cb_ref, clg_ref, clb_ref, qn_ref, kn_ref, ropec_ref, ropes_ref,
                  bias_ref, bd_ref, wb_ref, wout_ref,
                  x1_ref, kwin_ref, vwin_ref, cst_ref,
                  kd_s, vd_s, abuf, oatt_s, *, layer, tm):
    t = pl.program_id(1)
    last = pl.num_programs(1) - 1
    nblk = tm // WINDOW

    @pl.when(t == 0)
    def _():
        kd_s[0:WINDOW, :] = jnp.zeros((WINDOW, 2 * LANES), BF)
        vd_s[0:WINDOW, :] = jnp.zeros((WINDOW, 2 * LANES), BF)
        abuf[0:HALO, :] = jnp.zeros((HALO, CONV_WIDTH), F32)

    x = x_ref[...]
    mod = mod_ref[...]
    sh1 = mod[:, 0:D_MODEL]
    sc1 = mod[:, D_MODEL:2 * D_MODEL]
    gt1 = mod[:, 2 * D_MODEL:3 * D_MODEL]
    hb = _modulate(x, n1g_ref[...], sh1, sc1).astype(BF)

    def proj(off, width):
        return _dot(hb, win_ref[:, off:off + width])

    u = _gelu(proj(OFF_GM_U, GM_WIDTH))
    gv = _gelu(proj(OFF_GM_V, GM_WIDTH))
    v = gv * lax.rsqrt(jnp.mean(gv * gv, axis=-1, keepdims=True) + EPS) * gmg_ref[...]
    vb = v.astype(BF)
    row = lax.broadcasted_iota(jnp.int32, (CHUNK, CHUNK), 0)
    col = lax.broadcasted_iota(jnp.int32, (CHUNK, CHUNK), 1)
    ws = [jnp.where(row >= col, gmws_ref[g], 0.0).astype(BF) for g in range(GM_GROUPS)]
    gmb = gmb_ref[...]
    chunks = []
    for c in range(nblk):
        parts = [_dot(ws[g], vb[c * CHUNK:(c + 1) * CHUNK, g * LANES:(g + 1) * LANES])
                 for g in range(GM_GROUPS)]
        chunks.append(jnp.concatenate(parts, axis=1) + gmb)
    o_gm = (u * jnp.concatenate(chunks, axis=0)).astype(BF)
    acc = _sigmoid(proj(OFF_GATE, D_MODEL)) * _dot(o_gm, wb_ref[0])

    a = proj(OFF_CV_A, CONV_WIDTH) * _sigmoid(proj(OFF_CV_G, CONV_WIDTH))
    abuf[HALO:HALO + tm, :] = a
    y = cb_ref[...] + cdw_ref[0:1, :] * abuf[pl.ds(HALO - (CONV_K - 1), tm), :]
    for j in range(1, CONV_K):
        y = y + cdw_ref[j:j + 1, :] * abuf[pl.ds(HALO - (CONV_K - 1) + j, tm), :]
    abuf[0:HALO, :] = abuf[tm:tm + HALO, :]
    o_cv = _layer_norm_silu(y, clg_ref[...], clb_ref[...]).astype(BF)
    acc = acc + _sigmoid(proj(OFF_GATE + D_MODEL, D_MODEL)) * _dot(o_cv, wb_ref[1])

    rc = ropec_ref[...]
    rs = ropes_ref[...]
    bd = bd_ref[...]
    q = _rope(_head_norm(proj(OFF_Q, ATT_WIDTH), bd, qn_ref[...]), rc, rs)
    qb = (q * (HEAD_DIM ** -0.5)).astype(BF)
    kd = _rope(_head_norm(proj(OFF_KD, 2 * LANES), bd[0:2 * LANES, 0:2 * LANES], kn_ref[...]), rc, rs)
    vd = proj(OFF_VD, 2 * LANES)
    kd_s[WINDOW:WINDOW + tm, :] = kd.astype(BF)
    vd_s[WINDOW:WINDOW + tm, :] = vd.astype(BF)

    lo_q = _low_half((WINDOW, LANES))
    zero_q = jnp.zeros((WINDOW, LANES), BF)
    band = bias_ref[1]
    for bi in range(nblk):
        bias = jnp.where(t == 0, bias_ref[0], band) if bi == 0 else band
        bias4 = jnp.concatenate([bias] * 4, axis=0)
        for g in range(N_KV_HEADS):
            tiles = [qb[bi * WINDOW:(bi + 1) * WINDOW, (2 * g + j) * LANES:(2 * g + j + 1) * LANES]
                     for j in range(2)]
            qs = jnp.concatenate([jnp.where(lo_q, tiles[0], zero_q), jnp.where(lo_q, zero_q, tiles[0]),
                                  jnp.where(lo_q, tiles[1], zero_q), jnp.where(lo_q, zero_q, tiles[1])],
                                 axis=0)
            keys = kd_s[bi * WINDOW:(bi + 2) * WINDOW, g * LANES:(g + 1) * LANES]
            vals = vd_s[bi * WINDOW:(bi + 2) * WINDOW, g * LANES:(g + 1) * LANES]
            s = _dot_nt(qs, keys) + bias4
            outs = []
            for hh in range(4):
                sink = sinks_ref[layer, 4 * g + hh]
                sh = s[hh * WINDOW:(hh + 1) * WINDOW, :]
                m = jnp.maximum(jnp.max(sh, axis=-1, keepdims=True), sink)
                p = jnp.exp(sh - m)
                den = jnp.sum(p, axis=-1, keepdims=True) + jnp.exp(sink - m)
                outs.append(_dot(p.astype(BF), vals) / den)
            for j in range(2):
                oatt_s[bi * WINDOW:(bi + 1) * WINDOW, (2 * g + j) * LANES:(2 * g + j + 1) * LANES] = (
                    jnp.where(lo_q, outs[2 * j], outs[2 * j + 1]).astype(BF))
    kd_s[0:WINDOW, :] = kd_s[tm:tm + WINDOW, :]
    vd_s[0:WINDOW, :] = vd_s[tm:tm + WINDOW, :]
    acc = acc + _sigmoid(proj(OFF_GATE + 2 * D_MODEL, D_MODEL)) * _dot(oatt_s[...], wb_ref[2])

    x1_ref[...] = x + gt1 * _dot(acc.astype(BF), wout_ref[...])

    @pl.when(t == last)
    def _():
        kl = kd[tm - WINDOW:tm, :]
        vl = vd[tm - WINDOW:tm, :]
        kwin_ref[...] = jnp.where(lo_q, kl[:, 0:LANES], kl[:, LANES:2 * LANES])
        vwin_ref[...] = jnp.where(lo_q, vl[:, 0:LANES], vl[:, LANES:2 * LANES])
        cst_ref[...] = a[tm - (CONV_K - 1):tm, :]


def _const_spec(shape, index_map):
    return pl.BlockSpec(shape, index_map, pipeline_mode=pl.Buffered(1))


def _mixer_call(layer, x, modp, p, tm):
    nb, seq, _ = x.shape
    nt = seq // tm
    lsel3 = lambda b, t: (layer, 0, 0)
    kernel = functools.partial(_mixer_kernel, layer=layer, tm=tm)
    return pl.pallas_call(
        kernel,
        out_shape=(
            jax.ShapeDtypeStruct((nb, seq, D_MODEL), F32),
            jax.ShapeDtypeStruct((nb, WINDOW, KV_WIDTH), F32),
            jax.ShapeDtypeStruct((nb, WINDOW, KV_WIDTH), F32),
            jax.ShapeDtypeStruct((nb, CONV_K - 1, CONV_WIDTH), F32),
        ),
        grid=(nb, nt),
        in_specs=[
            pl.BlockSpec(memory_space=pltpu.SMEM),
            pl.BlockSpec((None, tm, D_MODEL), lambda b, t: (b, t, 0)),
            pl.BlockSpec((None, None, 1, 3 * D_MODEL), lambda b, t: (layer, b, 0, 0)),
            _const_spec((None, 1, D_MODEL), lsel3),
            _const_spec((None, D_MODEL, IN_WIDTH_X), lsel3),
            _const_spec((None, 1, GM_WIDTH), lsel3),
            _const_spec((None, GM_GROUPS, CHUNK, CHUNK), lambda b, t: (layer, 0, 0, 0)),
            _const_spec((None, CHUNK, GM_WIDTH), lsel3),
            _const_spec((None, CONV_K, CONV_WIDTH), lsel3),
            _const_spec((None, 1, CONV_WIDTH), lsel3),
            _const_spec((None, 1, CONV_WIDTH), lsel3),
            _const_spec((None, 1, CONV_WIDTH), lsel3),
            _const_spec((None, 1, ATT_WIDTH), lsel3),
            _const_spec((None, 1, 2 * LANES), lsel3),
            pl.BlockSpec((tm, LANES), lambda b, t: (t, 0)),
            pl.BlockSpec((tm, LANES), lambda b, t: (t, 0)),
            _const_spec((2, WINDOW, 2 * WINDOW), lambda b, t: (0, 0, 0)),
            _const_spec((ATT_WIDTH, ATT_WIDTH), lambda b, t: (0, 0)),
            _const_spec((None, N_BRANCH, GM_WIDTH, D_MODEL), lambda b, t: (layer, 0, 0, 0)),
            _const_spec((None, D_MODEL, D_MODEL), lsel3),
        ],
        out_specs=(
            pl.BlockSpec((None, tm, D_MODEL), lambda b, t: (b, t, 0)),
            pl.BlockSpec((None, WINDOW, KV_WIDTH), lambda b, t: (b, 0, 0)),
            pl.BlockSpec((None, WINDOW, KV_WIDTH), lambda b, t: (b, 0, 0)),
            pl.BlockSpec((None, CONV_K - 1, CONV_WIDTH), lambda b, t: (b, 0, 0)),
        ),
        scratch_shapes=[
            pltpu.VMEM((WINDOW + tm, 2 * LANES), BF),
            pltpu.VMEM((WINDOW + tm, 2 * LANES), BF),
            pltpu.VMEM((HALO + tm, CONV_WIDTH), F32),
            pltpu.VMEM((tm, ATT_WIDTH), BF),
        ],
        compiler_params=pltpu.CompilerParams(
            dimension_semantics=("arbitrary", "arbitrary"),
            vmem_limit_bytes=VMEM_LIMIT),
        name=f"prompt_mixers_l{layer}",
    )(p["sinks"], x, modp, p["n1g"], p["w_in"], p["gm_g"], p["gm_ws"], p["gm_bias"],
      p["conv_dw"], p["conv_b"], p["conv_ln_g"], p["conv_ln_b"], p["qn"], p["kn"],
      p["rope_c"], p["rope_s"], p["band_bias"], p["blockdiag"], p["w_branch"], p["w_out"])


def _group_rot(x, lane, k, group):
    width = x.shape[-1]
    fwd = pltpu.roll(x, width - k, 1)
    back = pltpu.roll(x, group - k, 1)
    return jnp.where(lane % group < group - k, fwd, back)


def _route(h2, rwh_ref, rwl_ref, rb_ref):
    hi, lo = _split_bf16(h2)
    rwh = rwh_ref[...]
    logits = _dot(hi, rwh) + _dot(lo, rwh) + _dot(hi, rwl_ref[...])
    scores = _sigmoid(logits)
    lane = lax.broadcasted_iota(jnp.int32, scores.shape, 1)
    biased = scores + rb_ref[...]
    pos = lane % EXPERTS_PER_GROUP
    r = [_group_rot(biased, lane, k, EXPERTS_PER_GROUP) for k in (1, 2, 3)]
    hi1, lo1 = jnp.maximum(biased, r[0]), jnp.minimum(biased, r[0])
    hi2, lo2 = jnp.maximum(r[1], r[2]), jnp.minimum(r[1], r[2])
    gs = jnp.maximum(hi1, hi2) + jnp.maximum(jnp.minimum(hi1, hi2), jnp.maximum(lo1, lo2))
    grp = (lane % N_EXPERTS) // EXPERTS_PER_GROUP
    in_group = lane < N_EXPERTS
    for k in (1, 2, 3):
        other = _group_rot(gs, lane, EXPERTS_PER_GROUP * k, N_EXPERTS)
        other_grp = (grp + k) % (N_EXPERTS // EXPERTS_PER_GROUP)
        in_group = in_group & ((gs > other) | ((gs == other) & (grp < other_grp)))
    rank = jnp.zeros(scores.shape, jnp.int32)
    for k in (1, 2, 3):
        other_pos = (pos + k) % EXPERTS_PER_GROUP
        beats = (r[k - 1] > biased) | ((r[k - 1] == biased) & (other_pos < pos))
        rank = rank + beats.astype(jnp.int32)
    sel = jnp.where(in_group & (rank < 2), scores, 0.0)
    return sel / jnp.sum(sel, axis=-1, keepdims=True)


def _moe_kernel(x_ref, mod_ref, n2g_ref, rwh_ref, rwl_ref, rb_ref, w1_ref, w3_ref, w2_ref,
                o_ref, h_s, comb_s, acc_s):
    e = pl.program_id(1)

    @pl.when(e == 0)
    def _():
        mod = mod_ref[...]
        h2 = _modulate(x_ref[...], n2g_ref[...], mod[:, 0:D_MODEL], mod[:, D_MODEL:2 * D_MODEL])
        h_s[...] = h2.astype(BF)
        comb_s[...] = _route(h2, rwh_ref, rwl_ref, rb_ref)
        acc_s[...] = jnp.zeros(acc_s.shape, F32)

    hb = h_s[...]
    comb = comb_s[...]
    lane = lax.broadcasted_iota(jnp.int32, comb.shape, 1)
    ce = jnp.sum(jnp.where(lane == e, comb, 0.0), axis=-1, keepdims=True)
    act = _silu(_dot(hb, w1_ref[...])) * _dot(hb, w3_ref[...]) * ce
    acc_s[...] += _dot(act.astype(BF), w2_ref[...])

    @pl.when(e == pl.num_programs(1) - 1)
    def _():
        gt2 = mod_ref[...][:, 2 * D_MODEL:3 * D_MODEL]
        o_ref[...] = x_ref[...] + gt2 * acc_s[...]


def _moe_call(layer, x2d, mod4, tiles_per_mod, p, tm, name):
    n = x2d.shape[0]
    mod_rows = mod4.shape[2]
    lsel3 = lambda i, e: (layer, 0, 0)
    return pl.pallas_call(
        _moe_kernel,
        out_shape=jax.ShapeDtypeStruct((n, D_MODEL), F32),
        grid=(n // tm, N_EXPERTS),
        in_specs=[
            pl.BlockSpec((tm, D_MODEL), lambda i, e: (i, 0)),
            pl.BlockSpec((None, None, mod_rows, 3 * D_MODEL),
                         lambda i, e: (layer, i // tiles_per_mod, 0, 1)),
            pl.BlockSpec((None, 1, D_MODEL), lsel3),
            pl.BlockSpec((D_MODEL, LANES), lambda i, e: (0, 0)),
            pl.BlockSpec((D_MODEL, LANES), lambda i, e: (0, 0)),
            pl.BlockSpec((1, LANES), lambda i, e: (0, 0)),
            pl.BlockSpec((None, None, D_MODEL, EXPERT_FF), lambda i, e: (layer, e, 0, 0)),
            pl.BlockSpec((None, None, D_MODEL, EXPERT_FF), lambda i, e: (layer, e, 0, 0)),
            pl.BlockSpec((None, None, EXPERT_FF, D_MODEL), lambda i, e: (layer, e, 0, 0)),
        ],
        out_specs=pl.BlockSpec((tm, D_MODEL), lambda i, e: (i, 0)),
        scratch_shapes=[
            pltpu.VMEM((tm, D_MODEL), BF),
            pltpu.VMEM((tm, LANES), F32),
            pltpu.VMEM((tm, D_MODEL), F32),
        ],
        compiler_params=pltpu.CompilerParams(
            dimension_semantics=("arbitrary", "arbitrary"),
            vmem_limit_bytes=VMEM_LIMIT),
        name=name,
    )(x2d, mod4, p["n2g"], p["rw_hi"], p["rw_lo"], p["rb"], p["moe_w1"], p["moe_w3"], p["moe_w2"])


def _sample_proj_kernel(x_ref, mod_ref, n1g_ref, w_ref, z_ref):
    mod = mod_ref[...]
    h = _modulate(x_ref[...], n1g_ref[...], mod[:, 0:D_MODEL], mod[:, D_MODEL:2 * D_MODEL])
    z_ref[...] = _dot(h.astype(BF), w_ref[...])


def _sample_proj_call(layer, xs, mods, p):
    n = xs.shape[0]
    nblk = IN_WIDTH_X // ADA_BLOCK
    return pl.pallas_call(
        _sample_proj_kernel,
        out_shape=jax.ShapeDtypeStruct((n, IN_WIDTH_X), F32),
        grid=(nblk,),
        in_specs=[
            pl.BlockSpec((n, D_MODEL), lambda j: (0, 0)),
            pl.BlockSpec((None, n, 3 * D_MODEL), lambda j: (layer, 0, 0)),
            pl.BlockSpec((None, 1, D_MODEL), lambda j: (layer, 0, 0)),
            pl.BlockSpec((None, D_MODEL, ADA_BLOCK), lambda j: (layer, 0, j)),
        ],
        out_specs=pl.BlockSpec((n, ADA_BLOCK), lambda j: (0, j)),
        compiler_params=pltpu.CompilerParams(dimension_semantics=("arbitrary",)),
        name=f"sample_proj_l{layer}",
    )(xs, mods, p["n1g"], p["w_in"])


def _sample_mixer_kernel(z_ref, zg_ref, x_ref, mod_ref, kc_ref, vc_ref, st_ref,
                         gmg_ref, gmw_ref, gmb_ref, cdw_ref, cb_ref, clg_ref, clb_ref,
                         qn_ref, kn_ref, ropec_ref, ropes_ref, sink_ref, bd_ref, wb_ref, wout_ref,
                         x1_ref, ko_ref, vo_ref, sto_ref, gv_ref,
                         qf_s, of_s, oc_s, *, bb):
    i = pl.program_id(0)
    z = z_ref[...]

    u = _gelu(z[:, OFF_GM_U:OFF_GM_U + GM_WIDTH])
    gv = _gelu(z[:, OFF_GM_V:OFF_GM_V + GM_WIDTH])
    v = gv * lax.rsqrt(jnp.mean(gv * gv, axis=-1, keepdims=True) + EPS) * gmg_ref[...]
    gv_ref[...] = v
    o_gm = u * (v * gmw_ref[...] + gmb_ref[...])

    a = z[:, OFF_CV_A:OFF_CV_A + CONV_WIDTH] * _sigmoid(z[:, OFF_CV_G:OFF_CV_G + CONV_WIDTH])
    y = cb_ref[...] + cdw_ref[CONV_K - 1:CONV_K, :] * a
    for j in range(CONV_K - 1):
        y = y + cdw_ref[j:j + 1, :] * st_ref[:, j * CONV_WIDTH:(j + 1) * CONV_WIDTH]
    o_cv = _layer_norm_silu(y, clg_ref[...], clb_ref[...])
    sto_ref[:, 0:(CONV_K - 2) * CONV_WIDTH] = st_ref[:, CONV_WIDTH:(CONV_K - 1) * CONV_WIDTH]
    sto_ref[:, (CONV_K - 2) * CONV_WIDTH:(CONV_K - 1) * CONV_WIDTH] = a

    rc = ropec_ref[...]
    rs = ropes_ref[...]
    bd = bd_ref[...]
    q = _rope(_head_norm(z[:, OFF_Q:OFF_Q + ATT_WIDTH], bd, qn_ref[...]), rc, rs) * (HEAD_DIM ** -0.5)
    kd = _rope(_head_norm(z[:, OFF_KD:OFF_KD + 2 * LANES], bd[0:2 * LANES, 0:2 * LANES], kn_ref[...]), rc, rs)
    vd = z[:, OFF_VD:OFF_VD + 2 * LANES]
    lo = _low_half((bb, LANES))
    knew = jnp.where(lo, kd[:, 0:LANES], kd[:, LANES:2 * LANES])
    vnew = jnp.where(lo, vd[:, 0:LANES], vd[:, LANES:2 * LANES])
    for h in range(N_HEADS):
        tile = q[:, (h // 2) * LANES:(h // 2 + 1) * LANES]
        masked = jnp.where(lo, tile, 0.0) if h % 2 == 0 else jnp.where(lo, 0.0, tile)
        if h % 2 != h // (N_HEADS // N_KV_HEADS):
            masked = pltpu.roll(masked, HEAD_DIM, 1)
        qf_s[:, h, :] = masked
    qf = qf_s[...]
    kc = kc_ref[...]
    vc = vc_ref[...]
    s = jnp.einsum("bhd,bpd->bhp", qf.astype(BF), kc.astype(BF), preferred_element_type=F32)
    s_new = jnp.sum(qf * knew[:, None, :], axis=-1, keepdims=True)
    sink = sink_ref[...][None, :, 0:1]
    m = jnp.maximum(jnp.maximum(jnp.max(s, axis=-1, keepdims=True), s_new), sink)
    pr = jnp.exp(s - m)
    p_new = jnp.exp(s_new - m)
    den = jnp.sum(pr, axis=-1, keepdims=True) + p_new + jnp.exp(sink - m)
    o = jnp.einsum("bhp,bpd->bhd", pr.astype(BF), vc.astype(BF), preferred_element_type=F32)
    of_s[...] = (o + p_new * vnew[:, None, :]) / den
    att_tiles = []
    for j in range(N_HEADS // 2):
        g = j // 2
        first = of_s[:, 2 * j, :]
        second = of_s[:, 2 * j + 1, :]
        if g == 1:
            first = pltpu.roll(first, HEAD_DIM, 1)
        else:
            second = pltpu.roll(second, HEAD_DIM, 1)
        att_tiles.append(jnp.where(lo, first, second))
    o_att = jnp.concatenate(att_tiles, axis=1)
    ko_ref[:, pl.ds(0, WINDOW - 1), :] = kc_ref[:, pl.ds(1, WINDOW - 1), :]
    ko_ref[:, WINDOW - 1, :] = knew
    vo_ref[:, pl.ds(0, WINDOW - 1), :] = vc_ref[:, pl.ds(1, WINDOW - 1), :]
    vo_ref[:, WINDOW - 1, :] = vnew

    r0 = pl.multiple_of(i * bb, bb)
    oc_s[pl.ds(r0, bb), 0:GM_WIDTH] = o_gm.astype(BF)
    oc_s[pl.ds(r0, bb), GM_WIDTH:GM_WIDTH + CONV_WIDTH] = o_cv.astype(BF)
    oc_s[pl.ds(r0, bb), GM_WIDTH + CONV_WIDTH:GM_WIDTH + CONV_WIDTH + ATT_WIDTH] = o_att.astype(BF)

    @pl.when(i == pl.num_programs(0) - 1)
    def _():
        acc = _sigmoid(zg_ref[:, 0:D_MODEL]) * _dot(oc_s[:, 0:GM_WIDTH], wb_ref[0])
        acc = acc + _sigmoid(zg_ref[:, D_MODEL:2 * D_MODEL]) * _dot(
            oc_s[:, GM_WIDTH:GM_WIDTH + CONV_WIDTH], wb_ref[1])
        acc = acc + _sigmoid(zg_ref[:, 2 * D_MODEL:3 * D_MODEL]) * _dot(
            oc_s[:, GM_WIDTH + CONV_WIDTH:GM_WIDTH + CONV_WIDTH + ATT_WIDTH], wb_ref[2])
        gt1 = mod_ref[...][:, 2 * D_MODEL:3 * D_MODEL]
        x1_ref[...] = x_ref[...] + gt1 * _dot(acc.astype(BF), wout_ref[...])


def _sample_mixer_call(layer, z, xs, mods, kc, vc, st, p, bb):
    n = xs.shape[0]
    state_w = (CONV_K - 1) * CONV_WIDTH
    lsel3 = lambda i: (layer, 0, 0)
    kernel = functools.partial(_sample_mixer_kernel, bb=bb)
    return pl.pallas_call(
        kernel,
        out_shape=(
            jax.ShapeDtypeStruct((n, D_MODEL), F32),
            jax.ShapeDtypeStruct((n, WINDOW, KV_WIDTH), F32),
            jax.ShapeDtypeStruct((n, WINDOW, KV_WIDTH), F32),
            jax.ShapeDtypeStruct((n, state_w), F32),
            jax.ShapeDtypeStruct((n, GM_WIDTH), F32),
        ),
        grid=(n // bb,),
        in_specs=[
            pl.BlockSpec((bb, IN_WIDTH_X), lambda i: (i, 0)),
            pl.BlockSpec((n, N_BRANCH * D_MODEL), lambda i: (0, 1)),
            pl.BlockSpec((n, D_MODEL), lambda i: (0, 0)),
            pl.BlockSpec((None, n, 3 * D_MODEL), lsel3),
            pl.BlockSpec((None, bb, WINDOW, KV_WIDTH), lambda i: (layer, i, 0, 0)),
            pl.BlockSpec((None, bb, WINDOW, KV_WIDTH), lambda i: (layer, i, 0, 0)),
            pl.BlockSpec((None, bb, state_w), lambda i: (layer, i, 0)),
            pl.BlockSpec((None, 1, GM_WIDTH), lsel3),
            pl.BlockSpec((None, 1, GM_WIDTH), lsel3),
            pl.BlockSpec((None, 1, GM_WIDTH), lsel3),
            pl.BlockSpec((None, CONV_K, CONV_WIDTH), lsel3),
            pl.BlockSpec((None, 1, CONV_WIDTH), lsel3),
            pl.BlockSpec((None, 1, CONV_WIDTH), lsel3),
            pl.BlockSpec((None, 1, CONV_WIDTH), lsel3),
            pl.BlockSpec((None, 1, ATT_WIDTH), lsel3),
            pl.BlockSpec((None, 1, 2 * LANES), lsel3),
            pl.BlockSpec((1, LANES), lambda i: (0, 0)),
            pl.BlockSpec((1, LANES), lambda i: (0, 0)),
            pl.BlockSpec((None, N_HEADS, LANES), lsel3),
            pl.BlockSpec((ATT_WIDTH, ATT_WIDTH), lambda i: (0, 0)),
            pl.BlockSpec((None, N_BRANCH, GM_WIDTH, D_MODEL), lambda i: (layer, 0, 0, 0)),
            pl.BlockSpec((None, D_MODEL, D_MODEL), lsel3),
        ],
        out_specs=(
            pl.BlockSpec((n, D_MODEL), lambda i: (0, 0)),
            pl.BlockSpec((bb, WINDOW, KV_WIDTH), lambda i: (i, 0, 0)),
            pl.BlockSpec((bb, WINDOW, KV_WIDTH), lambda i: (i, 0, 0)),
            pl.BlockSpec((bb, state_w), lambda i: (i, 0)),
            pl.BlockSpec((bb, GM_WIDTH), lambda i: (i, 0)),
        ),
        scratch_shapes=[
            pltpu.VMEM((bb, N_HEADS, LANES), F32),
            pltpu.VMEM((bb, N_HEADS, LANES), F32),
            pltpu.VMEM((n, GM_WIDTH + CONV_WIDTH + ATT_WIDTH), BF),
        ],
        compiler_params=pltpu.CompilerParams(
            dimension_semantics=("arbitrary",), vmem_limit_bytes=VMEM_LIMIT),
        name=f"sample_mixers_l{layer}",
    )(z, z, xs, mods, kc, vc, st, p["gm_g"], p["gm_w0"], p["gm_b0"], p["conv_dw"], p["conv_b"],
      p["conv_ln_g"], p["conv_ln_b"], p["qn"], p["kn"], p["rope_c1"], p["rope_s1"],
      p["sink_lanes"], p["blockdiag"], p["w_branch"], p["w_out"])


def _rope_tables(pos):
    half = ROT_DIM // 2
    freqs = jnp.exp(-math.log(ROPE_THETA) * jnp.arange(half, dtype=F32) * (2.0 / ROT_DIM))
    ang = pos.astype(F32)[:, None] * freqs[None, :]
    cos, sin = jnp.cos(ang), jnp.sin(ang)
    n = pos.shape[0]
    ones = jnp.ones((n, HEAD_DIM - ROT_DIM), F32)
    zeros = jnp.zeros((n, HEAD_DIM - ROT_DIM), F32)
    c64 = jnp.concatenate([cos, cos, ones], axis=1)
    s64 = jnp.concatenate([-sin, sin, zeros], axis=1)
    return jnp.concatenate([c64, c64], axis=1), jnp.concatenate([s64, s64], axis=1)


def _band_bias():
    i = np.arange(WINDOW)[:, None]
    j = np.arange(2 * WINDOW)[None, :]
    band = (j >= i) & (j <= i + WINDOW)
    first = band & (j >= WINDOW)
    out = np.where(np.stack([first, band]), 0.0, NEG_BIG).astype(np.float32)
    return jnp.asarray(out)


def _blockdiag():
    idx = np.arange(ATT_WIDTH) // HEAD_DIM
    return jnp.asarray((idx[:, None] == idx[None, :]).astype(np.float32), dtype=BF)


def _dup_heads(w):
    h0, h1 = w[..., :HEAD_DIM], w[..., HEAD_DIM:]
    return jnp.concatenate([h0, h0, h1, h1], axis=-1)


def _prepare(norm1_g, norm2_g, w_in, gm_norm_g, gm_ws, gm_b, conv_dw, conv_b, conv_ln_g,
             conv_ln_b, q_norm_g, k_norm_g, attn_sinks, w_branch, w_out, router_w, router_b,
             moe_w1, moe_w3, moe_w2, seq):
    w_in_x = jnp.concatenate([
        w_in[:, :, :SRC_OFF_K],
        _dup_heads(w_in[:, :, SRC_OFF_K:SRC_OFF_V]),
        _dup_heads(w_in[:, :, SRC_OFF_V:SRC_OFF_GATE]),
        w_in[:, :, SRC_OFF_GATE:],
    ], axis=-1).astype(BF)
    rw = jnp.pad(router_w.astype(F32), ((0, 0), (0, LANES - N_EXPERTS)))
    rw_hi = rw.astype(BF)
    rw_lo = (rw - rw_hi.astype(F32)).astype(BF)
    rope_c, rope_s = _rope_tables(jnp.arange(seq, dtype=jnp.int32))
    rope_c1, rope_s1 = _rope_tables(PAST_LEN + jnp.arange(1, dtype=jnp.int32))
    row3 = lambda a: a.reshape(DEPTH, 1, a.shape[-1])
    return {
        "n1g": row3(norm1_g), "n2g": row3(norm2_g), "w_in": w_in_x,
        "gm_g": row3(gm_norm_g), "gm_ws": gm_ws,
        "gm_bias": jnp.repeat(jnp.swapaxes(gm_b, 1, 2), LANES, axis=2),
        "gm_w0": jnp.repeat(gm_ws[:, :, 0, 0], LANES, axis=1).reshape(DEPTH, 1, GM_WIDTH),
        "gm_b0": jnp.repeat(gm_b[:, :, 0], LANES, axis=1).reshape(DEPTH, 1, GM_WIDTH),
        "conv_dw": conv_dw, "conv_b": row3(conv_b), "conv_ln_g": row3(conv_ln_g),
        "conv_ln_b": row3(conv_ln_b),
        "qn": row3(jnp.tile(q_norm_g, (1, N_HEADS))),
        "kn": row3(jnp.tile(k_norm_g, (1, 2 * N_KV_HEADS))),
        "sinks": attn_sinks,
        "sink_lanes": jnp.broadcast_to(attn_sinks[:, :, None], (DEPTH, N_HEADS, LANES)),
        "rope_c": rope_c, "rope_s": rope_s, "rope_c1": rope_c1, "rope_s1": rope_s1,
        "band_bias": _band_bias(), "blockdiag": _blockdiag(),
        "w_branch": w_branch.astype(BF), "w_out": w_out.astype(BF),
        "rw_hi": rw_hi, "rw_lo": rw_lo,
        "rb": jnp.pad(router_b.astype(F32), (0, LANES - N_EXPERTS)).reshape(1, LANES),
        "moe_w1": moe_w1.astype(BF), "moe_w3": moe_w3.astype(BF), "moe_w2": moe_w2.astype(BF),
    }


def kernel(x_prompt, x_sample, cache_win_k, cache_win_v, state_conv, c_prompt, c_sample, norm1_g, norm2_g, w_ada, b_ada, w_in, gm_norm_g, gm_ws, gm_b, conv_dw, conv_b, conv_ln_g, conv_ln_b, q_norm_g, k_norm_g, attn_sinks, w_branch, w_out, router_w, router_b, moe_w1, moe_w3, moe_w2):
    nb, seq, _ = x_prompt.shape
    ns = x_sample.shape[0]
    p = _prepare(norm1_g, norm2_g, w_in, gm_norm_g, gm_ws, gm_b, conv_dw, conv_b, conv_ln_g,
                 conv_ln_b, q_norm_g, k_norm_g, attn_sinks, w_branch, w_out, router_w, router_b,
                 moe_w1, moe_w3, moe_w2, seq)
    c_all = jnp.concatenate(
        [c_prompt, c_sample, jnp.zeros((ADA_ROWS - nb - ns, D_MODEL), F32)], axis=0)
    mod = _ada_call(c_all, w_ada, b_ada)
    mod_p = mod[:, :nb].reshape(DEPTH, nb, 1, 6 * D_MODEL)
    mod_s = mod[:, nb:nb + ns]
    mod_s4 = mod_s.reshape(DEPTH, 1, ns, 6 * D_MODEL)

    kc = cache_win_k.reshape(DEPTH, ns, WINDOW, KV_WIDTH)
    vc = cache_win_v.reshape(DEPTH, ns, WINDOW, KV_WIDTH)
    st = state_conv.reshape(DEPTH, ns, (CONV_K - 1) * CONV_WIDTH)

    tm = min(MIX_TILE, seq)
    tmoe = min(MOE_TILE, seq)
    xp = x_prompt
    xs = x_sample.reshape(ns, D_MODEL)
    kp_l, vp_l, cp_l, ks_l, vs_l, cs_l, gs_l = [], [], [], [], [], [], []
    for l in range(DEPTH):
        xp, kp, vp, cp = _mixer_call(l, xp, mod_p, p, tm)
        xp = _moe_call(l, xp.reshape(nb * seq, D_MODEL), mod_p, seq // tmoe, p, tmoe,
                       f"prompt_moe_l{l}").reshape(nb, seq, D_MODEL)
        kp_l.append(kp)
        vp_l.append(vp)
        cp_l.append(cp)

        z = _sample_proj_call(l, xs, mod_s, p)
        xs, kn, vn, cn, gv = _sample_mixer_call(l, z, xs, mod_s, kc, vc, st, p, SAMPLE_BLOCK)
        xs = _moe_call(l, xs, mod_s4, 1, p, ns, f"sample_moe_l{l}")
        ks_l.append(kn)
        vs_l.append(vn)
        cs_l.append(cn)
        gs_l.append(gv)

    kv_shape = (DEPTH, -1, WINDOW, N_KV_HEADS, HEAD_DIM)
    return (xp, xs.reshape(ns, 1, D_MODEL),
            jnp.stack(kp_l).reshape(kv_shape), jnp.stack(vp_l).reshape(kv_shape),
            jnp.stack(cp_l),
            jnp.stack(ks_l).reshape(kv_shape), jnp.stack(vs_l).reshape(kv_shape),
            jnp.stack(cs_l).reshape(DEPTH, ns, CONV_K - 1, CONV_WIDTH),
            jnp.stack(gs_l).reshape(DEPTH, ns, 1, GM_WIDTH))
```

```python
import functools
import math

import jax
import jax.numpy as jnp
import numpy as np
from jax import lax
from jax.experimental import pallas as pl
from jax.experimental.pallas import tpu as pltpu

F32 = jnp.float32
BF = jnp.bfloat16

D_MODEL = 1024
DEPTH = 2
PAST_LEN = 8192
CHUNK = 128
GM_GROUPS = 4
GM_WIDTH = 512
CONV_WIDTH = 512
CONV_K = 31
N_HEADS = 8
N_KV_HEADS = 2
HEAD_DIM = 64
ATT_WIDTH = N_HEADS * HEAD_DIM
KV_WIDTH = N_KV_HEADS * HEAD_DIM
WINDOW = 128
ROPE_THETA = 500000.0
ROT_DIM = HEAD_DIM // 4
N_BRANCH = 3
N_EXPERTS = 16
EXPERTS_PER_GROUP = 4
N_GROUPS = N_EXPERTS // EXPERTS_PER_GROUP
EXPERT_FF = 256
EPS = 1e-6

OFF_GM_U = 0
OFF_GM_V = 512
OFF_CV_A = 1024
OFF_CV_G = 1536
OFF_Q = 2048
OFF_KD = 2560
OFF_VD = 2816
OFF_GATE = 3072
IN_WIDTH_X = OFF_GATE + N_BRANCH * D_MODEL
SRC_OFF_K = 2560
SRC_OFF_V = 2688
SRC_OFF_GATE = 2816

LANES = 128
SUBLANES = 8
HALO = 32
MIX_TILE = 512
MOE_TILE = 512
MOE_CAP = 144
ADA_ROWS = 136
PROMPT_MOD_ROW = 128
ADA_BLOCK = 768
SAMPLE_BLOCK = 32
VMEM_LIMIT = 56 * 1024 * 1024
NEG_BIG = -1e30


def _dot(a, b):
    return jnp.dot(a, b, preferred_element_type=F32)


def _dot_nt(a, b):
    return lax.dot_general(a, b, (((1,), (1,)), ((), ())), preferred_element_type=F32)


def _sigmoid(x):
    return 0.5 * jnp.tanh(0.5 * x) + 0.5


def _silu(x):
    return x * _sigmoid(x)


def _gelu(x):
    return 0.5 * x * (1.0 + jnp.tanh(0.7978845608028654 * (x + 0.044715 * (x * x * x))))


def _split_bf16(x):
    hi = x.astype(BF)
    lo = (x - hi.astype(F32)).astype(BF)
    return hi, lo


def _modulate(x, g, shift, scale):
    ms = jnp.mean(x * x, axis=-1, keepdims=True)
    return (x * lax.rsqrt(ms + EPS) * g) * (1.0 + scale) + shift


def _head_norm(x, blockdiag, g):
    hi, lo = _split_bf16(x * x)
    ssum = _dot(hi, blockdiag) + _dot(lo, blockdiag)
    return x * lax.rsqrt(ssum * (1.0 / HEAD_DIM) + EPS) * g


def _rope(x, c, s):
    width = x.shape[-1]
    reps = width // LANES
    cc = jnp.concatenate([c] * reps, axis=-1)
    ss = jnp.concatenate([s] * reps, axis=-1)
    lane = lax.broadcasted_iota(jnp.int32, x.shape, x.ndim - 1) % HEAD_DIM
    partner = jnp.where(lane < ROT_DIM // 2,
                        pltpu.roll(x, width - ROT_DIM // 2, x.ndim - 1),
                        pltpu.roll(x, ROT_DIM // 2, x.ndim - 1))
    return x * cc + partner * ss


def _layer_norm_silu(y, g, b):
    mu = jnp.mean(y, axis=-1, keepdims=True)
    yc = y - mu
    var = jnp.mean(yc * yc, axis=-1, keepdims=True)
    return _silu(yc * lax.rsqrt(var + EPS) * g + b)


def _low_half(shape):
    return lax.broadcasted_iota(jnp.int32, shape, len(shape) - 1) % LANES < HEAD_DIM


def _ada_kernel(c_ref, w_ref, b_ref, o_ref):
    s = _silu(c_ref[...]).astype(BF)
    o_ref[...] = _dot(s, w_ref[...].astype(BF)) + b_ref[...]


def _ada_call(c_all, w_ada, b_ada):
    nb = (6 * D_MODEL) // ADA_BLOCK
    return pl.pallas_call(
        _ada_kernel,
        out_shape=jax.ShapeDtypeStruct((DEPTH, ADA_ROWS, 6 * D_MODEL), F32),
        grid=(DEPTH, nb),
        in_specs=[
            pl.BlockSpec((ADA_ROWS, D_MODEL), lambda l, j: (0, 0)),
            pl.BlockSpec((None, D_MODEL, ADA_BLOCK), lambda l, j: (l, 0, j)),
            pl.BlockSpec((None, 1, ADA_BLOCK), lambda l, j: (l, 0, j)),
        ],
        out_specs=pl.BlockSpec((None, ADA_ROWS, ADA_BLOCK), lambda l, j: (l, 0, j)),
        compiler_params=pltpu.CompilerParams(
            dimension_semantics=("arbitrary", "arbitrary")),
        name="ada_mod",
    )(c_all, w_ada, b_ada.reshape(DEPTH, 1, 6 * D_MODEL))


def _mixer_kernel(sinks_ref, x_ref, mod_ref, n1g_ref, win_ref, gmg_ref, gmws_ref, gmb_ref,
                  cdw_ref, cb_ref, clg_ref, clb_ref, qn_ref, kn_ref, ropec_ref, ropes_ref,
                  bias_ref, bd_ref, wb_ref, wout_ref,
                  x1_ref, kwin_ref, vwin_ref, cst_ref,
                  kd_s, vd_s, abuf, oatt_s, *, layer, tm):
    t = pl.program_id(1)
    last = pl.num_programs(1) - 1
    nblk = tm // WINDOW

    @pl.when(t == 0)
    def _():
        kd_s[0:WINDOW, :] = jnp.zeros((WINDOW, 2 * LANES), BF)
        vd_s[0:WINDOW, :] = jnp.zeros((WINDOW, 2 * LANES), BF)
        abuf[0:HALO, :] = jnp.zeros((HALO, CONV_WIDTH), F32)
        abuf[HALO + tm:HALO + tm + SUBLANES, :] = jnp.zeros((SUBLANES, CONV_WIDTH), F32)

    x = x_ref[...]
    mod = mod_ref[pl.ds(pl.program_id(0), 1), :]
    sh1 = mod[:, 0:D_MODEL]
    sc1 = mod[:, D_MODEL:2 * D_MODEL]
    gt1 = mod[:, 2 * D_MODEL:3 * D_MODEL]
    hb = _modulate(x, n1g_ref[...], sh1, sc1).astype(BF)

    def proj(off, width):
        return _dot(hb, win_ref[:, off:off + width])

    u = _gelu(proj(OFF_GM_U, GM_WIDTH))
    gv = _gelu(proj(OFF_GM_V, GM_WIDTH))
    v = gv * lax.rsqrt(jnp.mean(gv * gv, axis=-1, keepdims=True) + EPS) * gmg_ref[...]
    vb = v.astype(BF)
    row = lax.broadcasted_iota(jnp.int32, (CHUNK, CHUNK), 0)
    col = lax.broadcasted_iota(jnp.int32, (CHUNK, CHUNK), 1)
    ws = [jnp.where(row >= col, gmws_ref[g], 0.0).astype(BF) for g in range(GM_GROUPS)]
    gmb = gmb_ref[...]
    chunks = []
    for c in range(nblk):
        parts = [_dot(ws[g], vb[c * CHUNK:(c + 1) * CHUNK, g * LANES:(g + 1) * LANES])
                 for g in range(GM_GROUPS)]
        chunks.append(jnp.concatenate(parts, axis=1) + gmb)
    o_gm = (u * jnp.concatenate(chunks, axis=0)).astype(BF)
    acc = _sigmoid(proj(OFF_GATE, D_MODEL)) * _dot(o_gm, wb_ref[0])

    a = proj(OFF_CV_A, CONV_WIDTH) * _sigmoid(proj(OFF_CV_G, CONV_WIDTH))
    abuf[HALO:HALO + tm, :] = a
    first_off = HALO - (CONV_K - 1)
    y = cb_ref[...]
    for b in range(SUBLANES):
        part = None
        for off in range(b, HALO + 1, SUBLANES):
            if off < first_off:
                continue
            term = cdw_ref[off - first_off:off - first_off + 1, :] * abuf[pl.ds(off - b, tm + SUBLANES), :]
            part = term if part is None else part + term
        y = y + part[b:b + tm, :]
    abuf[0:HALO, :] = abuf[tm:tm + HALO, :]
    o_cv = _layer_norm_silu(y, clg_ref[...], clb_ref[...]).astype(BF)
    acc = acc + _sigmoid(proj(OFF_GATE + D_MODEL, D_MODEL)) * _dot(o_cv, wb_ref[1])

    rc = ropec_ref[...]
    rs = ropes_ref[...]
    bd = bd_ref[...]
    q = _rope(_head_norm(proj(OFF_Q, ATT_WIDTH), bd, qn_ref[...]), rc, rs)
    qb = (q * (HEAD_DIM ** -0.5)).astype(BF)
    kd = _rope(_head_norm(proj(OFF_KD, 2 * LANES), bd[0:2 * LANES, 0:2 * LANES], kn_ref[...]), rc, rs)
    vd = proj(OFF_VD, 2 * LANES)
    kd_s[WINDOW:WINDOW + tm, :] = kd.astype(BF)
    vd_s[WINDOW:WINDOW + tm, :] = vd.astype(BF)

    lo_q = _low_half((WINDOW, LANES))
    zero_q = jnp.zeros((WINDOW, LANES), BF)
    band = bias_ref[1]
    for bi in range(nblk):
        bias = jnp.where(t == 0, bias_ref[0], band) if bi == 0 else band
        bias4 = jnp.concatenate([bias] * 4, axis=0)
        for g in range(N_KV_HEADS):
            tiles = [qb[bi * WINDOW:(bi + 1) * WINDOW, (2 * g + j) * LANES:(2 * g + j + 1) * LANES]
                     for j in range(2)]
            qs = jnp.concatenate([jnp.where(lo_q, tiles[0], zero_q), jnp.where(lo_q, zero_q, tiles[0]),
                                  jnp.where(lo_q, tiles[1], zero_q), jnp.where(lo_q, zero_q, tiles[1])],
                                 axis=0)
            keys = kd_s[bi * WINDOW:(bi + 2) * WINDOW, g * LANES:(g + 1) * LANES]
            vals = vd_s[bi * WINDOW:(bi + 2) * WINDOW, g * LANES:(g + 1) * LANES]
            s = _dot_nt(qs, keys) + bias4
            outs = []
            for hh in range(4):
                sink = sinks_ref[layer, 4 * g + hh]
                sh = s[hh * WINDOW:(hh + 1) * WINDOW, :]
                m = jnp.maximum(jnp.max(sh, axis=-1, keepdims=True), sink)
                p = jnp.exp(sh - m)
                den = jnp.sum(p, axis=-1, keepdims=True) + jnp.exp(sink - m)
                outs.append(_dot(p.astype(BF), vals) / den)
            for j in range(2):
                oatt_s[bi * WINDOW:(bi + 1) * WINDOW, (2 * g + j) * LANES:(2 * g + j + 1) * LANES] = (
                    jnp.where(lo_q, outs[2 * j], outs[2 * j + 1]).astype(BF))
    kd_s[0:WINDOW, :] = kd_s[tm:tm + WINDOW, :]
    vd_s[0:WINDOW, :] = vd_s[tm:tm + WINDOW, :]
    acc = acc + _sigmoid(proj(OFF_GATE + 2 * D_MODEL, D_MODEL)) * _dot(oatt_s[...], wb_ref[2])

    x1_ref[...] = x + gt1 * _dot(acc.astype(BF), wout_ref[...])

    @pl.when(t == last)
    def _():
        kl = kd[tm - WINDOW:tm, :]
        vl = vd[tm - WINDOW:tm, :]
        kwin_ref[...] = jnp.where(lo_q, kl[:, 0:LANES], kl[:, LANES:2 * LANES])
        vwin_ref[...] = jnp.where(lo_q, vl[:, 0:LANES], vl[:, LANES:2 * LANES])
        cst_ref[...] = a[tm - (CONV_K - 1):tm, :]


def _const_spec(shape, index_map):
    return pl.BlockSpec(shape, index_map, pipeline_mode=pl.Buffered(1))


def _mixer_call(layer, x, modp, p, tm):
    nb, seq, _ = x.shape
    nt = seq // tm
    lsel3 = lambda b, t: (layer, 0, 0)
    kernel = functools.partial(_mixer_kernel, layer=layer, tm=tm)
    return pl.pallas_call(
        kernel,
        out_shape=(
            jax.ShapeDtypeStruct((nb, seq, D_MODEL), F32),
            jax.ShapeDtypeStruct((nb, WINDOW, KV_WIDTH), F32),
            jax.ShapeDtypeStruct((nb, WINDOW, KV_WIDTH), F32),
            jax.ShapeDtypeStruct((nb, CONV_K - 1, CONV_WIDTH), F32),
        ),
        grid=(nb, nt),
        in_specs=[
            pl.BlockSpec(memory_space=pltpu.SMEM),
            pl.BlockSpec((None, tm, D_MODEL), lambda b, t: (b, t, 0)),
            pl.BlockSpec((None, SUBLANES, 3 * D_MODEL),
                         lambda b, t: (layer, PROMPT_MOD_ROW // SUBLANES, 0)),
            _const_spec((None, 1, D_MODEL), lsel3),
            _const_spec((None, D_MODEL, IN_WIDTH_X), lsel3),
            _const_spec((None, 1, GM_WIDTH), lsel3),
            _const_spec((None, GM_GROUPS, CHUNK, CHUNK), lambda b, t: (layer, 0, 0, 0)),
            _const_spec((None, CHUNK, GM_WIDTH), lsel3),
            _const_spec((None, CONV_K, CONV_WIDTH), lsel3),
            _const_spec((None, 1, CONV_WIDTH), lsel3),
            _const_spec((None, 1, CONV_WIDTH), lsel3),
            _const_spec((None, 1, CONV_WIDTH), lsel3),
            _const_spec((None, 1, ATT_WIDTH), lsel3),
            _const_spec((None, 1, 2 * LANES), lsel3),
            pl.BlockSpec((tm, LANES), lambda b, t: (t, 0)),
            pl.BlockSpec((tm, LANES), lambda b, t: (t, 0)),
            _const_spec((2, WINDOW, 2 * WINDOW), lambda b, t: (0, 0, 0)),
            _const_spec((ATT_WIDTH, ATT_WIDTH), lambda b, t: (0, 0)),
            _const_spec((None, N_BRANCH, GM_WIDTH, D_MODEL), lambda b, t: (layer, 0, 0, 0)),
            _const_spec((None, D_MODEL, D_MODEL), lsel3),
        ],
        out_specs=(
            pl.BlockSpec((None, tm, D_MODEL), lambda b, t: (b, t, 0)),
            pl.BlockSpec((None, WINDOW, KV_WIDTH), lambda b, t: (b, 0, 0)),
            pl.BlockSpec((None, WINDOW, KV_WIDTH), lambda b, t: (b, 0, 0)),
            pl.BlockSpec((None, CONV_K - 1, CONV_WIDTH), lambda b, t: (b, 0, 0)),
        ),
        scratch_shapes=[
            pltpu.VMEM((WINDOW + tm, 2 * LANES), BF),
            pltpu.VMEM((WINDOW + tm, 2 * LANES), BF),
            pltpu.VMEM((HALO + tm + SUBLANES, CONV_WIDTH), F32),
            pltpu.VMEM((tm, ATT_WIDTH), BF),
        ],
        compiler_params=pltpu.CompilerParams(
            dimension_semantics=("arbitrary", "arbitrary"),
            vmem_limit_bytes=VMEM_LIMIT),
        name=f"prompt_mixers_l{layer}",
    )(p["sinks"], x, modp, p["n1g"], p["w_in"], p["gm_g"], p["gm_ws"], p["gm_bias"],
      p["conv_dw"], p["conv_b"], p["conv_ln_g"], p["conv_ln_b"], p["qn"], p["kn"],
      p["rope_c"], p["rope_s"], p["band_bias"], p["blockdiag"], p["w_branch"], p["w_out"])


def _route_rows(logits_t, rb_ref):
    scores = _sigmoid(logits_t)
    biased = scores + rb_ref[...]
    rows = lambda a, gi: [a[EXPERTS_PER_GROUP * gi + k:EXPERTS_PER_GROUP * gi + k + 1, :]
                          for k in range(EXPERTS_PER_GROUP)]
    best = None
    idx = None
    for gi in range(N_GROUPS):
        b = rows(biased, gi)
        hi1, lo1 = jnp.maximum(b[0], b[1]), jnp.minimum(b[0], b[1])
        hi2, lo2 = jnp.maximum(b[2], b[3]), jnp.minimum(b[2], b[3])
        gs = jnp.maximum(hi1, hi2) + jnp.maximum(jnp.minimum(hi1, hi2), jnp.maximum(lo1, lo2))
        if gi == 0:
            best, idx = gs, jnp.zeros(gs.shape, jnp.int32)
        else:
            better = gs > best
            idx = jnp.where(better, gi, idx)
            best = jnp.where(better, gs, best)
    bsel = rows(biased, 0)
    ssel = rows(scores, 0)
    for gi in range(1, N_GROUPS):
        bg, sg = rows(biased, gi), rows(scores, gi)
        pick = idx == gi
        bsel = [jnp.where(pick, bg[k], bsel[k]) for k in range(EXPERTS_PER_GROUP)]
        ssel = [jnp.where(pick, sg[k], ssel[k]) for k in range(EXPERTS_PER_GROUP)]
    chosen = []
    for k in range(EXPERTS_PER_GROUP):
        rank = jnp.zeros(idx.shape, jnp.int32)
        for k2 in range(EXPERTS_PER_GROUP):
            if k2 == k:
                continue
            beats = (bsel[k2] > bsel[k]) | ((bsel[k2] == bsel[k]) & (k2 < k))
            rank = rank + beats.astype(jnp.int32)
        chosen.append(jnp.where(rank < 2, ssel[k], 0.0))
    den = chosen[0] + chosen[1] + chosen[2] + chosen[3]
    return idx, [c / den for c in chosen]


def _moe_kernel(x_ref, mod_ref, n2g_ref, rwh_ref, rwl_ref, rb_ref, upper_ref, expand_ref,
                w1_ref, w3_ref, w2_ref, o_ref, h_s, rt_s, y_s, cnt_s, *, tm, cap, tiles_per_row):
    if tiles_per_row is None:
        mod = mod_ref[...]
    else:
        mod = mod_ref[pl.ds(pl.program_id(0) // tiles_per_row, 1), :]
    h2 = _modulate(x_ref[...], n2g_ref[...], mod[:, 0:D_MODEL], mod[:, D_MODEL:2 * D_MODEL])
    hi, lo = _split_bf16(h2)
    rwh = rwh_ref[...]
    logits = _dot(hi, rwh) + _dot(lo, rwh) + _dot(hi, rwl_ref[...])
    idx, comb = _route_rows(logits.T[0:N_EXPERTS, :], rb_ref)

    onehot = [(idx == g).astype(F32) for g in range(N_GROUPS)]
    oh8 = jnp.concatenate(onehot + [jnp.zeros((8 - N_GROUPS, tm), F32)], axis=0)
    prefix = _dot(oh8.astype(BF), upper_ref[...])
    slot = onehot[0] * prefix[0:1, :]
    for g in range(1, N_GROUPS):
        slot = slot + onehot[g] * prefix[g:g + 1, :]
    rt_s[0:1, :] = idx
    rt_s[1:2, :] = slot.astype(jnp.int32)
    for g in range(N_GROUPS):
        cnt_s[g] = jnp.sum(onehot[g]).astype(jnp.int32)

    chi = [c.astype(BF).astype(F32) for c in comb]
    clo = [c - h for c, h in zip(comb, chi)]
    crow = jnp.concatenate(chi + clo + [jnp.zeros((LANES - 2 * EXPERTS_PER_GROUP, tm), F32)], axis=0)
    h_s[:, 0:D_MODEL] = hi
    h_s[:, D_MODEL:D_MODEL + LANES] = crow.T.astype(BF)
    y_s[...] = jnp.zeros(y_s.shape, F32)

    def group_body(g, carry):
        nblk = (cnt_s[g] + (cap - 1)) // cap

        def block_body(j, carry2):
            want = lax.broadcasted_iota(jnp.int32, (cap, tm), 0) + j * cap
            hit = (rt_s[1:2, :] == want) & (rt_s[0:1, :] == g)
            pmat = jnp.where(hit, 1.0, 0.0).astype(BF)
            gathered = _dot(pmat, h_s[...])
            hg = gathered[:, 0:D_MODEL].astype(BF)
            cexp = _dot(gathered[:, D_MODEL:D_MODEL + LANES].astype(BF), expand_ref[...])
            parts = []
            for e in range(EXPERTS_PER_GROUP):
                a = _dot(hg, w1_ref[EXPERTS_PER_GROUP * g + e])
                b = _dot(hg, w3_ref[EXPERTS_PER_GROUP * g + e])
                parts.append(_silu(a) * b)
            act = (jnp.concatenate(parts, axis=1) * cexp).astype(BF)
            y = _dot(act, w2_ref[g]).astype(BF)
            y_s[...] += lax.dot_general(pmat, y, (((0,), (0,)), ((), ())),
                                        preferred_element_type=F32)
            return carry2

        lax.fori_loop(0, nblk, block_body, 0)
        return carry

    lax.fori_loop(0, N_GROUPS, group_body, 0)
    o_ref[...] = x_ref[...] + mod[:, 2 * D_MODEL:3 * D_MODEL] * y_s[...]


def _moe_call(layer, x2d, mod, tiles_per_row, p, tm, name):
    n = x2d.shape[0]
    if tiles_per_row is None:
        mod_spec = pl.BlockSpec((None, n, 3 * D_MODEL), lambda i: (layer, 0, 1))
    else:
        mod_spec = pl.BlockSpec((None, SUBLANES, 3 * D_MODEL),
                                lambda i: (layer, PROMPT_MOD_ROW // SUBLANES, 1))
    lsel3 = lambda i: (layer, 0, 0)
    lsel4 = lambda i: (layer, 0, 0, 0)
    upper = np.triu(np.ones((tm, tm), np.float32), k=1)
    kernel = functools.partial(_moe_kernel, tm=tm, cap=MOE_CAP, tiles_per_row=tiles_per_row)
    return pl.pallas_call(
        kernel,
        out_shape=jax.ShapeDtypeStruct((n, D_MODEL), F32),
        grid=(n // tm,),
        in_specs=[
            pl.BlockSpec((tm, D_MODEL), lambda i: (i, 0)),
            mod_spec,
            _const_spec((None, 1, D_MODEL), lsel3),
            _const_spec((D_MODEL, LANES), lambda i: (0, 0)),
            _const_spec((D_MODEL, LANES), lambda i: (0, 0)),
            _const_spec((N_EXPERTS, 1), lambda i: (0, 0)),
            _const_spec((tm, tm), lambda i: (0, 0)),
            _const_spec((LANES, EXPERTS_PER_GROUP * EXPERT_FF), lambda i: (0, 0)),
            _const_spec((None, N_EXPERTS, D_MODEL, EXPERT_FF), lsel4),
            _const_spec((None, N_EXPERTS, D_MODEL, EXPERT_FF), lsel4),
            _const_spec((None, N_GROUPS, EXPERTS_PER_GROUP * EXPERT_FF, D_MODEL), lsel4),
        ],
        out_specs=pl.BlockSpec((tm, D_MODEL), lambda i: (i, 0)),
        scratch_shapes=[
            pltpu.VMEM((tm, D_MODEL + LANES), BF),
            pltpu.VMEM((8, tm), jnp.int32),
            pltpu.VMEM((tm, D_MODEL), F32),
            pltpu.SMEM((N_GROUPS,), jnp.int32),
        ],
        compiler_params=pltpu.CompilerParams(
            dimension_semantics=("arbitrary",),
            vmem_limit_bytes=VMEM_LIMIT),
        name=name,
    )(x2d, mod, p["n2g"], p["rw_hi"], p["rw_lo"], p["rb"], jnp.asarray(upper, dtype=BF),
      p["expand"], p["moe_w1"], p["moe_w3"], p["moe_w2"])


def _sample_proj_kernel(x_ref, mod_ref, n1g_ref, w_ref, z_ref):
    mod = mod_ref[...]
    h = _modulate(x_ref[...], n1g_ref[...], mod[:, 0:D_MODEL], mod[:, D_MODEL:2 * D_MODEL])
    z_ref[...] = _dot(h.astype(BF), w_ref[...])


def _sample_proj_call(layer, xs, mods, p):
    n = xs.shape[0]
    nblk = IN_WIDTH_X // ADA_BLOCK
    return pl.pallas_call(
        _sample_proj_kernel,
        out_shape=jax.ShapeDtypeStruct((n, IN_WIDTH_X), F32),
        grid=(nblk,),
        in_specs=[
            pl.BlockSpec((n, D_MODEL), lambda j: (0, 0)),
            pl.BlockSpec((None, n, 3 * D_MODEL), lambda j: (layer, 0, 0)),
            pl.BlockSpec((None, 1, D_MODEL), lambda j: (layer, 0, 0)),
            pl.BlockSpec((None, D_MODEL, ADA_BLOCK), lambda j: (layer, 0, j)),
        ],
        out_specs=pl.BlockSpec((n, ADA_BLOCK), lambda j: (0, j)),
        compiler_params=pltpu.CompilerParams(dimension_semantics=("arbitrary",)),
        name=f"sample_proj_l{layer}",
    )(xs, mods, p["n1g"], p["w_in"])


def _sample_mixer_kernel(z_ref, zg_ref, x_ref, mod_ref, kc_ref, vc_ref, st_ref,
                         gmg_ref, gmw_ref, gmb_ref, cdw_ref, cb_ref, clg_ref, clb_ref,
                         qn_ref, kn_ref, ropec_ref, ropes_ref, sink_ref, bd_ref, wb_ref, wout_ref,
                         x1_ref, ko_ref, vo_ref, sto_ref, gv_ref,
                         qf_s, of_s, oc_s, *, bb):
    i = pl.program_id(0)
    z = z_ref[...]

    u = _gelu(z[:, OFF_GM_U:OFF_GM_U + GM_WIDTH])
    gv = _gelu(z[:, OFF_GM_V:OFF_GM_V + GM_WIDTH])
    v = gv * lax.rsqrt(jnp.mean(gv * gv, axis=-1, keepdims=True) + EPS) * gmg_ref[...]
    gv_ref[...] = v
    o_gm = u * (v * gmw_ref[...] + gmb_ref[...])

    a = z[:, OFF_CV_A:OFF_CV_A + CONV_WIDTH] * _sigmoid(z[:, OFF_CV_G:OFF_CV_G + CONV_WIDTH])
    y = cb_ref[...] + cdw_ref[CONV_K - 1:CONV_K, :] * a
    for j in range(CONV_K - 1):
        y = y + cdw_ref[j:j + 1, :] * st_ref[:, j * CONV_WIDTH:(j + 1) * CONV_WIDTH]
    o_cv = _layer_norm_silu(y, clg_ref[...], clb_ref[...])
    sto_ref[:, 0:(CONV_K - 2) * CONV_WIDTH] = st_ref[:, CONV_WIDTH:(CONV_K - 1) * CONV_WIDTH]
    sto_ref[:, (CONV_K - 2) * CONV_WIDTH:(CONV_K - 1) * CONV_WIDTH] = a

    rc = ropec_ref[...]
    rs = ropes_ref[...]
    bd = bd_ref[...]
    q = _rope(_head_norm(z[:, OFF_Q:OFF_Q + ATT_WIDTH], bd, qn_ref[...]), rc, rs) * (HEAD_DIM ** -0.5)
    kd = _rope(_head_norm(z[:, OFF_KD:OFF_KD + 2 * LANES], bd[0:2 * LANES, 0:2 * LANES], kn_ref[...]), rc, rs)
    vd = z[:, OFF_VD:OFF_VD + 2 * LANES]
    lo = _low_half((bb, LANES))
    knew = jnp.where(lo, kd[:, 0:LANES], kd[:, LANES:2 * LANES])
    vnew = jnp.where(lo, vd[:, 0:LANES], vd[:, LANES:2 * LANES])
    for h in range(N_HEADS):
        tile = q[:, (h // 2) * LANES:(h // 2 + 1) * LANES]
        masked = jnp.where(lo, tile, 0.0) if h % 2 == 0 else jnp.where(lo, 0.0, tile)
        if h % 2 != h // (N_HEADS // N_KV_HEADS):
            masked = pltpu.roll(masked, HEAD_DIM, 1)
        qf_s[:, h, :] = masked
    qf = qf_s[...]
    kc = kc_ref[...]
    vc = vc_ref[...]
    s = jnp.einsum("bhd,bpd->bhp", qf.astype(BF), kc.astype(BF), preferred_element_type=F32)
    s_new = jnp.sum(qf * knew[:, None, :], axis=-1, keepdims=True)
    sink = sink_ref[...][None, :, 0:1]
    m = jnp.maximum(jnp.maximum(jnp.max(s, axis=-1, keepdims=True), s_new), sink)
    pr = jnp.exp(s - m)
    p_new = jnp.exp(s_new - m)
    den = jnp.sum(pr, axis=-1, keepdims=True) + p_new + jnp.exp(sink - m)
    o = jnp.einsum("bhp,bpd->bhd", pr.astype(BF), vc.astype(BF), preferred_element_type=F32)
    of_s[...] = (o + p_new * vnew[:, None, :]) / den
    att_tiles = []
    for j in range(N_HEADS // 2):
        g = j // 2
        first = of_s[:, 2 * j, :]
        second = of_s[:, 2 * j + 1, :]
        if g == 1:
            first = pltpu.roll(first, HEAD_DIM, 1)
        else:
            second = pltpu.roll(second, HEAD_DIM, 1)
        att_tiles.append(jnp.where(lo, first, second))
    o_att = jnp.concatenate(att_tiles, axis=1)
    ko_ref[:, pl.ds(0, WINDOW - 1), :] = kc_ref[:, pl.ds(1, WINDOW - 1), :]
    ko_ref[:, WINDOW - 1, :] = knew
    vo_ref[:, pl.ds(0, WINDOW - 1), :] = vc_ref[:, pl.ds(1, WINDOW - 1), :]
    vo_ref[:, WINDOW - 1, :] = vnew

    r0 = pl.multiple_of(i * bb, bb)
    oc_s[pl.ds(r0, bb), 0:GM_WIDTH] = o_gm.astype(BF)
    oc_s[pl.ds(r0, bb), GM_WIDTH:GM_WIDTH + CONV_WIDTH] = o_cv.astype(BF)
    oc_s[pl.ds(r0, bb), GM_WIDTH + CONV_WIDTH:GM_WIDTH + CONV_WIDTH + ATT_WIDTH] = o_att.astype(BF)

    @pl.when(i == pl.num_programs(0) - 1)
    def _():
        acc = _sigmoid(zg_ref[:, 0:D_MODEL]) * _dot(oc_s[:, 0:GM_WIDTH], wb_ref[0])
        acc = acc + _sigmoid(zg_ref[:, D_MODEL:2 * D_MODEL]) * _dot(
            oc_s[:, GM_WIDTH:GM_WIDTH + CONV_WIDTH], wb_ref[1])
        acc = acc + _sigmoid(zg_ref[:, 2 * D_MODEL:3 * D_MODEL]) * _dot(
            oc_s[:, GM_WIDTH + CONV_WIDTH:GM_WIDTH + CONV_WIDTH + ATT_WIDTH], wb_ref[2])
        gt1 = mod_ref[...][:, 2 * D_MODEL:3 * D_MODEL]
        x1_ref[...] = x_ref[...] + gt1 * _dot(acc.astype(BF), wout_ref[...])


def _sample_mixer_call(layer, z, xs, mods, kc, vc, st, p, bb):
    n = xs.shape[0]
    state_w = (CONV_K - 1) * CONV_WIDTH
    lsel3 = lambda i: (layer, 0, 0)
    kernel = functools.partial(_sample_mixer_kernel, bb=bb)
    return pl.pallas_call(
        kernel,
        out_shape=(
            jax.ShapeDtypeStruct((n, D_MODEL), F32),
            jax.ShapeDtypeStruct((n, WINDOW, KV_WIDTH), F32),
            jax.ShapeDtypeStruct((n, WINDOW, KV_WIDTH), F32),
            jax.ShapeDtypeStruct((n, state_w), F32),
            jax.ShapeDtypeStruct((n, GM_WIDTH), F32),
        ),
        grid=(n // bb,),
        in_specs=[
            pl.BlockSpec((bb, IN_WIDTH_X), lambda i: (i, 0)),
            pl.BlockSpec((n, N_BRANCH * D_MODEL), lambda i: (0, 1)),
            pl.BlockSpec((n, D_MODEL), lambda i: (0, 0)),
            pl.BlockSpec((None, n, 3 * D_MODEL), lsel3),
            pl.BlockSpec((None, bb, WINDOW, KV_WIDTH), lambda i: (layer, i, 0, 0)),
            pl.BlockSpec((None, bb, WINDOW, KV_WIDTH), lambda i: (layer, i, 0, 0)),
            pl.BlockSpec((None, bb, state_w), lambda i: (layer, i, 0)),
            pl.BlockSpec((None, 1, GM_WIDTH), lsel3),
            pl.BlockSpec((None, 1, GM_WIDTH), lsel3),
            pl.BlockSpec((None, 1, GM_WIDTH), lsel3),
            pl.BlockSpec((None, CONV_K, CONV_WIDTH), lsel3),
            pl.BlockSpec((None, 1, CONV_WIDTH), lsel3),
            pl.BlockSpec((None, 1, CONV_WIDTH), lsel3),
            pl.BlockSpec((None, 1, CONV_WIDTH), lsel3),
            pl.BlockSpec((None, 1, ATT_WIDTH), lsel3),
            pl.BlockSpec((None, 1, 2 * LANES), lsel3),
            pl.BlockSpec((1, LANES), lambda i: (0, 0)),
            pl.BlockSpec((1, LANES), lambda i: (0, 0)),
            pl.BlockSpec((None, N_HEADS, LANES), lsel3),
            pl.BlockSpec((ATT_WIDTH, ATT_WIDTH), lambda i: (0, 0)),
            pl.BlockSpec((None, N_BRANCH, GM_WIDTH, D_MODEL), lambda i: (layer, 0, 0, 0)),
            pl.BlockSpec((None, D_MODEL, D_MODEL), lsel3),
        ],
        out_specs=(
            pl.BlockSpec((n, D_MODEL), lambda i: (0, 0)),
            pl.BlockSpec((bb, WINDOW, KV_WIDTH), lambda i: (i, 0, 0)),
            pl.BlockSpec((bb, WINDOW, KV_WIDTH), lambda i: (i, 0, 0)),
            pl.BlockSpec((bb, state_w), lambda i: (i, 0)),
            pl.BlockSpec((bb, GM_WIDTH), lambda i: (i, 0)),
        ),
        scratch_shapes=[
            pltpu.VMEM((bb, N_HEADS, LANES), F32),
            pltpu.VMEM((bb, N_HEADS, LANES), F32),
            pltpu.VMEM((n, GM_WIDTH + CONV_WIDTH + ATT_WIDTH), BF),
        ],
        compiler_params=pltpu.CompilerParams(
            dimension_semantics=("arbitrary",), vmem_limit_bytes=VMEM_LIMIT),
        name=f"sample_mixers_l{layer}",
    )(z, z, xs, mods, kc, vc, st, p["gm_g"], p["gm_w0"], p["gm_b0"], p["conv_dw"], p["conv_b"],
      p["conv_ln_g"], p["conv_ln_b"], p["qn"], p["kn"], p["rope_c1"], p["rope_s1"],
      p["sink_lanes"], p["blockdiag"], p["w_branch"], p["w_out"])


def _rope_tables(pos):
    half = ROT_DIM // 2
    freqs = jnp.exp(-math.log(ROPE_THETA) * jnp.arange(half, dtype=F32) * (2.0 / ROT_DIM))
    ang = pos.astype(F32)[:, None] * freqs[None, :]
    cos, sin = jnp.cos(ang), jnp.sin(ang)
    n = pos.shape[0]
    ones = jnp.ones((n, HEAD_DIM - ROT_DIM), F32)
    zeros = jnp.zeros((n, HEAD_DIM - ROT_DIM), F32)
    c64 = jnp.concatenate([cos, cos, ones], axis=1)
    s64 = jnp.concatenate([-sin, sin, zeros], axis=1)
    return jnp.concatenate([c64, c64], axis=1), jnp.concatenate([s64, s64], axis=1)


def _band_bias():
    i = np.arange(WINDOW)[:, None]
    j = np.arange(2 * WINDOW)[None, :]
    band = (j >= i) & (j <= i + WINDOW)
    first = band & (j >= WINDOW)
    out = np.where(np.stack([first, band]), 0.0, NEG_BIG).astype(np.float32)
    return jnp.asarray(out)


def _blockdiag():
    idx = np.arange(ATT_WIDTH) // HEAD_DIM
    return jnp.asarray((idx[:, None] == idx[None, :]).astype(np.float32), dtype=BF)


def _expand_matrix():
    k = np.arange(LANES)[:, None]
    e = (np.arange(EXPERTS_PER_GROUP * EXPERT_FF) // EXPERT_FF)[None, :]
    return jnp.asarray(((k == e) | (k == e + EXPERTS_PER_GROUP)).astype(np.float32), dtype=BF)


def _dup_heads(w):
    h0, h1 = w[..., :HEAD_DIM], w[..., HEAD_DIM:]
    return jnp.concatenate([h0, h0, h1, h1], axis=-1)


def _prepare(norm1_g, norm2_g, w_in, gm_norm_g, gm_ws, gm_b, conv_dw, conv_b, conv_ln_g,
             conv_ln_b, q_norm_g, k_norm_g, attn_sinks, w_branch, w_out, router_w, router_b,
             moe_w1, moe_w3, moe_w2, seq):
    w_in_x = jnp.concatenate([
        w_in[:, :, :SRC_OFF_K],
        _dup_heads(w_in[:, :, SRC_OFF_K:SRC_OFF_V]),
        _dup_heads(w_in[:, :, SRC_OFF_V:SRC_OFF_GATE]),
        w_in[:, :, SRC_OFF_GATE:],
    ], axis=-1).astype(BF)
    rw = jnp.pad(router_w.astype(F32), ((0, 0), (0, LANES - N_EXPERTS)))
    rw_hi = rw.astype(BF)
    rw_lo = (rw - rw_hi.astype(F32)).astype(BF)
    rope_c, rope_s = _rope_tables(jnp.arange(seq, dtype=jnp.int32))
    rope_c1, rope_s1 = _rope_tables(PAST_LEN + jnp.arange(1, dtype=jnp.int32))
    row3 = lambda a: a.reshape(DEPTH, 1, a.shape[-1])
    return {
        "n1g": row3(norm1_g), "n2g": row3(norm2_g), "w_in": w_in_x,
        "gm_g": row3(gm_norm_g), "gm_ws": gm_ws,
        "gm_bias": jnp.repeat(jnp.swapaxes(gm_b, 1, 2), LANES, axis=2),
        "gm_w0": jnp.repeat(gm_ws[:, :, 0, 0], LANES, axis=1).reshape(DEPTH, 1, GM_WIDTH),
        "gm_b0": jnp.repeat(gm_b[:, :, 0], LANES, axis=1).reshape(DEPTH, 1, GM_WIDTH),
        "conv_dw": conv_dw, "conv_b": row3(conv_b), "conv_ln_g": row3(conv_ln_g),
        "conv_ln_b": row3(conv_ln_b),
        "qn": row3(jnp.tile(q_norm_g, (1, N_HEADS))),
        "kn": row3(jnp.tile(k_norm_g, (1, 2 * N_KV_HEADS))),
        "sinks": attn_sinks,
        "sink_lanes": jnp.broadcast_to(attn_sinks[:, :, None], (DEPTH, N_HEADS, LANES)),
        "rope_c": rope_c, "rope_s": rope_s, "rope_c1": rope_c1, "rope_s1": rope_s1,
        "band_bias": _band_bias(), "blockdiag": _blockdiag(),
        "w_branch": w_branch.astype(BF), "w_out": w_out.astype(BF),
        "rw_hi": rw_hi, "rw_lo": rw_lo,
        "rb": router_b.astype(F32).reshape(N_EXPERTS, 1),
        "expand": _expand_matrix(),
        "moe_w1": moe_w1.astype(BF), "moe_w3": moe_w3.astype(BF),
        "moe_w2": moe_w2.astype(BF).reshape(DEPTH, N_GROUPS, EXPERTS_PER_GROUP * EXPERT_FF, D_MODEL),
    }


def kernel(x_prompt, x_sample, cache_win_k, cache_win_v, state_conv, c_prompt, c_sample, norm1_g, norm2_g, w_ada, b_ada, w_in, gm_norm_g, gm_ws, gm_b, conv_dw, conv_b, conv_ln_g, conv_ln_b, q_norm_g, k_norm_g, attn_sinks, w_branch, w_out, router_w, router_b, moe_w1, moe_w3, moe_w2):
    nb, seq, _ = x_prompt.shape
    ns = x_sample.shape[0]
    p = _prepare(norm1_g, norm2_g, w_in, gm_norm_g, gm_ws, gm_b, conv_dw, conv_b, conv_ln_g,
                 conv_ln_b, q_norm_g, k_norm_g, attn_sinks, w_branch, w_out, router_w, router_b,
                 moe_w1, moe_w3, moe_w2, seq)
    assert ns == PROMPT_MOD_ROW and nb <= ADA_ROWS - PROMPT_MOD_ROW
    c_all = jnp.concatenate(
        [c_sample, c_prompt, jnp.zeros((ADA_ROWS - nb - ns, D_MODEL), F32)], axis=0)
    mod = _ada_call(c_all, w_ada, b_ada)

    kc = cache_win_k.reshape(DEPTH, ns, WINDOW, KV_WIDTH)
    vc = cache_win_v.reshape(DEPTH, ns, WINDOW, KV_WIDTH)
    st = state_conv.reshape(DEPTH, ns, (CONV_K - 1) * CONV_WIDTH)

    tm = min(MIX_TILE, seq)
    tmoe = min(MOE_TILE, seq)
    xp = x_prompt
    xs = x_sample.reshape(ns, D_MODEL)
    kp_l, vp_l, cp_l, ks_l, vs_l, cs_l, gs_l = [], [], [], [], [], [], []
    for l in range(DEPTH):
        xp, kp, vp, cp = _mixer_call(l, xp, mod, p, tm)
        xp = _moe_call(l, xp.reshape(nb * seq, D_MODEL), mod, seq // tmoe, p, tmoe,
                       f"prompt_moe_l{l}").reshape(nb, seq, D_MODEL)
        kp_l.append(kp)
        vp_l.append(vp)
        cp_l.append(cp)

        z = _sample_proj_call(l, xs, mod, p)
        xs, kn, vn, cn, gv = _sample_mixer_call(l, z, xs, mod, kc, vc, st, p, SAMPLE_BLOCK)
        xs = _moe_call(l, xs, mod, None, p, ns, f"sample_moe_l{l}")
        ks_l.append(kn)
        vs_l.append(vn)
        cs_l.append(cn)
        gs_l.append(gv)

    kv_shape = (DEPTH, -1, WINDOW, N_KV_HEADS, HEAD_DIM)
    return (xp, xs.reshape(ns, 1, D_MODEL),
            jnp.stack(kp_l).reshape(kv_shape), jnp.stack(vp_l).reshape(kv_shape),
            jnp.stack(cp_l),
            jnp.stack(ks_l).reshape(kv_shape), jnp.stack(vs_l).reshape(kv_shape),
            jnp.stack(cs_l).reshape(DEPTH, ns, CONV_K - 1, CONV_WIDTH),
            jnp.stack(gs_l).reshape(DEPTH, ns, 1, GM_WIDTH))
```

```python
import functools
import math

import jax
import jax.numpy as jnp
import numpy as np
from jax import lax
from jax.experimental import pallas as pl
from jax.experimental.pallas import tpu as pltpu

F32 = jnp.float32
BF = jnp.bfloat16

D_MODEL = 1024
DEPTH = 2
PAST_LEN = 8192
CHUNK = 128
GM_GROUPS = 4
GM_WIDTH = 512
CONV_WIDTH = 512
CONV_K = 31
N_HEADS = 8
N_KV_HEADS = 2
Q_REP = N_HEADS // N_KV_HEADS
HEAD_DIM = 64
ATT_WIDTH = N_HEADS * HEAD_DIM
KV_WIDTH = N_KV_HEADS * HEAD_DIM
WINDOW = 128
ROPE_THETA = 500000.0
ROT_DIM = HEAD_DIM // 4
N_BRANCH = 3
N_EXPERTS = 16
EXPERTS_PER_GROUP = 4
N_GROUPS = N_EXPERTS // EXPERTS_PER_GROUP
EXPERT_FF = 256
EPS = 1e-6

OFF_GM_U = 0
OFF_GM_V = 512
OFF_CV_A = 1024
OFF_CV_G = 1536
OFF_Q = 2048
OFF_K = 2560
OFF_V = 2688
OFF_GATE = 2816
IN_WIDTH = OFF_GATE + N_BRANCH * D_MODEL

LANES = 128
SUBLANES = 8
HALO = 32
MIX_TILE = 512
MOE_TILE = 512
MOE_CAP = 144
ADA_ROWS = 136
PROMPT_MOD_ROW = 128
ADA_BLOCK = 768
PROJ_BLOCK = IN_WIDTH // 2
SAMPLE_BLOCK = 32
VMEM_LIMIT = 56 * 1024 * 1024
NEG_BIG = -1e30


def _dot(a, b):
    return jnp.dot(a, b, preferred_element_type=F32)


def _dot_nt(a, b):
    return lax.dot_general(a, b, (((1,), (1,)), ((), ())), preferred_element_type=F32)


def _sigmoid(x):
    return 0.5 * jnp.tanh(0.5 * x) + 0.5


def _silu(x):
    return x * _sigmoid(x)


def _gelu(x):
    return 0.5 * x * (1.0 + jnp.tanh(0.7978845608028654 * (x + 0.044715 * (x * x * x))))


def _split_bf16(x):
    hi = x.astype(BF)
    lo = (x - hi.astype(F32)).astype(BF)
    return hi, lo


def _modulate(x, g, shift, scale):
    ms = jnp.mean(x * x, axis=-1, keepdims=True)
    return (x * lax.rsqrt(ms + EPS) * g) * (1.0 + scale) + shift


def _head_norm(x, blockdiag, g):
    hi, lo = _split_bf16(x * x)
    ssum = _dot(hi, blockdiag) + _dot(lo, blockdiag)
    return x * lax.rsqrt(ssum * (1.0 / HEAD_DIM) + EPS) * g


def _rope(x, c, s):
    width = x.shape[-1]
    reps = width // LANES
    cc = jnp.concatenate([c] * reps, axis=-1)
    ss = jnp.concatenate([s] * reps, axis=-1)
    lane = lax.broadcasted_iota(jnp.int32, x.shape, x.ndim - 1) % HEAD_DIM
    partner = jnp.where(lane < ROT_DIM // 2,
                        pltpu.roll(x, width - ROT_DIM // 2, x.ndim - 1),
                        pltpu.roll(x, ROT_DIM // 2, x.ndim - 1))
    return x * cc + partner * ss


def _layer_norm_silu(y, g, b):
    mu = jnp.mean(y, axis=-1, keepdims=True)
    yc = y - mu
    var = jnp.mean(yc * yc, axis=-1, keepdims=True)
    return _silu(yc * lax.rsqrt(var + EPS) * g + b)


def _low_half(shape):
    return lax.broadcasted_iota(jnp.int32, shape, len(shape) - 1) % LANES < HEAD_DIM


def _const_spec(shape, index_map):
    return pl.BlockSpec(shape, index_map, pipeline_mode=pl.Buffered(1))


def _ada_kernel(c_ref, w_ref, b_ref, o_ref):
    s = _silu(c_ref[...]).astype(BF)
    o_ref[...] = _dot(s, w_ref[...].astype(BF)) + b_ref[...]


def _ada_call(c_all, w_ada, b_ada):
    nb = (6 * D_MODEL) // ADA_BLOCK
    return pl.pallas_call(
        _ada_kernel,
        out_shape=jax.ShapeDtypeStruct((DEPTH, ADA_ROWS, 6 * D_MODEL), F32),
        grid=(DEPTH, nb),
        in_specs=[
            pl.BlockSpec((ADA_ROWS, D_MODEL), lambda l, j: (0, 0)),
            pl.BlockSpec((None, D_MODEL, ADA_BLOCK), lambda l, j: (l, 0, j)),
            pl.BlockSpec((None, 1, ADA_BLOCK), lambda l, j: (l, 0, j)),
        ],
        out_specs=pl.BlockSpec((None, ADA_ROWS, ADA_BLOCK), lambda l, j: (l, 0, j)),
        compiler_params=pltpu.CompilerParams(
            dimension_semantics=("arbitrary", "arbitrary")),
        name="ada_mod",
    )(c_all, w_ada, b_ada.reshape(DEPTH, 1, 6 * D_MODEL))


def _mixer_kernel(sinks_ref, x_ref, mod_ref, n1g_ref, win_ref, wkv_ref, gmg_ref, gmws_ref, gmb_ref,
                  cdw_ref, cb_ref, clg_ref, clb_ref, qn_ref, kn_ref, ropeb_ref, ropeo_ref,
                  sign_ref, bias_ref, bd_ref, wb_ref, wout_ref,
                  x1_ref, kwin_ref, vwin_ref, cst_ref,
                  kd_s, vd_s, abuf, oatt_s, *, layer, tm):
    t = pl.program_id(1)
    last = pl.num_programs(1) - 1
    nblk = tm // WINDOW

    @pl.when(t == 0)
    def _():
        kd_s[0:WINDOW, :] = jnp.zeros((WINDOW, 2 * LANES), BF)
        vd_s[0:WINDOW, :] = jnp.zeros((WINDOW, 2 * LANES), BF)
        abuf[0:HALO, :] = jnp.zeros((HALO, CONV_WIDTH), F32)
        abuf[HALO + tm:HALO + tm + SUBLANES, :] = jnp.zeros((SUBLANES, CONV_WIDTH), F32)

    x = x_ref[...]
    mod = mod_ref[pl.ds(pl.program_id(0), 1), :]
    sh1 = mod[:, 0:D_MODEL]
    sc1 = mod[:, D_MODEL:2 * D_MODEL]
    gt1 = mod[:, 2 * D_MODEL:3 * D_MODEL]
    hb = _modulate(x, n1g_ref[...], sh1, sc1).astype(BF)

    def proj(off, width):
        return _dot(hb, win_ref[:, off:off + width])

    u = _gelu(proj(OFF_GM_U, GM_WIDTH))
    gv = _gelu(proj(OFF_GM_V, GM_WIDTH))
    v = gv * lax.rsqrt(jnp.mean(gv * gv, axis=-1, keepdims=True) + EPS) * gmg_ref[...]
    vb = v.astype(BF)
    row = lax.broadcasted_iota(jnp.int32, (CHUNK, CHUNK), 0)
    col = lax.broadcasted_iota(jnp.int32, (CHUNK, CHUNK), 1)
    ws = [jnp.where(row >= col, gmws_ref[g], 0.0).astype(BF) for g in range(GM_GROUPS)]
    gmb = gmb_ref[...]
    chunks = []
    for c in range(nblk):
        parts = [_dot(ws[g], vb[c * CHUNK:(c + 1) * CHUNK, g * LANES:(g + 1) * LANES])
                 for g in range(GM_GROUPS)]
        chunks.append(jnp.concatenate(parts, axis=1) + gmb)
    o_gm = (u * jnp.concatenate(chunks, axis=0)).astype(BF)
    acc = _sigmoid(proj(OFF_GATE, D_MODEL)) * _dot(o_gm, wb_ref[0])

    a = proj(OFF_CV_A, CONV_WIDTH) * _sigmoid(proj(OFF_CV_G, CONV_WIDTH))
    abuf[HALO:HALO + tm, :] = a
    first_off = HALO - (CONV_K - 1)
    y = cb_ref[...]
    for b in range(SUBLANES):
        part = None
        for off in range(b, HALO + 1, SUBLANES):
            if off < first_off:
                continue
            term = cdw_ref[off - first_off:off - first_off + 1, :] * abuf[pl.ds(off - b, tm + SUBLANES), :]
            part = term if part is None else part + term
        y = y + part[b:b + tm, :]
    abuf[0:HALO, :] = abuf[tm:tm + HALO, :]
    o_cv = _layer_norm_silu(y, clg_ref[...], clb_ref[...]).astype(BF)
    acc = acc + _sigmoid(proj(OFF_GATE + D_MODEL, D_MODEL)) * _dot(o_cv, wb_ref[1])

    rbase = ropeb_ref[...]
    roff = ropeo_ref[...]
    cb_, sb_ = rbase[:, 0:LANES], rbase[:, LANES:2 * LANES]
    co_, so_ = roff[:, 0:LANES], roff[:, LANES:2 * LANES]
    rc = cb_ * co_ - sb_ * so_
    rs = (sb_ * co_ + cb_ * so_) * sign_ref[...]
    bd = bd_ref[...]
    q = _rope(_head_norm(proj(OFF_Q, ATT_WIDTH), bd, qn_ref[...]), rc, rs)
    qb = (q * (HEAD_DIM ** -0.5)).astype(BF)
    kd = _rope(_head_norm(_dot(hb, wkv_ref[:, 0:2 * LANES]), bd[0:2 * LANES, 0:2 * LANES], kn_ref[...]),
               rc, rs)
    vd = _dot(hb, wkv_ref[:, 2 * LANES:4 * LANES])
    kd_s[WINDOW:WINDOW + tm, :] = kd.astype(BF)
    vd_s[WINDOW:WINDOW + tm, :] = vd.astype(BF)

    lo_q = _low_half((WINDOW, LANES))
    zero_q = jnp.zeros((WINDOW, LANES), BF)
    band = bias_ref[1]
    for bi in range(nblk):
        bias = jnp.where(t == 0, bias_ref[0], band) if bi == 0 else band
        bias4 = jnp.concatenate([bias] * Q_REP, axis=0)
        for g in range(N_KV_HEADS):
            tiles = [qb[bi * WINDOW:(bi + 1) * WINDOW, (2 * g + j) * LANES:(2 * g + j + 1) * LANES]
                     for j in range(2)]
            qs = jnp.concatenate([jnp.where(lo_q, tiles[0], zero_q), jnp.where(lo_q, zero_q, tiles[0]),
                                  jnp.where(lo_q, tiles[1], zero_q), jnp.where(lo_q, zero_q, tiles[1])],
                                 axis=0)
            keys = kd_s[bi * WINDOW:(bi + 2) * WINDOW, g * LANES:(g + 1) * LANES]
            vals = vd_s[bi * WINDOW:(bi + 2) * WINDOW, g * LANES:(g + 1) * LANES]
            s = _dot_nt(qs, keys) + bias4
            outs = []
            for hh in range(Q_REP):
                sink = sinks_ref[layer, Q_REP * g + hh]
                sh = s[hh * WINDOW:(hh + 1) * WINDOW, :]
                m = jnp.maximum(jnp.max(sh, axis=-1, keepdims=True), sink)
                p = jnp.exp(sh - m)
                den = jnp.sum(p, axis=-1, keepdims=True) + jnp.exp(sink - m)
                outs.append(_dot(p.astype(BF), vals) / den)
            for j in range(2):
                oatt_s[bi * WINDOW:(bi + 1) * WINDOW, (2 * g + j) * LANES:(2 * g + j + 1) * LANES] = (
                    jnp.where(lo_q, outs[2 * j], outs[2 * j + 1]).astype(BF))
    kd_s[0:WINDOW, :] = kd_s[tm:tm + WINDOW, :]
    vd_s[0:WINDOW, :] = vd_s[tm:tm + WINDOW, :]
    acc = acc + _sigmoid(proj(OFF_GATE + 2 * D_MODEL, D_MODEL)) * _dot(oatt_s[...], wb_ref[2])

    x1_ref[...] = x + gt1 * _dot(acc.astype(BF), wout_ref[...])

    @pl.when(t == last)
    def _():
        kl = kd[tm - WINDOW:tm, :]
        vl = vd[tm - WINDOW:tm, :]
        kwin_ref[...] = jnp.where(lo_q, kl[:, 0:LANES], kl[:, LANES:2 * LANES])
        vwin_ref[...] = jnp.where(lo_q, vl[:, 0:LANES], vl[:, LANES:2 * LANES])
        cst_ref[...] = a[tm - (CONV_K - 1):tm, :]


def _mixer_call(layer, x, mod, p, tm):
    nb, seq, _ = x.shape
    nt = seq // tm
    lsel3 = lambda b, t: (layer, 0, 0)
    kernel = functools.partial(_mixer_kernel, layer=layer, tm=tm)
    return pl.pallas_call(
        kernel,
        out_shape=(
            jax.ShapeDtypeStruct((nb, seq, D_MODEL), F32),
            jax.ShapeDtypeStruct((nb, WINDOW, KV_WIDTH), F32),
            jax.ShapeDtypeStruct((nb, WINDOW, KV_WIDTH), F32),
            jax.ShapeDtypeStruct((nb, CONV_K - 1, CONV_WIDTH), F32),
        ),
        grid=(nb, nt),
        in_specs=[
            pl.BlockSpec(memory_space=pltpu.SMEM),
            pl.BlockSpec((None, tm, D_MODEL), lambda b, t: (b, t, 0)),
            pl.BlockSpec((None, SUBLANES, 3 * D_MODEL),
                         lambda b, t: (layer, PROMPT_MOD_ROW // SUBLANES, 0)),
            _const_spec((None, 1, D_MODEL), lsel3),
            _const_spec((None, D_MODEL, IN_WIDTH), lsel3),
            _const_spec((None, D_MODEL, 4 * LANES), lsel3),
            _const_spec((None, 1, GM_WIDTH), lsel3),
            _const_spec((None, GM_GROUPS, CHUNK, CHUNK), lambda b, t: (layer, 0, 0, 0)),
            _const_spec((None, CHUNK, GM_WIDTH), lsel3),
            _const_spec((None, CONV_K, CONV_WIDTH), lsel3),
            _const_spec((None, 1, CONV_WIDTH), lsel3),
            _const_spec((None, 1, CONV_WIDTH), lsel3),
            _const_spec((None, 1, CONV_WIDTH), lsel3),
            _const_spec((None, 1, ATT_WIDTH), lsel3),
            _const_spec((None, 1, 2 * LANES), lsel3),
            pl.BlockSpec((None, 1, 2 * LANES), lambda b, t: (t, 0, 0)),
            _const_spec((tm, 2 * LANES), lambda b, t: (0, 0)),
            _const_spec((1, LANES), lambda b, t: (0, 0)),
            _const_spec((2, WINDOW, 2 * WINDOW), lambda b, t: (0, 0, 0)),
            _const_spec((ATT_WIDTH, ATT_WIDTH), lambda b, t: (0, 0)),
            _const_spec((None, N_BRANCH, GM_WIDTH, D_MODEL), lambda b, t: (layer, 0, 0, 0)),
            _const_spec((None, D_MODEL, D_MODEL), lsel3),
        ],
        out_specs=(
            pl.BlockSpec((None, tm, D_MODEL), lambda b, t: (b, t, 0)),
            pl.BlockSpec((None, WINDOW, KV_WIDTH), lambda b, t: (b, 0, 0)),
            pl.BlockSpec((None, WINDOW, KV_WIDTH), lambda b, t: (b, 0, 0)),
            pl.BlockSpec((None, CONV_K - 1, CONV_WIDTH), lambda b, t: (b, 0, 0)),
        ),
        scratch_shapes=[
            pltpu.VMEM((WINDOW + tm, 2 * LANES), BF),
            pltpu.VMEM((WINDOW + tm, 2 * LANES), BF),
            pltpu.VMEM((HALO + tm + SUBLANES, CONV_WIDTH), F32),
            pltpu.VMEM((tm, ATT_WIDTH), BF),
        ],
        compiler_params=pltpu.CompilerParams(
            dimension_semantics=("arbitrary", "arbitrary"),
            vmem_limit_bytes=VMEM_LIMIT),
        name=f"prompt_mixers_l{layer}",
    )(p["sinks"], x, mod, p["n1g"], p["w_in"], p["w_kv2"], p["gm_g"], p["gm_ws"], p["gm_bias"],
      p["conv_dw"], p["conv_b"], p["conv_ln_g"], p["conv_ln_b"], p["qn"], p["kn"],
      p["rope_base"], p["rope_off"], p["rope_sign"], p["band_bias"], p["blockdiag"],
      p["w_branch"], p["w_out"])


def _route_rows(logits_t, rb_ref):
    scores = _sigmoid(logits_t)
    biased = scores + rb_ref[...]
    rows = lambda a, gi: [a[EXPERTS_PER_GROUP * gi + k:EXPERTS_PER_GROUP * gi + k + 1, :]
                          for k in range(EXPERTS_PER_GROUP)]
    best = None
    idx = None
    for gi in range(N_GROUPS):
        b = rows(biased, gi)
        hi1, lo1 = jnp.maximum(b[0], b[1]), jnp.minimum(b[0], b[1])
        hi2, lo2 = jnp.maximum(b[2], b[3]), jnp.minimum(b[2], b[3])
        gs = jnp.maximum(hi1, hi2) + jnp.maximum(jnp.minimum(hi1, hi2), jnp.maximum(lo1, lo2))
        if gi == 0:
            best, idx = gs, jnp.zeros(gs.shape, jnp.int32)
        else:
            better = gs > best
            idx = jnp.where(better, gi, idx)
            best = jnp.where(better, gs, best)
    bsel = rows(biased, 0)
    ssel = rows(scores, 0)
    for gi in range(1, N_GROUPS):
        bg, sg = rows(biased, gi), rows(scores, gi)
        pick = idx == gi
        bsel = [jnp.where(pick, bg[k], bsel[k]) for k in range(EXPERTS_PER_GROUP)]
        ssel = [jnp.where(pick, sg[k], ssel[k]) for k in range(EXPERTS_PER_GROUP)]
    chosen = []
    for k in range(EXPERTS_PER_GROUP):
        rank = jnp.zeros(idx.shape, jnp.int32)
        for k2 in range(EXPERTS_PER_GROUP):
            if k2 == k:
                continue
            beats = (bsel[k2] > bsel[k]) | ((bsel[k2] == bsel[k]) & (k2 < k))
            rank = rank + beats.astype(jnp.int32)
        chosen.append(jnp.where(rank < 2, ssel[k], 0.0))
    den = chosen[0] + chosen[1] + chosen[2] + chosen[3]
    return idx, [c / den for c in chosen]


def _moe_kernel(x_ref, mod_ref, n2g_ref, rw2_ref, rb_ref, upper_ref, expand_ref,
                w1_ref, w3_ref, w2_ref, o_ref, h_s, rt_s, y_s, cnt_s, *, tm, cap, tiles_per_row):
    if tiles_per_row is None:
        mod = mod_ref[...]
    else:
        mod = mod_ref[pl.ds(pl.program_id(0) // tiles_per_row, 1), :]
    h2 = _modulate(x_ref[...], n2g_ref[...], mod[:, 0:D_MODEL], mod[:, D_MODEL:2 * D_MODEL])
    hi, lo = _split_bf16(h2)
    rw2 = rw2_ref[...]
    lt = (_dot(hi, rw2) + _dot(lo, rw2)).T
    idx, comb = _route_rows(lt[0:N_EXPERTS, :] + lt[N_EXPERTS:2 * N_EXPERTS, :], rb_ref)

    onehot = [(idx == g).astype(F32) for g in range(N_GROUPS)]
    oh8 = jnp.concatenate(onehot + [jnp.zeros((SUBLANES - N_GROUPS, tm), F32)], axis=0)
    prefix = _dot(oh8.astype(BF), upper_ref[...])
    slot = onehot[0] * prefix[0:1, :]
    for g in range(1, N_GROUPS):
        slot = slot + onehot[g] * prefix[g:g + 1, :]
    rt_s[0:1, :] = idx
    rt_s[1:2, :] = slot.astype(jnp.int32)
    for g in range(N_GROUPS):
        cnt_s[g] = jnp.sum(onehot[g]).astype(jnp.int32)

    chi = [c.astype(BF).astype(F32) for c in comb]
    clo = [c - h for c, h in zip(comb, chi)]
    crow = jnp.concatenate(chi + clo + [jnp.zeros((LANES - 2 * EXPERTS_PER_GROUP, tm), F32)], axis=0)
    h_s[:, 0:D_MODEL] = hi
    h_s[:, D_MODEL:D_MODEL + LANES] = crow.T.astype(BF)
    y_s[...] = jnp.zeros(y_s.shape, F32)

    def group_body(g, carry):
        nblk = (cnt_s[g] + (cap - 1)) // cap

        def block_body(j, carry2):
            want = lax.broadcasted_iota(jnp.int32, (cap, tm), 0) + j * cap
            hit = (rt_s[1:2, :] == want) & (rt_s[0:1, :] == g)
            pmat = jnp.where(hit, 1.0, 0.0).astype(BF)
            gathered = _dot(pmat, h_s[...])
            hg = gathered[:, 0:D_MODEL].astype(BF)
            cexp = _dot(gathered[:, D_MODEL:D_MODEL + LANES].astype(BF), expand_ref[...])
            parts = []
            for e in range(EXPERTS_PER_GROUP):
                a = _dot(hg, w1_ref[EXPERTS_PER_GROUP * g + e])
                b = _dot(hg, w3_ref[EXPERTS_PER_GROUP * g + e])
                parts.append(_silu(a) * b)
            act = (jnp.concatenate(parts, axis=1) * cexp).astype(BF)
            y = _dot(act, w2_ref[g]).astype(BF)
            y_s[...] += lax.dot_general(pmat, y, (((0,), (0,)), ((), ())),
                                        preferred_element_type=F32)
            return carry2

        lax.fori_loop(0, nblk, block_body, 0)
        return carry

    lax.fori_loop(0, N_GROUPS, group_body, 0)
    o_ref[...] = x_ref[...] + mod[:, 2 * D_MODEL:3 * D_MODEL] * y_s[...]


def _moe_call(layer, x2d, mod, tiles_per_row, p, tm, name):
    n = x2d.shape[0]
    if tiles_per_row is None:
        mod_spec = pl.BlockSpec((None, n, 3 * D_MODEL), lambda i: (layer, 0, 1))
    else:
        mod_spec = pl.BlockSpec((None, SUBLANES, 3 * D_MODEL),
                                lambda i: (layer, PROMPT_MOD_ROW // SUBLANES, 1))
    lsel3 = lambda i: (layer, 0, 0)
    lsel4 = lambda i: (layer, 0, 0, 0)
    upper = np.triu(np.ones((tm, tm), np.float32), k=1)
    kernel = functools.partial(_moe_kernel, tm=tm, cap=MOE_CAP, tiles_per_row=tiles_per_row)
    return pl.pallas_call(
        kernel,
        out_shape=jax.ShapeDtypeStruct((n, D_MODEL), F32),
        grid=(n // tm,),
        in_specs=[
            pl.BlockSpec((tm, D_MODEL), lambda i: (i, 0)),
            mod_spec,
            _const_spec((None, 1, D_MODEL), lsel3),
            _const_spec((D_MODEL, LANES), lambda i: (0, 0)),
            _const_spec((N_EXPERTS, 1), lambda i: (0, 0)),
            _const_spec((tm, tm), lambda i: (0, 0)),
            _const_spec((LANES, EXPERTS_PER_GROUP * EXPERT_FF), lambda i: (0, 0)),
            _const_spec((None, N_EXPERTS, D_MODEL, EXPERT_FF), lsel4),
            _const_spec((None, N_EXPERTS, D_MODEL, EXPERT_FF), lsel4),
            _const_spec((None, N_GROUPS, EXPERTS_PER_GROUP * EXPERT_FF, D_MODEL), lsel4),
        ],
        out_specs=pl.BlockSpec((tm, D_MODEL), lambda i: (i, 0)),
        scratch_shapes=[
            pltpu.VMEM((tm, D_MODEL + LANES), BF),
            pltpu.VMEM((SUBLANES, tm), jnp.int32),
            pltpu.VMEM((tm, D_MODEL), F32),
            pltpu.SMEM((N_GROUPS,), jnp.int32),
        ],
        compiler_params=pltpu.CompilerParams(
            dimension_semantics=("arbitrary",),
            vmem_limit_bytes=VMEM_LIMIT),
        name=name,
    )(x2d, mod, p["n2g"], p["rw2"], p["rb"], jnp.asarray(upper, dtype=BF),
      p["expand"], p["moe_w1"], p["moe_w3"], p["moe_w2"])


def _sample_proj_kernel(x_ref, mod_ref, n1g_ref, w_ref, z_ref):
    mod = mod_ref[...]
    h = _modulate(x_ref[...], n1g_ref[...], mod[:, 0:D_MODEL], mod[:, D_MODEL:2 * D_MODEL])
    z_ref[...] = _dot(h.astype(BF), w_ref[...])


def _sample_proj_call(layer, xs, mod, p):
    n = xs.shape[0]
    return pl.pallas_call(
        _sample_proj_kernel,
        out_shape=jax.ShapeDtypeStruct((n, IN_WIDTH), F32),
        grid=(IN_WIDTH // PROJ_BLOCK,),
        in_specs=[
            pl.BlockSpec((n, D_MODEL), lambda j: (0, 0)),
            pl.BlockSpec((None, n, 3 * D_MODEL), lambda j: (layer, 0, 0)),
            pl.BlockSpec((None, 1, D_MODEL), lambda j: (layer, 0, 0)),
            pl.BlockSpec((None, D_MODEL, PROJ_BLOCK), lambda j: (layer, 0, j)),
        ],
        out_specs=pl.BlockSpec((n, PROJ_BLOCK), lambda j: (0, j)),
        compiler_params=pltpu.CompilerParams(dimension_semantics=("arbitrary",)),
        name=f"sample_proj_l{layer}",
    )(xs, mod, p["n1g"], p["w_in"])


def _shift_in_column(cache_t, new_rows, bb):
    flat = cache_t.reshape(bb * HEAD_DIM, WINDOW)
    shifted = pltpu.roll(flat, WINDOW - 1, 1).reshape(bb, HEAD_DIM, WINDOW)
    padded = jnp.concatenate([new_rows, jnp.zeros((LANES - bb, LANES), F32)], axis=0)
    new_t = padded.T
    is_last = lax.broadcasted_iota(jnp.int32, (HEAD_DIM, WINDOW), 1) == WINDOW - 1
    out = []
    for b in range(bb):
        col = new_t[0:HEAD_DIM, b:b + 1]
        out.append(jnp.where(is_last, col, shifted[b]))
    return out


def _sample_mixer_kernel(z_ref, x_ref, mod_ref, kt_ref, vt_ref, st_ref,
                         gmg_ref, gmw_ref, gmb_ref, cdw_ref, cb_ref, clg_ref, clb_ref,
                         qn_ref, kn_ref, ropec_ref, ropes_ref, sink_ref, bd_ref, wb_ref, wout_ref,
                         x1_ref, kto_ref, vto_ref, sto_ref, gv_ref,
                         qf_s, of_s, oc_s, zg_s, *, bb):
    i = pl.program_id(0)
    r0 = pl.multiple_of(i * bb, bb)
    z = z_ref[...]
    zg_s[pl.ds(r0, bb), :] = z[:, OFF_GATE:OFF_GATE + N_BRANCH * D_MODEL]

    u = _gelu(z[:, OFF_GM_U:OFF_GM_U + GM_WIDTH])
    gv = _gelu(z[:, OFF_GM_V:OFF_GM_V + GM_WIDTH])
    v = gv * lax.rsqrt(jnp.mean(gv * gv, axis=-1, keepdims=True) + EPS) * gmg_ref[...]
    gv_ref[...] = v
    o_gm = u * (v * gmw_ref[...] + gmb_ref[...])

    a = z[:, OFF_CV_A:OFF_CV_A + CONV_WIDTH] * _sigmoid(z[:, OFF_CV_G:OFF_CV_G + CONV_WIDTH])
    y = cb_ref[...] + cdw_ref[CONV_K - 1:CONV_K, :] * a
    for j in range(CONV_K - 1):
        y = y + cdw_ref[j:j + 1, :] * st_ref[j]
    o_cv = _layer_norm_silu(y, clg_ref[...], clb_ref[...])
    sto_ref[0:CONV_K - 2] = st_ref[1:CONV_K - 1]
    sto_ref[CONV_K - 2] = a

    rc = ropec_ref[...]
    rs = ropes_ref[...]
    bd = bd_ref[...]
    q = _rope(_head_norm(z[:, OFF_Q:OFF_Q + ATT_WIDTH], bd, qn_ref[...]), rc, rs) * (HEAD_DIM ** -0.5)
    knew = _rope(_head_norm(z[:, OFF_K:OFF_K + KV_WIDTH], bd[0:LANES, 0:LANES], kn_ref[...]), rc, rs)
    vnew = z[:, OFF_V:OFF_V + KV_WIDTH]
    lo = _low_half((bb, LANES))
    for h in range(N_HEADS):
        tile = q[:, (h // 2) * LANES:(h // 2 + 1) * LANES]
        if h % 2 == 1:
            tile = pltpu.roll(tile, HEAD_DIM, 1)
        qf_s[:, h, :] = jnp.where(lo, tile, 0.0)
    k_low = [knew, pltpu.roll(knew, HEAD_DIM, 1)]
    v_low = [vnew, pltpu.roll(vnew, HEAD_DIM, 1)]
    for g in range(N_KV_HEADS):
        qg = qf_s[:, Q_REP * g:Q_REP * (g + 1), :]
        kt = kt_ref[:, g]
        vt = vt_ref[:, g]
        s = jnp.einsum("brd,bdp->brp", qg[:, :, 0:HEAD_DIM].astype(BF), kt.astype(BF),
                       preferred_element_type=F32)
        s_new = jnp.sum(qg * k_low[g][:, None, :], axis=-1, keepdims=True)
        sink = sink_ref[Q_REP * g:Q_REP * (g + 1), 0:1][None]
        m = jnp.maximum(jnp.maximum(jnp.max(s, axis=-1, keepdims=True), s_new), sink)
        pr = jnp.exp(s - m)
        p_new = jnp.exp(s_new - m)
        den = jnp.sum(pr, axis=-1, keepdims=True) + p_new + jnp.exp(sink - m)
        o = jnp.einsum("brp,bdp->brd", pr.astype(BF), vt.astype(BF), preferred_element_type=F32)
        o = (o + p_new * v_low[g][:, None, 0:HEAD_DIM]) / den
        of_s[:, Q_REP * g:Q_REP * (g + 1), :] = jnp.concatenate([o, jnp.zeros(o.shape, F32)], axis=-1)
        for b, tile in enumerate(_shift_in_column(kt, k_low[g], bb)):
            kto_ref[b, g] = tile
        for b, tile in enumerate(_shift_in_column(vt, v_low[g], bb)):
            vto_ref[b, g] = tile
    att_tiles = []
    for j in range(N_HEADS // 2):
        second = pltpu.roll(of_s[:, 2 * j + 1, :], HEAD_DIM, 1)
        att_tiles.append(jnp.where(lo, of_s[:, 2 * j, :], second))
    o_att = jnp.concatenate(att_tiles, axis=1)

    oc_s[pl.ds(r0, bb), 0:GM_WIDTH] = o_gm.astype(BF)
    oc_s[pl.ds(r0, bb), GM_WIDTH:GM_WIDTH + CONV_WIDTH] = o_cv.astype(BF)
    oc_s[pl.ds(r0, bb), GM_WIDTH + CONV_WIDTH:GM_WIDTH + CONV_WIDTH + ATT_WIDTH] = o_att.astype(BF)

    @pl.when(i == pl.num_programs(0) - 1)
    def _():
        acc = _sigmoid(zg_s[:, 0:D_MODEL]) * _dot(oc_s[:, 0:GM_WIDTH], wb_ref[0])
        acc = acc + _sigmoid(zg_s[:, D_MODEL:2 * D_MODEL]) * _dot(
            oc_s[:, GM_WIDTH:GM_WIDTH + CONV_WIDTH], wb_ref[1])
        acc = acc + _sigmoid(zg_s[:, 2 * D_MODEL:3 * D_MODEL]) * _dot(
            oc_s[:, GM_WIDTH + CONV_WIDTH:GM_WIDTH + CONV_WIDTH + ATT_WIDTH], wb_ref[2])
        gt1 = mod_ref[...][:, 2 * D_MODEL:3 * D_MODEL]
        x1_ref[...] = x_ref[...] + gt1 * _dot(acc.astype(BF), wout_ref[...])


def _sample_mixer_call(layer, z, xs, mod, kt, vt, st, p, bb):
    n = xs.shape[0]
    lsel3 = lambda i: (layer, 0, 0)
    cache_block = (bb, N_KV_HEADS, HEAD_DIM, WINDOW)
    kernel = functools.partial(_sample_mixer_kernel, bb=bb)
    return pl.pallas_call(
        kernel,
        out_shape=(
            jax.ShapeDtypeStruct((n, D_MODEL), F32),
            jax.ShapeDtypeStruct((n, N_KV_HEADS, HEAD_DIM, WINDOW), F32),
            jax.ShapeDtypeStruct((n, N_KV_HEADS, HEAD_DIM, WINDOW), F32),
            jax.ShapeDtypeStruct((CONV_K - 1, n, CONV_WIDTH), F32),
            jax.ShapeDtypeStruct((n, GM_WIDTH), F32),
        ),
        grid=(n // bb,),
        in_specs=[
            pl.BlockSpec((bb, IN_WIDTH), lambda i: (i, 0)),
            pl.BlockSpec((n, D_MODEL), lambda i: (0, 0)),
            pl.BlockSpec((None, n, 3 * D_MODEL), lsel3),
            pl.BlockSpec((None,) + cache_block, lambda i: (layer, i, 0, 0, 0)),
            pl.BlockSpec((None,) + cache_block, lambda i: (layer, i, 0, 0, 0)),
            pl.BlockSpec((None, CONV_K - 1, bb, CONV_WIDTH), lambda i: (layer, 0, i, 0)),
            pl.BlockSpec((None, 1, GM_WIDTH), lsel3),
            pl.BlockSpec((None, 1, GM_WIDTH), lsel3),
            pl.BlockSpec((None, 1, GM_WIDTH), lsel3),
            pl.BlockSpec((None, CONV_K, CONV_WIDTH), lsel3),
            pl.BlockSpec((None, 1, CONV_WIDTH), lsel3),
            pl.BlockSpec((None, 1, CONV_WIDTH), lsel3),
            pl.BlockSpec((None, 1, CONV_WIDTH), lsel3),
            pl.BlockSpec((None, 1, ATT_WIDTH), lsel3),
            pl.BlockSpec((None, 1, LANES), lsel3),
            pl.BlockSpec((1, LANES), lambda i: (0, 0)),
            pl.BlockSpec((1, LANES), lambda i: (0, 0)),
            pl.BlockSpec((None, N_HEADS, LANES), lsel3),
            pl.BlockSpec((ATT_WIDTH, ATT_WIDTH), lambda i: (0, 0)),
            pl.BlockSpec((None, N_BRANCH, GM_WIDTH, D_MODEL), lambda i: (layer, 0, 0, 0)),
            pl.BlockSpec((None, D_MODEL, D_MODEL), lsel3),
        ],
        out_specs=(
            pl.BlockSpec((n, D_MODEL), lambda i: (0, 0)),
            pl.BlockSpec(cache_block, lambda i: (i, 0, 0, 0)),
            pl.BlockSpec(cache_block, lambda i: (i, 0, 0, 0)),
            pl.BlockSpec((CONV_K - 1, bb, CONV_WIDTH), lambda i: (0, i, 0)),
            pl.BlockSpec((bb, GM_WIDTH), lambda i: (i, 0)),
        ),
        scratch_shapes=[
            pltpu.VMEM((bb, N_HEADS, LANES), F32),
            pltpu.VMEM((bb, N_HEADS, LANES), F32),
            pltpu.VMEM((n, GM_WIDTH + CONV_WIDTH + ATT_WIDTH), BF),
            pltpu.VMEM((n, N_BRANCH * D_MODEL), F32),
        ],
        compiler_params=pltpu.CompilerParams(
            dimension_semantics=("arbitrary",), vmem_limit_bytes=VMEM_LIMIT),
        name=f"sample_mixers_l{layer}",
    )(z, xs, mod, kt, vt, st, p["gm_g"], p["gm_w0"], p["gm_b0"], p["conv_dw"], p["conv_b"],
      p["conv_ln_g"], p["conv_ln_b"], p["qn"], p["kn"], p["rope_c1"], p["rope_s1"],
      p["sink_lanes"], p["blockdiag"], p["w_branch"], p["w_out"])


def _rope_lane_tables():
    half = ROT_DIM // 2
    freqs = jnp.exp(-math.log(ROPE_THETA) * jnp.arange(half, dtype=F32) * (2.0 / ROT_DIM))
    rest = jnp.zeros((HEAD_DIM - ROT_DIM,), F32)
    freq64 = jnp.concatenate([freqs, freqs, rest])
    sign64 = jnp.concatenate([-jnp.ones((half,), F32), jnp.ones((half,), F32), rest])
    reps = LANES // HEAD_DIM
    return jnp.tile(freq64, reps)[None, :], jnp.tile(sign64, reps)[None, :]


def _cos_sin(pos, lane_freq):
    ang = pos.astype(F32)[:, None] * lane_freq
    return jnp.cos(ang), jnp.sin(ang)


def _band_bias():
    i = np.arange(WINDOW)[:, None]
    j = np.arange(2 * WINDOW)[None, :]
    band = (j >= i) & (j <= i + WINDOW)
    first = band & (j >= WINDOW)
    out = np.where(np.stack([first, band]), 0.0, NEG_BIG).astype(np.float32)
    return jnp.asarray(out)


def _blockdiag():
    idx = np.arange(ATT_WIDTH) // HEAD_DIM
    return jnp.asarray((idx[:, None] == idx[None, :]).astype(np.float32), dtype=BF)


def _expand_matrix():
    k = np.arange(LANES)[:, None]
    e = (np.arange(EXPERTS_PER_GROUP * EXPERT_FF) // EXPERT_FF)[None, :]
    return jnp.asarray(((k == e) | (k == e + EXPERTS_PER_GROUP)).astype(np.float32), dtype=BF)


def _dup_heads(w):
    h0, h1 = w[..., :HEAD_DIM], w[..., HEAD_DIM:]
    return jnp.concatenate([h0, h0, h1, h1], axis=-1)


def _prepare(norm1_g, norm2_g, w_in, gm_norm_g, gm_ws, gm_b, conv_dw, conv_b, conv_ln_g,
             conv_ln_b, q_norm_g, k_norm_g, attn_sinks, w_branch, w_out, router_w, router_b,
             moe_w1, moe_w3, moe_w2, seq, tm):
    w_kv2 = jnp.concatenate([_dup_heads(w_in[:, :, OFF_K:OFF_V]),
                             _dup_heads(w_in[:, :, OFF_V:OFF_GATE])], axis=-1).astype(BF)
    rw = router_w.astype(F32)
    rw_hi = rw.astype(BF)
    rw_lo = (rw - rw_hi.astype(F32)).astype(BF)
    rw2 = jnp.concatenate(
        [rw_hi, rw_lo, jnp.zeros((D_MODEL, LANES - 2 * N_EXPERTS), BF)], axis=1)
    lane_freq, lane_sign = _rope_lane_tables()
    cb, sb = _cos_sin(jnp.arange(seq // tm, dtype=jnp.int32) * tm, lane_freq)
    co, so = _cos_sin(jnp.arange(tm, dtype=jnp.int32), lane_freq)
    c1, s1 = _cos_sin(PAST_LEN + jnp.arange(1, dtype=jnp.int32), lane_freq)
    row3 = lambda a: a.reshape(DEPTH, 1, a.shape[-1])
    return {
        "n1g": row3(norm1_g), "n2g": row3(norm2_g), "w_in": w_in.astype(BF), "w_kv2": w_kv2,
        "gm_g": row3(gm_norm_g), "gm_ws": gm_ws,
        "gm_bias": jnp.repeat(jnp.swapaxes(gm_b, 1, 2), LANES, axis=2),
        "gm_w0": jnp.repeat(gm_ws[:, :, 0, 0], LANES, axis=1).reshape(DEPTH, 1, GM_WIDTH),
        "gm_b0": jnp.repeat(gm_b[:, :, 0], LANES, axis=1).reshape(DEPTH, 1, GM_WIDTH),
        "conv_dw": conv_dw, "conv_b": row3(conv_b), "conv_ln_g": row3(conv_ln_g),
        "conv_ln_b": row3(conv_ln_b),
        "qn": row3(jnp.tile(q_norm_g, (1, N_HEADS))),
        "kn": row3(jnp.tile(k_norm_g, (1, 2 * N_KV_HEADS))),
        "sinks": attn_sinks,
        "sink_lanes": jnp.broadcast_to(attn_sinks[:, :, None], (DEPTH, N_HEADS, LANES)),
        "rope_base": jnp.concatenate([cb, sb], axis=1)[:, None, :],
        "rope_off": jnp.concatenate([co, so], axis=1),
        "rope_sign": lane_sign, "rope_c1": c1, "rope_s1": s1 * lane_sign,
        "band_bias": _band_bias(), "blockdiag": _blockdiag(),
        "w_branch": w_branch.astype(BF), "w_out": w_out.astype(BF),
        "rw2": rw2,
        "rb": router_b.astype(F32).reshape(N_EXPERTS, 1),
        "expand": _expand_matrix(),
        "moe_w1": moe_w1.astype(BF), "moe_w3": moe_w3.astype(BF),
        "moe_w2": moe_w2.astype(BF).reshape(DEPTH, N_GROUPS, EXPERTS_PER_GROUP * EXPERT_FF, D_MODEL),
    }


def kernel(x_prompt, x_sample, cache_win_k, cache_win_v, state_conv, c_prompt, c_sample, norm1_g, norm2_g, w_ada, b_ada, w_in, gm_norm_g, gm_ws, gm_b, conv_dw, conv_b, conv_ln_g, conv_ln_b, q_norm_g, k_norm_g, attn_sinks, w_branch, w_out, router_w, router_b, moe_w1, moe_w3, moe_w2):
    nb, seq, _ = x_prompt.shape
    ns = x_sample.shape[0]
    tm = min(MIX_TILE, seq)
    tmoe = min(MOE_TILE, seq)
    p = _prepare(norm1_g, norm2_g, w_in, gm_norm_g, gm_ws, gm_b, conv_dw, conv_b, conv_ln_g,
                 conv_ln_b, q_norm_g, k_norm_g, attn_sinks, w_branch, w_out, router_w, router_b,
                 moe_w1, moe_w3, moe_w2, seq, tm)
    assert ns == PROMPT_MOD_ROW and nb <= ADA_ROWS - PROMPT_MOD_ROW
    c_all = jnp.concatenate(
        [c_sample, c_prompt, jnp.zeros((ADA_ROWS - nb - ns, D_MODEL), F32)], axis=0)
    mod = _ada_call(c_all, w_ada, b_ada)

    kt = jnp.transpose(cache_win_k, (0, 1, 3, 4, 2))
    vt = jnp.transpose(cache_win_v, (0, 1, 3, 4, 2))
    st = jnp.transpose(state_conv, (0, 2, 1, 3))

    xp = x_prompt
    xs = x_sample.reshape(ns, D_MODEL)
    kp_l, vp_l, cp_l, ks_l, vs_l, cs_l, gs_l = [], [], [], [], [], [], []
    for l in range(DEPTH):
        xp, kp, vp, cp = _mixer_call(l, xp, mod, p, tm)
        xp = _moe_call(l, xp.reshape(nb * seq, D_MODEL), mod, seq // tmoe, p, tmoe,
                       f"prompt_moe_l{l}").reshape(nb, seq, D_MODEL)
        kp_l.append(kp)
        vp_l.append(vp)
        cp_l.append(cp)

        z = _sample_proj_call(l, xs, mod, p)
        xs, kn, vn, cn, gv = _sample_mixer_call(l, z, xs, mod, kt, vt, st, p, SAMPLE_BLOCK)
        xs = _moe_call(l, xs, mod, None, p, ns, f"sample_moe_l{l}")
        ks_l.append(kn)
        vs_l.append(vn)
        cs_l.append(cn)
        gs_l.append(gv)

    kv_shape = (DEPTH, nb, WINDOW, N_KV_HEADS, HEAD_DIM)
    return (xp, xs.reshape(ns, 1, D_MODEL),
            jnp.stack(kp_l).reshape(kv_shape), jnp.stack(vp_l).reshape(kv_shape),
            jnp.stack(cp_l),
            jnp.transpose(jnp.stack(ks_l), (0, 1, 4, 2, 3)),
            jnp.transpose(jnp.stack(vs_l), (0, 1, 4, 2, 3)),
            jnp.transpose(jnp.stack(cs_l), (0, 2, 1, 3)),
            jnp.stack(gs_l).reshape(DEPTH, ns, 1, GM_WIDTH))
```

```python
import functools
import math

import jax
import jax.numpy as jnp
import numpy as np
from jax import lax
from jax.experimental import pallas as pl
from jax.experimental.pallas import tpu as pltpu

F32 = jnp.float32
BF = jnp.bfloat16

D_MODEL = 1024
DEPTH = 2
PAST_LEN = 8192
CHUNK = 128
GM_GROUPS = 4
GM_WIDTH = 512
CONV_WIDTH = 512
CONV_K = 31
N_HEADS = 8
N_KV_HEADS = 2
Q_REP = N_HEADS // N_KV_HEADS
HEAD_DIM = 64
ATT_WIDTH = N_HEADS * HEAD_DIM
KV_WIDTH = N_KV_HEADS * HEAD_DIM
WINDOW = 128
ROPE_THETA = 500000.0
ROT_DIM = HEAD_DIM // 4
N_BRANCH = 3
N_EXPERTS = 16
EXPERTS_PER_GROUP = 4
N_GROUPS = N_EXPERTS // EXPERTS_PER_GROUP
EXPERT_FF = 256
EPS = 1e-6

OFF_GM_U = 0
OFF_GM_V = 512
OFF_CV_A = 1024
OFF_CV_G = 1536
OFF_Q = 2048
OFF_K = 2560
OFF_V = 2688
OFF_GATE = 2816
IN_WIDTH = OFF_GATE + N_BRANCH * D_MODEL

LANES = 128
SUBLANES = 8
HALO = 32
MIX_TILE = 512
MOE_TILE = 512
MOE_CAP = 144
ADA_ROWS = 136
PROMPT_MOD_ROW = 128
ADA_BLOCK = 768
PROJ_BLOCK = IN_WIDTH // 2
SAMPLE_BLOCK = 32
VMEM_LIMIT = 56 * 1024 * 1024
NEG_BIG = -1e30


def _dot(a, b):
    return jnp.dot(a, b, preferred_element_type=F32)


def _dot_nt(a, b):
    return lax.dot_general(a, b, (((1,), (1,)), ((), ())), preferred_element_type=F32)


def _sigmoid(x):
    return 0.5 * jnp.tanh(0.5 * x) + 0.5


def _silu(x):
    return x * _sigmoid(x)


def _gelu(x):
    return 0.5 * x * (1.0 + jnp.tanh(0.7978845608028654 * (x + 0.044715 * (x * x * x))))


def _split_bf16(x):
    hi = x.astype(BF)
    lo = (x - hi.astype(F32)).astype(BF)
    return hi, lo


def _modulate(x, g, shift, scale):
    ms = jnp.mean(x * x, axis=-1, keepdims=True)
    return (x * lax.rsqrt(ms + EPS) * g) * (1.0 + scale) + shift


def _head_norm(x, blockdiag, g):
    hi, lo = _split_bf16(x * x)
    ssum = _dot(hi, blockdiag) + _dot(lo, blockdiag)
    return x * lax.rsqrt(ssum * (1.0 / HEAD_DIM) + EPS) * g


def _rope(x, c, s):
    width = x.shape[-1]
    reps = width // LANES
    cc = jnp.concatenate([c] * reps, axis=-1)
    ss = jnp.concatenate([s] * reps, axis=-1)
    lane = lax.broadcasted_iota(jnp.int32, x.shape, x.ndim - 1) % HEAD_DIM
    partner = jnp.where(lane < ROT_DIM // 2,
                        pltpu.roll(x, width - ROT_DIM // 2, x.ndim - 1),
                        pltpu.roll(x, ROT_DIM // 2, x.ndim - 1))
    return x * cc + partner * ss


def _layer_norm_silu(y, g, b):
    mu = jnp.mean(y, axis=-1, keepdims=True)
    yc = y - mu
    var = jnp.mean(yc * yc, axis=-1, keepdims=True)
    return _silu(yc * lax.rsqrt(var + EPS) * g + b)


def _low_half(shape):
    return lax.broadcasted_iota(jnp.int32, shape, len(shape) - 1) % LANES < HEAD_DIM


def _const_spec(shape, index_map):
    return pl.BlockSpec(shape, index_map, pipeline_mode=pl.Buffered(1))


def _ada_kernel(c_ref, w_ref, b_ref, o_ref):
    s = _silu(c_ref[...]).astype(BF)
    o_ref[...] = _dot(s, w_ref[...].astype(BF)) + b_ref[...]


def _ada_call(c_all, w_ada, b_ada):
    nb = (6 * D_MODEL) // ADA_BLOCK
    return pl.pallas_call(
        _ada_kernel,
        out_shape=jax.ShapeDtypeStruct((DEPTH, ADA_ROWS, 6 * D_MODEL), F32),
        grid=(DEPTH, nb),
        in_specs=[
            pl.BlockSpec((ADA_ROWS, D_MODEL), lambda l, j: (0, 0)),
            pl.BlockSpec((None, D_MODEL, ADA_BLOCK), lambda l, j: (l, 0, j)),
            pl.BlockSpec((None, 1, ADA_BLOCK), lambda l, j: (l, 0, j)),
        ],
        out_specs=pl.BlockSpec((None, ADA_ROWS, ADA_BLOCK), lambda l, j: (l, 0, j)),
        compiler_params=pltpu.CompilerParams(
            dimension_semantics=("arbitrary", "arbitrary")),
        name="ada_mod",
    )(c_all, w_ada, b_ada.reshape(DEPTH, 1, 6 * D_MODEL))


def _mixer_kernel(sinks_ref, x_ref, mod_ref, n1g_ref, win_ref, wkv_ref, gmg_ref, gmws_ref, gmb_ref,
                  cdw_ref, cb_ref, clg_ref, clb_ref, qn_ref, kn_ref, ropeb_ref, ropeo_ref,
                  sign_ref, bias_ref, bd_ref, wb_ref, wout_ref,
                  x1_ref, kwin_ref, vwin_ref, cst_ref,
                  kd_s, vd_s, abuf, oatt_s, hb_s, z_s, *, layer, tm):
    t = pl.program_id(1)
    last = pl.num_programs(1) - 1
    nblk = tm // WINDOW

    @pl.when(t == 0)
    def _():
        kd_s[0:WINDOW, :] = jnp.zeros((WINDOW, 2 * LANES), BF)
        vd_s[0:WINDOW, :] = jnp.zeros((WINDOW, 2 * LANES), BF)
        abuf[0:HALO, :] = jnp.zeros((HALO, CONV_WIDTH), F32)
        abuf[HALO + tm:HALO + tm + SUBLANES, :] = jnp.zeros((SUBLANES, CONV_WIDTH), F32)

    x = x_ref[...]
    mod = mod_ref[pl.ds(pl.program_id(0), 1), :]
    sh1 = mod[:, 0:D_MODEL]
    sc1 = mod[:, D_MODEL:2 * D_MODEL]
    gt1 = mod[:, 2 * D_MODEL:3 * D_MODEL]
    hb_s[...] = _modulate(x, n1g_ref[...], sh1, sc1).astype(BF)

    def proj(off, width):
        z_s[:, off:off + width] = _dot(hb_s[...], win_ref[:, off:off + width])

    def zcols(off, width):
        return z_s[:, off:off + width]

    proj(OFF_CV_A, 2 * CONV_WIDTH)
    a = zcols(OFF_CV_A, CONV_WIDTH) * _sigmoid(zcols(OFF_CV_G, CONV_WIDTH))
    abuf[HALO:HALO + tm, :] = a
    first_off = HALO - (CONV_K - 1)

    def conv_tile(ci):
        lanes = slice(ci * LANES, (ci + 1) * LANES)
        yt = cb_ref[:, lanes]
        for b in range(SUBLANES):
            part = None
            for off in range(b, HALO + 1, SUBLANES):
                if off < first_off:
                    continue
                term = (cdw_ref[off - first_off:off - first_off + 1, lanes]
                        * abuf[pl.ds(off - b, tm + SUBLANES), lanes])
                part = term if part is None else part + term
            yt = yt + part[b:b + tm, :]
        return yt

    proj(OFF_GM_U, 2 * GM_WIDTH)
    proj(OFF_Q, ATT_WIDTH)
    z_s[:, IN_WIDTH:IN_WIDTH + 4 * LANES] = _dot(hb_s[...], wkv_ref[...])
    proj(OFF_GATE, N_BRANCH * D_MODEL)
    y = jnp.concatenate([conv_tile(ci) for ci in range(CONV_WIDTH // LANES)], axis=1)
    abuf[0:HALO, :] = abuf[tm:tm + HALO, :]
    o_cv = _layer_norm_silu(y, clg_ref[...], clb_ref[...]).astype(BF)
    acc = _sigmoid(zcols(OFF_GATE + D_MODEL, D_MODEL)) * _dot(o_cv, wb_ref[1])

    u = _gelu(zcols(OFF_GM_U, GM_WIDTH))
    gv = _gelu(zcols(OFF_GM_V, GM_WIDTH))
    v = gv * lax.rsqrt(jnp.mean(gv * gv, axis=-1, keepdims=True) + EPS) * gmg_ref[...]
    vb = v.astype(BF)
    row = lax.broadcasted_iota(jnp.int32, (CHUNK, CHUNK), 0)
    col = lax.broadcasted_iota(jnp.int32, (CHUNK, CHUNK), 1)
    ws = [jnp.where(row >= col, gmws_ref[g], 0.0).astype(BF) for g in range(GM_GROUPS)]
    gmb = gmb_ref[...]
    chunks = []
    for c in range(nblk):
        parts = [_dot(ws[g], vb[c * CHUNK:(c + 1) * CHUNK, g * LANES:(g + 1) * LANES])
                 for g in range(GM_GROUPS)]
        chunks.append(jnp.concatenate(parts, axis=1) + gmb)
    o_gm = (u * jnp.concatenate(chunks, axis=0)).astype(BF)
    acc = acc + _sigmoid(zcols(OFF_GATE, D_MODEL)) * _dot(o_gm, wb_ref[0])

    rbase = ropeb_ref[...]
    roff = ropeo_ref[...]
    cb_, sb_ = rbase[:, 0:LANES], rbase[:, LANES:2 * LANES]
    co_, so_ = roff[:, 0:LANES], roff[:, LANES:2 * LANES]
    rc = cb_ * co_ - sb_ * so_
    rs = (sb_ * co_ + cb_ * so_) * sign_ref[...]
    bd = bd_ref[...]
    q = _rope(_head_norm(zcols(OFF_Q, ATT_WIDTH), bd, qn_ref[...]), rc, rs)
    qb = (q * (HEAD_DIM ** -0.5)).astype(BF)
    kd = _rope(_head_norm(zcols(IN_WIDTH, 2 * LANES), bd[0:2 * LANES, 0:2 * LANES], kn_ref[...]),
               rc, rs)
    vd = zcols(IN_WIDTH + 2 * LANES, 2 * LANES)
    kd_s[WINDOW:WINDOW + tm, :] = kd.astype(BF)
    vd_s[WINDOW:WINDOW + tm, :] = vd.astype(BF)

    lo_q = _low_half((WINDOW, LANES))
    zero_q = jnp.zeros((WINDOW, LANES), BF)
    band = bias_ref[1]
    for bi in range(nblk):
        bias = jnp.where(t == 0, bias_ref[0], band) if bi == 0 else band
        bias4 = jnp.concatenate([bias] * Q_REP, axis=0)
        for g in range(N_KV_HEADS):
            tiles = [qb[bi * WINDOW:(bi + 1) * WINDOW, (2 * g + j) * LANES:(2 * g + j + 1) * LANES]
                     for j in range(2)]
            qs = jnp.concatenate([jnp.where(lo_q, tiles[0], zero_q), jnp.where(lo_q, zero_q, tiles[0]),
                                  jnp.where(lo_q, tiles[1], zero_q), jnp.where(lo_q, zero_q, tiles[1])],
                                 axis=0)
            keys = kd_s[bi * WINDOW:(bi + 2) * WINDOW, g * LANES:(g + 1) * LANES]
            vals = vd_s[bi * WINDOW:(bi + 2) * WINDOW, g * LANES:(g + 1) * LANES]
            s = _dot_nt(qs, keys) + bias4
            outs = []
            for hh in range(Q_REP):
                sink = sinks_ref[layer, Q_REP * g + hh]
                sh = s[hh * WINDOW:(hh + 1) * WINDOW, :]
                m = jnp.maximum(jnp.max(sh, axis=-1, keepdims=True), sink)
                p = jnp.exp(sh - m)
                den = jnp.sum(p, axis=-1, keepdims=True) + jnp.exp(sink - m)
                outs.append(_dot(p.astype(BF), vals) / den)
            for j in range(2):
                oatt_s[bi * WINDOW:(bi + 1) * WINDOW, (2 * g + j) * LANES:(2 * g + j + 1) * LANES] = (
                    jnp.where(lo_q, outs[2 * j], outs[2 * j + 1]).astype(BF))
    kd_s[0:WINDOW, :] = kd_s[tm:tm + WINDOW, :]
    vd_s[0:WINDOW, :] = vd_s[tm:tm + WINDOW, :]
    acc = acc + _sigmoid(zcols(OFF_GATE + 2 * D_MODEL, D_MODEL)) * _dot(oatt_s[...], wb_ref[2])

    x1_ref[...] = x + gt1 * _dot(acc.astype(BF), wout_ref[...])

    @pl.when(t == last)
    def _():
        kl = kd[tm - WINDOW:tm, :]
        vl = vd[tm - WINDOW:tm, :]
        kwin_ref[...] = jnp.where(lo_q, kl[:, 0:LANES], kl[:, LANES:2 * LANES])
        vwin_ref[...] = jnp.where(lo_q, vl[:, 0:LANES], vl[:, LANES:2 * LANES])
        cst_ref[...] = a[tm - (CONV_K - 1):tm, :]


def _mixer_call(layer, x, mod, p, tm):
    nb, seq, _ = x.shape
    nt = seq // tm
    lsel3 = lambda b, t: (layer, 0, 0)
    kernel = functools.partial(_mixer_kernel, layer=layer, tm=tm)
    return pl.pallas_call(
        kernel,
        out_shape=(
            jax.ShapeDtypeStruct((nb, seq, D_MODEL), F32),
            jax.ShapeDtypeStruct((nb, WINDOW, KV_WIDTH), F32),
            jax.ShapeDtypeStruct((nb, WINDOW, KV_WIDTH), F32),
            jax.ShapeDtypeStruct((nb, CONV_K - 1, CONV_WIDTH), F32),
        ),
        grid=(nb, nt),
        in_specs=[
            pl.BlockSpec(memory_space=pltpu.SMEM),
            pl.BlockSpec((None, tm, D_MODEL), lambda b, t: (b, t, 0)),
            pl.BlockSpec((None, SUBLANES, 3 * D_MODEL),
                         lambda b, t: (layer, PROMPT_MOD_ROW // SUBLANES, 0)),
            _const_spec((None, 1, D_MODEL), lsel3),
            _const_spec((None, D_MODEL, IN_WIDTH), lsel3),
            _const_spec((None, D_MODEL, 4 * LANES), lsel3),
            _const_spec((None, 1, GM_WIDTH), lsel3),
            _const_spec((None, GM_GROUPS, CHUNK, CHUNK), lambda b, t: (layer, 0, 0, 0)),
            _const_spec((None, CHUNK, GM_WIDTH), lsel3),
            _const_spec((None, CONV_K, CONV_WIDTH), lsel3),
            _const_spec((None, 1, CONV_WIDTH), lsel3),
            _const_spec((None, 1, CONV_WIDTH), lsel3),
            _const_spec((None, 1, CONV_WIDTH), lsel3),
            _const_spec((None, 1, ATT_WIDTH), lsel3),
            _const_spec((None, 1, 2 * LANES), lsel3),
            pl.BlockSpec((None, 1, 2 * LANES), lambda b, t: (t, 0, 0)),
            _const_spec((tm, 2 * LANES), lambda b, t: (0, 0)),
            _const_spec((1, LANES), lambda b, t: (0, 0)),
            _const_spec((2, WINDOW, 2 * WINDOW), lambda b, t: (0, 0, 0)),
            _const_spec((ATT_WIDTH, ATT_WIDTH), lambda b, t: (0, 0)),
            _const_spec((None, N_BRANCH, GM_WIDTH, D_MODEL), lambda b, t: (layer, 0, 0, 0)),
            _const_spec((None, D_MODEL, D_MODEL), lsel3),
        ],
        out_specs=(
            pl.BlockSpec((None, tm, D_MODEL), lambda b, t: (b, t, 0)),
            pl.BlockSpec((None, WINDOW, KV_WIDTH), lambda b, t: (b, 0, 0)),
            pl.BlockSpec((None, WINDOW, KV_WIDTH), lambda b, t: (b, 0, 0)),
            pl.BlockSpec((None, CONV_K - 1, CONV_WIDTH), lambda b, t: (b, 0, 0)),
        ),
        scratch_shapes=[
            pltpu.VMEM((WINDOW + tm, 2 * LANES), BF),
            pltpu.VMEM((WINDOW + tm, 2 * LANES), BF),
            pltpu.VMEM((HALO + tm + SUBLANES, CONV_WIDTH), F32),
            pltpu.VMEM((tm, ATT_WIDTH), BF),
            pltpu.VMEM((tm, D_MODEL), BF),
            pltpu.VMEM((tm, IN_WIDTH + 4 * LANES), F32),
        ],
        compiler_params=pltpu.CompilerParams(
            dimension_semantics=("arbitrary", "arbitrary"),
            vmem_limit_bytes=VMEM_LIMIT),
        name=f"prompt_mixers_l{layer}",
    )(p["sinks"], x, mod, p["n1g"], p["w_in"], p["w_kv2"], p["gm_g"], p["gm_ws"], p["gm_bias"],
      p["conv_dw"], p["conv_b"], p["conv_ln_g"], p["conv_ln_b"], p["qn"], p["kn"],
      p["rope_base"], p["rope_off"], p["rope_sign"], p["band_bias"], p["blockdiag"],
      p["w_branch"], p["w_out"])


def _route_rows(logits_t, rb_ref):
    scores = _sigmoid(logits_t)
    biased = scores + rb_ref[...]
    rows = lambda a, gi: [a[EXPERTS_PER_GROUP * gi + k:EXPERTS_PER_GROUP * gi + k + 1, :]
                          for k in range(EXPERTS_PER_GROUP)]
    best = None
    idx = None
    for gi in range(N_GROUPS):
        b = rows(biased, gi)
        hi1, lo1 = jnp.maximum(b[0], b[1]), jnp.minimum(b[0], b[1])
        hi2, lo2 = jnp.maximum(b[2], b[3]), jnp.minimum(b[2], b[3])
        gs = jnp.maximum(hi1, hi2) + jnp.maximum(jnp.minimum(hi1, hi2), jnp.maximum(lo1, lo2))
        if gi == 0:
            best, idx = gs, jnp.zeros(gs.shape, jnp.int32)
        else:
            better = gs > best
            idx = jnp.where(better, gi, idx)
            best = jnp.where(better, gs, best)
    bsel = rows(biased, 0)
    ssel = rows(scores, 0)
    for gi in range(1, N_GROUPS):
        bg, sg = rows(biased, gi), rows(scores, gi)
        pick = idx == gi
        bsel = [jnp.where(pick, bg[k], bsel[k]) for k in range(EXPERTS_PER_GROUP)]
        ssel = [jnp.where(pick, sg[k], ssel[k]) for k in range(EXPERTS_PER_GROUP)]
    chosen = []
    for k in range(EXPERTS_PER_GROUP):
        rank = jnp.zeros(idx.shape, jnp.int32)
        for k2 in range(EXPERTS_PER_GROUP):
            if k2 == k:
                continue
            beats = (bsel[k2] > bsel[k]) | ((bsel[k2] == bsel[k]) & (k2 < k))
            rank = rank + beats.astype(jnp.int32)
        chosen.append(jnp.where(rank < 2, ssel[k], 0.0))
    den = chosen[0] + chosen[1] + chosen[2] + chosen[3]
    return idx, [c / den for c in chosen]


def _moe_route(x, mod, n2g_ref, rw2_ref, rb_ref, upper_ref, h_buf, rt_buf, cnt_s, buf, tm):
    h2 = _modulate(x, n2g_ref[...], mod[:, 0:D_MODEL], mod[:, D_MODEL:2 * D_MODEL])
    hi, lo = _split_bf16(h2)
    rw2 = rw2_ref[...]
    lt = (_dot(hi, rw2) + _dot(lo, rw2)).T
    idx, comb = _route_rows(lt[0:N_EXPERTS, :] + lt[N_EXPERTS:2 * N_EXPERTS, :], rb_ref)

    onehot = [(idx == g).astype(F32) for g in range(N_GROUPS)]
    oh8 = jnp.concatenate(onehot + [jnp.zeros((SUBLANES - N_GROUPS, tm), F32)], axis=0)
    prefix = _dot(oh8.astype(BF), upper_ref[...])
    slot = onehot[0] * prefix[0:1, :]
    for g in range(1, N_GROUPS):
        slot = slot + onehot[g] * prefix[g:g + 1, :]
    rt_buf[0:1, :] = idx
    rt_buf[1:2, :] = slot.astype(jnp.int32)
    for g in range(N_GROUPS):
        cnt_s[buf, g] = jnp.sum(onehot[g]).astype(jnp.int32)

    chi = [c.astype(BF).astype(F32) for c in comb]
    clo = [c - h for c, h in zip(comb, chi)]
    crow = jnp.concatenate(chi + clo + [jnp.zeros((LANES - 2 * EXPERTS_PER_GROUP, tm), F32)], axis=0)
    h_buf[:, 0:D_MODEL] = hi
    h_buf[:, D_MODEL:D_MODEL + LANES] = crow.T.astype(BF)


def _moe_experts(x_ref, gt2, expand_ref, w1_ref, w3_ref, w2_ref, o_ref, h_buf, rt_buf, cnt_s, buf,
                 tm, cap, overflow):
    def dispatch(g, j):
        want = lax.broadcasted_iota(jnp.int32, (cap, tm), 0) + j * cap
        hit = (rt_buf[1:2, :] == want) & (rt_buf[0:1, :] == g)
        return jnp.where(hit, 1.0, 0.0).astype(BF)

    def run_experts(g, pmat):
        gathered = _dot(pmat, h_buf[...])
        hg = gathered[:, 0:D_MODEL].astype(BF)
        cexp = _dot(gathered[:, D_MODEL:D_MODEL + LANES].astype(BF), expand_ref[...])
        parts = []
        for e in range(EXPERTS_PER_GROUP):
            a = _dot(hg, w1_ref[EXPERTS_PER_GROUP * g + e])
            b = _dot(hg, w3_ref[EXPERTS_PER_GROUP * g + e])
            parts.append(_silu(a) * b)
        act = (jnp.concatenate(parts, axis=1) * cexp).astype(BF)
        return _dot(act, w2_ref[g]).astype(BF)

    def scatter(pmat, y):
        return lax.dot_general(pmat, y, (((0,), (0,)), ((), ())), preferred_element_type=F32)

    if not overflow:
        pmats = [dispatch(g, 0) for g in range(N_GROUPS)]
        ys = [run_experts(g, pmats[g]) for g in range(N_GROUPS)]
        y_tile = scatter(jnp.concatenate(pmats, axis=0), jnp.concatenate(ys, axis=0))
        o_ref[...] = x_ref[...] + gt2 * y_tile
        return

    def group_body(g, carry):
        nblk = (cnt_s[buf, g] + (cap - 1)) // cap

        def block_body(j, carry2):
            pmat = dispatch(g, j)
            o_ref[...] += gt2 * scatter(pmat, run_experts(g, pmat))
            return carry2

        lax.fori_loop(1, nblk, block_body, 0)
        return carry

    lax.fori_loop(0, N_GROUPS, group_body, 0)


def _moe_kernel(x_ref, xn_ref, mod_ref, n2g_ref, rw2_ref, rb_ref, upper_ref, expand_ref,
                w1_ref, w3_ref, w2_ref, o_ref, h0_s, h1_s, rt0_s, rt1_s, cnt_s,
                *, tm, cap, tiles_per_row, n_tiles):
    route = functools.partial(_moe_route, n2g_ref=n2g_ref, rw2_ref=rw2_ref, rb_ref=rb_ref,
                              upper_ref=upper_ref, cnt_s=cnt_s, tm=tm)
    experts = functools.partial(_moe_experts, x_ref, expand_ref=expand_ref, w1_ref=w1_ref,
                                w3_ref=w3_ref, w2_ref=w2_ref, o_ref=o_ref, cnt_s=cnt_s, tm=tm, cap=cap)
    h_bufs = (h0_s, h1_s)
    rt_bufs = (rt0_s, rt1_s)
    if tiles_per_row is None:
        mod = mod_ref[...]
        route(x_ref[...], mod, h_buf=h0_s, rt_buf=rt0_s, buf=0)
        for overflow in (False, True):
            experts(mod[:, 2 * D_MODEL:3 * D_MODEL], h_buf=h0_s, rt_buf=rt0_s, buf=0,
                    overflow=overflow)
        return

    i = pl.program_id(0)
    mod_row = lambda tile: mod_ref[pl.ds(tile // tiles_per_row, 1), :]

    @pl.when(i == 0)
    def _():
        route(x_ref[...], mod_row(i), h_buf=h0_s, rt_buf=rt0_s, buf=0)

    def step(cur):
        nxt = 1 - cur
        gt2 = mod_row(i)[:, 2 * D_MODEL:3 * D_MODEL]
        experts(gt2, h_buf=h_bufs[cur], rt_buf=rt_bufs[cur], buf=cur, overflow=False)
        route(xn_ref[...], mod_row(jnp.minimum(i + 1, n_tiles - 1)),
              h_buf=h_bufs[nxt], rt_buf=rt_bufs[nxt], buf=nxt)
        experts(gt2, h_buf=h_bufs[cur], rt_buf=rt_bufs[cur], buf=cur, overflow=True)

    @pl.when(i % 2 == 0)
    def _():
        step(0)

    @pl.when(i % 2 == 1)
    def _():
        step(1)


def _moe_call(layer, x2d, mod, tiles_per_row, p, tm, name):
    n = x2d.shape[0]
    n_tiles = n // tm
    if tiles_per_row is None:
        assert n_tiles == 1
        mod_spec = pl.BlockSpec((None, n, 3 * D_MODEL), lambda i: (layer, 0, 1))
    else:
        mod_spec = pl.BlockSpec((None, SUBLANES, 3 * D_MODEL),
                                lambda i: (layer, PROMPT_MOD_ROW // SUBLANES, 1))
    lsel3 = lambda i: (layer, 0, 0)
    lsel4 = lambda i: (layer, 0, 0, 0)
    upper = np.triu(np.ones((tm, tm), np.float32), k=1)
    kernel = functools.partial(_moe_kernel, tm=tm, cap=MOE_CAP, tiles_per_row=tiles_per_row,
                               n_tiles=n_tiles)
    return pl.pallas_call(
        kernel,
        out_shape=jax.ShapeDtypeStruct((n, D_MODEL), F32),
        grid=(n_tiles,),
        in_specs=[
            pl.BlockSpec((tm, D_MODEL), lambda i: (i, 0)),
            pl.BlockSpec((tm, D_MODEL), lambda i: (jnp.minimum(i + 1, n_tiles - 1), 0)),
            mod_spec,
            _const_spec((None, 1, D_MODEL), lsel3),
            _const_spec((D_MODEL, LANES), lambda i: (0, 0)),
            _const_spec((N_EXPERTS, 1), lambda i: (0, 0)),
            _const_spec((tm, tm), lambda i: (0, 0)),
            _const_spec((LANES, EXPERTS_PER_GROUP * EXPERT_FF), lambda i: (0, 0)),
            _const_spec((None, N_EXPERTS, D_MODEL, EXPERT_FF), lsel4),
            _const_spec((None, N_EXPERTS, D_MODEL, EXPERT_FF), lsel4),
            _const_spec((None, N_GROUPS, EXPERTS_PER_GROUP * EXPERT_FF, D_MODEL), lsel4),
        ],
        out_specs=pl.BlockSpec((tm, D_MODEL), lambda i: (i, 0)),
        scratch_shapes=[
            pltpu.VMEM((tm, D_MODEL + LANES), BF),
            pltpu.VMEM((tm, D_MODEL + LANES), BF),
            pltpu.VMEM((SUBLANES, tm), jnp.int32),
            pltpu.VMEM((SUBLANES, tm), jnp.int32),
            pltpu.SMEM((2, N_GROUPS), jnp.int32),
        ],
        compiler_params=pltpu.CompilerParams(
            dimension_semantics=("arbitrary",),
            vmem_limit_bytes=VMEM_LIMIT),
        name=name,
    )(x2d, x2d, mod, p["n2g"], p["rw2"], p["rb"], jnp.asarray(upper, dtype=BF),
      p["expand"], p["moe_w1"], p["moe_w3"], p["moe_w2"])


def _sample_proj_kernel(x_ref, mod_ref, n1g_ref, w_ref, z_ref):
    mod = mod_ref[...]
    h = _modulate(x_ref[...], n1g_ref[...], mod[:, 0:D_MODEL], mod[:, D_MODEL:2 * D_MODEL])
    z_ref[...] = _dot(h.astype(BF), w_ref[...])


def _sample_proj_call(layer, xs, mod, p):
    n = xs.shape[0]
    return pl.pallas_call(
        _sample_proj_kernel,
        out_shape=jax.ShapeDtypeStruct((n, IN_WIDTH), F32),
        grid=(IN_WIDTH // PROJ_BLOCK,),
        in_specs=[
            pl.BlockSpec((n, D_MODEL), lambda j: (0, 0)),
            pl.BlockSpec((None, n, 3 * D_MODEL), lambda j: (layer, 0, 0)),
            pl.BlockSpec((None, 1, D_MODEL), lambda j: (layer, 0, 0)),
            pl.BlockSpec((None, D_MODEL, PROJ_BLOCK), lambda j: (layer, 0, j)),
        ],
        out_specs=pl.BlockSpec((n, PROJ_BLOCK), lambda j: (0, j)),
        compiler_params=pltpu.CompilerParams(dimension_semantics=("arbitrary",)),
        name=f"sample_proj_l{layer}",
    )(xs, mod, p["n1g"], p["w_in"])


def _shift_in_column(cache_t, new_rows, bb):
    flat = cache_t.reshape(bb * HEAD_DIM, WINDOW)
    shifted = pltpu.roll(flat, WINDOW - 1, 1).reshape(bb, HEAD_DIM, WINDOW)
    padded = jnp.concatenate([new_rows, jnp.zeros((LANES - bb, LANES), F32)], axis=0)
    new_t = padded.T
    is_last = lax.broadcasted_iota(jnp.int32, (HEAD_DIM, WINDOW), 1) == WINDOW - 1
    out = []
    for b in range(bb):
        col = new_t[0:HEAD_DIM, b:b + 1]
        out.append(jnp.where(is_last, col, shifted[b]))
    return out


def _sample_mixer_kernel(z_ref, x_ref, mod_ref, kt_ref, vt_ref, st_ref,
                         gmg_ref, gmw_ref, gmb_ref, cdw_ref, cb_ref, clg_ref, clb_ref,
                         qn_ref, kn_ref, ropec_ref, ropes_ref, sink_ref, bd_ref, wb_ref, wout_ref,
                         x1_ref, kto_ref, vto_ref, sto_ref, gv_ref,
                         qf_s, of_s, oc_s, zg_s, *, bb):
    i = pl.program_id(0)
    r0 = pl.multiple_of(i * bb, bb)
    z = z_ref[...]
    zg_s[pl.ds(r0, bb), :] = z[:, OFF_GATE:OFF_GATE + N_BRANCH * D_MODEL]

    u = _gelu(z[:, OFF_GM_U:OFF_GM_U + GM_WIDTH])
    gv = _gelu(z[:, OFF_GM_V:OFF_GM_V + GM_WIDTH])
    v = gv * lax.rsqrt(jnp.mean(gv * gv, axis=-1, keepdims=True) + EPS) * gmg_ref[...]
    gv_ref[...] = v
    o_gm = u * (v * gmw_ref[...] + gmb_ref[...])

    a = z[:, OFF_CV_A:OFF_CV_A + CONV_WIDTH] * _sigmoid(z[:, OFF_CV_G:OFF_CV_G + CONV_WIDTH])
    y = cb_ref[...] + cdw_ref[CONV_K - 1:CONV_K, :] * a
    for j in range(CONV_K - 1):
        y = y + cdw_ref[j:j + 1, :] * st_ref[j]
    o_cv = _layer_norm_silu(y, clg_ref[...], clb_ref[...])
    sto_ref[0:CONV_K - 2] = st_ref[1:CONV_K - 1]
    sto_ref[CONV_K - 2] = a

    rc = ropec_ref[...]
    rs = ropes_ref[...]
    bd = bd_ref[...]
    q = _rope(_head_norm(z[:, OFF_Q:OFF_Q + ATT_WIDTH], bd, qn_ref[...]), rc, rs) * (HEAD_DIM ** -0.5)
    knew = _rope(_head_norm(z[:, OFF_K:OFF_K + KV_WIDTH], bd[0:LANES, 0:LANES], kn_ref[...]), rc, rs)
    vnew = z[:, OFF_V:OFF_V + KV_WIDTH]
    lo = _low_half((bb, LANES))
    for h in range(N_HEADS):
        tile = q[:, (h // 2) * LANES:(h // 2 + 1) * LANES]
        if h % 2 == 1:
            tile = pltpu.roll(tile, HEAD_DIM, 1)
        qf_s[:, h, :] = jnp.where(lo, tile, 0.0)
    k_low = [knew, pltpu.roll(knew, HEAD_DIM, 1)]
    v_low = [vnew, pltpu.roll(vnew, HEAD_DIM, 1)]
    for g in range(N_KV_HEADS):
        qg = qf_s[:, Q_REP * g:Q_REP * (g + 1), :]
        kt = kt_ref[:, g]
        vt = vt_ref[:, g]
        s = jnp.einsum("brd,bdp->brp", qg[:, :, 0:HEAD_DIM].astype(BF), kt.astype(BF),
                       preferred_element_type=F32)
        s_new = jnp.sum(qg * k_low[g][:, None, :], axis=-1, keepdims=True)
        sink = sink_ref[Q_REP * g:Q_REP * (g + 1), 0:1][None]
        m = jnp.maximum(jnp.maximum(jnp.max(s, axis=-1, keepdims=True), s_new), sink)
        pr = jnp.exp(s - m)
        p_new = jnp.exp(s_new - m)
        den = jnp.sum(pr, axis=-1, keepdims=True) + p_new + jnp.exp(sink - m)
        o = jnp.einsum("brp,bdp->brd", pr.astype(BF), vt.astype(BF), preferred_element_type=F32)
        o = (o + p_new * v_low[g][:, None, 0:HEAD_DIM]) / den
        of_s[:, Q_REP * g:Q_REP * (g + 1), :] = jnp.concatenate([o, jnp.zeros(o.shape, F32)], axis=-1)
        for b, tile in enumerate(_shift_in_column(kt, k_low[g], bb)):
            kto_ref[b, g] = tile
        for b, tile in enumerate(_shift_in_column(vt, v_low[g], bb)):
            vto_ref[b, g] = tile
    att_tiles = []
    for j in range(N_HEADS // 2):
        second = pltpu.roll(of_s[:, 2 * j + 1, :], HEAD_DIM, 1)
        att_tiles.append(jnp.where(lo, of_s[:, 2 * j, :], second))
    o_att = jnp.concatenate(att_tiles, axis=1)

    oc_s[pl.ds(r0, bb), 0:GM_WIDTH] = o_gm.astype(BF)
    oc_s[pl.ds(r0, bb), GM_WIDTH:GM_WIDTH + CONV_WIDTH] = o_cv.astype(BF)
    oc_s[pl.ds(r0, bb), GM_WIDTH + CONV_WIDTH:GM_WIDTH + CONV_WIDTH + ATT_WIDTH] = o_att.astype(BF)

    @pl.when(i == pl.num_programs(0) - 1)
    def _():
        acc = _sigmoid(zg_s[:, 0:D_MODEL]) * _dot(oc_s[:, 0:GM_WIDTH], wb_ref[0])
        acc = acc + _sigmoid(zg_s[:, D_MODEL:2 * D_MODEL]) * _dot(
            oc_s[:, GM_WIDTH:GM_WIDTH + CONV_WIDTH], wb_ref[1])
        acc = acc + _sigmoid(zg_s[:, 2 * D_MODEL:3 * D_MODEL]) * _dot(
            oc_s[:, GM_WIDTH + CONV_WIDTH:GM_WIDTH + CONV_WIDTH + ATT_WIDTH], wb_ref[2])
        gt1 = mod_ref[...][:, 2 * D_MODEL:3 * D_MODEL]
        x1_ref[...] = x_ref[...] + gt1 * _dot(acc.astype(BF), wout_ref[...])


def _sample_mixer_call(layer, z, xs, mod, kt, vt, st, p, bb):
    n = xs.shape[0]
    lsel3 = lambda i: (layer, 0, 0)
    cache_block = (bb, N_KV_HEADS, HEAD_DIM, WINDOW)
    kernel = functools.partial(_sample_mixer_kernel, bb=bb)
    return pl.pallas_call(
        kernel,
        out_shape=(
            jax.ShapeDtypeStruct((n, D_MODEL), F32),
            jax.ShapeDtypeStruct((n, N_KV_HEADS, HEAD_DIM, WINDOW), F32),
            jax.ShapeDtypeStruct((n, N_KV_HEADS, HEAD_DIM, WINDOW), F32),
            jax.ShapeDtypeStruct((CONV_K - 1, n, CONV_WIDTH), F32),
            jax.ShapeDtypeStruct((n, GM_WIDTH), F32),
        ),
        grid=(n // bb,),
        in_specs=[
            pl.BlockSpec((bb, IN_WIDTH), lambda i: (i, 0)),
            pl.BlockSpec((n, D_MODEL), lambda i: (0, 0)),
            pl.BlockSpec((None, n, 3 * D_MODEL), lsel3),
            pl.BlockSpec((None,) + cache_block, lambda i: (layer, i, 0, 0, 0)),
            pl.BlockSpec((None,) + cache_block, lambda i: (layer, i, 0, 0, 0)),
            pl.BlockSpec((None, CONV_K - 1, bb, CONV_WIDTH), lambda i: (layer, 0, i, 0)),
            pl.BlockSpec((None, 1, GM_WIDTH), lsel3),
            pl.BlockSpec((None, 1, GM_WIDTH), lsel3),
            pl.BlockSpec((None, 1, GM_WIDTH), lsel3),
            pl.BlockSpec((None, CONV_K, CONV_WIDTH), lsel3),
            pl.BlockSpec((None, 1, CONV_WIDTH), lsel3),
            pl.BlockSpec((None, 1, CONV_WIDTH), lsel3),
            pl.BlockSpec((None, 1, CONV_WIDTH), lsel3),
            pl.BlockSpec((None, 1, ATT_WIDTH), lsel3),
            pl.BlockSpec((None, 1, LANES), lsel3),
            pl.BlockSpec((1, LANES), lambda i: (0, 0)),
            pl.BlockSpec((1, LANES), lambda i: (0, 0)),
            pl.BlockSpec((None, N_HEADS, LANES), lsel3),
            pl.BlockSpec((ATT_WIDTH, ATT_WIDTH), lambda i: (0, 0)),
            pl.BlockSpec((None, N_BRANCH, GM_WIDTH, D_MODEL), lambda i: (layer, 0, 0, 0)),
            pl.BlockSpec((None, D_MODEL, D_MODEL), lsel3),
        ],
        out_specs=(
            pl.BlockSpec((n, D_MODEL), lambda i: (0, 0)),
            pl.BlockSpec(cache_block, lambda i: (i, 0, 0, 0)),
            pl.BlockSpec(cache_block, lambda i: (i, 0, 0, 0)),
            pl.BlockSpec((CONV_K - 1, bb, CONV_WIDTH), lambda i: (0, i, 0)),
            pl.BlockSpec((bb, GM_WIDTH), lambda i: (i, 0)),
        ),
        scratch_shapes=[
            pltpu.VMEM((bb, N_HEADS, LANES), F32),
            pltpu.VMEM((bb, N_HEADS, LANES), F32),
            pltpu.VMEM((n, GM_WIDTH + CONV_WIDTH + ATT_WIDTH), BF),
            pltpu.VMEM((n, N_BRANCH * D_MODEL), F32),
        ],
        compiler_params=pltpu.CompilerParams(
            dimension_semantics=("arbitrary",), vmem_limit_bytes=VMEM_LIMIT),
        name=f"sample_mixers_l{layer}",
    )(z, xs, mod, kt, vt, st, p["gm_g"], p["gm_w0"], p["gm_b0"], p["conv_dw"], p["conv_b"],
      p["conv_ln_g"], p["conv_ln_b"], p["qn"], p["kn"], p["rope_c1"], p["rope_s1"],
      p["sink_lanes"], p["blockdiag"], p["w_branch"], p["w_out"])


def _rope_lane_tables():
    half = ROT_DIM // 2
    freqs = jnp.exp(-math.log(ROPE_THETA) * jnp.arange(half, dtype=F32) * (2.0 / ROT_DIM))
    rest = jnp.zeros((HEAD_DIM - ROT_DIM,), F32)
    freq64 = jnp.concatenate([freqs, freqs, rest])
    sign64 = jnp.concatenate([-jnp.ones((half,), F32), jnp.ones((half,), F32), rest])
    reps = LANES // HEAD_DIM
    return jnp.tile(freq64, reps)[None, :], jnp.tile(sign64, reps)[None, :]


def _cos_sin(pos, lane_freq):
    ang = pos.astype(F32)[:, None] * lane_freq
    return jnp.cos(ang), jnp.sin(ang)


def _band_bias():
    i = np.arange(WINDOW)[:, None]
    j = np.arange(2 * WINDOW)[None, :]
    band = (j >= i) & (j <= i + WINDOW)
    first = band & (j >= WINDOW)
    out = np.where(np.stack([first, band]), 0.0, NEG_BIG).astype(np.float32)
    return jnp.asarray(out)


def _blockdiag():
    idx = np.arange(ATT_WIDTH) // HEAD_DIM
    return jnp.asarray((idx[:, None] == idx[None, :]).astype(np.float32), dtype=BF)


def _expand_matrix():
    k = np.arange(LANES)[:, None]
    e = (np.arange(EXPERTS_PER_GROUP * EXPERT_FF) // EXPERT_FF)[None, :]
    return jnp.asarray(((k == e) | (k == e + EXPERTS_PER_GROUP)).astype(np.float32), dtype=BF)


def _dup_heads(w):
    h0, h1 = w[..., :HEAD_DIM], w[..., HEAD_DIM:]
    return jnp.concatenate([h0, h0, h1, h1], axis=-1)


def _prepare(norm1_g, norm2_g, w_in, gm_norm_g, gm_ws, gm_b, conv_dw, conv_b, conv_ln_g,
             conv_ln_b, q_norm_g, k_norm_g, attn_sinks, w_branch, w_out, router_w, router_b,
             moe_w1, moe_w3, moe_w2, seq, tm):
    w_kv2 = jnp.concatenate([_dup_heads(w_in[:, :, OFF_K:OFF_V]),
                             _dup_heads(w_in[:, :, OFF_V:OFF_GATE])], axis=-1).astype(BF)
    rw = router_w.astype(F32)
    rw_hi = rw.astype(BF)
    rw_lo = (rw - rw_hi.astype(F32)).astype(BF)
    rw2 = jnp.concatenate(
        [rw_hi, rw_lo, jnp.zeros((D_MODEL, LANES - 2 * N_EXPERTS), BF)], axis=1)
    lane_freq, lane_sign = _rope_lane_tables()
    cb, sb = _cos_sin(jnp.arange(seq // tm, dtype=jnp.int32) * tm, lane_freq)
    co, so = _cos_sin(jnp.arange(tm, dtype=jnp.int32), lane_freq)
    c1, s1 = _cos_sin(PAST_LEN + jnp.arange(1, dtype=jnp.int32), lane_freq)
    row3 = lambda a: a.reshape(DEPTH, 1, a.shape[-1])
    return {
        "n1g": row3(norm1_g), "n2g": row3(norm2_g), "w_in": w_in.astype(BF), "w_kv2": w_kv2,
        "gm_g": row3(gm_norm_g), "gm_ws": gm_ws,
        "gm_bias": jnp.repeat(jnp.swapaxes(gm_b, 1, 2), LANES, axis=2),
        "gm_w0": jnp.repeat(gm_ws[:, :, 0, 0], LANES, axis=1).reshape(DEPTH, 1, GM_WIDTH),
        "gm_b0": jnp.repeat(gm_b[:, :, 0], LANES, axis=1).reshape(DEPTH, 1, GM_WIDTH),
        "conv_dw": conv_dw, "conv_b": row3(conv_b), "conv_ln_g": row3(conv_ln_g),
        "conv_ln_b": row3(conv_ln_b),
        "qn": row3(jnp.tile(q_norm_g, (1, N_HEADS))),
        "kn": row3(jnp.tile(k_norm_g, (1, 2 * N_KV_HEADS))),
        "sinks": attn_sinks,
        "sink_lanes": jnp.broadcast_to(attn_sinks[:, :, None], (DEPTH, N_HEADS, LANES)),
        "rope_base": jnp.concatenate([cb, sb], axis=1)[:, None, :],
        "rope_off": jnp.concatenate([co, so], axis=1),
        "rope_sign": lane_sign, "rope_c1": c1, "rope_s1": s1 * lane_sign,
        "band_bias": _band_bias(), "blockdiag": _blockdiag(),
        "w_branch": w_branch.astype(BF), "w_out": w_out.astype(BF),
        "rw2": rw2,
        "rb": router_b.astype(F32).reshape(N_EXPERTS, 1),
        "expand": _expand_matrix(),
        "moe_w1": moe_w1.astype(BF), "moe_w3": moe_w3.astype(BF),
        "moe_w2": moe_w2.astype(BF).reshape(DEPTH, N_GROUPS, EXPERTS_PER_GROUP * EXPERT_FF, D_MODEL),
    }


def kernel(x_prompt, x_sample, cache_win_k, cache_win_v, state_conv, c_prompt, c_sample, norm1_g, norm2_g, w_ada, b_ada, w_in, gm_norm_g, gm_ws, gm_b, conv_dw, conv_b, conv_ln_g, conv_ln_b, q_norm_g, k_norm_g, attn_sinks, w_branch, w_out, router_w, router_b, moe_w1, moe_w3, moe_w2):
    nb, seq, _ = x_prompt.shape
    ns = x_sample.shape[0]
    tm = min(MIX_TILE, seq)
    tmoe = min(MOE_TILE, seq)
    p = _prepare(norm1_g, norm2_g, w_in, gm_norm_g, gm_ws, gm_b, conv_dw, conv_b, conv_ln_g,
                 conv_ln_b, q_norm_g, k_norm_g, attn_sinks, w_branch, w_out, router_w, router_b,
                 moe_w1, moe_w3, moe_w2, seq, tm)
    assert ns == PROMPT_MOD_ROW and nb <= ADA_ROWS - PROMPT_MOD_ROW
    c_all = jnp.concatenate(
        [c_sample, c_prompt, jnp.zeros((ADA_ROWS - nb - ns, D_MODEL), F32)], axis=0)
    mod = _ada_call(c_all, w_ada, b_ada)

    kt = jnp.transpose(cache_win_k, (0, 1, 3, 4, 2))
    vt = jnp.transpose(cache_win_v, (0, 1, 3, 4, 2))
    st = jnp.transpose(state_conv, (0, 2, 1, 3))

    xp = x_prompt
    xs = x_sample.reshape(ns, D_MODEL)
    kp_l, vp_l, cp_l, ks_l, vs_l, cs_l, gs_l = [], [], [], [], [], [], []
    for l in range(DEPTH):
        xp, kp, vp, cp = _mixer_call(l, xp, mod, p, tm)
        xp = _moe_call(l, xp.reshape(nb * seq, D_MODEL), mod, seq // tmoe, p, tmoe,
                       f"prompt_moe_l{l}").reshape(nb, seq, D_MODEL)
        kp_l.append(kp)
        vp_l.append(vp)
        cp_l.append(cp)

        z = _sample_proj_call(l, xs, mod, p)
        xs, kn, vn, cn, gv = _sample_mixer_call(l, z, xs, mod, kt, vt, st, p, SAMPLE_BLOCK)
        xs = _moe_call(l, xs, mod, None, p, ns, f"sample_moe_l{l}")
        ks_l.append(kn)
        vs_l.append(vn)
        cs_l.append(cn)
        gs_l.append(gv)

    kv_shape = (DEPTH, nb, WINDOW, N_KV_HEADS, HEAD_DIM)
    return (xp, xs.reshape(ns, 1, D_MODEL),
            jnp.stack(kp_l).reshape(kv_shape), jnp.stack(vp_l).reshape(kv_shape),
            jnp.stack(cp_l),
            jnp.transpose(jnp.stack(ks_l), (0, 1, 4, 2, 3)),
            jnp.transpose(jnp.stack(vs_l), (0, 1, 4, 2, 3)),
            jnp.transpose(jnp.stack(cs_l), (0, 2, 1, 3)),
            jnp.stack(gs_l).reshape(DEPTH, ns, 1, GM_WIDTH))
```

```python
import functools
import math

import jax
import jax.numpy as jnp
import numpy as np
from jax import lax
from jax.experimental import pallas as pl
from jax.experimental.pallas import tpu as pltpu

F32 = jnp.float32
BF = jnp.bfloat16

D_MODEL = 1024
DEPTH = 2
PAST_LEN = 8192
CHUNK = 128
GM_GROUPS = 4
GM_WIDTH = 512
CONV_WIDTH = 512
CONV_K = 31
N_HEADS = 8
N_KV_HEADS = 2
Q_REP = N_HEADS // N_KV_HEADS
HEAD_DIM = 64
ATT_WIDTH = N_HEADS * HEAD_DIM
KV_WIDTH = N_KV_HEADS * HEAD_DIM
WINDOW = 128
ROPE_THETA = 500000.0
ROT_DIM = HEAD_DIM // 4
N_BRANCH = 3
N_EXPERTS = 16
EXPERTS_PER_GROUP = 4
N_GROUPS = N_EXPERTS // EXPERTS_PER_GROUP
EXPERT_FF = 256
EPS = 1e-6

OFF_GM_U = 0
OFF_GM_V = 512
OFF_CV_A = 1024
OFF_CV_G = 1536
OFF_Q = 2048
OFF_K = 2560
OFF_V = 2688
OFF_GATE = 2816
IN_WIDTH = OFF_GATE + N_BRANCH * D_MODEL

LANES = 128
SUBLANES = 8
HALO = 32
MIX_TILE = 512
MOE_TILE = 512
MOE_CAP = 144
ADA_ROWS = 136
PROMPT_MOD_ROW = 128
ADA_BLOCK = 1536
PROJ_BLOCK = IN_WIDTH // 2
SAMPLE_BLOCK = 32
VMEM_LIMIT = 56 * 1024 * 1024
NEG_BIG = -1e30


def _dot(a, b):
    return jnp.dot(a, b, preferred_element_type=F32)


def _dot_nt(a, b):
    return lax.dot_general(a, b, (((1,), (1,)), ((), ())), preferred_element_type=F32)


def _sigmoid(x):
    return 0.5 * jnp.tanh(0.5 * x) + 0.5


def _silu(x):
    return x * _sigmoid(x)


GELU_C = 0.7978845608028654
GELU_K = 0.044715


def _gelu_half(xh):
    return xh + xh * jnp.tanh(xh * (2.0 * GELU_C + (8.0 * GELU_C * GELU_K) * (xh * xh)))


def _sigmoid_times(zh, dh):
    return jnp.tanh(zh) * dh + dh


def _split_bf16(x):
    hi = x.astype(BF)
    lo = (x - hi.astype(F32)).astype(BF)
    return hi, lo


def _modulate(x, g, shift, scale):
    ms = jnp.mean(x * x, axis=-1, keepdims=True)
    return (x * lax.rsqrt(ms + EPS)) * (g * (1.0 + scale)) + shift


def _head_norm(x, blockdiag, g):
    ssum = _dot((x * x).astype(BF), blockdiag)
    return x * lax.rsqrt(ssum * (1.0 / HEAD_DIM) + EPS) * g


def _rope(x, c, s):
    width = x.shape[-1]
    reps = width // LANES
    cc = jnp.concatenate([c] * reps, axis=-1)
    ss = jnp.concatenate([s] * reps, axis=-1)
    lane = lax.broadcasted_iota(jnp.int32, x.shape, x.ndim - 1) % HEAD_DIM
    partner = jnp.where(lane < ROT_DIM // 2,
                        pltpu.roll(x, width - ROT_DIM // 2, x.ndim - 1),
                        pltpu.roll(x, ROT_DIM // 2, x.ndim - 1))
    return x * cc + partner * ss


def _layer_norm_silu(y, gh, bh):
    mu = jnp.mean(y, axis=-1, keepdims=True)
    yc = y - mu
    var = jnp.mean(yc * yc, axis=-1, keepdims=True)
    h = yc * lax.rsqrt(var + EPS) * gh + bh
    return h * jnp.tanh(h) + h


def _low_half(shape):
    return lax.broadcasted_iota(jnp.int32, shape, len(shape) - 1) % LANES < HEAD_DIM


def _const_spec(shape, index_map):
    return pl.BlockSpec(shape, index_map, pipeline_mode=pl.Buffered(1))


def _ada_kernel(c_ref, w_ref, b_ref, o_ref):
    s = _silu(c_ref[...]).astype(BF)
    o_ref[...] = _dot(s, w_ref[...].astype(BF)) + b_ref[...]


def _ada_call(c_all, w_ada, b_ada):
    nb = (6 * D_MODEL) // ADA_BLOCK
    return pl.pallas_call(
        _ada_kernel,
        out_shape=jax.ShapeDtypeStruct((DEPTH, ADA_ROWS, 6 * D_MODEL), F32),
        grid=(DEPTH, nb),
        in_specs=[
            pl.BlockSpec((ADA_ROWS, D_MODEL), lambda l, j: (0, 0)),
            pl.BlockSpec((None, D_MODEL, ADA_BLOCK), lambda l, j: (l, 0, j)),
            pl.BlockSpec((None, 1, ADA_BLOCK), lambda l, j: (l, 0, j)),
        ],
        out_specs=pl.BlockSpec((None, ADA_ROWS, ADA_BLOCK), lambda l, j: (l, 0, j)),
        compiler_params=pltpu.CompilerParams(
            dimension_semantics=("arbitrary", "arbitrary")),
        name="ada_mod",
    )(c_all, w_ada, b_ada.reshape(DEPTH, 1, 6 * D_MODEL))


def _mixer_kernel(sinks_ref, x_ref, mod_ref, n1g_ref, win_ref, wkv_ref, gmg_ref, gmws_ref, gmb_ref,
                  cdw_ref, cb_ref, clg_ref, clb_ref, qn_ref, kn_ref, ropeb_ref, ropeo_ref,
                  sign_ref, bias_ref, bd_ref, wb_ref, wout_ref,
                  x1_ref, kwin_ref, vwin_ref, cst_ref,
                  kd_s, vd_s, abuf, oatt_s, hb_s, z_s, *, layer, tm):
    t = pl.program_id(1)
    last = pl.num_programs(1) - 1
    nblk = tm // WINDOW

    @pl.when(t == 0)
    def _():
        kd_s[0:WINDOW, :] = jnp.zeros((WINDOW, 2 * LANES), BF)
        vd_s[0:WINDOW, :] = jnp.zeros((WINDOW, 2 * LANES), BF)
        abuf[0:HALO, :] = jnp.zeros((HALO, CONV_WIDTH), F32)
        abuf[HALO + tm:HALO + tm + SUBLANES, :] = jnp.zeros((SUBLANES, CONV_WIDTH), F32)

    x = x_ref[...]
    mod = mod_ref[pl.ds(pl.program_id(0), 1), :]
    sh1 = mod[:, 0:D_MODEL]
    sc1 = mod[:, D_MODEL:2 * D_MODEL]
    gt1 = mod[:, 2 * D_MODEL:3 * D_MODEL]
    hb_s[...] = _modulate(x, n1g_ref[...], sh1, sc1).astype(BF)

    def proj(off, width):
        z_s[:, off:off + width] = _dot(hb_s[...], win_ref[:, off:off + width])

    def zcols(off, width):
        return z_s[:, off:off + width]

    proj(OFF_CV_A, 2 * CONV_WIDTH)
    a = _sigmoid_times(zcols(OFF_CV_G, CONV_WIDTH), zcols(OFF_CV_A, CONV_WIDTH))
    abuf[HALO:HALO + tm, :] = a
    first_off = HALO - (CONV_K - 1)

    def conv_tile(ci):
        lanes = slice(ci * LANES, (ci + 1) * LANES)
        yt = cb_ref[:, lanes]
        for b in range(SUBLANES):
            part = None
            for off in range(b, HALO + 1, SUBLANES):
                if off < first_off:
                    continue
                term = (cdw_ref[off - first_off:off - first_off + 1, lanes]
                        * abuf[pl.ds(off - b, tm + SUBLANES), lanes])
                part = term if part is None else part + term
            yt = yt + part[b:b + tm, :]
        return yt

    proj(OFF_GM_U, 2 * GM_WIDTH)
    proj(OFF_Q, ATT_WIDTH)
    z_s[:, IN_WIDTH:IN_WIDTH + 4 * LANES] = _dot(hb_s[...], wkv_ref[...])
    proj(OFF_GATE, N_BRANCH * D_MODEL)
    y = jnp.concatenate([conv_tile(ci) for ci in range(CONV_WIDTH // LANES)], axis=1)
    abuf[0:HALO, :] = abuf[tm:tm + HALO, :]
    o_cv = _layer_norm_silu(y, clg_ref[...], clb_ref[...]).astype(BF)
    acc = _sigmoid_times(zcols(OFF_GATE + D_MODEL, D_MODEL), _dot(o_cv, wb_ref[1]))

    u = _gelu_half(zcols(OFF_GM_U, GM_WIDTH))
    gv = _gelu_half(zcols(OFF_GM_V, GM_WIDTH))
    v = gv * lax.rsqrt(jnp.mean(gv * gv, axis=-1, keepdims=True) + EPS) * gmg_ref[...]
    vb = v.astype(BF)
    row = lax.broadcasted_iota(jnp.int32, (CHUNK, CHUNK), 0)
    col = lax.broadcasted_iota(jnp.int32, (CHUNK, CHUNK), 1)
    ws = [jnp.where(row >= col, gmws_ref[g], 0.0).astype(BF) for g in range(GM_GROUPS)]
    gmb = gmb_ref[...]
    chunks = []
    for c in range(nblk):
        parts = [_dot(ws[g], vb[c * CHUNK:(c + 1) * CHUNK, g * LANES:(g + 1) * LANES])
                 for g in range(GM_GROUPS)]
        chunks.append(jnp.concatenate(parts, axis=1) + gmb)
    o_gm = (u * jnp.concatenate(chunks, axis=0)).astype(BF)
    acc = acc + _sigmoid_times(zcols(OFF_GATE, D_MODEL), _dot(o_gm, wb_ref[0]))

    rbase = ropeb_ref[...]
    roff = ropeo_ref[...]
    cb_, sb_ = rbase[:, 0:LANES], rbase[:, LANES:2 * LANES]
    co_, so_ = roff[:, 0:LANES], roff[:, LANES:2 * LANES]
    rc = cb_ * co_ - sb_ * so_
    rs = (sb_ * co_ + cb_ * so_) * sign_ref[...]
    bd = bd_ref[...]
    q = _rope(_head_norm(zcols(OFF_Q, ATT_WIDTH), bd, qn_ref[...]), rc, rs)
    qb = (q * (HEAD_DIM ** -0.5)).astype(BF)
    kd = _rope(_head_norm(zcols(IN_WIDTH, 2 * LANES), bd[0:2 * LANES, 0:2 * LANES], kn_ref[...]),
               rc, rs)
    vd = zcols(IN_WIDTH + 2 * LANES, 2 * LANES)
    kd_s[WINDOW:WINDOW + tm, :] = kd.astype(BF)
    vd_s[WINDOW:WINDOW + tm, :] = vd.astype(BF)

    lo_q = _low_half((WINDOW, LANES))
    zero_q = jnp.zeros((WINDOW, LANES), BF)
    band = bias_ref[1]
    for bi in range(nblk):
        bias = jnp.where(t == 0, bias_ref[0], band) if bi == 0 else band
        bias4 = jnp.concatenate([bias] * Q_REP, axis=0)
        for g in range(N_KV_HEADS):
            tiles = [qb[bi * WINDOW:(bi + 1) * WINDOW, (2 * g + j) * LANES:(2 * g + j + 1) * LANES]
                     for j in range(2)]
            qs = jnp.concatenate([jnp.where(lo_q, tiles[0], zero_q), jnp.where(lo_q, zero_q, tiles[0]),
                                  jnp.where(lo_q, tiles[1], zero_q), jnp.where(lo_q, zero_q, tiles[1])],
                                 axis=0)
            keys = kd_s[bi * WINDOW:(bi + 2) * WINDOW, g * LANES:(g + 1) * LANES]
            vals = vd_s[bi * WINDOW:(bi + 2) * WINDOW, g * LANES:(g + 1) * LANES]
            s = _dot_nt(qs, keys) + bias4
            outs = []
            for hh in range(Q_REP):
                sink = sinks_ref[layer, Q_REP * g + hh]
                sh = s[hh * WINDOW:(hh + 1) * WINDOW, :]
                m = jnp.maximum(jnp.max(sh, axis=-1, keepdims=True), sink)
                p = jnp.exp(sh - m)
                den = jnp.sum(p, axis=-1, keepdims=True) + jnp.exp(sink - m)
                outs.append(_dot(p.astype(BF), vals) / den)
            for j in range(2):
                oatt_s[bi * WINDOW:(bi + 1) * WINDOW, (2 * g + j) * LANES:(2 * g + j + 1) * LANES] = (
                    jnp.where(lo_q, outs[2 * j], outs[2 * j + 1]).astype(BF))
    kd_s[0:WINDOW, :] = kd_s[tm:tm + WINDOW, :]
    vd_s[0:WINDOW, :] = vd_s[tm:tm + WINDOW, :]
    acc = acc + _sigmoid_times(zcols(OFF_GATE + 2 * D_MODEL, D_MODEL), _dot(oatt_s[...], wb_ref[2]))

    x1_ref[...] = x + gt1 * _dot(acc.astype(BF), wout_ref[...])

    @pl.when(t == last)
    def _():
        kl = kd[tm - WINDOW:tm, :]
        vl = vd[tm - WINDOW:tm, :]
        kwin_ref[...] = jnp.where(lo_q, kl[:, 0:LANES], kl[:, LANES:2 * LANES])
        vwin_ref[...] = jnp.where(lo_q, vl[:, 0:LANES], vl[:, LANES:2 * LANES])
        cst_ref[...] = a[tm - (CONV_K - 1):tm, :]


def _mixer_call(layer, x, mod, p, tm):
    nb, seq, _ = x.shape
    nt = seq // tm
    lsel3 = lambda b, t: (layer, 0, 0)
    kernel = functools.partial(_mixer_kernel, layer=layer, tm=tm)
    return pl.pallas_call(
        kernel,
        out_shape=(
            jax.ShapeDtypeStruct((nb, seq, D_MODEL), F32),
            jax.ShapeDtypeStruct((nb, WINDOW, KV_WIDTH), F32),
            jax.ShapeDtypeStruct((nb, WINDOW, KV_WIDTH), F32),
            jax.ShapeDtypeStruct((nb, CONV_K - 1, CONV_WIDTH), F32),
        ),
        grid=(nb, nt),
        in_specs=[
            pl.BlockSpec(memory_space=pltpu.SMEM),
            pl.BlockSpec((None, tm, D_MODEL), lambda b, t: (b, t, 0)),
            pl.BlockSpec((None, SUBLANES, 3 * D_MODEL),
                         lambda b, t: (layer, PROMPT_MOD_ROW // SUBLANES, 0)),
            _const_spec((None, 1, D_MODEL), lsel3),
            _const_spec((None, D_MODEL, IN_WIDTH), lsel3),
            _const_spec((None, D_MODEL, 4 * LANES), lsel3),
            _const_spec((None, 1, GM_WIDTH), lsel3),
            _const_spec((None, GM_GROUPS, CHUNK, CHUNK), lambda b, t: (layer, 0, 0, 0)),
            _const_spec((None, CHUNK, GM_WIDTH), lsel3),
            _const_spec((None, CONV_K, CONV_WIDTH), lsel3),
            _const_spec((None, 1, CONV_WIDTH), lsel3),
            _const_spec((None, 1, CONV_WIDTH), lsel3),
            _const_spec((None, 1, CONV_WIDTH), lsel3),
            _const_spec((None, 1, ATT_WIDTH), lsel3),
            _const_spec((None, 1, 2 * LANES), lsel3),
            pl.BlockSpec((None, 1, 2 * LANES), lambda b, t: (t, 0, 0)),
            _const_spec((tm, 2 * LANES), lambda b, t: (0, 0)),
            _const_spec((1, LANES), lambda b, t: (0, 0)),
            _const_spec((2, WINDOW, 2 * WINDOW), lambda b, t: (0, 0, 0)),
            _const_spec((ATT_WIDTH, ATT_WIDTH), lambda b, t: (0, 0)),
            _const_spec((None, N_BRANCH, GM_WIDTH, D_MODEL), lambda b, t: (layer, 0, 0, 0)),
            _const_spec((None, D_MODEL, D_MODEL), lsel3),
        ],
        out_specs=(
            pl.BlockSpec((None, tm, D_MODEL), lambda b, t: (b, t, 0)),
            pl.BlockSpec((None, WINDOW, KV_WIDTH), lambda b, t: (b, 0, 0)),
            pl.BlockSpec((None, WINDOW, KV_WIDTH), lambda b, t: (b, 0, 0)),
            pl.BlockSpec((None, CONV_K - 1, CONV_WIDTH), lambda b, t: (b, 0, 0)),
        ),
        scratch_shapes=[
            pltpu.VMEM((WINDOW + tm, 2 * LANES), BF),
            pltpu.VMEM((WINDOW + tm, 2 * LANES), BF),
            pltpu.VMEM((HALO + tm + SUBLANES, CONV_WIDTH), F32),
            pltpu.VMEM((tm, ATT_WIDTH), BF),
            pltpu.VMEM((tm, D_MODEL), BF),
            pltpu.VMEM((tm, IN_WIDTH + 4 * LANES), F32),
        ],
        compiler_params=pltpu.CompilerParams(
            dimension_semantics=("arbitrary", "arbitrary"),
            vmem_limit_bytes=VMEM_LIMIT),
        name=f"prompt_mixers_l{layer}",
    )(p["sinks"], x, mod, p["n1g"], p["w_in"], p["w_kv2"], p["gm_g"], p["gm_ws"], p["gm_bias"],
      p["conv_dw"], p["conv_b"], p["conv_ln_g"], p["conv_ln_b"], p["qn"], p["kn"],
      p["rope_base"], p["rope_off"], p["rope_sign"], p["band_bias"], p["blockdiag"],
      p["w_branch"], p["w_out"])


def _route_rows(logits_t, rb_ref):
    scores = _sigmoid(logits_t)
    biased = scores + rb_ref[...]
    rows = lambda a, gi: [a[EXPERTS_PER_GROUP * gi + k:EXPERTS_PER_GROUP * gi + k + 1, :]
                          for k in range(EXPERTS_PER_GROUP)]
    best = None
    idx = None
    for gi in range(N_GROUPS):
        b = rows(biased, gi)
        hi1, lo1 = jnp.maximum(b[0], b[1]), jnp.minimum(b[0], b[1])
        hi2, lo2 = jnp.maximum(b[2], b[3]), jnp.minimum(b[2], b[3])
        gs = jnp.maximum(hi1, hi2) + jnp.maximum(jnp.minimum(hi1, hi2), jnp.maximum(lo1, lo2))
        if gi == 0:
            best, idx = gs, jnp.zeros(gs.shape, jnp.int32)
        else:
            better = gs > best
            idx = jnp.where(better, gi, idx)
            best = jnp.where(better, gs, best)
    bsel = rows(biased, 0)
    ssel = rows(scores, 0)
    for gi in range(1, N_GROUPS):
        bg, sg = rows(biased, gi), rows(scores, gi)
        pick = idx == gi
        bsel = [jnp.where(pick, bg[k], bsel[k]) for k in range(EXPERTS_PER_GROUP)]
        ssel = [jnp.where(pick, sg[k], ssel[k]) for k in range(EXPERTS_PER_GROUP)]
    chosen = []
    for k in range(EXPERTS_PER_GROUP):
        rank = jnp.zeros(idx.shape, jnp.int32)
        for k2 in range(EXPERTS_PER_GROUP):
            if k2 == k:
                continue
            beats = (bsel[k2] > bsel[k]) | ((bsel[k2] == bsel[k]) & (k2 < k))
            rank = rank + beats.astype(jnp.int32)
        chosen.append(jnp.where(rank < 2, ssel[k], 0.0))
    den = chosen[0] + chosen[1] + chosen[2] + chosen[3]
    return idx, [c / den for c in chosen]


def _moe_route(x, mod, n2g_ref, rw2_ref, rb_ref, upper_ref, h_buf, rt_buf, cnt_s, buf, tm):
    h2 = _modulate(x, n2g_ref[...], mod[:, 0:D_MODEL], mod[:, D_MODEL:2 * D_MODEL])
    hi, lo = _split_bf16(h2)
    rw2 = rw2_ref[...]
    lt = (_dot(hi, rw2) + _dot(lo, rw2)).T
    idx, comb = _route_rows(lt[0:N_EXPERTS, :] + lt[N_EXPERTS:2 * N_EXPERTS, :], rb_ref)

    onehot = [(idx == g).astype(F32) for g in range(N_GROUPS)]
    oh8 = jnp.concatenate(onehot + [jnp.zeros((SUBLANES - N_GROUPS, tm), F32)], axis=0)
    prefix = _dot(oh8.astype(BF), upper_ref[...])
    slot = onehot[0] * prefix[0:1, :]
    for g in range(1, N_GROUPS):
        slot = slot + onehot[g] * prefix[g:g + 1, :]
    rt_buf[0:1, :] = idx
    rt_buf[1:2, :] = slot.astype(jnp.int32)
    for g in range(N_GROUPS):
        cnt_s[buf, g] = jnp.sum(onehot[g]).astype(jnp.int32)

    chi = [c.astype(BF).astype(F32) for c in comb]
    clo = [c - h for c, h in zip(comb, chi)]
    crow = jnp.concatenate(chi + clo + [jnp.zeros((LANES - 2 * EXPERTS_PER_GROUP, tm), F32)], axis=0)
    h_buf[:, 0:D_MODEL] = hi
    h_buf[:, D_MODEL:D_MODEL + LANES] = crow.T.astype(BF)


def _moe_experts(x_ref, gt2, expand_ref, w1_ref, w3_ref, w2_ref, o_ref, h_buf, rt_buf, cnt_s, buf,
                 tm, cap, overflow):
    def dispatch(g, j):
        want = lax.broadcasted_iota(jnp.int32, (cap, tm), 0) + j * cap
        hit = (rt_buf[1:2, :] == want) & (rt_buf[0:1, :] == g)
        return jnp.where(hit, 1.0, 0.0).astype(BF)

    def run_experts(g, pmat):
        gathered = _dot(pmat, h_buf[...])
        hg = gathered[:, 0:D_MODEL].astype(BF)
        cexp = _dot(gathered[:, D_MODEL:D_MODEL + LANES].astype(BF), expand_ref[...])
        parts = []
        for e in range(EXPERTS_PER_GROUP):
            a = _dot(hg, w1_ref[EXPERTS_PER_GROUP * g + e])
            b = _dot(hg, w3_ref[EXPERTS_PER_GROUP * g + e])
            parts.append(_silu(a) * b)
        act = (jnp.concatenate(parts, axis=1) * cexp).astype(BF)
        return _dot(act, w2_ref[g]).astype(BF)

    def scatter(pmat, y):
        return lax.dot_general(pmat, y, (((0,), (0,)), ((), ())), preferred_element_type=F32)

    if not overflow:
        pmats = [dispatch(g, 0) for g in range(N_GROUPS)]
        ys = [run_experts(g, pmats[g]) for g in range(N_GROUPS)]
        y_tile = scatter(jnp.concatenate(pmats, axis=0), jnp.concatenate(ys, axis=0))
        o_ref[...] = x_ref[...] + gt2 * y_tile
        return

    def group_body(g, carry):
        nblk = (cnt_s[buf, g] + (cap - 1)) // cap

        def block_body(j, carry2):
            pmat = dispatch(g, j)
            o_ref[...] += gt2 * scatter(pmat, run_experts(g, pmat))
            return carry2

        lax.fori_loop(1, nblk, block_body, 0)
        return carry

    lax.fori_loop(0, N_GROUPS, group_body, 0)


def _moe_kernel(x_ref, xn_ref, mod_ref, n2g_ref, rw2_ref, rb_ref, upper_ref, expand_ref,
                w1_ref, w3_ref, w2_ref, o_ref, h0_s, h1_s, rt0_s, rt1_s, cnt_s,
                *, tm, cap, tiles_per_row, n_tiles):
    route = functools.partial(_moe_route, n2g_ref=n2g_ref, rw2_ref=rw2_ref, rb_ref=rb_ref,
                              upper_ref=upper_ref, cnt_s=cnt_s, tm=tm)
    experts = functools.partial(_moe_experts, x_ref, expand_ref=expand_ref, w1_ref=w1_ref,
                                w3_ref=w3_ref, w2_ref=w2_ref, o_ref=o_ref, cnt_s=cnt_s, tm=tm, cap=cap)
    h_bufs = (h0_s, h1_s)
    rt_bufs = (rt0_s, rt1_s)
    if tiles_per_row is None:
        mod = mod_ref[...]
        route(x_ref[...], mod, h_buf=h0_s, rt_buf=rt0_s, buf=0)
        for overflow in (False, True):
            experts(mod[:, 2 * D_MODEL:3 * D_MODEL], h_buf=h0_s, rt_buf=rt0_s, buf=0,
                    overflow=overflow)
        return

    i = pl.program_id(0)
    mod_row = lambda tile: mod_ref[pl.ds(tile // tiles_per_row, 1), :]

    @pl.when(i == 0)
    def _():
        route(x_ref[...], mod_row(i), h_buf=h0_s, rt_buf=rt0_s, buf=0)

    def step(cur):
        nxt = 1 - cur
        gt2 = mod_row(i)[:, 2 * D_MODEL:3 * D_MODEL]
        experts(gt2, h_buf=h_bufs[cur], rt_buf=rt_bufs[cur], buf=cur, overflow=False)
        route(xn_ref[...], mod_row(jnp.minimum(i + 1, n_tiles - 1)),
              h_buf=h_bufs[nxt], rt_buf=rt_bufs[nxt], buf=nxt)
        experts(gt2, h_buf=h_bufs[cur], rt_buf=rt_bufs[cur], buf=cur, overflow=True)

    @pl.when(i % 2 == 0)
    def _():
        step(0)

    @pl.when(i % 2 == 1)
    def _():
        step(1)


def _moe_call(layer, x2d, mod, tiles_per_row, p, tm, name):
    n = x2d.shape[0]
    n_tiles = n // tm
    if tiles_per_row is None:
        assert n_tiles == 1
        mod_spec = pl.BlockSpec((None, n, 3 * D_MODEL), lambda i: (layer, 0, 1))
    else:
        mod_spec = pl.BlockSpec((None, SUBLANES, 3 * D_MODEL),
                                lambda i: (layer, PROMPT_MOD_ROW // SUBLANES, 1))
    lsel3 = lambda i: (layer, 0, 0)
    lsel4 = lambda i: (layer, 0, 0, 0)
    upper = np.triu(np.ones((tm, tm), np.float32), k=1)
    kernel = functools.partial(_moe_kernel, tm=tm, cap=MOE_CAP, tiles_per_row=tiles_per_row,
                               n_tiles=n_tiles)
    return pl.pallas_call(
        kernel,
        out_shape=jax.ShapeDtypeStruct((n, D_MODEL), F32),
        grid=(n_tiles,),
        in_specs=[
            pl.BlockSpec((tm, D_MODEL), lambda i: (i, 0)),
            pl.BlockSpec((tm, D_MODEL), lambda i: (jnp.minimum(i + 1, n_tiles - 1), 0)),
            mod_spec,
            _const_spec((None, 1, D_MODEL), lsel3),
            _const_spec((D_MODEL, LANES), lambda i: (0, 0)),
            _const_spec((N_EXPERTS, 1), lambda i: (0, 0)),
            _const_spec((tm, tm), lambda i: (0, 0)),
            _const_spec((LANES, EXPERTS_PER_GROUP * EXPERT_FF), lambda i: (0, 0)),
            _const_spec((None, N_EXPERTS, D_MODEL, EXPERT_FF), lsel4),
            _const_spec((None, N_EXPERTS, D_MODEL, EXPERT_FF), lsel4),
            _const_spec((None, N_GROUPS, EXPERTS_PER_GROUP * EXPERT_FF, D_MODEL), lsel4),
        ],
        out_specs=pl.BlockSpec((tm, D_MODEL), lambda i: (i, 0)),
        scratch_shapes=[
            pltpu.VMEM((tm, D_MODEL + LANES), BF),
            pltpu.VMEM((tm, D_MODEL + LANES), BF),
            pltpu.VMEM((SUBLANES, tm), jnp.int32),
            pltpu.VMEM((SUBLANES, tm), jnp.int32),
            pltpu.SMEM((2, N_GROUPS), jnp.int32),
        ],
        compiler_params=pltpu.CompilerParams(
            dimension_semantics=("arbitrary",),
            vmem_limit_bytes=VMEM_LIMIT),
        name=name,
    )(x2d, x2d, mod, p["n2g"], p["rw2"], p["rb"], jnp.asarray(upper, dtype=BF),
      p["expand"], p["moe_w1"], p["moe_w3"], p["moe_w2"])


def _sample_proj_kernel(x_ref, mod_ref, n1g_ref, w_ref, z_ref):
    mod = mod_ref[...]
    h = _modulate(x_ref[...], n1g_ref[...], mod[:, 0:D_MODEL], mod[:, D_MODEL:2 * D_MODEL])
    z_ref[...] = _dot(h.astype(BF), w_ref[...])


def _sample_proj_call(layer, xs, mod, p):
    n = xs.shape[0]
    return pl.pallas_call(
        _sample_proj_kernel,
        out_shape=jax.ShapeDtypeStruct((n, IN_WIDTH), F32),
        grid=(IN_WIDTH // PROJ_BLOCK,),
        in_specs=[
            pl.BlockSpec((n, D_MODEL), lambda j: (0, 0)),
            pl.BlockSpec((None, n, 3 * D_MODEL), lambda j: (layer, 0, 0)),
            pl.BlockSpec((None, 1, D_MODEL), lambda j: (layer, 0, 0)),
            pl.BlockSpec((None, D_MODEL, PROJ_BLOCK), lambda j: (layer, 0, j)),
        ],
        out_specs=pl.BlockSpec((n, PROJ_BLOCK), lambda j: (0, j)),
        compiler_params=pltpu.CompilerParams(dimension_semantics=("arbitrary",)),
        name=f"sample_proj_l{layer}",
    )(xs, mod, p["n1g"], p["w_in"])


def _shift_in_column(cache_t, new_rows, bb):
    flat = cache_t.reshape(bb * HEAD_DIM, WINDOW)
    shifted = pltpu.roll(flat, WINDOW - 1, 1).reshape(bb, HEAD_DIM, WINDOW)
    padded = jnp.concatenate([new_rows, jnp.zeros((LANES - bb, LANES), F32)], axis=0)
    new_t = padded.T
    is_last = lax.broadcasted_iota(jnp.int32, (HEAD_DIM, WINDOW), 1) == WINDOW - 1
    out = []
    for b in range(bb):
        col = new_t[0:HEAD_DIM, b:b + 1]
        out.append(jnp.where(is_last, col, shifted[b]))
    return out


def _sample_mixer_kernel(z_ref, x_ref, mod_ref, kt_ref, vt_ref, st_ref,
                         gmg_ref, gmw_ref, gmb_ref, cdw_ref, cb_ref, clg_ref, clb_ref,
                         qn_ref, kn_ref, ropec_ref, ropes_ref, sink_ref, bd_ref, wb_ref, wout_ref,
                         _k_all, _v_all, _st_all,
                         x1_ref, kto_ref, vto_ref, sto_ref, gv_ref,
                         qf_s, of_s, oc_s, zg_s, *, bb):
    i = pl.program_id(0)
    r0 = pl.multiple_of(i * bb, bb)
    z = z_ref[...]
    zg_s[pl.ds(r0, bb), :] = z[:, OFF_GATE:OFF_GATE + N_BRANCH * D_MODEL]

    u = _gelu_half(z[:, OFF_GM_U:OFF_GM_U + GM_WIDTH])
    gv = _gelu_half(z[:, OFF_GM_V:OFF_GM_V + GM_WIDTH])
    v = gv * lax.rsqrt(jnp.mean(gv * gv, axis=-1, keepdims=True) + EPS) * gmg_ref[...]
    gv_ref[...] = v
    o_gm = u * (v * gmw_ref[...] + gmb_ref[...])

    a = _sigmoid_times(z[:, OFF_CV_G:OFF_CV_G + CONV_WIDTH], z[:, OFF_CV_A:OFF_CV_A + CONV_WIDTH])
    y = cb_ref[...] + cdw_ref[CONV_K - 1:CONV_K, :] * a
    for j in range(CONV_K - 1):
        y = y + cdw_ref[j:j + 1, :] * st_ref[j]
    o_cv = _layer_norm_silu(y, clg_ref[...], clb_ref[...])
    sto_ref[0:CONV_K - 2] = st_ref[1:CONV_K - 1]
    sto_ref[CONV_K - 2] = a

    rc = ropec_ref[...]
    rs = ropes_ref[...]
    bd = bd_ref[...]
    q = _rope(_head_norm(z[:, OFF_Q:OFF_Q + ATT_WIDTH], bd, qn_ref[...]), rc, rs) * (HEAD_DIM ** -0.5)
    knew = _rope(_head_norm(z[:, OFF_K:OFF_K + KV_WIDTH], bd[0:LANES, 0:LANES], kn_ref[...]), rc, rs)
    vnew = z[:, OFF_V:OFF_V + KV_WIDTH]
    lo = _low_half((bb, LANES))
    for h in range(N_HEADS):
        tile = q[:, (h // 2) * LANES:(h // 2 + 1) * LANES]
        if h % 2 == 1:
            tile = pltpu.roll(tile, HEAD_DIM, 1)
        qf_s[:, h, :] = jnp.where(lo, tile, 0.0)
    k_low = [knew, pltpu.roll(knew, HEAD_DIM, 1)]
    v_low = [vnew, pltpu.roll(vnew, HEAD_DIM, 1)]
    for g in range(N_KV_HEADS):
        qg = qf_s[:, Q_REP * g:Q_REP * (g + 1), :]
        kt = kt_ref[:, g]
        vt = vt_ref[:, g]
        s = jnp.einsum("brd,bdp->brp", qg[:, :, 0:HEAD_DIM].astype(BF), kt.astype(BF),
                       preferred_element_type=F32)
        s_new = jnp.sum(qg * k_low[g][:, None, :], axis=-1, keepdims=True)
        sink = sink_ref[Q_REP * g:Q_REP * (g + 1), 0:1][None]
        m = jnp.maximum(jnp.maximum(jnp.max(s, axis=-1, keepdims=True), s_new), sink)
        pr = jnp.exp(s - m)
        p_new = jnp.exp(s_new - m)
        den = jnp.sum(pr, axis=-1, keepdims=True) + p_new + jnp.exp(sink - m)
        o = jnp.einsum("brp,bdp->brd", pr.astype(BF), vt.astype(BF), preferred_element_type=F32)
        o = (o + p_new * v_low[g][:, None, 0:HEAD_DIM]) / den
        of_s[:, Q_REP * g:Q_REP * (g + 1), :] = jnp.concatenate([o, jnp.zeros(o.shape, F32)], axis=-1)
        for b, tile in enumerate(_shift_in_column(kt, k_low[g], bb)):
            kto_ref[b, g] = tile
        for b, tile in enumerate(_shift_in_column(vt, v_low[g], bb)):
            vto_ref[b, g] = tile
    att_tiles = []
    for j in range(N_HEADS // 2):
        second = pltpu.roll(of_s[:, 2 * j + 1, :], HEAD_DIM, 1)
        att_tiles.append(jnp.where(lo, of_s[:, 2 * j, :], second))
    o_att = jnp.concatenate(att_tiles, axis=1)

    oc_s[pl.ds(r0, bb), 0:GM_WIDTH] = o_gm.astype(BF)
    oc_s[pl.ds(r0, bb), GM_WIDTH:GM_WIDTH + CONV_WIDTH] = o_cv.astype(BF)
    oc_s[pl.ds(r0, bb), GM_WIDTH + CONV_WIDTH:GM_WIDTH + CONV_WIDTH + ATT_WIDTH] = o_att.astype(BF)

    @pl.when(i == pl.num_programs(0) - 1)
    def _():
        acc = _sigmoid_times(zg_s[:, 0:D_MODEL], _dot(oc_s[:, 0:GM_WIDTH], wb_ref[0]))
        acc = acc + _sigmoid_times(zg_s[:, D_MODEL:2 * D_MODEL], _dot(
            oc_s[:, GM_WIDTH:GM_WIDTH + CONV_WIDTH], wb_ref[1]))
        acc = acc + _sigmoid_times(zg_s[:, 2 * D_MODEL:3 * D_MODEL], _dot(
            oc_s[:, GM_WIDTH + CONV_WIDTH:GM_WIDTH + CONV_WIDTH + ATT_WIDTH], wb_ref[2]))
        gt1 = mod_ref[...][:, 2 * D_MODEL:3 * D_MODEL]
        x1_ref[...] = x_ref[...] + gt1 * _dot(acc.astype(BF), wout_ref[...])


def _sample_mixer_call(layer, z, xs, mod, kt, vt, st, new_k, new_v, new_st, p, bb):
    n = xs.shape[0]
    lsel3 = lambda i: (layer, 0, 0)
    cache_block = (bb, N_KV_HEADS, HEAD_DIM, WINDOW)
    kernel = functools.partial(_sample_mixer_kernel, bb=bb)
    args = (z, xs, mod, kt, vt, st, p["gm_g"], p["gm_w0"], p["gm_b0"], p["conv_dw"], p["conv_b"],
            p["conv_ln_g"], p["conv_ln_b"], p["qn"], p["kn"], p["rope_c1"], p["rope_s1"],
            p["sink_lanes"], p["blockdiag"], p["w_branch"], p["w_out"], new_k, new_v, new_st)
    n_in = len(args)
    return pl.pallas_call(
        kernel,
        out_shape=(
            jax.ShapeDtypeStruct((n, D_MODEL), F32),
            jax.ShapeDtypeStruct(new_k.shape, F32),
            jax.ShapeDtypeStruct(new_v.shape, F32),
            jax.ShapeDtypeStruct(new_st.shape, F32),
            jax.ShapeDtypeStruct((n, GM_WIDTH), F32),
        ),
        input_output_aliases={n_in - 3: 1, n_in - 2: 2, n_in - 1: 3},
        grid=(n // bb,),
        in_specs=[
            pl.BlockSpec((bb, IN_WIDTH), lambda i: (i, 0)),
            pl.BlockSpec((n, D_MODEL), lambda i: (0, 0)),
            pl.BlockSpec((None, n, 3 * D_MODEL), lsel3),
            pl.BlockSpec((None,) + cache_block, lambda i: (layer, i, 0, 0, 0)),
            pl.BlockSpec((None,) + cache_block, lambda i: (layer, i, 0, 0, 0)),
            pl.BlockSpec((None, CONV_K - 1, bb, CONV_WIDTH), lambda i: (layer, 0, i, 0)),
            pl.BlockSpec((None, 1, GM_WIDTH), lsel3),
            pl.BlockSpec((None, 1, GM_WIDTH), lsel3),
            pl.BlockSpec((None, 1, GM_WIDTH), lsel3),
            pl.BlockSpec((None, CONV_K, CONV_WIDTH), lsel3),
            pl.BlockSpec((None, 1, CONV_WIDTH), lsel3),
            pl.BlockSpec((None, 1, CONV_WIDTH), lsel3),
            pl.BlockSpec((None, 1, CONV_WIDTH), lsel3),
            pl.BlockSpec((None, 1, ATT_WIDTH), lsel3),
            pl.BlockSpec((None, 1, LANES), lsel3),
            pl.BlockSpec((1, LANES), lambda i: (0, 0)),
            pl.BlockSpec((1, LANES), lambda i: (0, 0)),
            pl.BlockSpec((None, N_HEADS, LANES), lsel3),
            pl.BlockSpec((ATT_WIDTH, ATT_WIDTH), lambda i: (0, 0)),
            pl.BlockSpec((None, N_BRANCH, GM_WIDTH, D_MODEL), lambda i: (layer, 0, 0, 0)),
            pl.BlockSpec((None, D_MODEL, D_MODEL), lsel3),
            pl.BlockSpec(memory_space=pl.ANY),
            pl.BlockSpec(memory_space=pl.ANY),
            pl.BlockSpec(memory_space=pl.ANY),
        ],
        out_specs=(
            pl.BlockSpec((n, D_MODEL), lambda i: (0, 0)),
            pl.BlockSpec((None,) + cache_block, lambda i: (layer, i, 0, 0, 0)),
            pl.BlockSpec((None,) + cache_block, lambda i: (layer, i, 0, 0, 0)),
            pl.BlockSpec((None, CONV_K - 1, bb, CONV_WIDTH), lambda i: (layer, 0, i, 0)),
            pl.BlockSpec((bb, GM_WIDTH), lambda i: (i, 0)),
        ),
        scratch_shapes=[
            pltpu.VMEM((bb, N_HEADS, LANES), F32),
            pltpu.VMEM((bb, N_HEADS, LANES), F32),
            pltpu.VMEM((n, GM_WIDTH + CONV_WIDTH + ATT_WIDTH), BF),
            pltpu.VMEM((n, N_BRANCH * D_MODEL), F32),
        ],
        compiler_params=pltpu.CompilerParams(
            dimension_semantics=("arbitrary",), vmem_limit_bytes=VMEM_LIMIT),
        name=f"sample_mixers_l{layer}",
    )(*args)


def _rope_lane_tables():
    half = ROT_DIM // 2
    freqs = jnp.exp(-math.log(ROPE_THETA) * jnp.arange(half, dtype=F32) * (2.0 / ROT_DIM))
    rest = jnp.zeros((HEAD_DIM - ROT_DIM,), F32)
    freq64 = jnp.concatenate([freqs, freqs, rest])
    sign64 = jnp.concatenate([-jnp.ones((half,), F32), jnp.ones((half,), F32), rest])
    reps = LANES // HEAD_DIM
    return jnp.tile(freq64, reps)[None, :], jnp.tile(sign64, reps)[None, :]


def _cos_sin(pos, lane_freq):
    ang = pos.astype(F32)[:, None] * lane_freq
    return jnp.cos(ang), jnp.sin(ang)


def _band_bias():
    i = np.arange(WINDOW)[:, None]
    j = np.arange(2 * WINDOW)[None, :]
    band = (j >= i) & (j <= i + WINDOW)
    first = band & (j >= WINDOW)
    out = np.where(np.stack([first, band]), 0.0, NEG_BIG).astype(np.float32)
    return jnp.asarray(out)


def _blockdiag():
    idx = np.arange(ATT_WIDTH) // HEAD_DIM
    return jnp.asarray((idx[:, None] == idx[None, :]).astype(np.float32), dtype=BF)


def _expand_matrix():
    k = np.arange(LANES)[:, None]
    e = (np.arange(EXPERTS_PER_GROUP * EXPERT_FF) // EXPERT_FF)[None, :]
    return jnp.asarray(((k == e) | (k == e + EXPERTS_PER_GROUP)).astype(np.float32), dtype=BF)


def _dup_heads(w):
    h0, h1 = w[..., :HEAD_DIM], w[..., HEAD_DIM:]
    return jnp.concatenate([h0, h0, h1, h1], axis=-1)


def _prepare(norm1_g, norm2_g, w_in, gm_norm_g, gm_ws, gm_b, conv_dw, conv_b, conv_ln_g,
             conv_ln_b, q_norm_g, k_norm_g, attn_sinks, w_branch, w_out, router_w, router_b,
             moe_w1, moe_w3, moe_w2, seq, tm):
    w_kv2 = jnp.concatenate([_dup_heads(w_in[:, :, OFF_K:OFF_V]),
                             _dup_heads(w_in[:, :, OFF_V:OFF_GATE])], axis=-1).astype(BF)
    rw = router_w.astype(F32)
    rw_hi = rw.astype(BF)
    rw_lo = (rw - rw_hi.astype(F32)).astype(BF)
    rw2 = jnp.concatenate(
        [rw_hi, rw_lo, jnp.zeros((D_MODEL, LANES - 2 * N_EXPERTS), BF)], axis=1)
    lane_freq, lane_sign = _rope_lane_tables()
    cb, sb = _cos_sin(jnp.arange(seq // tm, dtype=jnp.int32) * tm, lane_freq)
    co, so = _cos_sin(jnp.arange(tm, dtype=jnp.int32), lane_freq)
    c1, s1 = _cos_sin(PAST_LEN + jnp.arange(1, dtype=jnp.int32), lane_freq)
    row3 = lambda a: a.reshape(DEPTH, 1, a.shape[-1])
    col = np.arange(IN_WIDTH)
    col_scale = jnp.asarray(np.where((col >= OFF_Q) & (col < OFF_GATE), 1.0, 0.5), F32)
    return {
        "n1g": row3(norm1_g), "n2g": row3(norm2_g), "w_in": (w_in * col_scale).astype(BF),
        "w_kv2": w_kv2,
        "gm_g": row3(gm_norm_g), "gm_ws": gm_ws,
        "gm_bias": jnp.repeat(jnp.swapaxes(gm_b, 1, 2), LANES, axis=2),
        "gm_w0": jnp.repeat(gm_ws[:, :, 0, 0], LANES, axis=1).reshape(DEPTH, 1, GM_WIDTH),
        "gm_b0": jnp.repeat(gm_b[:, :, 0], LANES, axis=1).reshape(DEPTH, 1, GM_WIDTH),
        "conv_dw": conv_dw, "conv_b": row3(conv_b), "conv_ln_g": row3(0.5 * conv_ln_g),
        "conv_ln_b": row3(0.5 * conv_ln_b),
        "qn": row3(jnp.tile(q_norm_g, (1, N_HEADS))),
        "kn": row3(jnp.tile(k_norm_g, (1, 2 * N_KV_HEADS))),
        "sinks": attn_sinks,
        "sink_lanes": jnp.broadcast_to(attn_sinks[:, :, None], (DEPTH, N_HEADS, LANES)),
        "rope_base": jnp.concatenate([cb, sb], axis=1)[:, None, :],
        "rope_off": jnp.concatenate([co, so], axis=1),
        "rope_sign": lane_sign, "rope_c1": c1, "rope_s1": s1 * lane_sign,
        "band_bias": _band_bias(), "blockdiag": _blockdiag(),
        "w_branch": (0.5 * w_branch).astype(BF), "w_out": w_out.astype(BF),
        "rw2": rw2,
        "rb": router_b.astype(F32).reshape(N_EXPERTS, 1),
        "expand": _expand_matrix(),
        "moe_w1": moe_w1.astype(BF), "moe_w3": moe_w3.astype(BF),
        "moe_w2": moe_w2.astype(BF).reshape(DEPTH, N_GROUPS, EXPERTS_PER_GROUP * EXPERT_FF, D_MODEL),
    }


def kernel(x_prompt, x_sample, cache_win_k, cache_win_v, state_conv, c_prompt, c_sample, norm1_g, norm2_g, w_ada, b_ada, w_in, gm_norm_g, gm_ws, gm_b, conv_dw, conv_b, conv_ln_g, conv_ln_b, q_norm_g, k_norm_g, attn_sinks, w_branch, w_out, router_w, router_b, moe_w1, moe_w3, moe_w2):
    nb, seq, _ = x_prompt.shape
    ns = x_sample.shape[0]
    tm = min(MIX_TILE, seq)
    tmoe = min(MOE_TILE, seq)
    p = _prepare(norm1_g, norm2_g, w_in, gm_norm_g, gm_ws, gm_b, conv_dw, conv_b, conv_ln_g,
                 conv_ln_b, q_norm_g, k_norm_g, attn_sinks, w_branch, w_out, router_w, router_b,
                 moe_w1, moe_w3, moe_w2, seq, tm)
    assert ns == PROMPT_MOD_ROW and nb <= ADA_ROWS - PROMPT_MOD_ROW
    c_all = jnp.concatenate(
        [c_sample, c_prompt, jnp.zeros((ADA_ROWS - nb - ns, D_MODEL), F32)], axis=0)
    mod = _ada_call(c_all, w_ada, b_ada)

    kt = jnp.transpose(cache_win_k, (0, 1, 3, 4, 2))
    vt = jnp.transpose(cache_win_v, (0, 1, 3, 4, 2))
    st = jnp.transpose(state_conv, (0, 2, 1, 3))

    xp = x_prompt
    xs = x_sample.reshape(ns, D_MODEL)
    kp_l, vp_l, cp_l, gs_l = [], [], [], []
    new_k = jnp.zeros(kt.shape, F32)
    new_v = jnp.zeros(vt.shape, F32)
    new_st = jnp.zeros(st.shape, F32)
    for l in range(DEPTH):
        xp, kp, vp, cp = _mixer_call(l, xp, mod, p, tm)
        xp = _moe_call(l, xp.reshape(nb * seq, D_MODEL), mod, seq // tmoe, p, tmoe,
                       f"prompt_moe_l{l}").reshape(nb, seq, D_MODEL)
        kp_l.append(kp)
        vp_l.append(vp)
        cp_l.append(cp)

        z = _sample_proj_call(l, xs, mod, p)
        xs, new_k, new_v, new_st, gv = _sample_mixer_call(l, z, xs, mod, kt, vt, st,
                                                          new_k, new_v, new_st, p, SAMPLE_BLOCK)
        xs = _moe_call(l, xs, mod, None, p, ns, f"sample_moe_l{l}")
        gs_l.append(gv)

    kv_shape = (DEPTH, nb, WINDOW, N_KV_HEADS, HEAD_DIM)
    return (xp, xs.reshape(ns, 1, D_MODEL),
            jnp.stack(kp_l).reshape(kv_shape), jnp.stack(vp_l).reshape(kv_shape),
            jnp.stack(cp_l),
            jnp.transpose(new_k, (0, 1, 4, 2, 3)),
            jnp.transpose(new_v, (0, 1, 4, 2, 3)),
            jnp.transpose(new_st, (0, 2, 1, 3)),
            jnp.stack(gs_l).reshape(DEPTH, ns, 1, GM_WIDTH))
```

```python
import functools
import math

import jax
import jax.numpy as jnp
import numpy as np
from jax import lax
from jax.experimental import pallas as pl
from jax.experimental.pallas import tpu as pltpu

F32 = jnp.float32
BF = jnp.bfloat16

D_MODEL = 1024
DEPTH = 2
PAST_LEN = 8192
CHUNK = 128
GM_GROUPS = 4
GM_WIDTH = 512
CONV_WIDTH = 512
CONV_K = 31
N_HEADS = 8
N_KV_HEADS = 2
Q_REP = N_HEADS // N_KV_HEADS
HEAD_DIM = 64
ATT_WIDTH = N_HEADS * HEAD_DIM
KV_WIDTH = N_KV_HEADS * HEAD_DIM
WINDOW = 128
ROPE_THETA = 500000.0
ROT_DIM = HEAD_DIM // 4
N_BRANCH = 3
N_EXPERTS = 16
EXPERTS_PER_GROUP = 4
N_GROUPS = N_EXPERTS // EXPERTS_PER_GROUP
EXPERT_FF = 256
EPS = 1e-6

OFF_GM_U = 0
OFF_GM_V = 512
OFF_CV_A = 1024
OFF_CV_G = 1536
OFF_Q = 2048
OFF_K = 2560
OFF_V = 2688
OFF_GATE = 2816
IN_WIDTH = OFF_GATE + N_BRANCH * D_MODEL

LANES = 128
SUBLANES = 8
MXU_COLS = 256
HALO = 32
MIX_TILE = 512
MOE_TILE = 512
MOE_CAP = 144
ADA_ROWS = 136
PROMPT_MOD_ROW = 128
ADA_BLOCK = 1536
PROJ_BLOCK = IN_WIDTH // 2
SAMPLE_BLOCK = 32
VMEM_LIMIT = 56 * 1024 * 1024
NEG_BIG = -1e30


def _dot(a, b):
    return jnp.dot(a, b, preferred_element_type=F32)


def _dot_nt(a, b):
    return lax.dot_general(a, b, (((1,), (1,)), ((), ())), preferred_element_type=F32)


def _sigmoid(x):
    return 0.5 * jnp.tanh(0.5 * x) + 0.5


def _silu(x):
    return x * _sigmoid(x)


GELU_C = 0.7978845608028654
GELU_K = 0.044715


def _gelu_half(xh):
    return xh + xh * jnp.tanh(xh * (2.0 * GELU_C + (8.0 * GELU_C * GELU_K) * (xh * xh)))


def _sigmoid_times(zh, dh):
    return jnp.tanh(zh) * dh + dh


def _split_bf16(x):
    hi = x.astype(BF)
    lo = (x - hi.astype(F32)).astype(BF)
    return hi, lo


def _modulate(x, g, shift, scale):
    ms = jnp.mean(x * x, axis=-1, keepdims=True)
    return (x * lax.rsqrt(ms + EPS)) * (g * (1.0 + scale)) + shift


def _head_norm(x, blockdiag, g):
    ssum = _dot((x * x).astype(BF), blockdiag)
    return x * lax.rsqrt(ssum * (1.0 / HEAD_DIM) + EPS) * g


def _rope(x, c, s):
    width = x.shape[-1]
    reps = width // LANES
    cc = jnp.concatenate([c] * reps, axis=-1)
    ss = jnp.concatenate([s] * reps, axis=-1)
    lane = lax.broadcasted_iota(jnp.int32, x.shape, x.ndim - 1) % HEAD_DIM
    partner = jnp.where(lane < ROT_DIM // 2,
                        pltpu.roll(x, width - ROT_DIM // 2, x.ndim - 1),
                        pltpu.roll(x, ROT_DIM // 2, x.ndim - 1))
    return x * cc + partner * ss


def _layer_norm_silu(y, gh, bh):
    mu = jnp.mean(y, axis=-1, keepdims=True)
    yc = y - mu
    var = jnp.mean(yc * yc, axis=-1, keepdims=True)
    h = yc * lax.rsqrt(var + EPS) * gh + bh
    return h * jnp.tanh(h) + h


def _low_half(shape):
    return lax.broadcasted_iota(jnp.int32, shape, len(shape) - 1) % LANES < HEAD_DIM


def _const_spec(shape, index_map):
    return pl.BlockSpec(shape, index_map, pipeline_mode=pl.Buffered(1))


def _ada_kernel(c_ref, w_ref, b_ref, o_ref):
    s = _silu(c_ref[...]).astype(BF)
    o_ref[...] = _dot(s, w_ref[...].astype(BF)) + b_ref[...]


def _ada_call(c_all, w_ada, b_ada):
    nb = (6 * D_MODEL) // ADA_BLOCK
    return pl.pallas_call(
        _ada_kernel,
        out_shape=jax.ShapeDtypeStruct((DEPTH, ADA_ROWS, 6 * D_MODEL), F32),
        grid=(DEPTH, nb),
        in_specs=[
            pl.BlockSpec((ADA_ROWS, D_MODEL), lambda l, j: (0, 0)),
            pl.BlockSpec((None, D_MODEL, ADA_BLOCK), lambda l, j: (l, 0, j)),
            pl.BlockSpec((None, 1, ADA_BLOCK), lambda l, j: (l, 0, j)),
        ],
        out_specs=pl.BlockSpec((None, ADA_ROWS, ADA_BLOCK), lambda l, j: (l, 0, j)),
        compiler_params=pltpu.CompilerParams(
            dimension_semantics=("arbitrary", "arbitrary")),
        name="ada_mod",
    )(c_all, w_ada, b_ada.reshape(DEPTH, 1, 6 * D_MODEL))


def _mixer_kernel(sinks_ref, x_ref, mod_ref, n1g_ref, win_ref, wkv_ref, gmg_ref, gmws_ref, gmb_ref,
                  cdw_ref, cb_ref, clg_ref, clb_ref, qn_ref, kn_ref, ropeb_ref, ropeo_ref,
                  sign_ref, bias_ref, bd_ref, wb_ref, wout_ref,
                  x1_ref, kwin_ref, vwin_ref, cst_ref,
                  kd_s, vd_s, abuf, oatt_s, hb_s, z_s, acc_s, *, layer, tm):
    t = pl.program_id(1)
    last = pl.num_programs(1) - 1
    nblk = tm // WINDOW

    @pl.when(t == 0)
    def _():
        kd_s[0:WINDOW, :] = jnp.zeros((WINDOW, 2 * LANES), BF)
        vd_s[0:WINDOW, :] = jnp.zeros((WINDOW, 2 * LANES), BF)
        abuf[0:HALO, :] = jnp.zeros((HALO, CONV_WIDTH), F32)
        abuf[HALO + tm:HALO + tm + SUBLANES, :] = jnp.zeros((SUBLANES, CONV_WIDTH), F32)

    x = x_ref[...]
    mod = mod_ref[pl.ds(pl.program_id(0), 1), :]
    sh1 = mod[:, 0:D_MODEL]
    sc1 = mod[:, D_MODEL:2 * D_MODEL]
    gt1 = mod[:, 2 * D_MODEL:3 * D_MODEL]
    hb_s[...] = _modulate(x, n1g_ref[...], sh1, sc1).astype(BF)

    def proj(off, width):
        z_s[:, off:off + width] = _dot(hb_s[...], win_ref[:, off:off + width])

    def zcols(off, width):
        return z_s[:, off:off + width]

    proj(OFF_CV_A, 2 * CONV_WIDTH)
    a = _sigmoid_times(zcols(OFF_CV_G, CONV_WIDTH), zcols(OFF_CV_A, CONV_WIDTH))
    abuf[HALO:HALO + tm, :] = a
    first_off = HALO - (CONV_K - 1)

    def conv_tile(ci):
        lanes = slice(ci * LANES, (ci + 1) * LANES)
        yt = cb_ref[:, lanes]
        for b in range(SUBLANES):
            part = None
            for off in range(b, HALO + 1, SUBLANES):
                if off < first_off:
                    continue
                term = (cdw_ref[off - first_off:off - first_off + 1, lanes]
                        * abuf[pl.ds(off - b, tm + SUBLANES), lanes])
                part = term if part is None else part + term
            yt = yt + part[b:b + tm, :]
        return yt

    proj(OFF_GM_U, 2 * GM_WIDTH)
    proj(OFF_Q, ATT_WIDTH)
    z_s[:, IN_WIDTH:IN_WIDTH + 4 * LANES] = _dot(hb_s[...], wkv_ref[...])

    def gated_branch(i, o_branch, first):
        for n in range(D_MODEL // MXU_COLS):
            cols = slice(n * MXU_COLS, (n + 1) * MXU_COLS)
            g0 = OFF_GATE + i * D_MODEL + n * MXU_COLS
            piece = _sigmoid_times(_dot(hb_s[...], win_ref[:, g0:g0 + MXU_COLS]),
                                   _dot(o_branch, wb_ref[i, :, cols]))
            if first:
                acc_s[:, cols] = piece
            else:
                acc_s[:, cols] += piece

    y = jnp.concatenate([conv_tile(ci) for ci in range(CONV_WIDTH // LANES)], axis=1)
    abuf[0:HALO, :] = abuf[tm:tm + HALO, :]
    o_cv = _layer_norm_silu(y, clg_ref[...], clb_ref[...]).astype(BF)
    gated_branch(1, o_cv, True)

    u = _gelu_half(zcols(OFF_GM_U, GM_WIDTH))
    gv = _gelu_half(zcols(OFF_GM_V, GM_WIDTH))
    v = gv * lax.rsqrt(jnp.mean(gv * gv, axis=-1, keepdims=True) + EPS) * gmg_ref[...]
    vb = v.astype(BF)
    row = lax.broadcasted_iota(jnp.int32, (CHUNK, CHUNK), 0)
    col = lax.broadcasted_iota(jnp.int32, (CHUNK, CHUNK), 1)
    ws = [jnp.where(row >= col, gmws_ref[g], 0.0).astype(BF) for g in range(GM_GROUPS)]
    gmb = gmb_ref[...]
    chunks = []
    for c in range(nblk):
        parts = [_dot(ws[g], vb[c * CHUNK:(c + 1) * CHUNK, g * LANES:(g + 1) * LANES])
                 for g in range(GM_GROUPS)]
        chunks.append(jnp.concatenate(parts, axis=1) + gmb)
    o_gm = (u * jnp.concatenate(chunks, axis=0)).astype(BF)
    gated_branch(0, o_gm, False)

    rbase = ropeb_ref[...]
    roff = ropeo_ref[...]
    cb_, sb_ = rbase[:, 0:LANES], rbase[:, LANES:2 * LANES]
    co_, so_ = roff[:, 0:LANES], roff[:, LANES:2 * LANES]
    rc = cb_ * co_ - sb_ * so_
    rs = (sb_ * co_ + cb_ * so_) * sign_ref[...]
    bd = bd_ref[...]
    q = _rope(_head_norm(zcols(OFF_Q, ATT_WIDTH), bd, qn_ref[...]), rc, rs)
    qb = (q * (HEAD_DIM ** -0.5)).astype(BF)
    kd = _rope(_head_norm(zcols(IN_WIDTH, 2 * LANES), bd[0:2 * LANES, 0:2 * LANES], kn_ref[...]),
               rc, rs)
    vd = zcols(IN_WIDTH + 2 * LANES, 2 * LANES)
    kd_s[WINDOW:WINDOW + tm, :] = kd.astype(BF)
    vd_s[WINDOW:WINDOW + tm, :] = vd.astype(BF)

    lo_q = _low_half((WINDOW, LANES))
    zero_q = jnp.zeros((WINDOW, LANES), BF)
    band = bias_ref[1]
    for bi in range(nblk):
        bias = jnp.where(t == 0, bias_ref[0], band) if bi == 0 else band
        bias4 = jnp.concatenate([bias] * Q_REP, axis=0)
        for g in range(N_KV_HEADS):
            tiles = [qb[bi * WINDOW:(bi + 1) * WINDOW, (2 * g + j) * LANES:(2 * g + j + 1) * LANES]
                     for j in range(2)]
            qs = jnp.concatenate([jnp.where(lo_q, tiles[0], zero_q), jnp.where(lo_q, zero_q, tiles[0]),
                                  jnp.where(lo_q, tiles[1], zero_q), jnp.where(lo_q, zero_q, tiles[1])],
                                 axis=0)
            keys = kd_s[bi * WINDOW:(bi + 2) * WINDOW, g * LANES:(g + 1) * LANES]
            vals = vd_s[bi * WINDOW:(bi + 2) * WINDOW, g * LANES:(g + 1) * LANES]
            s = _dot_nt(qs, keys) + bias4
            outs = []
            for hh in range(Q_REP):
                sink = sinks_ref[layer, Q_REP * g + hh]
                sh = s[hh * WINDOW:(hh + 1) * WINDOW, :]
                m = jnp.maximum(jnp.max(sh, axis=-1, keepdims=True), sink)
                p = jnp.exp(sh - m)
                den = jnp.sum(p, axis=-1, keepdims=True) + jnp.exp(sink - m)
                outs.append(_dot(p.astype(BF), vals) / den)
            for j in range(2):
                oatt_s[bi * WINDOW:(bi + 1) * WINDOW, (2 * g + j) * LANES:(2 * g + j + 1) * LANES] = (
                    jnp.where(lo_q, outs[2 * j], outs[2 * j + 1]).astype(BF))
    kd_s[0:WINDOW, :] = kd_s[tm:tm + WINDOW, :]
    vd_s[0:WINDOW, :] = vd_s[tm:tm + WINDOW, :]
    gated_branch(2, oatt_s[...], False)

    accb = acc_s[...].astype(BF)
    for n in range(D_MODEL // MXU_COLS):
        cols = slice(n * MXU_COLS, (n + 1) * MXU_COLS)
        x1_ref[:, cols] = x_ref[:, cols] + gt1[:, cols] * _dot(accb, wout_ref[:, cols])

    @pl.when(t == last)
    def _():
        kl = kd[tm - WINDOW:tm, :]
        vl = vd[tm - WINDOW:tm, :]
        kwin_ref[...] = jnp.where(lo_q, kl[:, 0:LANES], kl[:, LANES:2 * LANES])
        vwin_ref[...] = jnp.where(lo_q, vl[:, 0:LANES], vl[:, LANES:2 * LANES])
        cst_ref[...] = a[tm - (CONV_K - 1):tm, :]


def _mixer_call(layer, x, mod, p, tm):
    nb, seq, _ = x.shape
    nt = seq // tm
    lsel3 = lambda b, t: (layer, 0, 0)
    kernel = functools.partial(_mixer_kernel, layer=layer, tm=tm)
    return pl.pallas_call(
        kernel,
        out_shape=(
            jax.ShapeDtypeStruct((nb, seq, D_MODEL), F32),
            jax.ShapeDtypeStruct((nb, WINDOW, KV_WIDTH), F32),
            jax.ShapeDtypeStruct((nb, WINDOW, KV_WIDTH), F32),
            jax.ShapeDtypeStruct((nb, CONV_K - 1, CONV_WIDTH), F32),
        ),
        grid=(nb, nt),
        in_specs=[
            pl.BlockSpec(memory_space=pltpu.SMEM),
            pl.BlockSpec((None, tm, D_MODEL), lambda b, t: (b, t, 0)),
            pl.BlockSpec((None, SUBLANES, 3 * D_MODEL),
                         lambda b, t: (layer, PROMPT_MOD_ROW // SUBLANES, 0)),
            _const_spec((None, 1, D_MODEL), lsel3),
            _const_spec((None, D_MODEL, IN_WIDTH), lsel3),
            _const_spec((None, D_MODEL, 4 * LANES), lsel3),
            _const_spec((None, 1, GM_WIDTH), lsel3),
            _const_spec((None, GM_GROUPS, CHUNK, CHUNK), lambda b, t: (layer, 0, 0, 0)),
            _const_spec((None, CHUNK, GM_WIDTH), lsel3),
            _const_spec((None, CONV_K, CONV_WIDTH), lsel3),
            _const_spec((None, 1, CONV_WIDTH), lsel3),
            _const_spec((None, 1, CONV_WIDTH), lsel3),
            _const_spec((None, 1, CONV_WIDTH), lsel3),
            _const_spec((None, 1, ATT_WIDTH), lsel3),
            _const_spec((None, 1, 2 * LANES), lsel3),
            pl.BlockSpec((None, 1, 2 * LANES), lambda b, t: (t, 0, 0)),
            _const_spec((tm, 2 * LANES), lambda b, t: (0, 0)),
            _const_spec((1, LANES), lambda b, t: (0, 0)),
            _const_spec((2, WINDOW, 2 * WINDOW), lambda b, t: (0, 0, 0)),
            _const_spec((ATT_WIDTH, ATT_WIDTH), lambda b, t: (0, 0)),
            _const_spec((None, N_BRANCH, GM_WIDTH, D_MODEL), lambda b, t: (layer, 0, 0, 0)),
            _const_spec((None, D_MODEL, D_MODEL), lsel3),
        ],
        out_specs=(
            pl.BlockSpec((None, tm, D_MODEL), lambda b, t: (b, t, 0)),
            pl.BlockSpec((None, WINDOW, KV_WIDTH), lambda b, t: (b, 0, 0)),
            pl.BlockSpec((None, WINDOW, KV_WIDTH), lambda b, t: (b, 0, 0)),
            pl.BlockSpec((None, CONV_K - 1, CONV_WIDTH), lambda b, t: (b, 0, 0)),
        ),
        scratch_shapes=[
            pltpu.VMEM((WINDOW + tm, 2 * LANES), BF),
            pltpu.VMEM((WINDOW + tm, 2 * LANES), BF),
            pltpu.VMEM((HALO + tm + SUBLANES, CONV_WIDTH), F32),
            pltpu.VMEM((tm, ATT_WIDTH), BF),
            pltpu.VMEM((tm, D_MODEL), BF),
            pltpu.VMEM((tm, IN_WIDTH + 4 * LANES), F32),
            pltpu.VMEM((tm, D_MODEL), F32),
        ],
        compiler_params=pltpu.CompilerParams(
            dimension_semantics=("arbitrary", "arbitrary"),
            vmem_limit_bytes=VMEM_LIMIT),
        name=f"prompt_mixers_l{layer}",
    )(p["sinks"], x, mod, p["n1g"], p["w_in"], p["w_kv2"], p["gm_g"], p["gm_ws"], p["gm_bias"],
      p["conv_dw"], p["conv_b"], p["conv_ln_g"], p["conv_ln_b"], p["qn"], p["kn"],
      p["rope_base"], p["rope_off"], p["rope_sign"], p["band_bias"], p["blockdiag"],
      p["w_branch"], p["w_out"])


def _route_rows(logits_t, rb_ref):
    scores = _sigmoid(logits_t)
    biased = scores + rb_ref[...]
    rows = lambda a, gi: [a[EXPERTS_PER_GROUP * gi + k:EXPERTS_PER_GROUP * gi + k + 1, :]
                          for k in range(EXPERTS_PER_GROUP)]
    best = None
    idx = None
    for gi in range(N_GROUPS):
        b = rows(biased, gi)
        hi1, lo1 = jnp.maximum(b[0], b[1]), jnp.minimum(b[0], b[1])
        hi2, lo2 = jnp.maximum(b[2], b[3]), jnp.minimum(b[2], b[3])
        gs = jnp.maximum(hi1, hi2) + jnp.maximum(jnp.minimum(hi1, hi2), jnp.maximum(lo1, lo2))
        if gi == 0:
            best, idx = gs, jnp.zeros(gs.shape, jnp.int32)
        else:
            better = gs > best
            idx = jnp.where(better, gi, idx)
            best = jnp.where(better, gs, best)
    bsel = rows(biased, 0)
    ssel = rows(scores, 0)
    for gi in range(1, N_GROUPS):
        bg, sg = rows(biased, gi), rows(scores, gi)
        pick = idx == gi
        bsel = [jnp.where(pick, bg[k], bsel[k]) for k in range(EXPERTS_PER_GROUP)]
        ssel = [jnp.where(pick, sg[k], ssel[k]) for k in range(EXPERTS_PER_GROUP)]
    chosen = []
    for k in range(EXPERTS_PER_GROUP):
        rank = jnp.zeros(idx.shape, jnp.int32)
        for k2 in range(EXPERTS_PER_GROUP):
            if k2 == k:
                continue
            beats = (bsel[k2] > bsel[k]) | ((bsel[k2] == bsel[k]) & (k2 < k))
            rank = rank + beats.astype(jnp.int32)
        chosen.append(jnp.where(rank < 2, ssel[k], 0.0))
    den = chosen[0] + chosen[1] + chosen[2] + chosen[3]
    return idx, [c / den for c in chosen]


def _moe_route(x, mod, n2g_ref, rw2_ref, rb_ref, upper_ref, h_buf, rt_buf, cnt_s, buf, tm):
    h2 = _modulate(x, n2g_ref[...], mod[:, 0:D_MODEL], mod[:, D_MODEL:2 * D_MODEL])
    hi, lo = _split_bf16(h2)
    rw2 = rw2_ref[...]
    lt = (_dot(hi, rw2) + _dot(lo, rw2)).T
    idx, comb = _route_rows(lt[0:N_EXPERTS, :] + lt[N_EXPERTS:2 * N_EXPERTS, :], rb_ref)

    onehot = [(idx == g).astype(F32) for g in range(N_GROUPS)]
    oh8 = jnp.concatenate(onehot + [jnp.zeros((SUBLANES - N_GROUPS, tm), F32)], axis=0)
    prefix = _dot(oh8.astype(BF), upper_ref[...])
    slot = onehot[0] * prefix[0:1, :]
    for g in range(1, N_GROUPS):
        slot = slot + onehot[g] * prefix[g:g + 1, :]
    rt_buf[0:1, :] = idx
    rt_buf[1:2, :] = slot.astype(jnp.int32)
    for g in range(N_GROUPS):
        cnt_s[buf, g] = jnp.sum(onehot[g]).astype(jnp.int32)

    chi = [c.astype(BF).astype(F32) for c in comb]
    clo = [c - h for c, h in zip(comb, chi)]
    crow = jnp.concatenate(chi + clo + [jnp.zeros((LANES - 2 * EXPERTS_PER_GROUP, tm), F32)], axis=0)
    h_buf[:, 0:D_MODEL] = hi
    h_buf[:, D_MODEL:D_MODEL + LANES] = crow.T.astype(BF)


def _moe_experts(x_ref, gt2, w1_ref, w3_ref, w2_ref, o_ref, h_buf, rt_buf, cnt_s, buf,
                 tm, cap, overflow):
    def dispatch(g, j):
        want = lax.broadcasted_iota(jnp.int32, (cap, tm), 0) + j * cap
        hit = (rt_buf[1:2, :] == want) & (rt_buf[0:1, :] == g)
        return jnp.where(hit, 1.0, 0.0).astype(BF)

    def run_experts(g, pmat):
        gathered = _dot(pmat, h_buf[...])
        hg = gathered[:, 0:D_MODEL].astype(BF)
        cs = gathered[:, D_MODEL:D_MODEL + LANES]
        parts = []
        for e in range(EXPERTS_PER_GROUP):
            a = _dot(hg, w1_ref[EXPERTS_PER_GROUP * g + e])
            b = _dot(hg, w3_ref[EXPERTS_PER_GROUP * g + e])
            ce = cs[:, e:e + 1] + cs[:, EXPERTS_PER_GROUP + e:EXPERTS_PER_GROUP + e + 1]
            parts.append(_silu(a) * b * ce)
        act = jnp.concatenate(parts, axis=1).astype(BF)
        return _dot(act, w2_ref[g]).astype(BF)

    def scatter(pmat, y):
        return lax.dot_general(pmat, y, (((0,), (0,)), ((), ())), preferred_element_type=F32)

    if not overflow:
        pmats = [dispatch(g, 0) for g in range(N_GROUPS)]
        ys = [run_experts(g, pmats[g]) for g in range(N_GROUPS)]
        y_tile = scatter(jnp.concatenate(pmats, axis=0), jnp.concatenate(ys, axis=0))
        o_ref[...] = x_ref[...] + gt2 * y_tile
        return

    def group_body(g, carry):
        nblk = (cnt_s[buf, g] + (cap - 1)) // cap

        def block_body(j, carry2):
            pmat = dispatch(g, j)
            o_ref[...] += gt2 * scatter(pmat, run_experts(g, pmat))
            return carry2

        lax.fori_loop(1, nblk, block_body, 0)
        return carry

    lax.fori_loop(0, N_GROUPS, group_body, 0)


def _moe_kernel(x_ref, xn_ref, mod_ref, n2g_ref, rw2_ref, rb_ref, upper_ref,
                w1_ref, w3_ref, w2_ref, o_ref, h0_s, h1_s, rt0_s, rt1_s, cnt_s,
                *, tm, cap, tiles_per_row, n_tiles):
    route = functools.partial(_moe_route, n2g_ref=n2g_ref, rw2_ref=rw2_ref, rb_ref=rb_ref,
                              upper_ref=upper_ref, cnt_s=cnt_s, tm=tm)
    experts = functools.partial(_moe_experts, x_ref, w1_ref=w1_ref,
                                w3_ref=w3_ref, w2_ref=w2_ref, o_ref=o_ref, cnt_s=cnt_s, tm=tm, cap=cap)
    h_bufs = (h0_s, h1_s)
    rt_bufs = (rt0_s, rt1_s)
    if tiles_per_row is None:
        mod = mod_ref[...]
        route(x_ref[...], mod, h_buf=h0_s, rt_buf=rt0_s, buf=0)
        for overflow in (False, True):
            experts(mod[:, 2 * D_MODEL:3 * D_MODEL], h_buf=h0_s, rt_buf=rt0_s, buf=0,
                    overflow=overflow)
        return

    i = pl.program_id(0)
    mod_row = lambda tile: mod_ref[pl.ds(tile // tiles_per_row, 1), :]

    @pl.when(i == 0)
    def _():
        route(x_ref[...], mod_row(i), h_buf=h0_s, rt_buf=rt0_s, buf=0)

    def step(cur):
        nxt = 1 - cur
        gt2 = mod_row(i)[:, 2 * D_MODEL:3 * D_MODEL]
        experts(gt2, h_buf=h_bufs[cur], rt_buf=rt_bufs[cur], buf=cur, overflow=False)
        route(xn_ref[...], mod_row(jnp.minimum(i + 1, n_tiles - 1)),
              h_buf=h_bufs[nxt], rt_buf=rt_bufs[nxt], buf=nxt)
        experts(gt2, h_buf=h_bufs[cur], rt_buf=rt_bufs[cur], buf=cur, overflow=True)

    @pl.when(i % 2 == 0)
    def _():
        step(0)

    @pl.when(i % 2 == 1)
    def _():
        step(1)


def _moe_call(layer, x2d, mod, tiles_per_row, p, tm, name):
    n = x2d.shape[0]
    n_tiles = n // tm
    if tiles_per_row is None:
        assert n_tiles == 1
        mod_spec = pl.BlockSpec((None, n, 3 * D_MODEL), lambda i: (layer, 0, 1))
    else:
        mod_spec = pl.BlockSpec((None, SUBLANES, 3 * D_MODEL),
                                lambda i: (layer, PROMPT_MOD_ROW // SUBLANES, 1))
    lsel3 = lambda i: (layer, 0, 0)
    lsel4 = lambda i: (layer, 0, 0, 0)
    upper = np.triu(np.ones((tm, tm), np.float32), k=1)
    kernel = functools.partial(_moe_kernel, tm=tm, cap=MOE_CAP, tiles_per_row=tiles_per_row,
                               n_tiles=n_tiles)
    return pl.pallas_call(
        kernel,
        out_shape=jax.ShapeDtypeStruct((n, D_MODEL), F32),
        grid=(n_tiles,),
        in_specs=[
            pl.BlockSpec((tm, D_MODEL), lambda i: (i, 0)),
            pl.BlockSpec((tm, D_MODEL), lambda i: (jnp.minimum(i + 1, n_tiles - 1), 0)),
            mod_spec,
            _const_spec((None, 1, D_MODEL), lsel3),
            _const_spec((D_MODEL, LANES), lambda i: (0, 0)),
            _const_spec((N_EXPERTS, 1), lambda i: (0, 0)),
            _const_spec((tm, tm), lambda i: (0, 0)),
            _const_spec((None, N_EXPERTS, D_MODEL, EXPERT_FF), lsel4),
            _const_spec((None, N_EXPERTS, D_MODEL, EXPERT_FF), lsel4),
            _const_spec((None, N_GROUPS, EXPERTS_PER_GROUP * EXPERT_FF, D_MODEL), lsel4),
        ],
        out_specs=pl.BlockSpec((tm, D_MODEL), lambda i: (i, 0)),
        scratch_shapes=[
            pltpu.VMEM((tm, D_MODEL + LANES), BF),
            pltpu.VMEM((tm, D_MODEL + LANES), BF),
            pltpu.VMEM((SUBLANES, tm), jnp.int32),
            pltpu.VMEM((SUBLANES, tm), jnp.int32),
            pltpu.SMEM((2, N_GROUPS), jnp.int32),
        ],
        compiler_params=pltpu.CompilerParams(
            dimension_semantics=("arbitrary",),
            vmem_limit_bytes=VMEM_LIMIT),
        name=name,
    )(x2d, x2d, mod, p["n2g"], p["rw2"], p["rb"], jnp.asarray(upper, dtype=BF),
      p["moe_w1"], p["moe_w3"], p["moe_w2"])


def _sample_proj_kernel(x_ref, mod_ref, n1g_ref, w_ref, z_ref):
    mod = mod_ref[...]
    h = _modulate(x_ref[...], n1g_ref[...], mod[:, 0:D_MODEL], mod[:, D_MODEL:2 * D_MODEL])
    z_ref[...] = _dot(h.astype(BF), w_ref[...])


def _sample_proj_call(layer, xs, mod, p):
    n = xs.shape[0]
    return pl.pallas_call(
        _sample_proj_kernel,
        out_shape=jax.ShapeDtypeStruct((n, IN_WIDTH), F32),
        grid=(IN_WIDTH // PROJ_BLOCK,),
        in_specs=[
            pl.BlockSpec((n, D_MODEL), lambda j: (0, 0)),
            pl.BlockSpec((None, n, 3 * D_MODEL), lambda j: (layer, 0, 0)),
            pl.BlockSpec((None, 1, D_MODEL), lambda j: (layer, 0, 0)),
            pl.BlockSpec((None, D_MODEL, PROJ_BLOCK), lambda j: (layer, 0, j)),
        ],
        out_specs=pl.BlockSpec((n, PROJ_BLOCK), lambda j: (0, j)),
        compiler_params=pltpu.CompilerParams(dimension_semantics=("arbitrary",)),
        name=f"sample_proj_l{layer}",
    )(xs, mod, p["n1g"], p["w_in"])


def _shift_in_column(cache_t, new_rows, bb):
    flat = cache_t.reshape(bb * HEAD_DIM, WINDOW)
    shifted = pltpu.roll(flat, WINDOW - 1, 1).reshape(bb, HEAD_DIM, WINDOW)
    padded = jnp.concatenate([new_rows, jnp.zeros((LANES - bb, LANES), F32)], axis=0)
    new_t = padded.T
    is_last = lax.broadcasted_iota(jnp.int32, (HEAD_DIM, WINDOW), 1) == WINDOW - 1
    out = []
    for b in range(bb):
        col = new_t[0:HEAD_DIM, b:b + 1]
        out.append(jnp.where(is_last, col, shifted[b]))
    return out


def _sample_mixer_kernel(z_ref, x_ref, mod_ref, kt_ref, vt_ref, st_ref,
                         gmg_ref, gmw_ref, gmb_ref, cdw_ref, cb_ref, clg_ref, clb_ref,
                         qn_ref, kn_ref, ropec_ref, ropes_ref, sink_ref, bd_ref, wb_ref, wout_ref,
                         _k_all, _v_all, _st_all,
                         x1_ref, kto_ref, vto_ref, sto_ref, gv_ref,
                         qf_s, of_s, oc_s, zg_s, *, bb):
    i = pl.program_id(0)
    r0 = pl.multiple_of(i * bb, bb)
    z = z_ref[...]
    zg_s[pl.ds(r0, bb), :] = z[:, OFF_GATE:OFF_GATE + N_BRANCH * D_MODEL]

    u = _gelu_half(z[:, OFF_GM_U:OFF_GM_U + GM_WIDTH])
    gv = _gelu_half(z[:, OFF_GM_V:OFF_GM_V + GM_WIDTH])
    v = gv * lax.rsqrt(jnp.mean(gv * gv, axis=-1, keepdims=True) + EPS) * gmg_ref[...]
    gv_ref[...] = v
    o_gm = u * (v * gmw_ref[...] + gmb_ref[...])

    a = _sigmoid_times(z[:, OFF_CV_G:OFF_CV_G + CONV_WIDTH], z[:, OFF_CV_A:OFF_CV_A + CONV_WIDTH])
    y = cb_ref[...] + cdw_ref[CONV_K - 1:CONV_K, :] * a
    for j in range(CONV_K - 1):
        y = y + cdw_ref[j:j + 1, :] * st_ref[j]
    o_cv = _layer_norm_silu(y, clg_ref[...], clb_ref[...])
    sto_ref[0:CONV_K - 2] = st_ref[1:CONV_K - 1]
    sto_ref[CONV_K - 2] = a

    rc = ropec_ref[...]
    rs = ropes_ref[...]
    bd = bd_ref[...]
    q = _rope(_head_norm(z[:, OFF_Q:OFF_Q + ATT_WIDTH], bd, qn_ref[...]), rc, rs) * (HEAD_DIM ** -0.5)
    knew = _rope(_head_norm(z[:, OFF_K:OFF_K + KV_WIDTH], bd[0:LANES, 0:LANES], kn_ref[...]), rc, rs)
    vnew = z[:, OFF_V:OFF_V + KV_WIDTH]
    lo = _low_half((bb, LANES))
    for h in range(N_HEADS):
        tile = q[:, (h // 2) * LANES:(h // 2 + 1) * LANES]
        if h % 2 == 1:
            tile = pltpu.roll(tile, HEAD_DIM, 1)
        qf_s[:, h, :] = jnp.where(lo, tile, 0.0)
    k_low = [knew, pltpu.roll(knew, HEAD_DIM, 1)]
    v_low = [vnew, pltpu.roll(vnew, HEAD_DIM, 1)]
    for g in range(N_KV_HEADS):
        qg = qf_s[:, Q_REP * g:Q_REP * (g + 1), :]
        kt = kt_ref[:, g]
        vt = vt_ref[:, g]
        s = jnp.einsum("brd,bdp->brp", qg[:, :, 0:HEAD_DIM].astype(BF), kt.astype(BF),
                       preferred_element_type=F32)
        s_new = jnp.sum(qg * k_low[g][:, None, :], axis=-1, keepdims=True)
        sink = sink_ref[Q_REP * g:Q_REP * (g + 1), 0:1][None]
        m = jnp.maximum(jnp.maximum(jnp.max(s, axis=-1, keepdims=True), s_new), sink)
        pr = jnp.exp(s - m)
        p_new = jnp.exp(s_new - m)
        den = jnp.sum(pr, axis=-1, keepdims=True) + p_new + jnp.exp(sink - m)
        o = jnp.einsum("brp,bdp->brd", pr.astype(BF), vt.astype(BF), preferred_element_type=F32)
        o = (o + p_new * v_low[g][:, None, 0:HEAD_DIM]) / den
        of_s[:, Q_REP * g:Q_REP * (g + 1), :] = jnp.concatenate([o, jnp.zeros(o.shape, F32)], axis=-1)
        for b, tile in enumerate(_shift_in_column(kt, k_low[g], bb)):
            kto_ref[b, g] = tile
        for b, tile in enumerate(_shift_in_column(vt, v_low[g], bb)):
            vto_ref[b, g] = tile
    att_tiles = []
    for j in range(N_HEADS // 2):
        second = pltpu.roll(of_s[:, 2 * j + 1, :], HEAD_DIM, 1)
        att_tiles.append(jnp.where(lo, of_s[:, 2 * j, :], second))
    o_att = jnp.concatenate(att_tiles, axis=1)

    oc_s[pl.ds(r0, bb), 0:GM_WIDTH] = o_gm.astype(BF)
    oc_s[pl.ds(r0, bb), GM_WIDTH:GM_WIDTH + CONV_WIDTH] = o_cv.astype(BF)
    oc_s[pl.ds(r0, bb), GM_WIDTH + CONV_WIDTH:GM_WIDTH + CONV_WIDTH + ATT_WIDTH] = o_att.astype(BF)

    @pl.when(i == pl.num_programs(0) - 1)
    def _():
        acc = _sigmoid_times(zg_s[:, 0:D_MODEL], _dot(oc_s[:, 0:GM_WIDTH], wb_ref[0]))
        acc = acc + _sigmoid_times(zg_s[:, D_MODEL:2 * D_MODEL], _dot(
            oc_s[:, GM_WIDTH:GM_WIDTH + CONV_WIDTH], wb_ref[1]))
        acc = acc + _sigmoid_times(zg_s[:, 2 * D_MODEL:3 * D_MODEL], _dot(
            oc_s[:, GM_WIDTH + CONV_WIDTH:GM_WIDTH + CONV_WIDTH + ATT_WIDTH], wb_ref[2]))
        gt1 = mod_ref[...][:, 2 * D_MODEL:3 * D_MODEL]
        x1_ref[...] = x_ref[...] + gt1 * _dot(acc.astype(BF), wout_ref[...])


def _sample_mixer_call(layer, z, xs, mod, kt, vt, st, new_k, new_v, new_st, p, bb):
    n = xs.shape[0]
    lsel3 = lambda i: (layer, 0, 0)
    cache_block = (bb, N_KV_HEADS, HEAD_DIM, WINDOW)
    kernel = functools.partial(_sample_mixer_kernel, bb=bb)
    args = (z, xs, mod, kt, vt, st, p["gm_g"], p["gm_w0"], p["gm_b0"], p["conv_dw"], p["conv_b"],
            p["conv_ln_g"], p["conv_ln_b"], p["qn"], p["kn"], p["rope_c1"], p["rope_s1"],
            p["sink_lanes"], p["blockdiag"], p["w_branch"], p["w_out"], new_k, new_v, new_st)
    n_in = len(args)
    return pl.pallas_call(
        kernel,
        out_shape=(
            jax.ShapeDtypeStruct((n, D_MODEL), F32),
            jax.ShapeDtypeStruct(new_k.shape, F32),
            jax.ShapeDtypeStruct(new_v.shape, F32),
            jax.ShapeDtypeStruct(new_st.shape, F32),
            jax.ShapeDtypeStruct((n, GM_WIDTH), F32),
        ),
        input_output_aliases={n_in - 3: 1, n_in - 2: 2, n_in - 1: 3},
        grid=(n // bb,),
        in_specs=[
            pl.BlockSpec((bb, IN_WIDTH), lambda i: (i, 0)),
            pl.BlockSpec((n, D_MODEL), lambda i: (0, 0)),
            pl.BlockSpec((None, n, 3 * D_MODEL), lsel3),
            pl.BlockSpec((None,) + cache_block, lambda i: (layer, i, 0, 0, 0)),
            pl.BlockSpec((None,) + cache_block, lambda i: (layer, i, 0, 0, 0)),
            pl.BlockSpec((None, CONV_K - 1, bb, CONV_WIDTH), lambda i: (layer, 0, i, 0)),
            pl.BlockSpec((None, 1, GM_WIDTH), lsel3),
            pl.BlockSpec((None, 1, GM_WIDTH), lsel3),
            pl.BlockSpec((None, 1, GM_WIDTH), lsel3),
            pl.BlockSpec((None, CONV_K, CONV_WIDTH), lsel3),
            pl.BlockSpec((None, 1, CONV_WIDTH), lsel3),
            pl.BlockSpec((None, 1, CONV_WIDTH), lsel3),
            pl.BlockSpec((None, 1, CONV_WIDTH), lsel3),
            pl.BlockSpec((None, 1, ATT_WIDTH), lsel3),
            pl.BlockSpec((None, 1, LANES), lsel3),
            pl.BlockSpec((1, LANES), lambda i: (0, 0)),
            pl.BlockSpec((1, LANES), lambda i: (0, 0)),
            pl.BlockSpec((None, N_HEADS, LANES), lsel3),
            pl.BlockSpec((ATT_WIDTH, ATT_WIDTH), lambda i: (0, 0)),
            pl.BlockSpec((None, N_BRANCH, GM_WIDTH, D_MODEL), lambda i: (layer, 0, 0, 0)),
            pl.BlockSpec((None, D_MODEL, D_MODEL), lsel3),
            pl.BlockSpec(memory_space=pl.ANY),
            pl.BlockSpec(memory_space=pl.ANY),
            pl.BlockSpec(memory_space=pl.ANY),
        ],
        out_specs=(
            pl.BlockSpec((n, D_MODEL), lambda i: (0, 0)),
            pl.BlockSpec((None,) + cache_block, lambda i: (layer, i, 0, 0, 0)),
            pl.BlockSpec((None,) + cache_block, lambda i: (layer, i, 0, 0, 0)),
            pl.BlockSpec((None, CONV_K - 1, bb, CONV_WIDTH), lambda i: (layer, 0, i, 0)),
            pl.BlockSpec((bb, GM_WIDTH), lambda i: (i, 0)),
        ),
        scratch_shapes=[
            pltpu.VMEM((bb, N_HEADS, LANES), F32),
            pltpu.VMEM((bb, N_HEADS, LANES), F32),
            pltpu.VMEM((n, GM_WIDTH + CONV_WIDTH + ATT_WIDTH), BF),
            pltpu.VMEM((n, N_BRANCH * D_MODEL), F32),
        ],
        compiler_params=pltpu.CompilerParams(
            dimension_semantics=("arbitrary",), vmem_limit_bytes=VMEM_LIMIT),
        name=f"sample_mixers_l{layer}",
    )(*args)


def _rope_lane_tables():
    half = ROT_DIM // 2
    freqs = jnp.exp(-math.log(ROPE_THETA) * jnp.arange(half, dtype=F32) * (2.0 / ROT_DIM))
    rest = jnp.zeros((HEAD_DIM - ROT_DIM,), F32)
    freq64 = jnp.concatenate([freqs, freqs, rest])
    sign64 = jnp.concatenate([-jnp.ones((half,), F32), jnp.ones((half,), F32), rest])
    reps = LANES // HEAD_DIM
    return jnp.tile(freq64, reps)[None, :], jnp.tile(sign64, reps)[None, :]


def _cos_sin(pos, lane_freq):
    ang = pos.astype(F32)[:, None] * lane_freq
    return jnp.cos(ang), jnp.sin(ang)


def _band_bias():
    i = np.arange(WINDOW)[:, None]
    j = np.arange(2 * WINDOW)[None, :]
    band = (j >= i) & (j <= i + WINDOW)
    first = band & (j >= WINDOW)
    out = np.where(np.stack([first, band]), 0.0, NEG_BIG).astype(np.float32)
    return jnp.asarray(out)


def _blockdiag():
    idx = np.arange(ATT_WIDTH) // HEAD_DIM
    return jnp.asarray((idx[:, None] == idx[None, :]).astype(np.float32), dtype=BF)


def _dup_heads(w):
    h0, h1 = w[..., :HEAD_DIM], w[..., HEAD_DIM:]
    return jnp.concatenate([h0, h0, h1, h1], axis=-1)


def _prepare(norm1_g, norm2_g, w_in, gm_norm_g, gm_ws, gm_b, conv_dw, conv_b, conv_ln_g,
             conv_ln_b, q_norm_g, k_norm_g, attn_sinks, w_branch, w_out, router_w, router_b,
             moe_w1, moe_w3, moe_w2, seq, tm):
    w_kv2 = jnp.concatenate([_dup_heads(w_in[:, :, OFF_K:OFF_V]),
                             _dup_heads(w_in[:, :, OFF_V:OFF_GATE])], axis=-1).astype(BF)
    rw = router_w.astype(F32)
    rw_hi = rw.astype(BF)
    rw_lo = (rw - rw_hi.astype(F32)).astype(BF)
    rw2 = jnp.concatenate(
        [rw_hi, rw_lo, jnp.zeros((D_MODEL, LANES - 2 * N_EXPERTS), BF)], axis=1)
    lane_freq, lane_sign = _rope_lane_tables()
    cb, sb = _cos_sin(jnp.arange(seq // tm, dtype=jnp.int32) * tm, lane_freq)
    co, so = _cos_sin(jnp.arange(tm, dtype=jnp.int32), lane_freq)
    c1, s1 = _cos_sin(PAST_LEN + jnp.arange(1, dtype=jnp.int32), lane_freq)
    row3 = lambda a: a.reshape(DEPTH, 1, a.shape[-1])
    col = np.arange(IN_WIDTH)
    col_scale = jnp.asarray(np.where((col >= OFF_Q) & (col < OFF_GATE), 1.0, 0.5), F32)
    return {
        "n1g": row3(norm1_g), "n2g": row3(norm2_g), "w_in": (w_in * col_scale).astype(BF),
        "w_kv2": w_kv2,
        "gm_g": row3(gm_norm_g), "gm_ws": gm_ws,
        "gm_bias": jnp.repeat(jnp.swapaxes(gm_b, 1, 2), LANES, axis=2),
        "gm_w0": jnp.repeat(gm_ws[:, :, 0, 0], LANES, axis=1).reshape(DEPTH, 1, GM_WIDTH),
        "gm_b0": jnp.repeat(gm_b[:, :, 0], LANES, axis=1).reshape(DEPTH, 1, GM_WIDTH),
        "conv_dw": conv_dw, "conv_b": row3(conv_b), "conv_ln_g": row3(0.5 * conv_ln_g),
        "conv_ln_b": row3(0.5 * conv_ln_b),
        "qn": row3(jnp.tile(q_norm_g, (1, N_HEADS))),
        "kn": row3(jnp.tile(k_norm_g, (1, 2 * N_KV_HEADS))),
        "sinks": attn_sinks,
        "sink_lanes": jnp.broadcast_to(attn_sinks[:, :, None], (DEPTH, N_HEADS, LANES)),
        "rope_base": jnp.concatenate([cb, sb], axis=1)[:, None, :],
        "rope_off": jnp.concatenate([co, so], axis=1),
        "rope_sign": lane_sign, "rope_c1": c1, "rope_s1": s1 * lane_sign,
        "band_bias": _band_bias(), "blockdiag": _blockdiag(),
        "w_branch": (0.5 * w_branch).astype(BF), "w_out": w_out.astype(BF),
        "rw2": rw2,
        "rb": router_b.astype(F32).reshape(N_EXPERTS, 1),
        "moe_w1": moe_w1.astype(BF), "moe_w3": moe_w3.astype(BF),
        "moe_w2": moe_w2.astype(BF).reshape(DEPTH, N_GROUPS, EXPERTS_PER_GROUP * EXPERT_FF, D_MODEL),
    }


def kernel(x_prompt, x_sample, cache_win_k, cache_win_v, state_conv, c_prompt, c_sample, norm1_g, norm2_g, w_ada, b_ada, w_in, gm_norm_g, gm_ws, gm_b, conv_dw, conv_b, conv_ln_g, conv_ln_b, q_norm_g, k_norm_g, attn_sinks, w_branch, w_out, router_w, router_b, moe_w1, moe_w3, moe_w2):
    nb, seq, _ = x_prompt.shape
    ns = x_sample.shape[0]
    tm = min(MIX_TILE, seq)
    tmoe = min(MOE_TILE, seq)
    p = _prepare(norm1_g, norm2_g, w_in, gm_norm_g, gm_ws, gm_b, conv_dw, conv_b, conv_ln_g,
                 conv_ln_b, q_norm_g, k_norm_g, attn_sinks, w_branch, w_out, router_w, router_b,
                 moe_w1, moe_w3, moe_w2, seq, tm)
    assert ns == PROMPT_MOD_ROW and nb <= ADA_ROWS - PROMPT_MOD_ROW
    c_all = jnp.concatenate(
        [c_sample, c_prompt, jnp.zeros((ADA_ROWS - nb - ns, D_MODEL), F32)], axis=0)
    mod = _ada_call(c_all, w_ada, b_ada)

    kt = jnp.transpose(cache_win_k, (0, 1, 3, 4, 2))
    vt = jnp.transpose(cache_win_v, (0, 1, 3, 4, 2))
    st = jnp.transpose(state_conv, (0, 2, 1, 3))

    xp = x_prompt
    xs = x_sample.reshape(ns, D_MODEL)
    kp_l, vp_l, cp_l, gs_l = [], [], [], []
    new_k = jnp.zeros(kt.shape, F32)
    new_v = jnp.zeros(vt.shape, F32)
    new_st = jnp.zeros(st.shape, F32)
    for l in range(DEPTH):
        xp, kp, vp, cp = _mixer_call(l, xp, mod, p, tm)
        xp = _moe_call(l, xp.reshape(nb * seq, D_MODEL), mod, seq // tmoe, p, tmoe,
                       f"prompt_moe_l{l}").reshape(nb, seq, D_MODEL)
        kp_l.append(kp)
        vp_l.append(vp)
        cp_l.append(cp)

        z = _sample_proj_call(l, xs, mod, p)
        xs, new_k, new_v, new_st, gv = _sample_mixer_call(l, z, xs, mod, kt, vt, st,
                                                          new_k, new_v, new_st, p, SAMPLE_BLOCK)
        xs = _moe_call(l, xs, mod, None, p, ns, f"sample_moe_l{l}")
        gs_l.append(gv)

    kv_shape = (DEPTH, nb, WINDOW, N_KV_HEADS, HEAD_DIM)
    return (xp, xs.reshape(ns, 1, D_MODEL),
            jnp.stack(kp_l).reshape(kv_shape), jnp.stack(vp_l).reshape(kv_shape),
            jnp.stack(cp_l),
            jnp.transpose(new_k, (0, 1, 4, 2, 3)),
            jnp.transpose(new_v, (0, 1, 4, 2, 3)),
            jnp.transpose(new_st, (0, 2, 1, 3)),
            jnp.stack(gs_l).reshape(DEPTH, ns, 1, GM_WIDTH))
```

```python
import functools
import math

import jax
import jax.numpy as jnp
import numpy as np
from jax import lax
from jax.experimental import pallas as pl
from jax.experimental.pallas import tpu as pltpu

F32 = jnp.float32
BF = jnp.bfloat16

D_MODEL = 1024
DEPTH = 2
PAST_LEN = 8192
CHUNK = 128
GM_GROUPS = 4
GM_WIDTH = 512
CONV_WIDTH = 512
CONV_K = 31
N_HEADS = 8
N_KV_HEADS = 2
Q_REP = N_HEADS // N_KV_HEADS
HEAD_DIM = 64
ATT_WIDTH = N_HEADS * HEAD_DIM
KV_WIDTH = N_KV_HEADS * HEAD_DIM
WINDOW = 128
ROPE_THETA = 500000.0
ROT_DIM = HEAD_DIM // 4
N_BRANCH = 3
N_EXPERTS = 16
EXPERTS_PER_GROUP = 4
N_GROUPS = N_EXPERTS // EXPERTS_PER_GROUP
EXPERT_FF = 256
EPS = 1e-6

OFF_GM_U = 0
OFF_GM_V = 512
OFF_CV_A = 1024
OFF_CV_G = 1536
OFF_Q = 2048
OFF_K = 2560
OFF_V = 2688
OFF_GATE = 2816
IN_WIDTH = OFF_GATE + N_BRANCH * D_MODEL

LANES = 128
SUBLANES = 8
MXU_COLS = 256
HALO = 32
MIX_TILE = 512
MOE_TILE = 512
MOE_CAP = 144
ADA_ROWS = 136
PROMPT_MOD_ROW = 128
ADA_BLOCK = 1536
PROJ_BLOCK = IN_WIDTH // 2
SAMPLE_BLOCK = 32
VMEM_LIMIT = 56 * 1024 * 1024
NEG_BIG = -1e30


def _dot(a, b):
    return jnp.dot(a, b, preferred_element_type=F32)


def _dot_nt(a, b):
    return lax.dot_general(a, b, (((1,), (1,)), ((), ())), preferred_element_type=F32)


def _sigmoid(x):
    return 0.5 * jnp.tanh(0.5 * x) + 0.5


def _silu(x):
    return x * _sigmoid(x)


GELU_C = 0.7978845608028654
GELU_K = 0.044715


def _gelu_half(xh):
    return xh + xh * jnp.tanh(xh * (2.0 * GELU_C + (8.0 * GELU_C * GELU_K) * (xh * xh)))


def _sigmoid_times(zh, dh):
    return jnp.tanh(zh) * dh + dh


def _modulate(x, g, shift, scale):
    ms = jnp.mean(x * x, axis=-1, keepdims=True)
    return (x * lax.rsqrt(ms + EPS)) * (g * (1.0 + scale)) + shift


def _head_norm(x, blockdiag, g):
    ssum = _dot((x * x).astype(BF), blockdiag)
    return x * lax.rsqrt(ssum * (1.0 / HEAD_DIM) + EPS) * g


def _rope(x, c, s):
    width = x.shape[-1]
    reps = width // LANES
    cc = jnp.concatenate([c] * reps, axis=-1)
    ss = jnp.concatenate([s] * reps, axis=-1)
    lane = lax.broadcasted_iota(jnp.int32, x.shape, x.ndim - 1) % HEAD_DIM
    partner = jnp.where(lane < ROT_DIM // 2,
                        pltpu.roll(x, width - ROT_DIM // 2, x.ndim - 1),
                        pltpu.roll(x, ROT_DIM // 2, x.ndim - 1))
    return x * cc + partner * ss


def _layer_norm_silu(y, gh, bh):
    mu = jnp.mean(y, axis=-1, keepdims=True)
    yc = y - mu
    var = jnp.mean(yc * yc, axis=-1, keepdims=True)
    h = yc * lax.rsqrt(var + EPS) * gh + bh
    return h * jnp.tanh(h) + h


def _low_half(shape):
    return lax.broadcasted_iota(jnp.int32, shape, len(shape) - 1) % LANES < HEAD_DIM


def _const_spec(shape, index_map):
    return pl.BlockSpec(shape, index_map, pipeline_mode=pl.Buffered(1))


def _ada_kernel(c_ref, w_ref, b_ref, o_ref):
    s = _silu(c_ref[...]).astype(BF)
    o_ref[...] = _dot(s, w_ref[...].astype(BF)) + b_ref[...]


def _ada_call(c_all, w_ada, b_ada):
    nb = (6 * D_MODEL) // ADA_BLOCK
    return pl.pallas_call(
        _ada_kernel,
        out_shape=jax.ShapeDtypeStruct((DEPTH, ADA_ROWS, 6 * D_MODEL), F32),
        grid=(DEPTH, nb),
        in_specs=[
            pl.BlockSpec((ADA_ROWS, D_MODEL), lambda l, j: (0, 0)),
            pl.BlockSpec((None, D_MODEL, ADA_BLOCK), lambda l, j: (l, 0, j)),
            pl.BlockSpec((None, 1, ADA_BLOCK), lambda l, j: (l, 0, j)),
        ],
        out_specs=pl.BlockSpec((None, ADA_ROWS, ADA_BLOCK), lambda l, j: (l, 0, j)),
        compiler_params=pltpu.CompilerParams(
            dimension_semantics=("arbitrary", "arbitrary")),
        name="ada_mod",
    )(c_all, w_ada, b_ada.reshape(DEPTH, 1, 6 * D_MODEL))


def _mixer_kernel(sinks_ref, x_ref, mod_ref, n1g_ref, win_ref, gmg_ref, gmws_ref, gmb_ref,
                  cdw_ref, cb_ref, clg_ref, clb_ref, qn_ref, kn_ref, ropeb_ref, ropeo_ref,
                  sign_ref, bias_ref, bd_ref, wb_ref, wout_ref,
                  x1_ref, kwin_ref, vwin_ref, cst_ref,
                  kd_s, vd_s, abuf, oatt_s, hb_s, z_s, acc_s, *, layer, tm):
    t = pl.program_id(1)
    last = pl.num_programs(1) - 1
    nblk = tm // WINDOW

    @pl.when(t == 0)
    def _():
        kd_s[0:WINDOW, :] = jnp.zeros((WINDOW, 2 * LANES), BF)
        vd_s[0:WINDOW, :] = jnp.zeros((WINDOW, 2 * LANES), BF)
        abuf[0:HALO, :] = jnp.zeros((HALO, CONV_WIDTH), F32)
        abuf[HALO + tm:HALO + tm + SUBLANES, :] = jnp.zeros((SUBLANES, CONV_WIDTH), F32)

    x = x_ref[...]
    mod = mod_ref[pl.ds(pl.program_id(0), 1), :]
    sh1 = mod[:, 0:D_MODEL]
    sc1 = mod[:, D_MODEL:2 * D_MODEL]
    gt1 = mod[:, 2 * D_MODEL:3 * D_MODEL]
    hb_s[...] = _modulate(x, n1g_ref[...], sh1, sc1).astype(BF)

    def proj(off, width):
        z_s[:, off:off + width] = _dot(hb_s[...], win_ref[:, off:off + width])

    def zcols(off, width):
        return z_s[:, off:off + width]

    proj(OFF_CV_A, 2 * CONV_WIDTH)
    a = _sigmoid_times(zcols(OFF_CV_G, CONV_WIDTH), zcols(OFF_CV_A, CONV_WIDTH))
    abuf[HALO:HALO + tm, :] = a
    first_off = HALO - (CONV_K - 1)

    def conv_tile(ci):
        lanes = slice(ci * LANES, (ci + 1) * LANES)
        yt = cb_ref[:, lanes]
        for b in range(SUBLANES):
            part = None
            for off in range(b, HALO + 1, SUBLANES):
                if off < first_off:
                    continue
                term = (cdw_ref[off - first_off:off - first_off + 1, lanes]
                        * abuf[pl.ds(off - b, tm + SUBLANES), lanes])
                part = term if part is None else part + term
            yt = yt + part[b:b + tm, :]
        return yt

    proj(OFF_GM_U, 2 * GM_WIDTH)
    proj(OFF_Q, ATT_WIDTH + 2 * KV_WIDTH)

    def gated_branch(i, o_branch, first):
        for n in range(D_MODEL // MXU_COLS):
            cols = slice(n * MXU_COLS, (n + 1) * MXU_COLS)
            g0 = OFF_GATE + i * D_MODEL + n * MXU_COLS
            piece = _sigmoid_times(_dot(hb_s[...], win_ref[:, g0:g0 + MXU_COLS]),
                                   _dot(o_branch, wb_ref[i, :, cols]))
            if first:
                acc_s[:, cols] = piece
            else:
                acc_s[:, cols] += piece

    y = jnp.concatenate([conv_tile(ci) for ci in range(CONV_WIDTH // LANES)], axis=1)
    abuf[0:HALO, :] = abuf[tm:tm + HALO, :]
    o_cv = _layer_norm_silu(y, clg_ref[...], clb_ref[...]).astype(BF)
    gated_branch(1, o_cv, True)

    u = _gelu_half(zcols(OFF_GM_U, GM_WIDTH))
    gv = _gelu_half(zcols(OFF_GM_V, GM_WIDTH))
    v = gv * lax.rsqrt(jnp.mean(gv * gv, axis=-1, keepdims=True) + EPS) * gmg_ref[...]
    vb = v.astype(BF)
    row = lax.broadcasted_iota(jnp.int32, (CHUNK, CHUNK), 0)
    col = lax.broadcasted_iota(jnp.int32, (CHUNK, CHUNK), 1)
    ws = [jnp.where(row >= col, gmws_ref[g], 0.0).astype(BF) for g in range(GM_GROUPS)]
    gmb = gmb_ref[...]
    chunks = []
    for c in range(nblk):
        parts = [_dot(ws[g], vb[c * CHUNK:(c + 1) * CHUNK, g * LANES:(g + 1) * LANES])
                 for g in range(GM_GROUPS)]
        chunks.append(jnp.concatenate(parts, axis=1) + gmb)
    o_gm = (u * jnp.concatenate(chunks, axis=0)).astype(BF)
    gated_branch(0, o_gm, False)

    rbase = ropeb_ref[...]
    roff = ropeo_ref[...]
    cb_, sb_ = rbase[:, 0:LANES], rbase[:, LANES:2 * LANES]
    co_, so_ = roff[:, 0:LANES], roff[:, LANES:2 * LANES]
    rc = cb_ * co_ - sb_ * so_
    rs = (sb_ * co_ + cb_ * so_) * sign_ref[...]
    bd = bd_ref[...]
    q = _rope(_head_norm(zcols(OFF_Q, ATT_WIDTH), bd, qn_ref[...]), rc, rs)
    qb = (q * (HEAD_DIM ** -0.5)).astype(BF)
    k_nat = _rope(_head_norm(zcols(OFF_K, KV_WIDTH), bd[0:LANES, 0:LANES], kn_ref[...]), rc, rs)
    v_nat = zcols(OFF_V, KV_WIDTH)
    lo_kv = _low_half((tm, LANES))

    def doubled(x_nat):
        swapped = pltpu.roll(x_nat, HEAD_DIM, 1)
        return jnp.concatenate([jnp.where(lo_kv, x_nat, swapped), jnp.where(lo_kv, swapped, x_nat)],
                               axis=1)

    kd = doubled(k_nat)
    vd = doubled(v_nat)
    kd_s[WINDOW:WINDOW + tm, :] = kd.astype(BF)
    vd_s[WINDOW:WINDOW + tm, :] = vd.astype(BF)

    lo_q = _low_half((WINDOW, LANES))
    zero_q = jnp.zeros((WINDOW, LANES), BF)
    band = bias_ref[1]
    for bi in range(nblk):
        bias = jnp.where(t == 0, bias_ref[0], band) if bi == 0 else band
        bias4 = jnp.concatenate([bias] * Q_REP, axis=0)
        for g in range(N_KV_HEADS):
            tiles = [qb[bi * WINDOW:(bi + 1) * WINDOW, (2 * g + j) * LANES:(2 * g + j + 1) * LANES]
                     for j in range(2)]
            qs = jnp.concatenate([jnp.where(lo_q, tiles[0], zero_q), jnp.where(lo_q, zero_q, tiles[0]),
                                  jnp.where(lo_q, tiles[1], zero_q), jnp.where(lo_q, zero_q, tiles[1])],
                                 axis=0)
            keys = kd_s[bi * WINDOW:(bi + 2) * WINDOW, g * LANES:(g + 1) * LANES]
            vals = vd_s[bi * WINDOW:(bi + 2) * WINDOW, g * LANES:(g + 1) * LANES]
            s = _dot_nt(qs, keys) + bias4
            outs = []
            for hh in range(Q_REP):
                sink = sinks_ref[layer, Q_REP * g + hh]
                sh = s[hh * WINDOW:(hh + 1) * WINDOW, :]
                m = jnp.maximum(jnp.max(sh, axis=-1, keepdims=True), sink)
                p = jnp.exp(sh - m)
                den = jnp.sum(p, axis=-1, keepdims=True) + jnp.exp(sink - m)
                outs.append(_dot(p.astype(BF), vals) / den)
            for j in range(2):
                oatt_s[bi * WINDOW:(bi + 1) * WINDOW, (2 * g + j) * LANES:(2 * g + j + 1) * LANES] = (
                    jnp.where(lo_q, outs[2 * j], outs[2 * j + 1]).astype(BF))
    kd_s[0:WINDOW, :] = kd_s[tm:tm + WINDOW, :]
    vd_s[0:WINDOW, :] = vd_s[tm:tm + WINDOW, :]
    gated_branch(2, oatt_s[...], False)

    accb = acc_s[...].astype(BF)
    for n in range(D_MODEL // MXU_COLS):
        cols = slice(n * MXU_COLS, (n + 1) * MXU_COLS)
        x1_ref[:, cols] = x_ref[:, cols] + gt1[:, cols] * _dot(accb, wout_ref[:, cols])

    @pl.when(t == last)
    def _():
        kwin_ref[...] = k_nat[tm - WINDOW:tm, :]
        vwin_ref[...] = v_nat[tm - WINDOW:tm, :]
        cst_ref[...] = a[tm - (CONV_K - 1):tm, :]


def _mixer_call(layer, x, mod, p, tm):
    nb, seq, _ = x.shape
    nt = seq // tm
    lsel3 = lambda b, t: (layer, 0, 0)
    kernel = functools.partial(_mixer_kernel, layer=layer, tm=tm)
    return pl.pallas_call(
        kernel,
        out_shape=(
            jax.ShapeDtypeStruct((nb, seq, D_MODEL), F32),
            jax.ShapeDtypeStruct((nb, WINDOW, KV_WIDTH), F32),
            jax.ShapeDtypeStruct((nb, WINDOW, KV_WIDTH), F32),
            jax.ShapeDtypeStruct((nb, CONV_K - 1, CONV_WIDTH), F32),
        ),
        grid=(nb, nt),
        in_specs=[
            pl.BlockSpec(memory_space=pltpu.SMEM),
            pl.BlockSpec((None, tm, D_MODEL), lambda b, t: (b, t, 0)),
            pl.BlockSpec((None, SUBLANES, 3 * D_MODEL),
                         lambda b, t: (layer, PROMPT_MOD_ROW // SUBLANES, 0)),
            _const_spec((None, 1, D_MODEL), lsel3),
            _const_spec((None, D_MODEL, IN_WIDTH), lsel3),
            _const_spec((None, 1, GM_WIDTH), lsel3),
            _const_spec((None, GM_GROUPS, CHUNK, CHUNK), lambda b, t: (layer, 0, 0, 0)),
            _const_spec((None, CHUNK, GM_WIDTH), lsel3),
            _const_spec((None, CONV_K, CONV_WIDTH), lsel3),
            _const_spec((None, 1, CONV_WIDTH), lsel3),
            _const_spec((None, 1, CONV_WIDTH), lsel3),
            _const_spec((None, 1, CONV_WIDTH), lsel3),
            _const_spec((None, 1, ATT_WIDTH), lsel3),
            _const_spec((None, 1, LANES), lsel3),
            pl.BlockSpec((None, 1, 2 * LANES), lambda b, t: (t, 0, 0)),
            _const_spec((tm, 2 * LANES), lambda b, t: (0, 0)),
            _const_spec((1, LANES), lambda b, t: (0, 0)),
            _const_spec((2, WINDOW, 2 * WINDOW), lambda b, t: (0, 0, 0)),
            _const_spec((ATT_WIDTH, ATT_WIDTH), lambda b, t: (0, 0)),
            _const_spec((None, N_BRANCH, GM_WIDTH, D_MODEL), lambda b, t: (layer, 0, 0, 0)),
            _const_spec((None, D_MODEL, D_MODEL), lsel3),
        ],
        out_specs=(
            pl.BlockSpec((None, tm, D_MODEL), lambda b, t: (b, t, 0)),
            pl.BlockSpec((None, WINDOW, KV_WIDTH), lambda b, t: (b, 0, 0)),
            pl.BlockSpec((None, WINDOW, KV_WIDTH), lambda b, t: (b, 0, 0)),
            pl.BlockSpec((None, CONV_K - 1, CONV_WIDTH), lambda b, t: (b, 0, 0)),
        ),
        scratch_shapes=[
            pltpu.VMEM((WINDOW + tm, 2 * LANES), BF),
            pltpu.VMEM((WINDOW + tm, 2 * LANES), BF),
            pltpu.VMEM((HALO + tm + SUBLANES, CONV_WIDTH), F32),
            pltpu.VMEM((tm, ATT_WIDTH), BF),
            pltpu.VMEM((tm, D_MODEL), BF),
            pltpu.VMEM((tm, OFF_GATE), F32),
            pltpu.VMEM((tm, D_MODEL), F32),
        ],
        compiler_params=pltpu.CompilerParams(
            dimension_semantics=("arbitrary", "arbitrary"),
            vmem_limit_bytes=VMEM_LIMIT),
        name=f"prompt_mixers_l{layer}",
    )(p["sinks"], x, mod, p["n1g"], p["w_in"], p["gm_g"], p["gm_ws"], p["gm_bias"],
      p["conv_dw"], p["conv_b"], p["conv_ln_g"], p["conv_ln_b"], p["qn"], p["kn"],
      p["rope_base"], p["rope_off"], p["rope_sign"], p["band_bias"], p["blockdiag"],
      p["w_branch"], p["w_out"])


def _route_rows(logits_t, rb_ref):
    scores = _sigmoid(logits_t)
    biased = scores + rb_ref[...]
    rows = lambda a, gi: [a[EXPERTS_PER_GROUP * gi + k:EXPERTS_PER_GROUP * gi + k + 1, :]
                          for k in range(EXPERTS_PER_GROUP)]
    best = None
    idx = None
    for gi in range(N_GROUPS):
        b = rows(biased, gi)
        hi1, lo1 = jnp.maximum(b[0], b[1]), jnp.minimum(b[0], b[1])
        hi2, lo2 = jnp.maximum(b[2], b[3]), jnp.minimum(b[2], b[3])
        gs = jnp.maximum(hi1, hi2) + jnp.maximum(jnp.minimum(hi1, hi2), jnp.maximum(lo1, lo2))
        if gi == 0:
            best, idx = gs, jnp.zeros(gs.shape, jnp.int32)
        else:
            better = gs > best
            idx = jnp.where(better, gi, idx)
            best = jnp.where(better, gs, best)
    bsel = rows(biased, 0)
    ssel = rows(scores, 0)
    for gi in range(1, N_GROUPS):
        bg, sg = rows(biased, gi), rows(scores, gi)
        pick = idx == gi
        bsel = [jnp.where(pick, bg[k], bsel[k]) for k in range(EXPERTS_PER_GROUP)]
        ssel = [jnp.where(pick, sg[k], ssel[k]) for k in range(EXPERTS_PER_GROUP)]
    chosen = []
    for k in range(EXPERTS_PER_GROUP):
        rank = jnp.zeros(idx.shape, jnp.int32)
        for k2 in range(EXPERTS_PER_GROUP):
            if k2 == k:
                continue
            beats = (bsel[k2] > bsel[k]) | ((bsel[k2] == bsel[k]) & (k2 < k))
            rank = rank + beats.astype(jnp.int32)
        chosen.append(jnp.where(rank < 2, ssel[k], 0.0))
    den = chosen[0] + chosen[1] + chosen[2] + chosen[3]
    return idx, [c / den for c in chosen]


def _moe_route(x, mod, n2g_ref, rw2_ref, rb_ref, upper_ref, h_buf, rt_buf, c_buf, cnt_s, buf, tm):
    h2 = _modulate(x, n2g_ref[...], mod[:, 0:D_MODEL], mod[:, D_MODEL:2 * D_MODEL])
    hi = h2.astype(BF)
    lo = (h2 - hi.astype(F32)).astype(BF)
    rw2 = rw2_ref[...]
    lt = (_dot(hi, rw2) + _dot(lo, rw2)).T
    idx, comb = _route_rows(lt[0:N_EXPERTS, :] + lt[N_EXPERTS:2 * N_EXPERTS, :], rb_ref)

    onehot = [(idx == g).astype(F32) for g in range(N_GROUPS)]
    oh8 = jnp.concatenate(onehot + [jnp.zeros((SUBLANES - N_GROUPS, tm), F32)], axis=0)
    prefix = _dot(oh8.astype(BF), upper_ref[...])
    slot = onehot[0] * prefix[0:1, :]
    for g in range(1, N_GROUPS):
        slot = slot + onehot[g] * prefix[g:g + 1, :]
    rt_buf[0:1, :] = idx
    rt_buf[1:2, :] = slot.astype(jnp.int32)
    for g in range(N_GROUPS):
        cnt_s[buf, g] = jnp.sum(onehot[g]).astype(jnp.int32)

    for k in range(EXPERTS_PER_GROUP):
        c_buf[k:k + 1, :] = comb[k]
    h_buf[...] = hi


def _moe_experts(x_ref, gt2, w1_ref, w3_ref, w2_ref, o_ref, h_buf, rt_buf, c_buf, cnt_s, buf,
                 tm, cap, overflow):
    def dispatch(g, j):
        want = lax.broadcasted_iota(jnp.int32, (cap, tm), 0) + j * cap
        hit = (rt_buf[1:2, :] == want) & (rt_buf[0:1, :] == g)
        return hit, jnp.where(hit, 1.0, 0.0).astype(BF)

    def run_experts(g, hit, pmat):
        hg = _dot(pmat, h_buf[...]).astype(BF)
        parts = []
        for e in range(EXPERTS_PER_GROUP):
            a = _dot(hg, w1_ref[EXPERTS_PER_GROUP * g + e])
            b = _dot(hg, w3_ref[EXPERTS_PER_GROUP * g + e])
            ce = jnp.sum(jnp.where(hit, c_buf[e:e + 1, :], 0.0), axis=-1, keepdims=True)
            parts.append(_silu(a) * b * ce)
        act = jnp.concatenate(parts, axis=1).astype(BF)
        return _dot(act, w2_ref[g]).astype(BF)

    def scatter(pmat, y):
        return lax.dot_general(pmat, y, (((0,), (0,)), ((), ())), preferred_element_type=F32)

    if not overflow:
        sel = [dispatch(g, 0) for g in range(N_GROUPS)]
        ys = [run_experts(g, *sel[g]) for g in range(N_GROUPS)]
        y_tile = scatter(jnp.concatenate([s[1] for s in sel], axis=0), jnp.concatenate(ys, axis=0))
        o_ref[...] = x_ref[...] + gt2 * y_tile
        return

    def group_body(g, carry):
        nblk = (cnt_s[buf, g] + (cap - 1)) // cap

        def block_body(j, carry2):
            hit, pmat = dispatch(g, j)
            o_ref[...] += gt2 * scatter(pmat, run_experts(g, hit, pmat))
            return carry2

        lax.fori_loop(1, nblk, block_body, 0)
        return carry

    lax.fori_loop(0, N_GROUPS, group_body, 0)


def _moe_kernel(x_ref, xn_ref, mod_ref, n2g_ref, rw2_ref, rb_ref, upper_ref,
                w1_ref, w3_ref, w2_ref, o_ref, h0_s, h1_s, rt0_s, rt1_s, c0_s, c1_s, cnt_s,
                *, tm, cap, tiles_per_row, n_tiles):
    route = functools.partial(_moe_route, n2g_ref=n2g_ref, rw2_ref=rw2_ref, rb_ref=rb_ref,
                              upper_ref=upper_ref, cnt_s=cnt_s, tm=tm)
    experts = functools.partial(_moe_experts, x_ref, w1_ref=w1_ref,
                                w3_ref=w3_ref, w2_ref=w2_ref, o_ref=o_ref, cnt_s=cnt_s, tm=tm, cap=cap)
    bufs = [dict(h_buf=h0_s, rt_buf=rt0_s, c_buf=c0_s, buf=0),
            dict(h_buf=h1_s, rt_buf=rt1_s, c_buf=c1_s, buf=1)]
    if tiles_per_row is None:
        mod = mod_ref[...]
        route(x_ref[...], mod, **bufs[0])
        for overflow in (False, True):
            experts(mod[:, 2 * D_MODEL:3 * D_MODEL], overflow=overflow, **bufs[0])
        return

    i = pl.program_id(0)
    mod_row = lambda tile: mod_ref[pl.ds(tile // tiles_per_row, 1), :]

    @pl.when(i == 0)
    def _():
        route(x_ref[...], mod_row(i), **bufs[0])

    def step(cur):
        gt2 = mod_row(i)[:, 2 * D_MODEL:3 * D_MODEL]
        experts(gt2, overflow=False, **bufs[cur])
        route(xn_ref[...], mod_row(jnp.minimum(i + 1, n_tiles - 1)), **bufs[1 - cur])
        experts(gt2, overflow=True, **bufs[cur])

    @pl.when(i % 2 == 0)
    def _():
        step(0)

    @pl.when(i % 2 == 1)
    def _():
        step(1)


def _moe_call(layer, x2d, mod, tiles_per_row, p, tm, name):
    n = x2d.shape[0]
    n_tiles = n // tm
    if tiles_per_row is None:
        assert n_tiles == 1
        mod_spec = pl.BlockSpec((None, n, 3 * D_MODEL), lambda i: (layer, 0, 1))
    else:
        mod_spec = pl.BlockSpec((None, SUBLANES, 3 * D_MODEL),
                                lambda i: (layer, PROMPT_MOD_ROW // SUBLANES, 1))
    lsel3 = lambda i: (layer, 0, 0)
    lsel4 = lambda i: (layer, 0, 0, 0)
    upper = np.triu(np.ones((tm, tm), np.float32), k=1)
    kernel = functools.partial(_moe_kernel, tm=tm, cap=MOE_CAP, tiles_per_row=tiles_per_row,
                               n_tiles=n_tiles)
    return pl.pallas_call(
        kernel,
        out_shape=jax.ShapeDtypeStruct((n, D_MODEL), F32),
        grid=(n_tiles,),
        in_specs=[
            pl.BlockSpec((tm, D_MODEL), lambda i: (i, 0)),
            pl.BlockSpec((tm, D_MODEL), lambda i: (jnp.minimum(i + 1, n_tiles - 1), 0)),
            mod_spec,
            _const_spec((None, 1, D_MODEL), lsel3),
            _const_spec((D_MODEL, LANES), lambda i: (0, 0)),
            _const_spec((N_EXPERTS, 1), lambda i: (0, 0)),
            _const_spec((tm, tm), lambda i: (0, 0)),
            _const_spec((None, N_EXPERTS, D_MODEL, EXPERT_FF), lsel4),
            _const_spec((None, N_EXPERTS, D_MODEL, EXPERT_FF), lsel4),
            _const_spec((None, N_GROUPS, EXPERTS_PER_GROUP * EXPERT_FF, D_MODEL), lsel4),
        ],
        out_specs=pl.BlockSpec((tm, D_MODEL), lambda i: (i, 0)),
        scratch_shapes=[
            pltpu.VMEM((tm, D_MODEL), BF),
            pltpu.VMEM((tm, D_MODEL), BF),
            pltpu.VMEM((SUBLANES, tm), jnp.int32),
            pltpu.VMEM((SUBLANES, tm), jnp.int32),
            pltpu.VMEM((SUBLANES, tm), F32),
            pltpu.VMEM((SUBLANES, tm), F32),
            pltpu.SMEM((2, N_GROUPS), jnp.int32),
        ],
        compiler_params=pltpu.CompilerParams(
            dimension_semantics=("arbitrary",),
            vmem_limit_bytes=VMEM_LIMIT),
        name=name,
    )(x2d, x2d, mod, p["n2g"], p["rw2"], p["rb"], jnp.asarray(upper, dtype=BF),
      p["moe_w1"], p["moe_w3"], p["moe_w2"])


def _sample_proj_kernel(x_ref, mod_ref, n1g_ref, w_ref, z_ref):
    mod = mod_ref[...]
    h = _modulate(x_ref[...], n1g_ref[...], mod[:, 0:D_MODEL], mod[:, D_MODEL:2 * D_MODEL])
    z_ref[...] = _dot(h.astype(BF), w_ref[...])


def _sample_proj_call(layer, xs, mod, p):
    n = xs.shape[0]
    return pl.pallas_call(
        _sample_proj_kernel,
        out_shape=jax.ShapeDtypeStruct((n, IN_WIDTH), F32),
        grid=(IN_WIDTH // PROJ_BLOCK,),
        in_specs=[
            pl.BlockSpec((n, D_MODEL), lambda j: (0, 0)),
            pl.BlockSpec((None, n, 3 * D_MODEL), lambda j: (layer, 0, 0)),
            pl.BlockSpec((None, 1, D_MODEL), lambda j: (layer, 0, 0)),
            pl.BlockSpec((None, D_MODEL, PROJ_BLOCK), lambda j: (layer, 0, j)),
        ],
        out_specs=pl.BlockSpec((n, PROJ_BLOCK), lambda j: (0, j)),
        compiler_params=pltpu.CompilerParams(dimension_semantics=("arbitrary",)),
        name=f"sample_proj_l{layer}",
    )(xs, mod, p["n1g"], p["w_in"])


def _shift_in_column(cache_t, new_rows, bb):
    flat = cache_t.reshape(bb * HEAD_DIM, WINDOW)
    shifted = pltpu.roll(flat, WINDOW - 1, 1).reshape(bb, HEAD_DIM, WINDOW)
    padded = jnp.concatenate([new_rows, jnp.zeros((LANES - bb, LANES), F32)], axis=0)
    new_t = padded.T
    is_last = lax.broadcasted_iota(jnp.int32, (HEAD_DIM, WINDOW), 1) == WINDOW - 1
    out = []
    for b in range(bb):
        col = new_t[0:HEAD_DIM, b:b + 1]
        out.append(jnp.where(is_last, col, shifted[b]))
    return out


def _sample_mixer_kernel(z_ref, x_ref, mod_ref, kt_ref, vt_ref, st_ref,
                         gmg_ref, gmw_ref, gmb_ref, cdw_ref, cb_ref, clg_ref, clb_ref,
                         qn_ref, kn_ref, ropec_ref, ropes_ref, sink_ref, bd_ref, wb_ref, wout_ref,
                         _k_all, _v_all, _st_all,
                         x1_ref, kto_ref, vto_ref, sto_ref, gv_ref,
                         qf_s, of_s, oc_s, zg_s, *, bb):
    i = pl.program_id(0)
    r0 = pl.multiple_of(i * bb, bb)
    z = z_ref[...]
    zg_s[pl.ds(r0, bb), :] = z[:, OFF_GATE:OFF_GATE + N_BRANCH * D_MODEL]

    u = _gelu_half(z[:, OFF_GM_U:OFF_GM_U + GM_WIDTH])
    gv = _gelu_half(z[:, OFF_GM_V:OFF_GM_V + GM_WIDTH])
    v = gv * lax.rsqrt(jnp.mean(gv * gv, axis=-1, keepdims=True) + EPS) * gmg_ref[...]
    gv_ref[...] = v
    o_gm = u * (v * gmw_ref[...] + gmb_ref[...])

    a = _sigmoid_times(z[:, OFF_CV_G:OFF_CV_G + CONV_WIDTH], z[:, OFF_CV_A:OFF_CV_A + CONV_WIDTH])
    y = cb_ref[...] + cdw_ref[CONV_K - 1:CONV_K, :] * a
    for j in range(CONV_K - 1):
        y = y + cdw_ref[j:j + 1, :] * st_ref[j]
    o_cv = _layer_norm_silu(y, clg_ref[...], clb_ref[...])
    sto_ref[0:CONV_K - 2] = st_ref[1:CONV_K - 1]
    sto_ref[CONV_K - 2] = a

    rc = ropec_ref[...]
    rs = ropes_ref[...]
    bd = bd_ref[...]
    q = _rope(_head_norm(z[:, OFF_Q:OFF_Q + ATT_WIDTH], bd, qn_ref[...]), rc, rs) * (HEAD_DIM ** -0.5)
    knew = _rope(_head_norm(z[:, OFF_K:OFF_K + KV_WIDTH], bd[0:LANES, 0:LANES], kn_ref[...]), rc, rs)
    vnew = z[:, OFF_V:OFF_V + KV_WIDTH]
    lo = _low_half((bb, LANES))
    for h in range(N_HEADS):
        tile = q[:, (h // 2) * LANES:(h // 2 + 1) * LANES]
        if h % 2 == 1:
            tile = pltpu.roll(tile, HEAD_DIM, 1)
        qf_s[:, h, :] = jnp.where(lo, tile, 0.0)
    k_low = [knew, pltpu.roll(knew, HEAD_DIM, 1)]
    v_low = [vnew, pltpu.roll(vnew, HEAD_DIM, 1)]
    for g in range(N_KV_HEADS):
        qg = qf_s[:, Q_REP * g:Q_REP * (g + 1), :]
        kt = kt_ref[:, g]
        vt = vt_ref[:, g]
        s = jnp.einsum("brd,bdp->brp", qg[:, :, 0:HEAD_DIM].astype(BF), kt.astype(BF),
                       preferred_element_type=F32)
        s_new = jnp.sum(qg * k_low[g][:, None, :], axis=-1, keepdims=True)
        sink = sink_ref[Q_REP * g:Q_REP * (g + 1), 0:1][None]
        m = jnp.maximum(jnp.maximum(jnp.max(s, axis=-1, keepdims=True), s_new), sink)
        pr = jnp.exp(s - m)
        p_new = jnp.exp(s_new - m)
        den = jnp.sum(pr, axis=-1, keepdims=True) + p_new + jnp.exp(sink - m)
        o = jnp.einsum("brp,bdp->brd", pr.astype(BF), vt.astype(BF), preferred_element_type=F32)
        o = (o + p_new * v_low[g][:, None, 0:HEAD_DIM]) / den
        of_s[:, Q_REP * g:Q_REP * (g + 1), :] = jnp.concatenate([o, jnp.zeros(o.shape, F32)], axis=-1)
        for b, tile in enumerate(_shift_in_column(kt, k_low[g], bb)):
            kto_ref[b, g] = tile
        for b, tile in enumerate(_shift_in_column(vt, v_low[g], bb)):
            vto_ref[b, g] = tile
    att_tiles = []
    for j in range(N_HEADS // 2):
        second = pltpu.roll(of_s[:, 2 * j + 1, :], HEAD_DIM, 1)
        att_tiles.append(jnp.where(lo, of_s[:, 2 * j, :], second))
    o_att = jnp.concatenate(att_tiles, axis=1)

    oc_s[pl.ds(r0, bb), 0:GM_WIDTH] = o_gm.astype(BF)
    oc_s[pl.ds(r0, bb), GM_WIDTH:GM_WIDTH + CONV_WIDTH] = o_cv.astype(BF)
    oc_s[pl.ds(r0, bb), GM_WIDTH + CONV_WIDTH:GM_WIDTH + CONV_WIDTH + ATT_WIDTH] = o_att.astype(BF)

    @pl.when(i == pl.num_programs(0) - 1)
    def _():
        acc = _sigmoid_times(zg_s[:, 0:D_MODEL], _dot(oc_s[:, 0:GM_WIDTH], wb_ref[0]))
        acc = acc + _sigmoid_times(zg_s[:, D_MODEL:2 * D_MODEL], _dot(
            oc_s[:, GM_WIDTH:GM_WIDTH + CONV_WIDTH], wb_ref[1]))
        acc = acc + _sigmoid_times(zg_s[:, 2 * D_MODEL:3 * D_MODEL], _dot(
            oc_s[:, GM_WIDTH + CONV_WIDTH:GM_WIDTH + CONV_WIDTH + ATT_WIDTH], wb_ref[2]))
        gt1 = mod_ref[...][:, 2 * D_MODEL:3 * D_MODEL]
        x1_ref[...] = x_ref[...] + gt1 * _dot(acc.astype(BF), wout_ref[...])


def _sample_mixer_call(layer, z, xs, mod, kt, vt, st, new_k, new_v, new_st, p, bb):
    n = xs.shape[0]
    lsel3 = lambda i: (layer, 0, 0)
    cache_block = (bb, N_KV_HEADS, HEAD_DIM, WINDOW)
    kernel = functools.partial(_sample_mixer_kernel, bb=bb)
    args = (z, xs, mod, kt, vt, st, p["gm_g"], p["gm_w0"], p["gm_b0"], p["conv_dw"], p["conv_b"],
            p["conv_ln_g"], p["conv_ln_b"], p["qn"], p["kn"], p["rope_c1"], p["rope_s1"],
            p["sink_lanes"], p["blockdiag"], p["w_branch"], p["w_out"], new_k, new_v, new_st)
    n_in = len(args)
    return pl.pallas_call(
        kernel,
        out_shape=(
            jax.ShapeDtypeStruct((n, D_MODEL), F32),
            jax.ShapeDtypeStruct(new_k.shape, F32),
            jax.ShapeDtypeStruct(new_v.shape, F32),
            jax.ShapeDtypeStruct(new_st.shape, F32),
            jax.ShapeDtypeStruct((n, GM_WIDTH), F32),
        ),
        input_output_aliases={n_in - 3: 1, n_in - 2: 2, n_in - 1: 3},
        grid=(n // bb,),
        in_specs=[
            pl.BlockSpec((bb, IN_WIDTH), lambda i: (i, 0)),
            pl.BlockSpec((n, D_MODEL), lambda i: (0, 0)),
            pl.BlockSpec((None, n, 3 * D_MODEL), lsel3),
            pl.BlockSpec((None,) + cache_block, lambda i: (layer, i, 0, 0, 0)),
            pl.BlockSpec((None,) + cache_block, lambda i: (layer, i, 0, 0, 0)),
            pl.BlockSpec((None, CONV_K - 1, bb, CONV_WIDTH), lambda i: (layer, 0, i, 0)),
            pl.BlockSpec((None, 1, GM_WIDTH), lsel3),
            pl.BlockSpec((None, 1, GM_WIDTH), lsel3),
            pl.BlockSpec((None, 1, GM_WIDTH), lsel3),
            pl.BlockSpec((None, CONV_K, CONV_WIDTH), lsel3),
            pl.BlockSpec((None, 1, CONV_WIDTH), lsel3),
            pl.BlockSpec((None, 1, CONV_WIDTH), lsel3),
            pl.BlockSpec((None, 1, CONV_WIDTH), lsel3),
            pl.BlockSpec((None, 1, ATT_WIDTH), lsel3),
            pl.BlockSpec((None, 1, LANES), lsel3),
            pl.BlockSpec((1, LANES), lambda i: (0, 0)),
            pl.BlockSpec((1, LANES), lambda i: (0, 0)),
            pl.BlockSpec((None, N_HEADS, LANES), lsel3),
            pl.BlockSpec((ATT_WIDTH, ATT_WIDTH), lambda i: (0, 0)),
            pl.BlockSpec((None, N_BRANCH, GM_WIDTH, D_MODEL), lambda i: (layer, 0, 0, 0)),
            pl.BlockSpec((None, D_MODEL, D_MODEL), lsel3),
            pl.BlockSpec(memory_space=pl.ANY),
            pl.BlockSpec(memory_space=pl.ANY),
            pl.BlockSpec(memory_space=pl.ANY),
        ],
        out_specs=(
            pl.BlockSpec((n, D_MODEL), lambda i: (0, 0)),
            pl.BlockSpec((None,) + cache_block, lambda i: (layer, i, 0, 0, 0)),
            pl.BlockSpec((None,) + cache_block, lambda i: (layer, i, 0, 0, 0)),
            pl.BlockSpec((None, CONV_K - 1, bb, CONV_WIDTH), lambda i: (layer, 0, i, 0)),
            pl.BlockSpec((bb, GM_WIDTH), lambda i: (i, 0)),
        ),
        scratch_shapes=[
            pltpu.VMEM((bb, N_HEADS, LANES), F32),
            pltpu.VMEM((bb, N_HEADS, LANES), F32),
            pltpu.VMEM((n, GM_WIDTH + CONV_WIDTH + ATT_WIDTH), BF),
            pltpu.VMEM((n, N_BRANCH * D_MODEL), F32),
        ],
        compiler_params=pltpu.CompilerParams(
            dimension_semantics=("arbitrary",), vmem_limit_bytes=VMEM_LIMIT),
        name=f"sample_mixers_l{layer}",
    )(*args)


def _rope_lane_tables():
    half = ROT_DIM // 2
    freqs = jnp.exp(-math.log(ROPE_THETA) * jnp.arange(half, dtype=F32) * (2.0 / ROT_DIM))
    rest = jnp.zeros((HEAD_DIM - ROT_DIM,), F32)
    freq64 = jnp.concatenate([freqs, freqs, rest])
    sign64 = jnp.concatenate([-jnp.ones((half,), F32), jnp.ones((half,), F32), rest])
    reps = LANES // HEAD_DIM
    return jnp.tile(freq64, reps)[None, :], jnp.tile(sign64, reps)[None, :]


def _cos_sin(pos, lane_freq):
    ang = pos.astype(F32)[:, None] * lane_freq
    return jnp.cos(ang), jnp.sin(ang)


def _band_bias():
    i = np.arange(WINDOW)[:, None]
    j = np.arange(2 * WINDOW)[None, :]
    band = (j >= i) & (j <= i + WINDOW)
    first = band & (j >= WINDOW)
    out = np.where(np.stack([first, band]), 0.0, NEG_BIG).astype(np.float32)
    return jnp.asarray(out)


def _blockdiag():
    idx = np.arange(ATT_WIDTH) // HEAD_DIM
    return jnp.asarray((idx[:, None] == idx[None, :]).astype(np.float32), dtype=BF)


def _router_hi_lo(router_w):
    rw = router_w.astype(F32)
    rw_hi = rw.astype(BF)
    rw_lo = (rw - rw_hi.astype(F32)).astype(BF)
    return jnp.concatenate(
        [rw_hi, rw_lo, jnp.zeros((D_MODEL, LANES - 2 * N_EXPERTS), BF)], axis=1)


def _prepare(norm1_g, norm2_g, w_in, gm_norm_g, gm_ws, gm_b, conv_dw, conv_b, conv_ln_g,
             conv_ln_b, q_norm_g, k_norm_g, attn_sinks, w_branch, w_out, router_w, router_b,
             moe_w1, moe_w3, moe_w2, seq, tm):
    lane_freq, lane_sign = _rope_lane_tables()
    cb, sb = _cos_sin(jnp.arange(seq // tm, dtype=jnp.int32) * tm, lane_freq)
    co, so = _cos_sin(jnp.arange(tm, dtype=jnp.int32), lane_freq)
    c1, s1 = _cos_sin(PAST_LEN + jnp.arange(1, dtype=jnp.int32), lane_freq)
    row3 = lambda a: a.reshape(DEPTH, 1, a.shape[-1])
    col = np.arange(IN_WIDTH)
    col_scale = jnp.asarray(np.where((col >= OFF_Q) & (col < OFF_GATE), 1.0, 0.5), F32)
    return {
        "n1g": row3(norm1_g), "n2g": row3(norm2_g), "w_in": (w_in * col_scale).astype(BF),
        "gm_g": row3(gm_norm_g), "gm_ws": gm_ws,
        "gm_bias": jnp.repeat(jnp.swapaxes(gm_b, 1, 2), LANES, axis=2),
        "gm_w0": jnp.repeat(gm_ws[:, :, 0, 0], LANES, axis=1).reshape(DEPTH, 1, GM_WIDTH),
        "gm_b0": jnp.repeat(gm_b[:, :, 0], LANES, axis=1).reshape(DEPTH, 1, GM_WIDTH),
        "conv_dw": conv_dw, "conv_b": row3(conv_b), "conv_ln_g": row3(0.5 * conv_ln_g),
        "conv_ln_b": row3(0.5 * conv_ln_b),
        "qn": row3(jnp.tile(q_norm_g, (1, N_HEADS))),
        "kn": row3(jnp.tile(k_norm_g, (1, N_KV_HEADS))),
        "sinks": attn_sinks,
        "sink_lanes": jnp.broadcast_to(attn_sinks[:, :, None], (DEPTH, N_HEADS, LANES)),
        "rope_base": jnp.concatenate([cb, sb], axis=1)[:, None, :],
        "rope_off": jnp.concatenate([co, so], axis=1),
        "rope_sign": lane_sign, "rope_c1": c1, "rope_s1": s1 * lane_sign,
        "band_bias": _band_bias(), "blockdiag": _blockdiag(),
        "w_branch": (0.5 * w_branch).astype(BF), "w_out": w_out.astype(BF),
        "rw2": _router_hi_lo(router_w),
        "rb": router_b.astype(F32).reshape(N_EXPERTS, 1),
        "moe_w1": moe_w1.astype(BF), "moe_w3": moe_w3.astype(BF),
        "moe_w2": moe_w2.astype(BF).reshape(DEPTH, N_GROUPS, EXPERTS_PER_GROUP * EXPERT_FF, D_MODEL),
    }


def kernel(x_prompt, x_sample, cache_win_k, cache_win_v, state_conv, c_prompt, c_sample, norm1_g, norm2_g, w_ada, b_ada, w_in, gm_norm_g, gm_ws, gm_b, conv_dw, conv_b, conv_ln_g, conv_ln_b, q_norm_g, k_norm_g, attn_sinks, w_branch, w_out, router_w, router_b, moe_w1, moe_w3, moe_w2):
    nb, seq, _ = x_prompt.shape
    ns = x_sample.shape[0]
    tm = min(MIX_TILE, seq)
    tmoe = min(MOE_TILE, seq)
    p = _prepare(norm1_g, norm2_g, w_in, gm_norm_g, gm_ws, gm_b, conv_dw, conv_b, conv_ln_g,
                 conv_ln_b, q_norm_g, k_norm_g, attn_sinks, w_branch, w_out, router_w, router_b,
                 moe_w1, moe_w3, moe_w2, seq, tm)
    assert ns == PROMPT_MOD_ROW and nb <= ADA_ROWS - PROMPT_MOD_ROW
    c_all = jnp.concatenate(
        [c_sample, c_prompt, jnp.zeros((ADA_ROWS - nb - ns, D_MODEL), F32)], axis=0)
    mod = _ada_call(c_all, w_ada, b_ada)

    kt = jnp.transpose(cache_win_k, (0, 1, 3, 4, 2))
    vt = jnp.transpose(cache_win_v, (0, 1, 3, 4, 2))
    st = jnp.transpose(state_conv, (0, 2, 1, 3))

    xp = x_prompt
    xs = x_sample.reshape(ns, D_MODEL)
    kp_l, vp_l, cp_l, gs_l = [], [], [], []
    new_k = jnp.zeros(kt.shape, F32)
    new_v = jnp.zeros(vt.shape, F32)
    new_st = jnp.zeros(st.shape, F32)
    for l in range(DEPTH):
        xp, kp, vp, cp = _mixer_call(l, xp, mod, p, tm)
        xp = _moe_call(l, xp.reshape(nb * seq, D_MODEL), mod, seq // tmoe, p, tmoe,
                       f"prompt_moe_l{l}").reshape(nb, seq, D_MODEL)
        kp_l.append(kp)
        vp_l.append(vp)
        cp_l.append(cp)

        z = _sample_proj_call(l, xs, mod, p)
        xs, new_k, new_v, new_st, gv = _sample_mixer_call(l, z, xs, mod, kt, vt, st,
                                                          new_k, new_v, new_st, p, SAMPLE_BLOCK)
        xs = _moe_call(l, xs, mod, None, p, ns, f"sample_moe_l{l}")
        gs_l.append(gv)

    kv_shape = (DEPTH, nb, WINDOW, N_KV_HEADS, HEAD_DIM)
    return (xp, xs.reshape(ns, 1, D_MODEL),
            jnp.stack(kp_l).reshape(kv_shape), jnp.stack(vp_l).reshape(kv_shape),
            jnp.stack(cp_l),
            jnp.transpose(new_k, (0, 1, 4, 2, 3)),
            jnp.transpose(new_v, (0, 1, 4, 2, 3)),
            jnp.transpose(new_st, (0, 2, 1, 3)),
            jnp.stack(gs_l).reshape(DEPTH, ns, 1, GM_WIDTH))
```

```python
import functools
import math

import jax
import jax.numpy as jnp
import numpy as np
from jax import lax
from jax.experimental import pallas as pl
from jax.experimental.pallas import tpu as pltpu

F32 = jnp.float32
BF = jnp.bfloat16

D_MODEL = 1024
DEPTH = 2
PAST_LEN = 8192
CHUNK = 128
GM_GROUPS = 4
GM_WIDTH = 512
CONV_WIDTH = 512
CONV_K = 31
N_HEADS = 8
N_KV_HEADS = 2
Q_REP = N_HEADS // N_KV_HEADS
HEAD_DIM = 64
ATT_WIDTH = N_HEADS * HEAD_DIM
KV_WIDTH = N_KV_HEADS * HEAD_DIM
WINDOW = 128
ROPE_THETA = 500000.0
ROT_DIM = HEAD_DIM // 4
N_BRANCH = 3
N_EXPERTS = 16
EXPERTS_PER_GROUP = 4
N_GROUPS = N_EXPERTS // EXPERTS_PER_GROUP
EXPERT_FF = 256
EPS = 1e-6

OFF_GM_U = 0
OFF_GM_V = 512
OFF_CV_A = 1024
OFF_CV_G = 1536
OFF_Q = 2048
OFF_K = 2560
OFF_V = 2688
OFF_GATE = 2816
IN_WIDTH = OFF_GATE + N_BRANCH * D_MODEL

LANES = 128
SUBLANES = 8
MXU_COLS = 256
HALO = 32
MIX_TILE = 512
MOE_TILE = 512
MOE_CAP = 144
ADA_ROWS = 136
PROMPT_MOD_ROW = 128
ADA_BLOCK = 1536
PROJ_BLOCK = IN_WIDTH // 2
SAMPLE_BLOCK = 32
VMEM_LIMIT = 56 * 1024 * 1024
NEG_BIG = -1e30


def _dot(a, b):
    return jnp.dot(a, b, preferred_element_type=F32)


def _dot_nt(a, b):
    return lax.dot_general(a, b, (((1,), (1,)), ((), ())), preferred_element_type=F32)


def _sigmoid(x):
    return 0.5 * jnp.tanh(0.5 * x) + 0.5


def _silu(x):
    return x * _sigmoid(x)


GELU_C = 0.7978845608028654
GELU_K = 0.044715


def _gelu_half(xh):
    return xh + xh * jnp.tanh(xh * (2.0 * GELU_C + (8.0 * GELU_C * GELU_K) * (xh * xh)))


def _sigmoid_times(zh, dh):
    return jnp.tanh(zh) * dh + dh


def _modulate(x, g, shift, scale):
    ms = jnp.mean(x * x, axis=-1, keepdims=True)
    return (x * lax.rsqrt(ms + EPS)) * (g * (1.0 + scale)) + shift


def _head_norm(x, blockdiag, g):
    sq = (x * x).astype(BF)
    width = x.shape[-1]
    step = min(width, MXU_COLS)
    ssum = jnp.concatenate([_dot(sq[:, c:c + step], blockdiag[0:step, 0:step])
                            for c in range(0, width, step)], axis=-1)
    return x * lax.rsqrt(ssum * (1.0 / HEAD_DIM) + EPS) * g


def _rope(x, c, s):
    width = x.shape[-1]
    reps = width // LANES
    cc = jnp.concatenate([c] * reps, axis=-1)
    ss = jnp.concatenate([s] * reps, axis=-1)
    lane = lax.broadcasted_iota(jnp.int32, x.shape, x.ndim - 1) % HEAD_DIM
    partner = jnp.where(lane < ROT_DIM // 2,
                        pltpu.roll(x, width - ROT_DIM // 2, x.ndim - 1),
                        pltpu.roll(x, ROT_DIM // 2, x.ndim - 1))
    return x * cc + partner * ss


def _layer_norm_silu(y, gh, bh):
    mu = jnp.mean(y, axis=-1, keepdims=True)
    yc = y - mu
    var = jnp.mean(yc * yc, axis=-1, keepdims=True)
    h = yc * lax.rsqrt(var + EPS) * gh + bh
    return h * jnp.tanh(h) + h


def _low_half(shape):
    return lax.broadcasted_iota(jnp.int32, shape, len(shape) - 1) % LANES < HEAD_DIM


def _const_spec(shape, index_map):
    return pl.BlockSpec(shape, index_map, pipeline_mode=pl.Buffered(1))


def _ada_kernel(c_ref, w_ref, b_ref, o_ref):
    s = _silu(c_ref[...]).astype(BF)
    o_ref[...] = _dot(s, w_ref[...].astype(BF)) + b_ref[...]


def _ada_call(c_all, w_ada, b_ada):
    nb = (6 * D_MODEL) // ADA_BLOCK
    return pl.pallas_call(
        _ada_kernel,
        out_shape=jax.ShapeDtypeStruct((DEPTH, ADA_ROWS, 6 * D_MODEL), F32),
        grid=(DEPTH, nb),
        in_specs=[
            pl.BlockSpec((ADA_ROWS, D_MODEL), lambda l, j: (0, 0)),
            pl.BlockSpec((None, D_MODEL, ADA_BLOCK), lambda l, j: (l, 0, j)),
            pl.BlockSpec((None, 1, ADA_BLOCK), lambda l, j: (l, 0, j)),
        ],
        out_specs=pl.BlockSpec((None, ADA_ROWS, ADA_BLOCK), lambda l, j: (l, 0, j)),
        compiler_params=pltpu.CompilerParams(
            dimension_semantics=("arbitrary", "arbitrary")),
        name="ada_mod",
    )(c_all, w_ada, b_ada.reshape(DEPTH, 1, 6 * D_MODEL))


def _mixer_kernel(sinks_ref, x_ref, mod_ref, n1g_ref, win_ref, gmg_ref, gmws_ref, gmb_ref,
                  cdw_ref, cb_ref, clg_ref, clb_ref, qn_ref, kn_ref, ropeb_ref, ropeo_ref,
                  sign_ref, bias_ref, bd_ref, wb_ref, wout_ref,
                  x1_ref, kwin_ref, vwin_ref, cst_ref,
                  kd_s, vd_s, abuf, oatt_s, hb_s, z_s, acc_s, *, layer, tm):
    t = pl.program_id(1)
    last = pl.num_programs(1) - 1
    nblk = tm // WINDOW

    @pl.when(t == 0)
    def _():
        kd_s[0:WINDOW, :] = jnp.zeros((WINDOW, 2 * LANES), BF)
        vd_s[0:WINDOW, :] = jnp.zeros((WINDOW, 2 * LANES), BF)
        abuf[0:HALO, :] = jnp.zeros((HALO, CONV_WIDTH), F32)
        abuf[HALO + tm:HALO + tm + SUBLANES, :] = jnp.zeros((SUBLANES, CONV_WIDTH), F32)

    x = x_ref[...]
    mod = mod_ref[pl.ds(pl.program_id(0), 1), :]
    sh1 = mod[:, 0:D_MODEL]
    sc1 = mod[:, D_MODEL:2 * D_MODEL]
    gt1 = mod[:, 2 * D_MODEL:3 * D_MODEL]
    hb_s[...] = _modulate(x, n1g_ref[...], sh1, sc1).astype(BF)

    def proj(off, width):
        z_s[:, off:off + width] = _dot(hb_s[...], win_ref[:, off:off + width])

    def zcols(off, width):
        return z_s[:, off:off + width]

    proj(OFF_CV_A, 2 * CONV_WIDTH)
    a = _sigmoid_times(zcols(OFF_CV_G, CONV_WIDTH), zcols(OFF_CV_A, CONV_WIDTH))
    abuf[HALO:HALO + tm, :] = a
    first_off = HALO - (CONV_K - 1)

    def conv_tile(ci):
        lanes = slice(ci * LANES, (ci + 1) * LANES)
        yt = cb_ref[:, lanes]
        for b in range(SUBLANES):
            part = None
            for off in range(b, HALO + 1, SUBLANES):
                if off < first_off:
                    continue
                term = (cdw_ref[off - first_off:off - first_off + 1, lanes]
                        * abuf[pl.ds(off - b, tm + SUBLANES), lanes])
                part = term if part is None else part + term
            yt = yt + part[b:b + tm, :]
        return yt

    proj(OFF_GM_U, 2 * GM_WIDTH)
    proj(OFF_Q, ATT_WIDTH + 2 * KV_WIDTH)

    def gated_branch(i, o_branch, first):
        for n in range(D_MODEL // MXU_COLS):
            cols = slice(n * MXU_COLS, (n + 1) * MXU_COLS)
            g0 = OFF_GATE + i * D_MODEL + n * MXU_COLS
            piece = _sigmoid_times(_dot(hb_s[...], win_ref[:, g0:g0 + MXU_COLS]),
                                   _dot(o_branch, wb_ref[i, :, cols]))
            if first:
                acc_s[:, cols] = piece
            else:
                acc_s[:, cols] += piece

    y = jnp.concatenate([conv_tile(ci) for ci in range(CONV_WIDTH // LANES)], axis=1)
    abuf[0:HALO, :] = abuf[tm:tm + HALO, :]
    o_cv = _layer_norm_silu(y, clg_ref[...], clb_ref[...]).astype(BF)
    gated_branch(1, o_cv, True)

    u = _gelu_half(zcols(OFF_GM_U, GM_WIDTH))
    gv = _gelu_half(zcols(OFF_GM_V, GM_WIDTH))
    v = gv * lax.rsqrt(jnp.mean(gv * gv, axis=-1, keepdims=True) + EPS) * gmg_ref[...]
    vb = v.astype(BF)
    row = lax.broadcasted_iota(jnp.int32, (CHUNK, CHUNK), 0)
    col = lax.broadcasted_iota(jnp.int32, (CHUNK, CHUNK), 1)
    ws = [jnp.where(row >= col, gmws_ref[g], 0.0).astype(BF) for g in range(GM_GROUPS)]
    gmb = gmb_ref[...]
    chunks = []
    for c in range(nblk):
        parts = [_dot(ws[g], vb[c * CHUNK:(c + 1) * CHUNK, g * LANES:(g + 1) * LANES])
                 for g in range(GM_GROUPS)]
        chunks.append(jnp.concatenate(parts, axis=1) + gmb)
    o_gm = (u * jnp.concatenate(chunks, axis=0)).astype(BF)
    gated_branch(0, o_gm, False)

    rbase = ropeb_ref[...]
    roff = ropeo_ref[...]
    cb_, sb_ = rbase[:, 0:LANES], rbase[:, LANES:2 * LANES]
    co_, so_ = roff[:, 0:LANES], roff[:, LANES:2 * LANES]
    rc = cb_ * co_ - sb_ * so_
    rs = (sb_ * co_ + cb_ * so_) * sign_ref[...]
    bd = bd_ref[...]
    q = _rope(_head_norm(zcols(OFF_Q, ATT_WIDTH), bd, qn_ref[...]), rc, rs)
    qb = (q * (HEAD_DIM ** -0.5)).astype(BF)
    k_nat = _rope(_head_norm(zcols(OFF_K, KV_WIDTH), bd[0:LANES, 0:LANES], kn_ref[...]), rc, rs)
    v_nat = zcols(OFF_V, KV_WIDTH)
    lo_kv = _low_half((tm, LANES))

    def doubled(x_nat):
        swapped = pltpu.roll(x_nat, HEAD_DIM, 1)
        return jnp.concatenate([jnp.where(lo_kv, x_nat, swapped), jnp.where(lo_kv, swapped, x_nat)],
                               axis=1)

    kd = doubled(k_nat)
    vd = doubled(v_nat)
    kd_s[WINDOW:WINDOW + tm, :] = kd.astype(BF)
    vd_s[WINDOW:WINDOW + tm, :] = vd.astype(BF)

    lo_q = _low_half((WINDOW, LANES))
    zero_q = jnp.zeros((WINDOW, LANES), BF)
    band = bias_ref[1]
    for bi in range(nblk):
        bias = jnp.where(t == 0, bias_ref[0], band) if bi == 0 else band
        bias4 = jnp.concatenate([bias] * Q_REP, axis=0)
        for g in range(N_KV_HEADS):
            tiles = [qb[bi * WINDOW:(bi + 1) * WINDOW, (2 * g + j) * LANES:(2 * g + j + 1) * LANES]
                     for j in range(2)]
            qs = jnp.concatenate([jnp.where(lo_q, tiles[0], zero_q), jnp.where(lo_q, zero_q, tiles[0]),
                                  jnp.where(lo_q, tiles[1], zero_q), jnp.where(lo_q, zero_q, tiles[1])],
                                 axis=0)
            keys = kd_s[bi * WINDOW:(bi + 2) * WINDOW, g * LANES:(g + 1) * LANES]
            vals = vd_s[bi * WINDOW:(bi + 2) * WINDOW, g * LANES:(g + 1) * LANES]
            s = _dot_nt(qs, keys) + bias4
            outs = []
            for hh in range(Q_REP):
                sink = sinks_ref[layer, Q_REP * g + hh]
                sh = s[hh * WINDOW:(hh + 1) * WINDOW, :]
                m = jnp.maximum(jnp.max(sh, axis=-1, keepdims=True), sink)
                p = jnp.exp(sh - m)
                den = jnp.sum(p, axis=-1, keepdims=True) + jnp.exp(sink - m)
                outs.append(_dot(p.astype(BF), vals) / den)
            for j in range(2):
                oatt_s[bi * WINDOW:(bi + 1) * WINDOW, (2 * g + j) * LANES:(2 * g + j + 1) * LANES] = (
                    jnp.where(lo_q, outs[2 * j], outs[2 * j + 1]).astype(BF))
    kd_s[0:WINDOW, :] = kd_s[tm:tm + WINDOW, :]
    vd_s[0:WINDOW, :] = vd_s[tm:tm + WINDOW, :]
    gated_branch(2, oatt_s[...], False)

    accb = acc_s[...].astype(BF)
    for n in range(D_MODEL // MXU_COLS):
        cols = slice(n * MXU_COLS, (n + 1) * MXU_COLS)
        x1_ref[:, cols] = x_ref[:, cols] + gt1[:, cols] * _dot(accb, wout_ref[:, cols])

    @pl.when(t == last)
    def _():
        kwin_ref[...] = k_nat[tm - WINDOW:tm, :]
        vwin_ref[...] = v_nat[tm - WINDOW:tm, :]
        cst_ref[...] = a[tm - (CONV_K - 1):tm, :]


def _mixer_call(layer, x, mod, p, tm):
    nb, seq, _ = x.shape
    nt = seq // tm
    lsel3 = lambda b, t: (layer, 0, 0)
    kernel = functools.partial(_mixer_kernel, layer=layer, tm=tm)
    return pl.pallas_call(
        kernel,
        out_shape=(
            jax.ShapeDtypeStruct((nb, seq, D_MODEL), F32),
            jax.ShapeDtypeStruct((nb, WINDOW, KV_WIDTH), F32),
            jax.ShapeDtypeStruct((nb, WINDOW, KV_WIDTH), F32),
            jax.ShapeDtypeStruct((nb, CONV_K - 1, CONV_WIDTH), F32),
        ),
        grid=(nb, nt),
        in_specs=[
            pl.BlockSpec(memory_space=pltpu.SMEM),
            pl.BlockSpec((None, tm, D_MODEL), lambda b, t: (b, t, 0)),
            pl.BlockSpec((None, SUBLANES, 3 * D_MODEL),
                         lambda b, t: (layer, PROMPT_MOD_ROW // SUBLANES, 0)),
            _const_spec((None, 1, D_MODEL), lsel3),
            _const_spec((None, D_MODEL, IN_WIDTH), lsel3),
            _const_spec((None, 1, GM_WIDTH), lsel3),
            _const_spec((None, GM_GROUPS, CHUNK, CHUNK), lambda b, t: (layer, 0, 0, 0)),
            _const_spec((None, CHUNK, GM_WIDTH), lsel3),
            _const_spec((None, CONV_K, CONV_WIDTH), lsel3),
            _const_spec((None, 1, CONV_WIDTH), lsel3),
            _const_spec((None, 1, CONV_WIDTH), lsel3),
            _const_spec((None, 1, CONV_WIDTH), lsel3),
            _const_spec((None, 1, ATT_WIDTH), lsel3),
            _const_spec((None, 1, LANES), lsel3),
            pl.BlockSpec((None, 1, 2 * LANES), lambda b, t: (t, 0, 0)),
            _const_spec((tm, 2 * LANES), lambda b, t: (0, 0)),
            _const_spec((1, LANES), lambda b, t: (0, 0)),
            _const_spec((2, WINDOW, 2 * WINDOW), lambda b, t: (0, 0, 0)),
            _const_spec((ATT_WIDTH, ATT_WIDTH), lambda b, t: (0, 0)),
            _const_spec((None, N_BRANCH, GM_WIDTH, D_MODEL), lambda b, t: (layer, 0, 0, 0)),
            _const_spec((None, D_MODEL, D_MODEL), lsel3),
        ],
        out_specs=(
            pl.BlockSpec((None, tm, D_MODEL), lambda b, t: (b, t, 0)),
            pl.BlockSpec((None, WINDOW, KV_WIDTH), lambda b, t: (b, 0, 0)),
            pl.BlockSpec((None, WINDOW, KV_WIDTH), lambda b, t: (b, 0, 0)),
            pl.BlockSpec((None, CONV_K - 1, CONV_WIDTH), lambda b, t: (b, 0, 0)),
        ),
        scratch_shapes=[
            pltpu.VMEM((WINDOW + tm, 2 * LANES), BF),
            pltpu.VMEM((WINDOW + tm, 2 * LANES), BF),
            pltpu.VMEM((HALO + tm + SUBLANES, CONV_WIDTH), F32),
            pltpu.VMEM((tm, ATT_WIDTH), BF),
            pltpu.VMEM((tm, D_MODEL), BF),
            pltpu.VMEM((tm, OFF_GATE), F32),
            pltpu.VMEM((tm, D_MODEL), F32),
        ],
        compiler_params=pltpu.CompilerParams(
            dimension_semantics=("arbitrary", "arbitrary"),
            vmem_limit_bytes=VMEM_LIMIT),
        name=f"prompt_mixers_l{layer}",
    )(p["sinks"], x, mod, p["n1g"], p["w_in"], p["gm_g"], p["gm_ws"], p["gm_bias"],
      p["conv_dw"], p["conv_b"], p["conv_ln_g"], p["conv_ln_b"], p["qn"], p["kn"],
      p["rope_base"], p["rope_off"], p["rope_sign"], p["band_bias"], p["blockdiag"],
      p["w_branch"], p["w_out"])


def _route_rows(logits_t, rb_ref):
    scores = _sigmoid(logits_t)
    biased = scores + rb_ref[...]
    rows = lambda a, gi: [a[EXPERTS_PER_GROUP * gi + k:EXPERTS_PER_GROUP * gi + k + 1, :]
                          for k in range(EXPERTS_PER_GROUP)]
    best = None
    idx = None
    for gi in range(N_GROUPS):
        b = rows(biased, gi)
        hi1, lo1 = jnp.maximum(b[0], b[1]), jnp.minimum(b[0], b[1])
        hi2, lo2 = jnp.maximum(b[2], b[3]), jnp.minimum(b[2], b[3])
        gs = jnp.maximum(hi1, hi2) + jnp.maximum(jnp.minimum(hi1, hi2), jnp.maximum(lo1, lo2))
        if gi == 0:
            best, idx = gs, jnp.zeros(gs.shape, jnp.int32)
        else:
            better = gs > best
            idx = jnp.where(better, gi, idx)
            best = jnp.where(better, gs, best)
    bsel = rows(biased, 0)
    ssel = rows(scores, 0)
    for gi in range(1, N_GROUPS):
        bg, sg = rows(biased, gi), rows(scores, gi)
        pick = idx == gi
        bsel = [jnp.where(pick, bg[k], bsel[k]) for k in range(EXPERTS_PER_GROUP)]
        ssel = [jnp.where(pick, sg[k], ssel[k]) for k in range(EXPERTS_PER_GROUP)]
    chosen = []
    for k in range(EXPERTS_PER_GROUP):
        rank = jnp.zeros(idx.shape, jnp.int32)
        for k2 in range(EXPERTS_PER_GROUP):
            if k2 == k:
                continue
            beats = (bsel[k2] > bsel[k]) | ((bsel[k2] == bsel[k]) & (k2 < k))
            rank = rank + beats.astype(jnp.int32)
        chosen.append(jnp.where(rank < 2, ssel[k], 0.0))
    den = chosen[0] + chosen[1] + chosen[2] + chosen[3]
    return idx, [c / den for c in chosen]


def _moe_route(x, mod, n2g_ref, rw2_ref, rb_ref, upper_ref, h_buf, rt_buf, c_buf, cnt_s, buf, tm):
    h2 = _modulate(x, n2g_ref[...], mod[:, 0:D_MODEL], mod[:, D_MODEL:2 * D_MODEL])
    hi = h2.astype(BF)
    lo = (h2 - hi.astype(F32)).astype(BF)
    rw2 = rw2_ref[...]
    lt = _dot_nt(rw2, hi) + _dot_nt(rw2, lo)
    idx, comb = _route_rows(lt[0:N_EXPERTS, :] + lt[N_EXPERTS:2 * N_EXPERTS, :], rb_ref)

    onehot = [(idx == g).astype(F32) for g in range(N_GROUPS)]
    oh8 = jnp.concatenate(onehot + [jnp.zeros((SUBLANES - N_GROUPS, tm), F32)], axis=0)
    prefix = _dot(oh8.astype(BF), upper_ref[...])
    slot = onehot[0] * prefix[0:1, :]
    for g in range(1, N_GROUPS):
        slot = slot + onehot[g] * prefix[g:g + 1, :]
    rt_buf[0:1, :] = idx
    rt_buf[1:2, :] = slot.astype(jnp.int32)
    for g in range(N_GROUPS):
        cnt_s[buf, g] = jnp.sum(onehot[g]).astype(jnp.int32)

    for k in range(EXPERTS_PER_GROUP):
        c_buf[k:k + 1, :] = comb[k]
    h_buf[...] = hi


def _moe_experts(x_ref, gt2, w1_ref, w3_ref, w2_ref, o_ref, h_buf, rt_buf, c_buf, cnt_s, buf,
                 tm, cap, overflow):
    def dispatch(g, j):
        want = lax.broadcasted_iota(jnp.int32, (cap, tm), 0) + j * cap
        hit = (rt_buf[1:2, :] == want) & (rt_buf[0:1, :] == g)
        return hit, jnp.where(hit, 1.0, 0.0).astype(BF)

    def run_experts(g, hit, pmat):
        hg = _dot(pmat, h_buf[...]).astype(BF)
        parts = []
        for e in range(EXPERTS_PER_GROUP):
            a = _dot(hg, w1_ref[EXPERTS_PER_GROUP * g + e])
            b = _dot(hg, w3_ref[EXPERTS_PER_GROUP * g + e])
            ce = jnp.sum(jnp.where(hit, c_buf[e:e + 1, :], 0.0), axis=-1, keepdims=True)
            parts.append(_silu(a) * b * ce)
        act = jnp.concatenate(parts, axis=1).astype(BF)
        return _dot(act, w2_ref[g]).astype(BF)

    def scatter(pmat, y):
        return lax.dot_general(pmat, y, (((0,), (0,)), ((), ())), preferred_element_type=F32)

    if not overflow:
        sel = [dispatch(g, 0) for g in range(N_GROUPS)]
        ys = [run_experts(g, *sel[g]) for g in range(N_GROUPS)]
        y_tile = scatter(jnp.concatenate([s[1] for s in sel], axis=0), jnp.concatenate(ys, axis=0))
        o_ref[...] = x_ref[...] + gt2 * y_tile
        return

    def group_body(g, carry):
        nblk = (cnt_s[buf, g] + (cap - 1)) // cap

        def block_body(j, carry2):
            hit, pmat = dispatch(g, j)
            o_ref[...] += gt2 * scatter(pmat, run_experts(g, hit, pmat))
            return carry2

        lax.fori_loop(1, nblk, block_body, 0)
        return carry

    lax.fori_loop(0, N_GROUPS, group_body, 0)


def _moe_kernel(x_ref, xn_ref, mod_ref, n2g_ref, rw2_ref, rb_ref, upper_ref,
                w1_ref, w3_ref, w2_ref, o_ref, h0_s, h1_s, rt0_s, rt1_s, c0_s, c1_s, cnt_s,
                *, tm, cap, tiles_per_row, n_tiles):
    route = functools.partial(_moe_route, n2g_ref=n2g_ref, rw2_ref=rw2_ref, rb_ref=rb_ref,
                              upper_ref=upper_ref, cnt_s=cnt_s, tm=tm)
    experts = functools.partial(_moe_experts, x_ref, w1_ref=w1_ref,
                                w3_ref=w3_ref, w2_ref=w2_ref, o_ref=o_ref, cnt_s=cnt_s, tm=tm, cap=cap)
    bufs = [dict(h_buf=h0_s, rt_buf=rt0_s, c_buf=c0_s, buf=0),
            dict(h_buf=h1_s, rt_buf=rt1_s, c_buf=c1_s, buf=1)]
    if tiles_per_row is None:
        mod = mod_ref[...]
        route(x_ref[...], mod, **bufs[0])
        for overflow in (False, True):
            experts(mod[:, 2 * D_MODEL:3 * D_MODEL], overflow=overflow, **bufs[0])
        return

    i = pl.program_id(0)
    mod_row = lambda tile: mod_ref[pl.ds(tile // tiles_per_row, 1), :]

    @pl.when(i == 0)
    def _():
        route(x_ref[...], mod_row(i), **bufs[0])

    def step(cur):
        gt2 = mod_row(i)[:, 2 * D_MODEL:3 * D_MODEL]
        experts(gt2, overflow=False, **bufs[cur])
        route(xn_ref[...], mod_row(jnp.minimum(i + 1, n_tiles - 1)), **bufs[1 - cur])
        experts(gt2, overflow=True, **bufs[cur])

    @pl.when(i % 2 == 0)
    def _():
        step(0)

    @pl.when(i % 2 == 1)
    def _():
        step(1)


def _moe_call(layer, x2d, mod, tiles_per_row, p, tm, name):
    n = x2d.shape[0]
    n_tiles = n // tm
    if tiles_per_row is None:
        assert n_tiles == 1
        mod_spec = pl.BlockSpec((None, n, 3 * D_MODEL), lambda i: (layer, 0, 1))
    else:
        mod_spec = pl.BlockSpec((None, SUBLANES, 3 * D_MODEL),
                                lambda i: (layer, PROMPT_MOD_ROW // SUBLANES, 1))
    lsel3 = lambda i: (layer, 0, 0)
    lsel4 = lambda i: (layer, 0, 0, 0)
    upper = np.triu(np.ones((tm, tm), np.float32), k=1)
    kernel = functools.partial(_moe_kernel, tm=tm, cap=MOE_CAP, tiles_per_row=tiles_per_row,
                               n_tiles=n_tiles)
    return pl.pallas_call(
        kernel,
        out_shape=jax.ShapeDtypeStruct((n, D_MODEL), F32),
        grid=(n_tiles,),
        in_specs=[
            pl.BlockSpec((tm, D_MODEL), lambda i: (i, 0)),
            pl.BlockSpec((tm, D_MODEL), lambda i: (jnp.minimum(i + 1, n_tiles - 1), 0)),
            mod_spec,
            _const_spec((None, 1, D_MODEL), lsel3),
            _const_spec((2 * N_EXPERTS, D_MODEL), lambda i: (0, 0)),
            _const_spec((N_EXPERTS, 1), lambda i: (0, 0)),
            _const_spec((tm, tm), lambda i: (0, 0)),
            _const_spec((None, N_EXPERTS, D_MODEL, EXPERT_FF), lsel4),
            _const_spec((None, N_EXPERTS, D_MODEL, EXPERT_FF), lsel4),
            _const_spec((None, N_GROUPS, EXPERTS_PER_GROUP * EXPERT_FF, D_MODEL), lsel4),
        ],
        out_specs=pl.BlockSpec((tm, D_MODEL), lambda i: (i, 0)),
        scratch_shapes=[
            pltpu.VMEM((tm, D_MODEL), BF),
            pltpu.VMEM((tm, D_MODEL), BF),
            pltpu.VMEM((SUBLANES, tm), jnp.int32),
            pltpu.VMEM((SUBLANES, tm), jnp.int32),
            pltpu.VMEM((SUBLANES, tm), F32),
            pltpu.VMEM((SUBLANES, tm), F32),
            pltpu.SMEM((2, N_GROUPS), jnp.int32),
        ],
        compiler_params=pltpu.CompilerParams(
            dimension_semantics=("arbitrary",),
            vmem_limit_bytes=VMEM_LIMIT),
        name=name,
    )(x2d, x2d, mod, p["n2g"], p["rw2"], p["rb"], jnp.asarray(upper, dtype=BF),
      p["moe_w1"], p["moe_w3"], p["moe_w2"])


def _sample_proj_kernel(x_ref, mod_ref, n1g_ref, w_ref, z_ref):
    mod = mod_ref[...]
    h = _modulate(x_ref[...], n1g_ref[...], mod[:, 0:D_MODEL], mod[:, D_MODEL:2 * D_MODEL])
    z_ref[...] = _dot(h.astype(BF), w_ref[...])


def _sample_proj_call(layer, xs, mod, p):
    n = xs.shape[0]
    return pl.pallas_call(
        _sample_proj_kernel,
        out_shape=jax.ShapeDtypeStruct((n, IN_WIDTH), F32),
        grid=(IN_WIDTH // PROJ_BLOCK,),
        in_specs=[
            pl.BlockSpec((n, D_MODEL), lambda j: (0, 0)),
            pl.BlockSpec((None, n, 3 * D_MODEL), lambda j: (layer, 0, 0)),
            pl.BlockSpec((None, 1, D_MODEL), lambda j: (layer, 0, 0)),
            pl.BlockSpec((None, D_MODEL, PROJ_BLOCK), lambda j: (layer, 0, j)),
        ],
        out_specs=pl.BlockSpec((n, PROJ_BLOCK), lambda j: (0, j)),
        compiler_params=pltpu.CompilerParams(dimension_semantics=("arbitrary",)),
        name=f"sample_proj_l{layer}",
    )(xs, mod, p["n1g"], p["w_in"])


def _shift_in_column(cache_t, new_rows, bb):
    flat = cache_t.reshape(bb * HEAD_DIM, WINDOW)
    shifted = pltpu.roll(flat, WINDOW - 1, 1).reshape(bb, HEAD_DIM, WINDOW)
    padded = jnp.concatenate([new_rows, jnp.zeros((LANES - bb, LANES), F32)], axis=0)
    new_t = padded.T
    is_last = lax.broadcasted_iota(jnp.int32, (HEAD_DIM, WINDOW), 1) == WINDOW - 1
    out = []
    for b in range(bb):
        col = new_t[0:HEAD_DIM, b:b + 1]
        out.append(jnp.where(is_last, col, shifted[b]))
    return out


def _sample_mixer_kernel(z_ref, x_ref, mod_ref, kt_ref, vt_ref, st_ref,
                         gmg_ref, gmw_ref, gmb_ref, cdw_ref, cb_ref, clg_ref, clb_ref,
                         qn_ref, kn_ref, ropec_ref, ropes_ref, sink_ref, bd_ref, wb_ref, wout_ref,
                         _k_all, _v_all, _st_all,
                         x1_ref, kto_ref, vto_ref, sto_ref, gv_ref,
                         qf_s, of_s, oc_s, zg_s, *, bb):
    i = pl.program_id(0)
    r0 = pl.multiple_of(i * bb, bb)
    z = z_ref[...]
    zg_s[pl.ds(r0, bb), :] = z[:, OFF_GATE:OFF_GATE + N_BRANCH * D_MODEL]

    u = _gelu_half(z[:, OFF_GM_U:OFF_GM_U + GM_WIDTH])
    gv = _gelu_half(z[:, OFF_GM_V:OFF_GM_V + GM_WIDTH])
    v = gv * lax.rsqrt(jnp.mean(gv * gv, axis=-1, keepdims=True) + EPS) * gmg_ref[...]
    gv_ref[...] = v
    o_gm = u * (v * gmw_ref[...] + gmb_ref[...])

    a = _sigmoid_times(z[:, OFF_CV_G:OFF_CV_G + CONV_WIDTH], z[:, OFF_CV_A:OFF_CV_A + CONV_WIDTH])
    y = cb_ref[...] + cdw_ref[CONV_K - 1:CONV_K, :] * a
    for j in range(CONV_K - 1):
        y = y + cdw_ref[j:j + 1, :] * st_ref[j]
    o_cv = _layer_norm_silu(y, clg_ref[...], clb_ref[...])
    sto_ref[0:CONV_K - 2] = st_ref[1:CONV_K - 1]
    sto_ref[CONV_K - 2] = a

    rc = ropec_ref[...]
    rs = ropes_ref[...]
    bd = bd_ref[...]
    q = _rope(_head_norm(z[:, OFF_Q:OFF_Q + ATT_WIDTH], bd, qn_ref[...]), rc, rs) * (HEAD_DIM ** -0.5)
    knew = _rope(_head_norm(z[:, OFF_K:OFF_K + KV_WIDTH], bd[0:LANES, 0:LANES], kn_ref[...]), rc, rs)
    vnew = z[:, OFF_V:OFF_V + KV_WIDTH]
    lo = _low_half((bb, LANES))
    for h in range(N_HEADS):
        tile = q[:, (h // 2) * LANES:(h // 2 + 1) * LANES]
        if h % 2 == 1:
            tile = pltpu.roll(tile, HEAD_DIM, 1)
        qf_s[:, h, :] = jnp.where(lo, tile, 0.0)
    k_low = [knew, pltpu.roll(knew, HEAD_DIM, 1)]
    v_low = [vnew, pltpu.roll(vnew, HEAD_DIM, 1)]
    for g in range(N_KV_HEADS):
        qg = qf_s[:, Q_REP * g:Q_REP * (g + 1), :]
        kt = kt_ref[:, g]
        vt = vt_ref[:, g]
        s = jnp.einsum("brd,bdp->brp", qg[:, :, 0:HEAD_DIM].astype(BF), kt.astype(BF),
                       preferred_element_type=F32)
        s_new = jnp.sum(qg * k_low[g][:, None, :], axis=-1, keepdims=True)
        sink = sink_ref[Q_REP * g:Q_REP * (g + 1), 0:1][None]
        m = jnp.maximum(jnp.maximum(jnp.max(s, axis=-1, keepdims=True), s_new), sink)
        pr = jnp.exp(s - m)
        p_new = jnp.exp(s_new - m)
        den = jnp.sum(pr, axis=-1, keepdims=True) + p_new + jnp.exp(sink - m)
        o = jnp.einsum("brp,bdp->brd", pr.astype(BF), vt.astype(BF), preferred_element_type=F32)
        o = (o + p_new * v_low[g][:, None, 0:HEAD_DIM]) / den
        of_s[:, Q_REP * g:Q_REP * (g + 1), :] = jnp.concatenate([o, jnp.zeros(o.shape, F32)], axis=-1)
        for b, tile in enumerate(_shift_in_column(kt, k_low[g], bb)):
            kto_ref[b, g] = tile
        for b, tile in enumerate(_shift_in_column(vt, v_low[g], bb)):
            vto_ref[b, g] = tile
    att_tiles = []
    for j in range(N_HEADS // 2):
        second = pltpu.roll(of_s[:, 2 * j + 1, :], HEAD_DIM, 1)
        att_tiles.append(jnp.where(lo, of_s[:, 2 * j, :], second))
    o_att = jnp.concatenate(att_tiles, axis=1)

    oc_s[pl.ds(r0, bb), 0:GM_WIDTH] = o_gm.astype(BF)
    oc_s[pl.ds(r0, bb), GM_WIDTH:GM_WIDTH + CONV_WIDTH] = o_cv.astype(BF)
    oc_s[pl.ds(r0, bb), GM_WIDTH + CONV_WIDTH:GM_WIDTH + CONV_WIDTH + ATT_WIDTH] = o_att.astype(BF)

    @pl.when(i == pl.num_programs(0) - 1)
    def _():
        acc = _sigmoid_times(zg_s[:, 0:D_MODEL], _dot(oc_s[:, 0:GM_WIDTH], wb_ref[0]))
        acc = acc + _sigmoid_times(zg_s[:, D_MODEL:2 * D_MODEL], _dot(
            oc_s[:, GM_WIDTH:GM_WIDTH + CONV_WIDTH], wb_ref[1]))
        acc = acc + _sigmoid_times(zg_s[:, 2 * D_MODEL:3 * D_MODEL], _dot(
            oc_s[:, GM_WIDTH + CONV_WIDTH:GM_WIDTH + CONV_WIDTH + ATT_WIDTH], wb_ref[2]))
        gt1 = mod_ref[...][:, 2 * D_MODEL:3 * D_MODEL]
        x1_ref[...] = x_ref[...] + gt1 * _dot(acc.astype(BF), wout_ref[...])


def _sample_mixer_call(layer, z, xs, mod, kt, vt, st, new_k, new_v, new_st, p, bb):
    n = xs.shape[0]
    lsel3 = lambda i: (layer, 0, 0)
    cache_block = (bb, N_KV_HEADS, HEAD_DIM, WINDOW)
    kernel = functools.partial(_sample_mixer_kernel, bb=bb)
    args = (z, xs, mod, kt, vt, st, p["gm_g"], p["gm_w0"], p["gm_b0"], p["conv_dw"], p["conv_b"],
            p["conv_ln_g"], p["conv_ln_b"], p["qn"], p["kn"], p["rope_c1"], p["rope_s1"],
            p["sink_lanes"], p["blockdiag"], p["w_branch"], p["w_out"], new_k, new_v, new_st)
    n_in = len(args)
    return pl.pallas_call(
        kernel,
        out_shape=(
            jax.ShapeDtypeStruct((n, D_MODEL), F32),
            jax.ShapeDtypeStruct(new_k.shape, F32),
            jax.ShapeDtypeStruct(new_v.shape, F32),
            jax.ShapeDtypeStruct(new_st.shape, F32),
            jax.ShapeDtypeStruct((n, GM_WIDTH), F32),
        ),
        input_output_aliases={n_in - 3: 1, n_in - 2: 2, n_in - 1: 3},
        grid=(n // bb,),
        in_specs=[
            pl.BlockSpec((bb, IN_WIDTH), lambda i: (i, 0)),
            pl.BlockSpec((n, D_MODEL), lambda i: (0, 0)),
            pl.BlockSpec((None, n, 3 * D_MODEL), lsel3),
            pl.BlockSpec((None,) + cache_block, lambda i: (layer, i, 0, 0, 0)),
            pl.BlockSpec((None,) + cache_block, lambda i: (layer, i, 0, 0, 0)),
            pl.BlockSpec((None, CONV_K - 1, bb, CONV_WIDTH), lambda i: (layer, 0, i, 0)),
            pl.BlockSpec((None, 1, GM_WIDTH), lsel3),
            pl.BlockSpec((None, 1, GM_WIDTH), lsel3),
            pl.BlockSpec((None, 1, GM_WIDTH), lsel3),
            pl.BlockSpec((None, CONV_K, CONV_WIDTH), lsel3),
            pl.BlockSpec((None, 1, CONV_WIDTH), lsel3),
            pl.BlockSpec((None, 1, CONV_WIDTH), lsel3),
            pl.BlockSpec((None, 1, CONV_WIDTH), lsel3),
            pl.BlockSpec((None, 1, ATT_WIDTH), lsel3),
            pl.BlockSpec((None, 1, LANES), lsel3),
            pl.BlockSpec((1, LANES), lambda i: (0, 0)),
            pl.BlockSpec((1, LANES), lambda i: (0, 0)),
            pl.BlockSpec((None, N_HEADS, LANES), lsel3),
            pl.BlockSpec((ATT_WIDTH, ATT_WIDTH), lambda i: (0, 0)),
            pl.BlockSpec((None, N_BRANCH, GM_WIDTH, D_MODEL), lambda i: (layer, 0, 0, 0)),
            pl.BlockSpec((None, D_MODEL, D_MODEL), lsel3),
            pl.BlockSpec(memory_space=pl.ANY),
            pl.BlockSpec(memory_space=pl.ANY),
            pl.BlockSpec(memory_space=pl.ANY),
        ],
        out_specs=(
            pl.BlockSpec((n, D_MODEL), lambda i: (0, 0)),
            pl.BlockSpec((None,) + cache_block, lambda i: (layer, i, 0, 0, 0)),
            pl.BlockSpec((None,) + cache_block, lambda i: (layer, i, 0, 0, 0)),
            pl.BlockSpec((None, CONV_K - 1, bb, CONV_WIDTH), lambda i: (layer, 0, i, 0)),
            pl.BlockSpec((bb, GM_WIDTH), lambda i: (i, 0)),
        ),
        scratch_shapes=[
            pltpu.VMEM((bb, N_HEADS, LANES), F32),
            pltpu.VMEM((bb, N_HEADS, LANES), F32),
            pltpu.VMEM((n, GM_WIDTH + CONV_WIDTH + ATT_WIDTH), BF),
            pltpu.VMEM((n, N_BRANCH * D_MODEL), F32),
        ],
        compiler_params=pltpu.CompilerParams(
            dimension_semantics=("arbitrary",), vmem_limit_bytes=VMEM_LIMIT),
        name=f"sample_mixers_l{layer}",
    )(*args)


def _rope_lane_tables():
    half = ROT_DIM // 2
    freqs = jnp.exp(-math.log(ROPE_THETA) * jnp.arange(half, dtype=F32) * (2.0 / ROT_DIM))
    rest = jnp.zeros((HEAD_DIM - ROT_DIM,), F32)
    freq64 = jnp.concatenate([freqs, freqs, rest])
    sign64 = jnp.concatenate([-jnp.ones((half,), F32), jnp.ones((half,), F32), rest])
    reps = LANES // HEAD_DIM
    return jnp.tile(freq64, reps)[None, :], jnp.tile(sign64, reps)[None, :]


def _cos_sin(pos, lane_freq):
    ang = pos.astype(F32)[:, None] * lane_freq
    return jnp.cos(ang), jnp.sin(ang)


def _band_bias():
    i = np.arange(WINDOW)[:, None]
    j = np.arange(2 * WINDOW)[None, :]
    band = (j >= i) & (j <= i + WINDOW)
    first = band & (j >= WINDOW)
    out = np.where(np.stack([first, band]), 0.0, NEG_BIG).astype(np.float32)
    return jnp.asarray(out)


def _blockdiag():
    idx = np.arange(ATT_WIDTH) // HEAD_DIM
    return jnp.asarray((idx[:, None] == idx[None, :]).astype(np.float32), dtype=BF)


def _router_hi_lo(router_w):
    rw = router_w.astype(F32).T
    rw_hi = rw.astype(BF)
    rw_lo = (rw - rw_hi.astype(F32)).astype(BF)
    return jnp.concatenate([rw_hi, rw_lo], axis=0)


def _prepare(norm1_g, norm2_g, w_in, gm_norm_g, gm_ws, gm_b, conv_dw, conv_b, conv_ln_g,
             conv_ln_b, q_norm_g, k_norm_g, attn_sinks, w_branch, w_out, router_w, router_b,
             moe_w1, moe_w3, moe_w2, seq, tm):
    lane_freq, lane_sign = _rope_lane_tables()
    cb, sb = _cos_sin(jnp.arange(seq // tm, dtype=jnp.int32) * tm, lane_freq)
    co, so = _cos_sin(jnp.arange(tm, dtype=jnp.int32), lane_freq)
    c1, s1 = _cos_sin(PAST_LEN + jnp.arange(1, dtype=jnp.int32), lane_freq)
    row3 = lambda a: a.reshape(DEPTH, 1, a.shape[-1])
    col = np.arange(IN_WIDTH)
    col_scale = jnp.asarray(np.where((col >= OFF_Q) & (col < OFF_GATE), 1.0, 0.5), F32)
    return {
        "n1g": row3(norm1_g), "n2g": row3(norm2_g), "w_in": (w_in * col_scale).astype(BF),
        "gm_g": row3(gm_norm_g), "gm_ws": gm_ws,
        "gm_bias": jnp.repeat(jnp.swapaxes(gm_b, 1, 2), LANES, axis=2),
        "gm_w0": jnp.repeat(gm_ws[:, :, 0, 0], LANES, axis=1).reshape(DEPTH, 1, GM_WIDTH),
        "gm_b0": jnp.repeat(gm_b[:, :, 0], LANES, axis=1).reshape(DEPTH, 1, GM_WIDTH),
        "conv_dw": conv_dw, "conv_b": row3(conv_b), "conv_ln_g": row3(0.5 * conv_ln_g),
        "conv_ln_b": row3(0.5 * conv_ln_b),
        "qn": row3(jnp.tile(q_norm_g, (1, N_HEADS))),
        "kn": row3(jnp.tile(k_norm_g, (1, N_KV_HEADS))),
        "sinks": attn_sinks,
        "sink_lanes": jnp.broadcast_to(attn_sinks[:, :, None], (DEPTH, N_HEADS, LANES)),
        "rope_base": jnp.concatenate([cb, sb], axis=1)[:, None, :],
        "rope_off": jnp.concatenate([co, so], axis=1),
        "rope_sign": lane_sign, "rope_c1": c1, "rope_s1": s1 * lane_sign,
        "band_bias": _band_bias(), "blockdiag": _blockdiag(),
        "w_branch": (0.5 * w_branch).astype(BF), "w_out": w_out.astype(BF),
        "rw2": _router_hi_lo(router_w),
        "rb": router_b.astype(F32).reshape(N_EXPERTS, 1),
        "moe_w1": moe_w1.astype(BF), "moe_w3": moe_w3.astype(BF),
        "moe_w2": moe_w2.astype(BF).reshape(DEPTH, N_GROUPS, EXPERTS_PER_GROUP * EXPERT_FF, D_MODEL),
    }


def kernel(x_prompt, x_sample, cache_win_k, cache_win_v, state_conv, c_prompt, c_sample, norm1_g, norm2_g, w_ada, b_ada, w_in, gm_norm_g, gm_ws, gm_b, conv_dw, conv_b, conv_ln_g, conv_ln_b, q_norm_g, k_norm_g, attn_sinks, w_branch, w_out, router_w, router_b, moe_w1, moe_w3, moe_w2):
    nb, seq, _ = x_prompt.shape
    ns = x_sample.shape[0]
    tm = min(MIX_TILE, seq)
    tmoe = min(MOE_TILE, seq)
    p = _prepare(norm1_g, norm2_g, w_in, gm_norm_g, gm_ws, gm_b, conv_dw, conv_b, conv_ln_g,
                 conv_ln_b, q_norm_g, k_norm_g, attn_sinks, w_branch, w_out, router_w, router_b,
                 moe_w1, moe_w3, moe_w2, seq, tm)
    assert ns == PROMPT_MOD_ROW and nb <= ADA_ROWS - PROMPT_MOD_ROW
    c_all = jnp.concatenate(
        [c_sample, c_prompt, jnp.zeros((ADA_ROWS - nb - ns, D_MODEL), F32)], axis=0)
    mod = _ada_call(c_all, w_ada, b_ada)

    kt = jnp.transpose(cache_win_k, (0, 1, 3, 4, 2))
    vt = jnp.transpose(cache_win_v, (0, 1, 3, 4, 2))
    st = jnp.transpose(state_conv, (0, 2, 1, 3))

    xp = x_prompt
    xs = x_sample.reshape(ns, D_MODEL)
    kp_l, vp_l, cp_l, gs_l = [], [], [], []
    new_k = jnp.zeros(kt.shape, F32)
    new_v = jnp.zeros(vt.shape, F32)
    new_st = jnp.zeros(st.shape, F32)
    for l in range(DEPTH):
        xp, kp, vp, cp = _mixer_call(l, xp, mod, p, tm)
        xp = _moe_call(l, xp.reshape(nb * seq, D_MODEL), mod, seq // tmoe, p, tmoe,
                       f"prompt_moe_l{l}").reshape(nb, seq, D_MODEL)
        kp_l.append(kp)
        vp_l.append(vp)
        cp_l.append(cp)

        z = _sample_proj_call(l, xs, mod, p)
        xs, new_k, new_v, new_st, gv = _sample_mixer_call(l, z, xs, mod, kt, vt, st,
                                                          new_k, new_v, new_st, p, SAMPLE_BLOCK)
        xs = _moe_call(l, xs, mod, None, p, ns, f"sample_moe_l{l}")
        gs_l.append(gv)

    kv_shape = (DEPTH, nb, WINDOW, N_KV_HEADS, HEAD_DIM)
    return (xp, xs.reshape(ns, 1, D_MODEL),
            jnp.stack(kp_l).reshape(kv_shape), jnp.stack(vp_l).reshape(kv_shape),
            jnp.stack(cp_l),
            jnp.transpose(new_k, (0, 1, 4, 2, 3)),
            jnp.transpose(new_v, (0, 1, 4, 2, 3)),
            jnp.transpose(new_st, (0, 2, 1, 3)),
            jnp.stack(gs_l).reshape(DEPTH, ns, 1, GM_WIDTH))
```

```python
import functools
import math

import jax
import jax.numpy as jnp
import numpy as np
from jax import lax
from jax.experimental import pallas as pl
from jax.experimental.pallas import tpu as pltpu

F32 = jnp.float32
BF = jnp.bfloat16

D_MODEL = 1024
DEPTH = 2
PAST_LEN = 8192
CHUNK = 128
GM_GROUPS = 4
GM_WIDTH = 512
CONV_WIDTH = 512
CONV_K = 31
N_HEADS = 8
N_KV_HEADS = 2
Q_REP = N_HEADS // N_KV_HEADS
HEAD_DIM = 64
ATT_WIDTH = N_HEADS * HEAD_DIM
KV_WIDTH = N_KV_HEADS * HEAD_DIM
WINDOW = 128
ROPE_THETA = 500000.0
ROT_DIM = HEAD_DIM // 4
N_BRANCH = 3
N_EXPERTS = 16
EXPERTS_PER_GROUP = 4
N_GROUPS = N_EXPERTS // EXPERTS_PER_GROUP
EXPERT_FF = 256
EPS = 1e-6

OFF_GM_U = 0
OFF_GM_V = 512
OFF_CV_A = 1024
OFF_CV_G = 1536
OFF_Q = 2048
OFF_K = 2560
OFF_V = 2688
OFF_GATE = 2816
IN_WIDTH = OFF_GATE + N_BRANCH * D_MODEL

LANES = 128
SUBLANES = 8
MXU_COLS = 256
HALO = 32
MIX_TILE = 512
MOE_TILE = 512
MOE_CAP = 144
ADA_ROWS = 136
PROMPT_MOD_ROW = 128
ADA_BLOCK = 1536
PROJ_BLOCK = IN_WIDTH // 2
SAMPLE_BLOCK = 32
VMEM_LIMIT = 56 * 1024 * 1024
NEG_BIG = -1e30


def _dot(a, b):
    return jnp.dot(a, b, preferred_element_type=F32)


def _dot_nt(a, b):
    return lax.dot_general(a, b, (((1,), (1,)), ((), ())), preferred_element_type=F32)


def _sigmoid(x):
    return 0.5 * jnp.tanh(0.5 * x) + 0.5


def _silu(x):
    return x * _sigmoid(x)


GELU_C = 0.7978845608028654
GELU_K = 0.044715


def _gelu_half(xh):
    return xh + xh * jnp.tanh(xh * (2.0 * GELU_C + (8.0 * GELU_C * GELU_K) * (xh * xh)))


def _sigmoid_times(zh, dh):
    return jnp.tanh(zh) * dh + dh


def _modulate(x, g, shift, scale):
    ms = jnp.mean(x * x, axis=-1, keepdims=True)
    return (x * lax.rsqrt(ms + EPS)) * (g * (1.0 + scale)) + shift


def _head_norm(x, blockdiag, g):
    sq = (x * x).astype(BF)
    width = x.shape[-1]
    step = min(width, MXU_COLS)
    ssum = jnp.concatenate([_dot(sq[:, c:c + step], blockdiag[0:step, 0:step])
                            for c in range(0, width, step)], axis=-1)
    return x * lax.rsqrt(ssum * (1.0 / HEAD_DIM) + EPS) * g


def _rope(x, c, s):
    width = x.shape[-1]
    reps = width // LANES
    cc = jnp.concatenate([c] * reps, axis=-1)
    ss = jnp.concatenate([s] * reps, axis=-1)
    lane = lax.broadcasted_iota(jnp.int32, x.shape, x.ndim - 1) % HEAD_DIM
    partner = jnp.where(lane < ROT_DIM // 2,
                        pltpu.roll(x, width - ROT_DIM // 2, x.ndim - 1),
                        pltpu.roll(x, ROT_DIM // 2, x.ndim - 1))
    return x * cc + partner * ss


def _layer_norm_silu(y, gh, bh):
    mu = jnp.mean(y, axis=-1, keepdims=True)
    yc = y - mu
    var = jnp.mean(yc * yc, axis=-1, keepdims=True)
    h = yc * lax.rsqrt(var + EPS) * gh + bh
    return h * jnp.tanh(h) + h


def _low_half(shape):
    return lax.broadcasted_iota(jnp.int32, shape, len(shape) - 1) % LANES < HEAD_DIM


def _const_spec(shape, index_map):
    return pl.BlockSpec(shape, index_map, pipeline_mode=pl.Buffered(1))


def _ada_kernel(c_ref, w_ref, b_ref, o_ref):
    s = _silu(c_ref[...]).astype(BF)
    o_ref[...] = _dot(s, w_ref[...].astype(BF)) + b_ref[...]


def _ada_call(c_all, w_ada, b_ada):
    nb = (6 * D_MODEL) // ADA_BLOCK
    return pl.pallas_call(
        _ada_kernel,
        out_shape=jax.ShapeDtypeStruct((DEPTH, ADA_ROWS, 6 * D_MODEL), F32),
        grid=(DEPTH, nb),
        in_specs=[
            pl.BlockSpec((ADA_ROWS, D_MODEL), lambda l, j: (0, 0)),
            pl.BlockSpec((None, D_MODEL, ADA_BLOCK), lambda l, j: (l, 0, j)),
            pl.BlockSpec((None, 1, ADA_BLOCK), lambda l, j: (l, 0, j)),
        ],
        out_specs=pl.BlockSpec((None, ADA_ROWS, ADA_BLOCK), lambda l, j: (l, 0, j)),
        compiler_params=pltpu.CompilerParams(
            dimension_semantics=("arbitrary", "arbitrary")),
        name="ada_mod",
    )(c_all, w_ada, b_ada.reshape(DEPTH, 1, 6 * D_MODEL))


def _mixer_kernel(sinks_ref, x_ref, mod_ref, n1g_ref, win_ref, gmg_ref, gmws_ref, gmb_ref,
                  cdw_ref, cb_ref, clg_ref, clb_ref, qn_ref, kn_ref, ropeb_ref, ropeo_ref,
                  sign_ref, bias_ref, bd_ref, wb_ref, wout_ref,
                  x1_ref, kwin_ref, vwin_ref, cst_ref,
                  kd_s, vd_s, abuf, oatt_s, hb_s, z_s, acc_s, *, layer, tm):
    t = pl.program_id(1)
    last = pl.num_programs(1) - 1
    nblk = tm // WINDOW

    @pl.when(t == 0)
    def _():
        kd_s[0:WINDOW, :] = jnp.zeros((WINDOW, 2 * LANES), BF)
        vd_s[0:WINDOW, :] = jnp.zeros((WINDOW, 2 * LANES), BF)
        abuf[0:HALO, :] = jnp.zeros((HALO, CONV_WIDTH), F32)
        abuf[HALO + tm:HALO + tm + SUBLANES, :] = jnp.zeros((SUBLANES, CONV_WIDTH), F32)

    x = x_ref[...]
    mod = mod_ref[pl.ds(pl.program_id(0), 1), :]
    sh1 = mod[:, 0:D_MODEL]
    sc1 = mod[:, D_MODEL:2 * D_MODEL]
    gt1 = mod[:, 2 * D_MODEL:3 * D_MODEL]
    hb_s[...] = _modulate(x, n1g_ref[...], sh1, sc1).astype(BF)

    def proj(off, width):
        z_s[:, off:off + width] = _dot(hb_s[...], win_ref[:, off:off + width])

    def zcols(off, width):
        return z_s[:, off:off + width]

    proj(OFF_CV_A, 2 * CONV_WIDTH)
    a = _sigmoid_times(zcols(OFF_CV_G, CONV_WIDTH), zcols(OFF_CV_A, CONV_WIDTH))
    abuf[HALO:HALO + tm, :] = a
    first_off = HALO - (CONV_K - 1)

    def conv_tile(ci):
        lanes = slice(ci * LANES, (ci + 1) * LANES)
        yt = cb_ref[:, lanes]
        for b in range(SUBLANES):
            part = None
            for off in range(b, HALO + 1, SUBLANES):
                if off < first_off:
                    continue
                term = (cdw_ref[off - first_off:off - first_off + 1, lanes]
                        * abuf[pl.ds(off - b, tm + SUBLANES), lanes])
                part = term if part is None else part + term
            yt = yt + part[b:b + tm, :]
        return yt

    proj(OFF_GM_U, 2 * GM_WIDTH)
    proj(OFF_Q, ATT_WIDTH + 2 * KV_WIDTH)

    def gated_branch(i, o_branch, first):
        for n in range(D_MODEL // MXU_COLS):
            cols = slice(n * MXU_COLS, (n + 1) * MXU_COLS)
            g0 = OFF_GATE + i * D_MODEL + n * MXU_COLS
            piece = _sigmoid_times(_dot(hb_s[...], win_ref[:, g0:g0 + MXU_COLS]),
                                   _dot(o_branch, wb_ref[i, :, cols]))
            if first:
                acc_s[:, cols] = piece
            else:
                acc_s[:, cols] += piece

    y = jnp.concatenate([conv_tile(ci) for ci in range(CONV_WIDTH // LANES)], axis=1)
    abuf[0:HALO, :] = abuf[tm:tm + HALO, :]
    o_cv = _layer_norm_silu(y, clg_ref[...], clb_ref[...]).astype(BF)
    gated_branch(1, o_cv, True)

    u = _gelu_half(zcols(OFF_GM_U, GM_WIDTH))
    gv = _gelu_half(zcols(OFF_GM_V, GM_WIDTH))
    v = gv * lax.rsqrt(jnp.mean(gv * gv, axis=-1, keepdims=True) + EPS) * gmg_ref[...]
    vb = v.astype(BF)
    row = lax.broadcasted_iota(jnp.int32, (CHUNK, CHUNK), 0)
    col = lax.broadcasted_iota(jnp.int32, (CHUNK, CHUNK), 1)
    ws = [jnp.where(row >= col, gmws_ref[g], 0.0).astype(BF) for g in range(GM_GROUPS)]
    gmb = gmb_ref[...]
    chunks = []
    for c in range(nblk):
        parts = [_dot(ws[g], vb[c * CHUNK:(c + 1) * CHUNK, g * LANES:(g + 1) * LANES])
                 for g in range(GM_GROUPS)]
        chunks.append(jnp.concatenate(parts, axis=1) + gmb)
    o_gm = (u * jnp.concatenate(chunks, axis=0)).astype(BF)
    gated_branch(0, o_gm, False)

    rbase = ropeb_ref[...]
    roff = ropeo_ref[...]
    cb_, sb_ = rbase[:, 0:LANES], rbase[:, LANES:2 * LANES]
    co_, so_ = roff[:, 0:LANES], roff[:, LANES:2 * LANES]
    rc = cb_ * co_ - sb_ * so_
    rs = (sb_ * co_ + cb_ * so_) * sign_ref[...]
    bd = bd_ref[...]
    q = _rope(_head_norm(zcols(OFF_Q, ATT_WIDTH), bd, qn_ref[...]), rc, rs)
    qb = (q * (HEAD_DIM ** -0.5)).astype(BF)
    k_nat = _rope(_head_norm(zcols(OFF_K, KV_WIDTH), bd[0:LANES, 0:LANES], kn_ref[...]), rc, rs)
    v_nat = zcols(OFF_V, KV_WIDTH)
    lo_kv = _low_half((tm, LANES))

    def doubled(x_nat):
        swapped = pltpu.roll(x_nat, HEAD_DIM, 1)
        return jnp.concatenate([jnp.where(lo_kv, x_nat, swapped), jnp.where(lo_kv, swapped, x_nat)],
                               axis=1)

    kd = doubled(k_nat)
    vd = doubled(v_nat)
    kd_s[WINDOW:WINDOW + tm, :] = kd.astype(BF)
    vd_s[WINDOW:WINDOW + tm, :] = vd.astype(BF)

    lo_q = _low_half((WINDOW, LANES))
    zero_q = jnp.zeros((WINDOW, LANES), BF)
    band = bias_ref[1]
    for bi in range(nblk):
        bias = jnp.where(t == 0, bias_ref[0], band) if bi == 0 else band
        bias4 = jnp.concatenate([bias] * Q_REP, axis=0)
        for g in range(N_KV_HEADS):
            tiles = [qb[bi * WINDOW:(bi + 1) * WINDOW, (2 * g + j) * LANES:(2 * g + j + 1) * LANES]
                     for j in range(2)]
            qs = jnp.concatenate([jnp.where(lo_q, tiles[0], zero_q), jnp.where(lo_q, zero_q, tiles[0]),
                                  jnp.where(lo_q, tiles[1], zero_q), jnp.where(lo_q, zero_q, tiles[1])],
                                 axis=0)
            keys = kd_s[bi * WINDOW:(bi + 2) * WINDOW, g * LANES:(g + 1) * LANES]
            vals = vd_s[bi * WINDOW:(bi + 2) * WINDOW, g * LANES:(g + 1) * LANES]
            s = _dot_nt(qs, keys) + bias4
            outs = []
            for hh in range(Q_REP):
                sink = sinks_ref[layer, Q_REP * g + hh]
                sh = s[hh * WINDOW:(hh + 1) * WINDOW, :]
                m = jnp.maximum(jnp.max(sh, axis=-1, keepdims=True), sink)
                p = jnp.exp(sh - m)
                den = jnp.sum(p, axis=-1, keepdims=True) + jnp.exp(sink - m)
                outs.append(_dot(p.astype(BF), vals) / den)
            for j in range(2):
                oatt_s[bi * WINDOW:(bi + 1) * WINDOW, (2 * g + j) * LANES:(2 * g + j + 1) * LANES] = (
                    jnp.where(lo_q, outs[2 * j], outs[2 * j + 1]).astype(BF))
    kd_s[0:WINDOW, :] = kd_s[tm:tm + WINDOW, :]
    vd_s[0:WINDOW, :] = vd_s[tm:tm + WINDOW, :]
    gated_branch(2, oatt_s[...], False)

    accb = acc_s[...].astype(BF)
    for n in range(D_MODEL // MXU_COLS):
        cols = slice(n * MXU_COLS, (n + 1) * MXU_COLS)
        x1_ref[:, cols] = x_ref[:, cols] + gt1[:, cols] * _dot(accb, wout_ref[:, cols])

    @pl.when(t == last)
    def _():
        kwin_ref[...] = k_nat[tm - WINDOW:tm, :]
        vwin_ref[...] = v_nat[tm - WINDOW:tm, :]
        cst_ref[...] = a[tm - (CONV_K - 1):tm, :]


def _mixer_call(layer, x, mod, p, tm):
    nb, seq, _ = x.shape
    nt = seq // tm
    lsel3 = lambda b, t: (layer, 0, 0)
    kernel = functools.partial(_mixer_kernel, layer=layer, tm=tm)
    return pl.pallas_call(
        kernel,
        out_shape=(
            jax.ShapeDtypeStruct((nb, seq, D_MODEL), F32),
            jax.ShapeDtypeStruct((nb, WINDOW, KV_WIDTH), F32),
            jax.ShapeDtypeStruct((nb, WINDOW, KV_WIDTH), F32),
            jax.ShapeDtypeStruct((nb, CONV_K - 1, CONV_WIDTH), F32),
        ),
        grid=(nb, nt),
        in_specs=[
            pl.BlockSpec(memory_space=pltpu.SMEM),
            pl.BlockSpec((None, tm, D_MODEL), lambda b, t: (b, t, 0)),
            pl.BlockSpec((None, SUBLANES, 3 * D_MODEL),
                         lambda b, t: (layer, PROMPT_MOD_ROW // SUBLANES, 0)),
            _const_spec((None, 1, D_MODEL), lsel3),
            _const_spec((None, D_MODEL, IN_WIDTH), lsel3),
            _const_spec((None, 1, GM_WIDTH), lsel3),
            _const_spec((None, GM_GROUPS, CHUNK, CHUNK), lambda b, t: (layer, 0, 0, 0)),
            _const_spec((None, CHUNK, GM_WIDTH), lsel3),
            _const_spec((None, CONV_K, CONV_WIDTH), lsel3),
            _const_spec((None, 1, CONV_WIDTH), lsel3),
            _const_spec((None, 1, CONV_WIDTH), lsel3),
            _const_spec((None, 1, CONV_WIDTH), lsel3),
            _const_spec((None, 1, ATT_WIDTH), lsel3),
            _const_spec((None, 1, LANES), lsel3),
            pl.BlockSpec((None, 1, 2 * LANES), lambda b, t: (t, 0, 0)),
            _const_spec((tm, 2 * LANES), lambda b, t: (0, 0)),
            _const_spec((1, LANES), lambda b, t: (0, 0)),
            _const_spec((2, WINDOW, 2 * WINDOW), lambda b, t: (0, 0, 0)),
            _const_spec((ATT_WIDTH, ATT_WIDTH), lambda b, t: (0, 0)),
            _const_spec((None, N_BRANCH, GM_WIDTH, D_MODEL), lambda b, t: (layer, 0, 0, 0)),
            _const_spec((None, D_MODEL, D_MODEL), lsel3),
        ],
        out_specs=(
            pl.BlockSpec((None, tm, D_MODEL), lambda b, t: (b, t, 0)),
            pl.BlockSpec((None, WINDOW, KV_WIDTH), lambda b, t: (b, 0, 0)),
            pl.BlockSpec((None, WINDOW, KV_WIDTH), lambda b, t: (b, 0, 0)),
            pl.BlockSpec((None, CONV_K - 1, CONV_WIDTH), lambda b, t: (b, 0, 0)),
        ),
        scratch_shapes=[
            pltpu.VMEM((WINDOW + tm, 2 * LANES), BF),
            pltpu.VMEM((WINDOW + tm, 2 * LANES), BF),
            pltpu.VMEM((HALO + tm + SUBLANES, CONV_WIDTH), F32),
            pltpu.VMEM((tm, ATT_WIDTH), BF),
            pltpu.VMEM((tm, D_MODEL), BF),
            pltpu.VMEM((tm, OFF_GATE), F32),
            pltpu.VMEM((tm, D_MODEL), F32),
        ],
        compiler_params=pltpu.CompilerParams(
            dimension_semantics=("arbitrary", "arbitrary"),
            vmem_limit_bytes=VMEM_LIMIT),
        name=f"prompt_mixers_l{layer}",
    )(p["sinks"], x, mod, p["n1g"], p["w_in"], p["gm_g"], p["gm_ws"], p["gm_bias"],
      p["conv_dw"], p["conv_b"], p["conv_ln_g"], p["conv_ln_b"], p["qn"], p["kn"],
      p["rope_base"], p["rope_off"], p["rope_sign"], p["band_bias"], p["blockdiag"],
      p["w_branch"], p["w_out"])


def _route_rows(logits_t, rb_ref):
    scores = _sigmoid(logits_t)
    biased = scores + rb_ref[...]
    rows = lambda a, gi: [a[EXPERTS_PER_GROUP * gi + k:EXPERTS_PER_GROUP * gi + k + 1, :]
                          for k in range(EXPERTS_PER_GROUP)]
    best = None
    idx = None
    for gi in range(N_GROUPS):
        b = rows(biased, gi)
        hi1, lo1 = jnp.maximum(b[0], b[1]), jnp.minimum(b[0], b[1])
        hi2, lo2 = jnp.maximum(b[2], b[3]), jnp.minimum(b[2], b[3])
        gs = jnp.maximum(hi1, hi2) + jnp.maximum(jnp.minimum(hi1, hi2), jnp.maximum(lo1, lo2))
        if gi == 0:
            best, idx = gs, jnp.zeros(gs.shape, jnp.int32)
        else:
            better = gs > best
            idx = jnp.where(better, gi, idx)
            best = jnp.where(better, gs, best)
    bsel = rows(biased, 0)
    ssel = rows(scores, 0)
    for gi in range(1, N_GROUPS):
        bg, sg = rows(biased, gi), rows(scores, gi)
        pick = idx == gi
        bsel = [jnp.where(pick, bg[k], bsel[k]) for k in range(EXPERTS_PER_GROUP)]
        ssel = [jnp.where(pick, sg[k], ssel[k]) for k in range(EXPERTS_PER_GROUP)]
    chosen = []
    for k in range(EXPERTS_PER_GROUP):
        rank = jnp.zeros(idx.shape, jnp.int32)
        for k2 in range(EXPERTS_PER_GROUP):
            if k2 == k:
                continue
            beats = (bsel[k2] > bsel[k]) | ((bsel[k2] == bsel[k]) & (k2 < k))
            rank = rank + beats.astype(jnp.int32)
        chosen.append(jnp.where(rank < 2, ssel[k], 0.0))
    den = chosen[0] + chosen[1] + chosen[2] + chosen[3]
    return idx, [c / den for c in chosen]


def _moe_route(x, mod, n2g_ref, rw2_ref, rb_ref, upper_ref, h_buf, rt_buf, c_buf, cnt_s, buf, tm):
    h2 = _modulate(x, n2g_ref[...], mod[:, 0:D_MODEL], mod[:, D_MODEL:2 * D_MODEL])
    hi = h2.astype(BF)
    lo = (h2 - hi.astype(F32)).astype(BF)
    rw2 = rw2_ref[...]
    lt = _dot_nt(rw2, hi) + _dot_nt(rw2, lo)
    idx, comb = _route_rows(lt[0:N_EXPERTS, :] + lt[N_EXPERTS:2 * N_EXPERTS, :], rb_ref)

    onehot = [(idx == g).astype(F32) for g in range(N_GROUPS)]
    oh8 = jnp.concatenate(onehot + [jnp.zeros((SUBLANES - N_GROUPS, tm), F32)], axis=0)
    prefix = _dot(oh8.astype(BF), upper_ref[...])
    slot = onehot[0] * prefix[0:1, :]
    for g in range(1, N_GROUPS):
        slot = slot + onehot[g] * prefix[g:g + 1, :]
    rt_buf[0:1, :] = idx
    rt_buf[1:2, :] = slot.astype(jnp.int32)
    counts = [jnp.sum(onehot[g], axis=-1, keepdims=True) for g in range(N_GROUPS)]
    for g in range(N_GROUPS):
        c_buf[EXPERTS_PER_GROUP + g:EXPERTS_PER_GROUP + g + 1, :] = jnp.broadcast_to(counts[g], (1, tm))
    most = jnp.maximum(jnp.maximum(counts[0], counts[1]), jnp.maximum(counts[2], counts[3]))
    cnt_s[buf] = most[0, 0].astype(jnp.int32)

    for k in range(EXPERTS_PER_GROUP):
        c_buf[k:k + 1, :] = comb[k]
    h_buf[...] = hi


def _moe_experts(x_ref, gt2, w1_ref, w3_ref, w2_ref, o_ref, h_buf, rt_buf, c_buf, cnt_s, buf,
                 tm, cap, overflow):
    def dispatch(g, j):
        want = lax.broadcasted_iota(jnp.int32, (cap, tm), 0) + j * cap
        hit = (rt_buf[1:2, :] == want) & (rt_buf[0:1, :] == g)
        return hit, jnp.where(hit, 1.0, 0.0).astype(BF)

    def run_experts(g, hit, pmat):
        hg = _dot(pmat, h_buf[...]).astype(BF)
        parts = []
        for e in range(EXPERTS_PER_GROUP):
            a = _dot(hg, w1_ref[EXPERTS_PER_GROUP * g + e])
            b = _dot(hg, w3_ref[EXPERTS_PER_GROUP * g + e])
            ce = jnp.sum(jnp.where(hit, c_buf[e:e + 1, :], 0.0), axis=-1, keepdims=True)
            parts.append(_silu(a) * b * ce)
        act = jnp.concatenate(parts, axis=1).astype(BF)
        return _dot(act, w2_ref[g]).astype(BF)

    def scatter(pmat, y):
        return lax.dot_general(pmat, y, (((0,), (0,)), ((), ())), preferred_element_type=F32)

    if not overflow:
        sel = [dispatch(g, 0) for g in range(N_GROUPS)]
        ys = [run_experts(g, *sel[g]) for g in range(N_GROUPS)]
        y_tile = scatter(jnp.concatenate([s[1] for s in sel], axis=0), jnp.concatenate(ys, axis=0))
        o_ref[...] = x_ref[...] + gt2 * y_tile
        return

    @pl.when(cnt_s[buf] > cap)
    def _():
        def group_body(g, carry):
            tile = c_buf[:, 0:LANES]
            row = lax.broadcasted_iota(jnp.int32, tile.shape, 0)
            lane = lax.broadcasted_iota(jnp.int32, tile.shape, 1)
            pick = (row == EXPERTS_PER_GROUP + g) & (lane == 0)
            count = jnp.sum(jnp.where(pick, tile, 0.0)).astype(jnp.int32)
            nblk = (count + (cap - 1)) // cap

            def block_body(j, carry2):
                hit, pmat = dispatch(g, j)
                o_ref[...] += gt2 * scatter(pmat, run_experts(g, hit, pmat))
                return carry2

            lax.fori_loop(1, nblk, block_body, 0)
            return carry

        lax.fori_loop(0, N_GROUPS, group_body, 0)


def _moe_kernel(x_ref, xn_ref, mod_ref, n2g_ref, rw2_ref, rb_ref, upper_ref,
                w1_ref, w3_ref, w2_ref, o_ref, h0_s, h1_s, rt0_s, rt1_s, c0_s, c1_s, cnt_s,
                *, tm, cap, tiles_per_row, n_tiles):
    route = functools.partial(_moe_route, n2g_ref=n2g_ref, rw2_ref=rw2_ref, rb_ref=rb_ref,
                              upper_ref=upper_ref, cnt_s=cnt_s, tm=tm)
    experts = functools.partial(_moe_experts, x_ref, w1_ref=w1_ref,
                                w3_ref=w3_ref, w2_ref=w2_ref, o_ref=o_ref, cnt_s=cnt_s, tm=tm, cap=cap)
    bufs = [dict(h_buf=h0_s, rt_buf=rt0_s, c_buf=c0_s, buf=0),
            dict(h_buf=h1_s, rt_buf=rt1_s, c_buf=c1_s, buf=1)]
    if tiles_per_row is None:
        mod = mod_ref[...]
        route(x_ref[...], mod, **bufs[0])
        for overflow in (False, True):
            experts(mod[:, 2 * D_MODEL:3 * D_MODEL], overflow=overflow, **bufs[0])
        return

    i = pl.program_id(0)
    mod_row = lambda tile: mod_ref[pl.ds(tile // tiles_per_row, 1), :]

    @pl.when(i == 0)
    def _():
        route(x_ref[...], mod_row(i), **bufs[0])

    def step(cur):
        gt2 = mod_row(i)[:, 2 * D_MODEL:3 * D_MODEL]
        experts(gt2, overflow=False, **bufs[cur])
        route(xn_ref[...], mod_row(jnp.minimum(i + 1, n_tiles - 1)), **bufs[1 - cur])
        experts(gt2, overflow=True, **bufs[cur])

    @pl.when(i % 2 == 0)
    def _():
        step(0)

    @pl.when(i % 2 == 1)
    def _():
        step(1)


def _moe_call(layer, x2d, mod, tiles_per_row, p, tm, name):
    n = x2d.shape[0]
    n_tiles = n // tm
    if tiles_per_row is None:
        assert n_tiles == 1
        mod_spec = pl.BlockSpec((None, n, 3 * D_MODEL), lambda i: (layer, 0, 1))
    else:
        mod_spec = pl.BlockSpec((None, SUBLANES, 3 * D_MODEL),
                                lambda i: (layer, PROMPT_MOD_ROW // SUBLANES, 1))
    lsel3 = lambda i: (layer, 0, 0)
    lsel4 = lambda i: (layer, 0, 0, 0)
    upper = np.triu(np.ones((tm, tm), np.float32), k=1)
    kernel = functools.partial(_moe_kernel, tm=tm, cap=MOE_CAP, tiles_per_row=tiles_per_row,
                               n_tiles=n_tiles)
    return pl.pallas_call(
        kernel,
        out_shape=jax.ShapeDtypeStruct((n, D_MODEL), F32),
        grid=(n_tiles,),
        in_specs=[
            pl.BlockSpec((tm, D_MODEL), lambda i: (i, 0)),
            pl.BlockSpec((tm, D_MODEL), lambda i: (jnp.minimum(i + 1, n_tiles - 1), 0)),
            mod_spec,
            _const_spec((None, 1, D_MODEL), lsel3),
            _const_spec((2 * N_EXPERTS, D_MODEL), lambda i: (0, 0)),
            _const_spec((N_EXPERTS, 1), lambda i: (0, 0)),
            _const_spec((tm, tm), lambda i: (0, 0)),
            _const_spec((None, N_EXPERTS, D_MODEL, EXPERT_FF), lsel4),
            _const_spec((None, N_EXPERTS, D_MODEL, EXPERT_FF), lsel4),
            _const_spec((None, N_GROUPS, EXPERTS_PER_GROUP * EXPERT_FF, D_MODEL), lsel4),
        ],
        out_specs=pl.BlockSpec((tm, D_MODEL), lambda i: (i, 0)),
        scratch_shapes=[
            pltpu.VMEM((tm, D_MODEL), BF),
            pltpu.VMEM((tm, D_MODEL), BF),
            pltpu.VMEM((SUBLANES, tm), jnp.int32),
            pltpu.VMEM((SUBLANES, tm), jnp.int32),
            pltpu.VMEM((SUBLANES, tm), F32),
            pltpu.VMEM((SUBLANES, tm), F32),
            pltpu.SMEM((2,), jnp.int32),
        ],
        compiler_params=pltpu.CompilerParams(
            dimension_semantics=("arbitrary",),
            vmem_limit_bytes=VMEM_LIMIT),
        name=name,
    )(x2d, x2d, mod, p["n2g"], p["rw2"], p["rb"], jnp.asarray(upper, dtype=BF),
      p["moe_w1"], p["moe_w3"], p["moe_w2"])


def _sample_proj_kernel(x_ref, mod_ref, n1g_ref, w_ref, z_ref):
    mod = mod_ref[...]
    h = _modulate(x_ref[...], n1g_ref[...], mod[:, 0:D_MODEL], mod[:, D_MODEL:2 * D_MODEL])
    z_ref[...] = _dot(h.astype(BF), w_ref[...])


def _sample_proj_call(layer, xs, mod, p):
    n = xs.shape[0]
    return pl.pallas_call(
        _sample_proj_kernel,
        out_shape=jax.ShapeDtypeStruct((n, IN_WIDTH), F32),
        grid=(IN_WIDTH // PROJ_BLOCK,),
        in_specs=[
            pl.BlockSpec((n, D_MODEL), lambda j: (0, 0)),
            pl.BlockSpec((None, n, 3 * D_MODEL), lambda j: (layer, 0, 0)),
            pl.BlockSpec((None, 1, D_MODEL), lambda j: (layer, 0, 0)),
            pl.BlockSpec((None, D_MODEL, PROJ_BLOCK), lambda j: (layer, 0, j)),
        ],
        out_specs=pl.BlockSpec((n, PROJ_BLOCK), lambda j: (0, j)),
        compiler_params=pltpu.CompilerParams(dimension_semantics=("arbitrary",)),
        name=f"sample_proj_l{layer}",
    )(xs, mod, p["n1g"], p["w_in"])


def _shift_in_column(cache_t, new_rows, bb):
    flat = cache_t.reshape(bb * HEAD_DIM, WINDOW)
    shifted = pltpu.roll(flat, WINDOW - 1, 1).reshape(bb, HEAD_DIM, WINDOW)
    padded = jnp.concatenate([new_rows, jnp.zeros((LANES - bb, LANES), F32)], axis=0)
    new_t = padded.T
    is_last = lax.broadcasted_iota(jnp.int32, (HEAD_DIM, WINDOW), 1) == WINDOW - 1
    out = []
    for b in range(bb):
        col = new_t[0:HEAD_DIM, b:b + 1]
        out.append(jnp.where(is_last, col, shifted[b]))
    return out


def _sample_mixer_kernel(z_ref, x_ref, mod_ref, kt_ref, vt_ref, st_ref,
                         gmg_ref, gmw_ref, gmb_ref, cdw_ref, cb_ref, clg_ref, clb_ref,
                         qn_ref, kn_ref, ropec_ref, ropes_ref, sink_ref, bd_ref, wb_ref, wout_ref,
                         _k_all, _v_all, _st_all,
                         x1_ref, kto_ref, vto_ref, sto_ref, gv_ref,
                         qf_s, of_s, oc_s, zg_s, *, bb):
    i = pl.program_id(0)
    r0 = pl.multiple_of(i * bb, bb)
    z = z_ref[...]
    zg_s[pl.ds(r0, bb), :] = z[:, OFF_GATE:OFF_GATE + N_BRANCH * D_MODEL]

    u = _gelu_half(z[:, OFF_GM_U:OFF_GM_U + GM_WIDTH])
    gv = _gelu_half(z[:, OFF_GM_V:OFF_GM_V + GM_WIDTH])
    v = gv * lax.rsqrt(jnp.mean(gv * gv, axis=-1, keepdims=True) + EPS) * gmg_ref[...]
    gv_ref[...] = v
    o_gm = u * (v * gmw_ref[...] + gmb_ref[...])

    a = _sigmoid_times(z[:, OFF_CV_G:OFF_CV_G + CONV_WIDTH], z[:, OFF_CV_A:OFF_CV_A + CONV_WIDTH])
    y = cb_ref[...] + cdw_ref[CONV_K - 1:CONV_K, :] * a
    for j in range(CONV_K - 1):
        y = y + cdw_ref[j:j + 1, :] * st_ref[j]
    o_cv = _layer_norm_silu(y, clg_ref[...], clb_ref[...])
    sto_ref[0:CONV_K - 2] = st_ref[1:CONV_K - 1]
    sto_ref[CONV_K - 2] = a

    rc = ropec_ref[...]
    rs = ropes_ref[...]
    bd = bd_ref[...]
    q = _rope(_head_norm(z[:, OFF_Q:OFF_Q + ATT_WIDTH], bd, qn_ref[...]), rc, rs) * (HEAD_DIM ** -0.5)
    knew = _rope(_head_norm(z[:, OFF_K:OFF_K + KV_WIDTH], bd[0:LANES, 0:LANES], kn_ref[...]), rc, rs)
    vnew = z[:, OFF_V:OFF_V + KV_WIDTH]
    lo = _low_half((bb, LANES))
    for h in range(N_HEADS):
        tile = q[:, (h // 2) * LANES:(h // 2 + 1) * LANES]
        if h % 2 == 1:
            tile = pltpu.roll(tile, HEAD_DIM, 1)
        qf_s[:, h, :] = jnp.where(lo, tile, 0.0)
    k_low = [knew, pltpu.roll(knew, HEAD_DIM, 1)]
    v_low = [vnew, pltpu.roll(vnew, HEAD_DIM, 1)]
    for g in range(N_KV_HEADS):
        qg = qf_s[:, Q_REP * g:Q_REP * (g + 1), :]
        kt = kt_ref[:, g]
        vt = vt_ref[:, g]
        s = jnp.einsum("brd,bdp->brp", qg[:, :, 0:HEAD_DIM].astype(BF), kt.astype(BF),
                       preferred_element_type=F32)
        s_new = jnp.sum(qg * k_low[g][:, None, :], axis=-1, keepdims=True)
        sink = sink_ref[Q_REP * g:Q_REP * (g + 1), 0:1][None]
        m = jnp.maximum(jnp.maximum(jnp.max(s, axis=-1, keepdims=True), s_new), sink)
        pr = jnp.exp(s - m)
        p_new = jnp.exp(s_new - m)
        den = jnp.sum(pr, axis=-1, keepdims=True) + p_new + jnp.exp(sink - m)
        o = jnp.einsum("brp,bdp->brd", pr.astype(BF), vt.astype(BF), preferred_element_type=F32)
        o = (o + p_new * v_low[g][:, None, 0:HEAD_DIM]) / den
        of_s[:, Q_REP * g:Q_REP * (g + 1), :] = jnp.concatenate([o, jnp.zeros(o.shape, F32)], axis=-1)
        for b, tile in enumerate(_shift_in_column(kt, k_low[g], bb)):
            kto_ref[b, g] = tile
        for b, tile in enumerate(_shift_in_column(vt, v_low[g], bb)):
            vto_ref[b, g] = tile
    att_tiles = []
    for j in range(N_HEADS // 2):
        second = pltpu.roll(of_s[:, 2 * j + 1, :], HEAD_DIM, 1)
        att_tiles.append(jnp.where(lo, of_s[:, 2 * j, :], second))
    o_att = jnp.concatenate(att_tiles, axis=1)

    oc_s[pl.ds(r0, bb), 0:GM_WIDTH] = o_gm.astype(BF)
    oc_s[pl.ds(r0, bb), GM_WIDTH:GM_WIDTH + CONV_WIDTH] = o_cv.astype(BF)
    oc_s[pl.ds(r0, bb), GM_WIDTH + CONV_WIDTH:GM_WIDTH + CONV_WIDTH + ATT_WIDTH] = o_att.astype(BF)

    @pl.when(i == pl.num_programs(0) - 1)
    def _():
        acc = _sigmoid_times(zg_s[:, 0:D_MODEL], _dot(oc_s[:, 0:GM_WIDTH], wb_ref[0]))
        acc = acc + _sigmoid_times(zg_s[:, D_MODEL:2 * D_MODEL], _dot(
            oc_s[:, GM_WIDTH:GM_WIDTH + CONV_WIDTH], wb_ref[1]))
        acc = acc + _sigmoid_times(zg_s[:, 2 * D_MODEL:3 * D_MODEL], _dot(
            oc_s[:, GM_WIDTH + CONV_WIDTH:GM_WIDTH + CONV_WIDTH + ATT_WIDTH], wb_ref[2]))
        gt1 = mod_ref[...][:, 2 * D_MODEL:3 * D_MODEL]
        x1_ref[...] = x_ref[...] + gt1 * _dot(acc.astype(BF), wout_ref[...])


def _sample_mixer_call(layer, z, xs, mod, kt, vt, st, new_k, new_v, new_st, p, bb):
    n = xs.shape[0]
    lsel3 = lambda i: (layer, 0, 0)
    cache_block = (bb, N_KV_HEADS, HEAD_DIM, WINDOW)
    kernel = functools.partial(_sample_mixer_kernel, bb=bb)
    args = (z, xs, mod, kt, vt, st, p["gm_g"], p["gm_w0"], p["gm_b0"], p["conv_dw"], p["conv_b"],
            p["conv_ln_g"], p["conv_ln_b"], p["qn"], p["kn"], p["rope_c1"], p["rope_s1"],
            p["sink_lanes"], p["blockdiag"], p["w_branch"], p["w_out"], new_k, new_v, new_st)
    n_in = len(args)
    return pl.pallas_call(
        kernel,
        out_shape=(
            jax.ShapeDtypeStruct((n, D_MODEL), F32),
            jax.ShapeDtypeStruct(new_k.shape, F32),
            jax.ShapeDtypeStruct(new_v.shape, F32),
            jax.ShapeDtypeStruct(new_st.shape, F32),
            jax.ShapeDtypeStruct((n, GM_WIDTH), F32),
        ),
        input_output_aliases={n_in - 3: 1, n_in - 2: 2, n_in - 1: 3},
        grid=(n // bb,),
        in_specs=[
            pl.BlockSpec((bb, IN_WIDTH), lambda i: (i, 0)),
            pl.BlockSpec((n, D_MODEL), lambda i: (0, 0)),
            pl.BlockSpec((None, n, 3 * D_MODEL), lsel3),
            pl.BlockSpec((None,) + cache_block, lambda i: (layer, i, 0, 0, 0)),
            pl.BlockSpec((None,) + cache_block, lambda i: (layer, i, 0, 0, 0)),
            pl.BlockSpec((None, CONV_K - 1, bb, CONV_WIDTH), lambda i: (layer, 0, i, 0)),
            pl.BlockSpec((None, 1, GM_WIDTH), lsel3),
            pl.BlockSpec((None, 1, GM_WIDTH), lsel3),
            pl.BlockSpec((None, 1, GM_WIDTH), lsel3),
            pl.BlockSpec((None, CONV_K, CONV_WIDTH), lsel3),
            pl.BlockSpec((None, 1, CONV_WIDTH), lsel3),
            pl.BlockSpec((None, 1, CONV_WIDTH), lsel3),
            pl.BlockSpec((None, 1, CONV_WIDTH), lsel3),
            pl.BlockSpec((None, 1, ATT_WIDTH), lsel3),
            pl.BlockSpec((None, 1, LANES), lsel3),
            pl.BlockSpec((1, LANES), lambda i: (0, 0)),
            pl.BlockSpec((1, LANES), lambda i: (0, 0)),
            pl.BlockSpec((None, N_HEADS, LANES), lsel3),
            pl.BlockSpec((ATT_WIDTH, ATT_WIDTH), lambda i: (0, 0)),
            pl.BlockSpec((None, N_BRANCH, GM_WIDTH, D_MODEL), lambda i: (layer, 0, 0, 0)),
            pl.BlockSpec((None, D_MODEL, D_MODEL), lsel3),
            pl.BlockSpec(memory_space=pl.ANY),
            pl.BlockSpec(memory_space=pl.ANY),
            pl.BlockSpec(memory_space=pl.ANY),
        ],
        out_specs=(
            pl.BlockSpec((n, D_MODEL), lambda i: (0, 0)),
            pl.BlockSpec((None,) + cache_block, lambda i: (layer, i, 0, 0, 0)),
            pl.BlockSpec((None,) + cache_block, lambda i: (layer, i, 0, 0, 0)),
            pl.BlockSpec((None, CONV_K - 1, bb, CONV_WIDTH), lambda i: (layer, 0, i, 0)),
            pl.BlockSpec((bb, GM_WIDTH), lambda i: (i, 0)),
        ),
        scratch_shapes=[
            pltpu.VMEM((bb, N_HEADS, LANES), F32),
            pltpu.VMEM((bb, N_HEADS, LANES), F32),
            pltpu.VMEM((n, GM_WIDTH + CONV_WIDTH + ATT_WIDTH), BF),
            pltpu.VMEM((n, N_BRANCH * D_MODEL), F32),
        ],
        compiler_params=pltpu.CompilerParams(
            dimension_semantics=("arbitrary",), vmem_limit_bytes=VMEM_LIMIT),
        name=f"sample_mixers_l{layer}",
    )(*args)


def _rope_lane_tables():
    half = ROT_DIM // 2
    freqs = jnp.exp(-math.log(ROPE_THETA) * jnp.arange(half, dtype=F32) * (2.0 / ROT_DIM))
    rest = jnp.zeros((HEAD_DIM - ROT_DIM,), F32)
    freq64 = jnp.concatenate([freqs, freqs, rest])
    sign64 = jnp.concatenate([-jnp.ones((half,), F32), jnp.ones((half,), F32), rest])
    reps = LANES // HEAD_DIM
    return jnp.tile(freq64, reps)[None, :], jnp.tile(sign64, reps)[None, :]


def _cos_sin(pos, lane_freq):
    ang = pos.astype(F32)[:, None] * lane_freq
    return jnp.cos(ang), jnp.sin(ang)


def _band_bias():
    i = np.arange(WINDOW)[:, None]
    j = np.arange(2 * WINDOW)[None, :]
    band = (j >= i) & (j <= i + WINDOW)
    first = band & (j >= WINDOW)
    out = np.where(np.stack([first, band]), 0.0, NEG_BIG).astype(np.float32)
    return jnp.asarray(out)


def _blockdiag():
    idx = np.arange(ATT_WIDTH) // HEAD_DIM
    return jnp.asarray((idx[:, None] == idx[None, :]).astype(np.float32), dtype=BF)


def _router_hi_lo(router_w):
    rw = router_w.astype(F32).T
    rw_hi = rw.astype(BF)
    rw_lo = (rw - rw_hi.astype(F32)).astype(BF)
    return jnp.concatenate([rw_hi, rw_lo], axis=0)


def _prepare(norm1_g, norm2_g, w_in, gm_norm_g, gm_ws, gm_b, conv_dw, conv_b, conv_ln_g,
             conv_ln_b, q_norm_g, k_norm_g, attn_sinks, w_branch, w_out, router_w, router_b,
             moe_w1, moe_w3, moe_w2, seq, tm):
    lane_freq, lane_sign = _rope_lane_tables()
    cb, sb = _cos_sin(jnp.arange(seq // tm, dtype=jnp.int32) * tm, lane_freq)
    co, so = _cos_sin(jnp.arange(tm, dtype=jnp.int32), lane_freq)
    c1, s1 = _cos_sin(PAST_LEN + jnp.arange(1, dtype=jnp.int32), lane_freq)
    row3 = lambda a: a.reshape(DEPTH, 1, a.shape[-1])
    col = np.arange(IN_WIDTH)
    col_scale = jnp.asarray(np.where((col >= OFF_Q) & (col < OFF_GATE), 1.0, 0.5), F32)
    return {
        "n1g": row3(norm1_g), "n2g": row3(norm2_g), "w_in": (w_in * col_scale).astype(BF),
        "gm_g": row3(gm_norm_g), "gm_ws": gm_ws,
        "gm_bias": jnp.repeat(jnp.swapaxes(gm_b, 1, 2), LANES, axis=2),
        "gm_w0": jnp.repeat(gm_ws[:, :, 0, 0], LANES, axis=1).reshape(DEPTH, 1, GM_WIDTH),
        "gm_b0": jnp.repeat(gm_b[:, :, 0], LANES, axis=1).reshape(DEPTH, 1, GM_WIDTH),
        "conv_dw": conv_dw, "conv_b": row3(conv_b), "conv_ln_g": row3(0.5 * conv_ln_g),
        "conv_ln_b": row3(0.5 * conv_ln_b),
        "qn": row3(jnp.tile(q_norm_g, (1, N_HEADS))),
        "kn": row3(jnp.tile(k_norm_g, (1, N_KV_HEADS))),
        "sinks": attn_sinks,
        "sink_lanes": jnp.broadcast_to(attn_sinks[:, :, None], (DEPTH, N_HEADS, LANES)),
        "rope_base": jnp.concatenate([cb, sb], axis=1)[:, None, :],
        "rope_off": jnp.concatenate([co, so], axis=1),
        "rope_sign": lane_sign, "rope_c1": c1, "rope_s1": s1 * lane_sign,
        "band_bias": _band_bias(), "blockdiag": _blockdiag(),
        "w_branch": (0.5 * w_branch).astype(BF), "w_out": w_out.astype(BF),
        "rw2": _router_hi_lo(router_w),
        "rb": router_b.astype(F32).reshape(N_EXPERTS, 1),
        "moe_w1": moe_w1.astype(BF), "moe_w3": moe_w3.astype(BF),
        "moe_w2": moe_w2.astype(BF).reshape(DEPTH, N_GROUPS, EXPERTS_PER_GROUP * EXPERT_FF, D_MODEL),
    }


def kernel(x_prompt, x_sample, cache_win_k, cache_win_v, state_conv, c_prompt, c_sample, norm1_g, norm2_g, w_ada, b_ada, w_in, gm_norm_g, gm_ws, gm_b, conv_dw, conv_b, conv_ln_g, conv_ln_b, q_norm_g, k_norm_g, attn_sinks, w_branch, w_out, router_w, router_b, moe_w1, moe_w3, moe_w2):
    nb, seq, _ = x_prompt.shape
    ns = x_sample.shape[0]
    tm = min(MIX_TILE, seq)
    tmoe = min(MOE_TILE, seq)
    p = _prepare(norm1_g, norm2_g, w_in, gm_norm_g, gm_ws, gm_b, conv_dw, conv_b, conv_ln_g,
                 conv_ln_b, q_norm_g, k_norm_g, attn_sinks, w_branch, w_out, router_w, router_b,
                 moe_w1, moe_w3, moe_w2, seq, tm)
    assert ns == PROMPT_MOD_ROW and nb <= ADA_ROWS - PROMPT_MOD_ROW
    c_all = jnp.concatenate(
        [c_sample, c_prompt, jnp.zeros((ADA_ROWS - nb - ns, D_MODEL), F32)], axis=0)
    mod = _ada_call(c_all, w_ada, b_ada)

    kt = jnp.transpose(cache_win_k, (0, 1, 3, 4, 2))
    vt = jnp.transpose(cache_win_v, (0, 1, 3, 4, 2))
    st = jnp.transpose(state_conv, (0, 2, 1, 3))

    xp = x_prompt
    xs = x_sample.reshape(ns, D_MODEL)
    kp_l, vp_l, cp_l, gs_l = [], [], [], []
    new_k = jnp.zeros(kt.shape, F32)
    new_v = jnp.zeros(vt.shape, F32)
    new_st = jnp.zeros(st.shape, F32)
    for l in range(DEPTH):
        xp, kp, vp, cp = _mixer_call(l, xp, mod, p, tm)
        xp = _moe_call(l, xp.reshape(nb * seq, D_MODEL), mod, seq // tmoe, p, tmoe,
                       f"prompt_moe_l{l}").reshape(nb, seq, D_MODEL)
        kp_l.append(kp)
        vp_l.append(vp)
        cp_l.append(cp)

        z = _sample_proj_call(l, xs, mod, p)
        xs, new_k, new_v, new_st, gv = _sample_mixer_call(l, z, xs, mod, kt, vt, st,
                                                          new_k, new_v, new_st, p, SAMPLE_BLOCK)
        xs = _moe_call(l, xs, mod, None, p, ns, f"sample_moe_l{l}")
        gs_l.append(gv)

    kv_shape = (DEPTH, nb, WINDOW, N_KV_HEADS, HEAD_DIM)
    return (xp, xs.reshape(ns, 1, D_MODEL),
            jnp.stack(kp_l).reshape(kv_shape), jnp.stack(vp_l).reshape(kv_shape),
            jnp.stack(cp_l),
            jnp.transpose(new_k, (0, 1, 4, 2, 3)),
            jnp.transpose(new_v, (0, 1, 4, 2, 3)),
            jnp.transpose(new_st, (0, 2, 1, 3)),
            jnp.stack(gs_l).reshape(DEPTH, ns, 1, GM_WIDTH))
```

```python
import functools
import math

import jax
import jax.numpy as jnp
import numpy as np
from jax import lax
from jax.experimental import pallas as pl
from jax.experimental.pallas import tpu as pltpu

F32 = jnp.float32
BF = jnp.bfloat16

D_MODEL = 1024
DEPTH = 2
PAST_LEN = 8192
CHUNK = 128
GM_GROUPS = 4
GM_WIDTH = 512
CONV_WIDTH = 512
CONV_K = 31
N_HEADS = 8
N_KV_HEADS = 2
Q_REP = N_HEADS // N_KV_HEADS
HEAD_DIM = 64
ATT_WIDTH = N_HEADS * HEAD_DIM
KV_WIDTH = N_KV_HEADS * HEAD_DIM
WINDOW = 128
ROPE_THETA = 500000.0
ROT_DIM = HEAD_DIM // 4
N_BRANCH = 3
N_EXPERTS = 16
EXPERTS_PER_GROUP = 4
N_GROUPS = N_EXPERTS // EXPERTS_PER_GROUP
EXPERT_FF = 256
EPS = 1e-6

OFF_GM_U = 0
OFF_GM_V = 512
OFF_CV_A = 1024
OFF_CV_G = 1536
OFF_Q = 2048
OFF_K = 2560
OFF_V = 2688
OFF_GATE = 2816
IN_WIDTH = OFF_GATE + N_BRANCH * D_MODEL

LANES = 128
SUBLANES = 8
MXU_COLS = 256
HALO = 32
MIX_TILE = 512
MOE_TILE = 1024
MOE_CAP = 288
ADA_ROWS = 136
PROMPT_MOD_ROW = 128
ADA_BLOCK = 1536
PROJ_BLOCK = IN_WIDTH // 2
SAMPLE_BLOCK = 32
VMEM_LIMIT = 56 * 1024 * 1024
MOE_VMEM_LIMIT = 62 * 1024 * 1024
NEG_BIG = -1e30


def _dot(a, b):
    return jnp.dot(a, b, preferred_element_type=F32)


def _dot_nt(a, b):
    return lax.dot_general(a, b, (((1,), (1,)), ((), ())), preferred_element_type=F32)


def _sigmoid(x):
    return 0.5 * jnp.tanh(0.5 * x) + 0.5


def _silu(x):
    return x * _sigmoid(x)


GELU_C = 0.7978845608028654
GELU_K = 0.044715


def _gelu_half(xh):
    return xh + xh * jnp.tanh(xh * (2.0 * GELU_C + (8.0 * GELU_C * GELU_K) * (xh * xh)))


def _sigmoid_times(zh, dh):
    return jnp.tanh(zh) * dh + dh


def _modulate(x, g, shift, scale):
    ms = jnp.mean(x * x, axis=-1, keepdims=True)
    return (x * lax.rsqrt(ms + EPS)) * (g * (1.0 + scale)) + shift


def _head_norm(x, blockdiag, g):
    sq = (x * x).astype(BF)
    width = x.shape[-1]
    step = min(width, MXU_COLS)
    ssum = jnp.concatenate([_dot(sq[:, c:c + step], blockdiag[0:step, 0:step])
                            for c in range(0, width, step)], axis=-1)
    return x * lax.rsqrt(ssum * (1.0 / HEAD_DIM) + EPS) * g


def _rope(x, c, s):
    width = x.shape[-1]
    reps = width // LANES
    cc = jnp.concatenate([c] * reps, axis=-1)
    ss = jnp.concatenate([s] * reps, axis=-1)
    lane = lax.broadcasted_iota(jnp.int32, x.shape, x.ndim - 1) % HEAD_DIM
    partner = jnp.where(lane < ROT_DIM // 2,
                        pltpu.roll(x, width - ROT_DIM // 2, x.ndim - 1),
                        pltpu.roll(x, ROT_DIM // 2, x.ndim - 1))
    return x * cc + partner * ss


def _layer_norm_silu(y, gh, bh):
    mu = jnp.mean(y, axis=-1, keepdims=True)
    yc = y - mu
    var = jnp.mean(yc * yc, axis=-1, keepdims=True)
    h = yc * lax.rsqrt(var + EPS) * gh + bh
    return h * jnp.tanh(h) + h


def _low_half(shape):
    return lax.broadcasted_iota(jnp.int32, shape, len(shape) - 1) % LANES < HEAD_DIM


def _const_spec(shape, index_map):
    return pl.BlockSpec(shape, index_map, pipeline_mode=pl.Buffered(1))


def _ada_kernel(c_ref, w_ref, b_ref, o_ref):
    s = _silu(c_ref[...]).astype(BF)
    o_ref[...] = _dot(s, w_ref[...].astype(BF)) + b_ref[...]


def _ada_call(c_all, w_ada, b_ada):
    nb = (6 * D_MODEL) // ADA_BLOCK
    return pl.pallas_call(
        _ada_kernel,
        out_shape=jax.ShapeDtypeStruct((DEPTH, ADA_ROWS, 6 * D_MODEL), F32),
        grid=(DEPTH, nb),
        in_specs=[
            pl.BlockSpec((ADA_ROWS, D_MODEL), lambda l, j: (0, 0)),
            pl.BlockSpec((None, D_MODEL, ADA_BLOCK), lambda l, j: (l, 0, j)),
            pl.BlockSpec((None, 1, ADA_BLOCK), lambda l, j: (l, 0, j)),
        ],
        out_specs=pl.BlockSpec((None, ADA_ROWS, ADA_BLOCK), lambda l, j: (l, 0, j)),
        compiler_params=pltpu.CompilerParams(
            dimension_semantics=("arbitrary", "arbitrary")),
        name="ada_mod",
    )(c_all, w_ada, b_ada.reshape(DEPTH, 1, 6 * D_MODEL))


def _mixer_kernel(sinks_ref, x_ref, mod_ref, n1g_ref, win_ref, gmg_ref, gmws_ref, gmb_ref,
                  cdw_ref, cb_ref, clg_ref, clb_ref, qn_ref, kn_ref, ropeb_ref, ropeo_ref,
                  sign_ref, bias_ref, bd_ref, wb_ref, wout_ref,
                  x1_ref, kwin_ref, vwin_ref, cst_ref,
                  kd_s, vd_s, abuf, oatt_s, hb_s, z_s, acc_s, *, layer, tm):
    t = pl.program_id(1)
    last = pl.num_programs(1) - 1
    nblk = tm // WINDOW

    @pl.when(t == 0)
    def _():
        kd_s[0:WINDOW, :] = jnp.zeros((WINDOW, 2 * LANES), BF)
        vd_s[0:WINDOW, :] = jnp.zeros((WINDOW, 2 * LANES), BF)
        abuf[0:HALO, :] = jnp.zeros((HALO, CONV_WIDTH), F32)
        abuf[HALO + tm:HALO + tm + SUBLANES, :] = jnp.zeros((SUBLANES, CONV_WIDTH), F32)

    x = x_ref[...]
    mod = mod_ref[pl.ds(pl.program_id(0), 1), :]
    sh1 = mod[:, 0:D_MODEL]
    sc1 = mod[:, D_MODEL:2 * D_MODEL]
    gt1 = mod[:, 2 * D_MODEL:3 * D_MODEL]
    hb_s[...] = _modulate(x, n1g_ref[...], sh1, sc1).astype(BF)

    def proj(off, width):
        z_s[:, off:off + width] = _dot(hb_s[...], win_ref[:, off:off + width])

    def zcols(off, width):
        return z_s[:, off:off + width]

    proj(OFF_CV_A, 2 * CONV_WIDTH)
    a = _sigmoid_times(zcols(OFF_CV_G, CONV_WIDTH), zcols(OFF_CV_A, CONV_WIDTH))
    abuf[HALO:HALO + tm, :] = a
    first_off = HALO - (CONV_K - 1)

    def conv_tile(ci):
        lanes = slice(ci * LANES, (ci + 1) * LANES)
        yt = cb_ref[:, lanes]
        for b in range(SUBLANES):
            part = None
            for off in range(b, HALO + 1, SUBLANES):
                if off < first_off:
                    continue
                term = (cdw_ref[off - first_off:off - first_off + 1, lanes]
                        * abuf[pl.ds(off - b, tm + SUBLANES), lanes])
                part = term if part is None else part + term
            yt = yt + part[b:b + tm, :]
        return yt

    proj(OFF_GM_U, 2 * GM_WIDTH)
    proj(OFF_Q, ATT_WIDTH + 2 * KV_WIDTH)

    def gated_branch(i, o_branch, first):
        for n in range(D_MODEL // MXU_COLS):
            cols = slice(n * MXU_COLS, (n + 1) * MXU_COLS)
            g0 = OFF_GATE + i * D_MODEL + n * MXU_COLS
            piece = _sigmoid_times(_dot(hb_s[...], win_ref[:, g0:g0 + MXU_COLS]),
                                   _dot(o_branch, wb_ref[i, :, cols]))
            if first:
                acc_s[:, cols] = piece
            else:
                acc_s[:, cols] += piece

    y = jnp.concatenate([conv_tile(ci) for ci in range(CONV_WIDTH // LANES)], axis=1)
    abuf[0:HALO, :] = abuf[tm:tm + HALO, :]
    o_cv = _layer_norm_silu(y, clg_ref[...], clb_ref[...]).astype(BF)
    gated_branch(1, o_cv, True)

    u = _gelu_half(zcols(OFF_GM_U, GM_WIDTH))
    gv = _gelu_half(zcols(OFF_GM_V, GM_WIDTH))
    v = gv * lax.rsqrt(jnp.mean(gv * gv, axis=-1, keepdims=True) + EPS) * gmg_ref[...]
    vb = v.astype(BF)
    row = lax.broadcasted_iota(jnp.int32, (CHUNK, CHUNK), 0)
    col = lax.broadcasted_iota(jnp.int32, (CHUNK, CHUNK), 1)
    ws = [jnp.where(row >= col, gmws_ref[g], 0.0).astype(BF) for g in range(GM_GROUPS)]
    gmb = gmb_ref[...]
    chunks = []
    for c in range(nblk):
        parts = [_dot(ws[g], vb[c * CHUNK:(c + 1) * CHUNK, g * LANES:(g + 1) * LANES])
                 for g in range(GM_GROUPS)]
        chunks.append(jnp.concatenate(parts, axis=1) + gmb)
    o_gm = (u * jnp.concatenate(chunks, axis=0)).astype(BF)
    gated_branch(0, o_gm, False)

    rbase = ropeb_ref[...]
    roff = ropeo_ref[...]
    cb_, sb_ = rbase[:, 0:LANES], rbase[:, LANES:2 * LANES]
    co_, so_ = roff[:, 0:LANES], roff[:, LANES:2 * LANES]
    rc = cb_ * co_ - sb_ * so_
    rs = (sb_ * co_ + cb_ * so_) * sign_ref[...]
    bd = bd_ref[...]
    q = _rope(_head_norm(zcols(OFF_Q, ATT_WIDTH), bd, qn_ref[...]), rc, rs)
    qb = (q * (HEAD_DIM ** -0.5)).astype(BF)
    k_nat = _rope(_head_norm(zcols(OFF_K, KV_WIDTH), bd[0:LANES, 0:LANES], kn_ref[...]), rc, rs)
    v_nat = zcols(OFF_V, KV_WIDTH)
    lo_kv = _low_half((tm, LANES))

    def doubled(x_nat):
        swapped = pltpu.roll(x_nat, HEAD_DIM, 1)
        return jnp.concatenate([jnp.where(lo_kv, x_nat, swapped), jnp.where(lo_kv, swapped, x_nat)],
                               axis=1)

    kd = doubled(k_nat)
    vd = doubled(v_nat)
    kd_s[WINDOW:WINDOW + tm, :] = kd.astype(BF)
    vd_s[WINDOW:WINDOW + tm, :] = vd.astype(BF)

    lo_q = _low_half((WINDOW, LANES))
    zero_q = jnp.zeros((WINDOW, LANES), BF)
    band = bias_ref[1]
    for bi in range(nblk):
        bias = jnp.where(t == 0, bias_ref[0], band) if bi == 0 else band
        bias4 = jnp.concatenate([bias] * Q_REP, axis=0)
        for g in range(N_KV_HEADS):
            tiles = [qb[bi * WINDOW:(bi + 1) * WINDOW, (2 * g + j) * LANES:(2 * g + j + 1) * LANES]
                     for j in range(2)]
            qs = jnp.concatenate([jnp.where(lo_q, tiles[0], zero_q), jnp.where(lo_q, zero_q, tiles[0]),
                                  jnp.where(lo_q, tiles[1], zero_q), jnp.where(lo_q, zero_q, tiles[1])],
                                 axis=0)
            keys = kd_s[bi * WINDOW:(bi + 2) * WINDOW, g * LANES:(g + 1) * LANES]
            vals = vd_s[bi * WINDOW:(bi + 2) * WINDOW, g * LANES:(g + 1) * LANES]
            s = _dot_nt(qs, keys) + bias4
            outs = []
            for hh in range(Q_REP):
                sink = sinks_ref[layer, Q_REP * g + hh]
                sh = s[hh * WINDOW:(hh + 1) * WINDOW, :]
                m = jnp.maximum(jnp.max(sh, axis=-1, keepdims=True), sink)
                p = jnp.exp(sh - m)
                den = jnp.sum(p, axis=-1, keepdims=True) + jnp.exp(sink - m)
                outs.append(_dot(p.astype(BF), vals) / den)
            for j in range(2):
                oatt_s[bi * WINDOW:(bi + 1) * WINDOW, (2 * g + j) * LANES:(2 * g + j + 1) * LANES] = (
                    jnp.where(lo_q, outs[2 * j], outs[2 * j + 1]).astype(BF))
    kd_s[0:WINDOW, :] = kd_s[tm:tm + WINDOW, :]
    vd_s[0:WINDOW, :] = vd_s[tm:tm + WINDOW, :]
    gated_branch(2, oatt_s[...], False)

    accb = acc_s[...].astype(BF)
    for n in range(D_MODEL // MXU_COLS):
        cols = slice(n * MXU_COLS, (n + 1) * MXU_COLS)
        x1_ref[:, cols] = x_ref[:, cols] + gt1[:, cols] * _dot(accb, wout_ref[:, cols])

    @pl.when(t == last)
    def _():
        kwin_ref[...] = k_nat[tm - WINDOW:tm, :]
        vwin_ref[...] = v_nat[tm - WINDOW:tm, :]
        cst_ref[...] = a[tm - (CONV_K - 1):tm, :]


def _mixer_call(layer, x, mod, p, tm):
    nb, seq, _ = x.shape
    nt = seq // tm
    lsel3 = lambda b, t: (layer, 0, 0)
    kernel = functools.partial(_mixer_kernel, layer=layer, tm=tm)
    return pl.pallas_call(
        kernel,
        out_shape=(
            jax.ShapeDtypeStruct((nb, seq, D_MODEL), F32),
            jax.ShapeDtypeStruct((nb, WINDOW, KV_WIDTH), F32),
            jax.ShapeDtypeStruct((nb, WINDOW, KV_WIDTH), F32),
            jax.ShapeDtypeStruct((nb, CONV_K - 1, CONV_WIDTH), F32),
        ),
        grid=(nb, nt),
        in_specs=[
            pl.BlockSpec(memory_space=pltpu.SMEM),
            pl.BlockSpec((None, tm, D_MODEL), lambda b, t: (b, t, 0)),
            pl.BlockSpec((None, SUBLANES, 3 * D_MODEL),
                         lambda b, t: (layer, PROMPT_MOD_ROW // SUBLANES, 0)),
            _const_spec((None, 1, D_MODEL), lsel3),
            _const_spec((None, D_MODEL, IN_WIDTH), lsel3),
            _const_spec((None, 1, GM_WIDTH), lsel3),
            _const_spec((None, GM_GROUPS, CHUNK, CHUNK), lambda b, t: (layer, 0, 0, 0)),
            _const_spec((None, CHUNK, GM_WIDTH), lsel3),
            _const_spec((None, CONV_K, CONV_WIDTH), lsel3),
            _const_spec((None, 1, CONV_WIDTH), lsel3),
            _const_spec((None, 1, CONV_WIDTH), lsel3),
            _const_spec((None, 1, CONV_WIDTH), lsel3),
            _const_spec((None, 1, ATT_WIDTH), lsel3),
            _const_spec((None, 1, LANES), lsel3),
            pl.BlockSpec((None, 1, 2 * LANES), lambda b, t: (t, 0, 0)),
            _const_spec((tm, 2 * LANES), lambda b, t: (0, 0)),
            _const_spec((1, LANES), lambda b, t: (0, 0)),
            _const_spec((2, WINDOW, 2 * WINDOW), lambda b, t: (0, 0, 0)),
            _const_spec((ATT_WIDTH, ATT_WIDTH), lambda b, t: (0, 0)),
            _const_spec((None, N_BRANCH, GM_WIDTH, D_MODEL), lambda b, t: (layer, 0, 0, 0)),
            _const_spec((None, D_MODEL, D_MODEL), lsel3),
        ],
        out_specs=(
            pl.BlockSpec((None, tm, D_MODEL), lambda b, t: (b, t, 0)),
            pl.BlockSpec((None, WINDOW, KV_WIDTH), lambda b, t: (b, 0, 0)),
            pl.BlockSpec((None, WINDOW, KV_WIDTH), lambda b, t: (b, 0, 0)),
            pl.BlockSpec((None, CONV_K - 1, CONV_WIDTH), lambda b, t: (b, 0, 0)),
        ),
        scratch_shapes=[
            pltpu.VMEM((WINDOW + tm, 2 * LANES), BF),
            pltpu.VMEM((WINDOW + tm, 2 * LANES), BF),
            pltpu.VMEM((HALO + tm + SUBLANES, CONV_WIDTH), F32),
            pltpu.VMEM((tm, ATT_WIDTH), BF),
            pltpu.VMEM((tm, D_MODEL), BF),
            pltpu.VMEM((tm, OFF_GATE), F32),
            pltpu.VMEM((tm, D_MODEL), F32),
        ],
        compiler_params=pltpu.CompilerParams(
            dimension_semantics=("arbitrary", "arbitrary"),
            vmem_limit_bytes=VMEM_LIMIT),
        name=f"prompt_mixers_l{layer}",
    )(p["sinks"], x, mod, p["n1g"], p["w_in"], p["gm_g"], p["gm_ws"], p["gm_bias"],
      p["conv_dw"], p["conv_b"], p["conv_ln_g"], p["conv_ln_b"], p["qn"], p["kn"],
      p["rope_base"], p["rope_off"], p["rope_sign"], p["band_bias"], p["blockdiag"],
      p["w_branch"], p["w_out"])


def _route_rows(logits_t, rb_ref):
    scores = _sigmoid(logits_t)
    biased = scores + rb_ref[...]
    rows = lambda a, gi: [a[EXPERTS_PER_GROUP * gi + k:EXPERTS_PER_GROUP * gi + k + 1, :]
                          for k in range(EXPERTS_PER_GROUP)]
    best = None
    idx = None
    for gi in range(N_GROUPS):
        b = rows(biased, gi)
        hi1, lo1 = jnp.maximum(b[0], b[1]), jnp.minimum(b[0], b[1])
        hi2, lo2 = jnp.maximum(b[2], b[3]), jnp.minimum(b[2], b[3])
        gs = jnp.maximum(hi1, hi2) + jnp.maximum(jnp.minimum(hi1, hi2), jnp.maximum(lo1, lo2))
        if gi == 0:
            best, idx = gs, jnp.zeros(gs.shape, jnp.int32)
        else:
            better = gs > best
            idx = jnp.where(better, gi, idx)
            best = jnp.where(better, gs, best)
    bsel = rows(biased, 0)
    ssel = rows(scores, 0)
    for gi in range(1, N_GROUPS):
        bg, sg = rows(biased, gi), rows(scores, gi)
        pick = idx == gi
        bsel = [jnp.where(pick, bg[k], bsel[k]) for k in range(EXPERTS_PER_GROUP)]
        ssel = [jnp.where(pick, sg[k], ssel[k]) for k in range(EXPERTS_PER_GROUP)]
    chosen = []
    for k in range(EXPERTS_PER_GROUP):
        rank = jnp.zeros(idx.shape, jnp.int32)
        for k2 in range(EXPERTS_PER_GROUP):
            if k2 == k:
                continue
            beats = (bsel[k2] > bsel[k]) | ((bsel[k2] == bsel[k]) & (k2 < k))
            rank = rank + beats.astype(jnp.int32)
        chosen.append(jnp.where(rank < 2, ssel[k], 0.0))
    den = chosen[0] + chosen[1] + chosen[2] + chosen[3]
    return idx, [c / den for c in chosen]


def _moe_route(x, mod, n2g_ref, rw2_ref, rb_ref, upper_ref, h_buf, rt_buf, c_buf, cnt_s, buf, tm):
    h2 = _modulate(x, n2g_ref[...], mod[:, 0:D_MODEL], mod[:, D_MODEL:2 * D_MODEL])
    hi = h2.astype(BF)
    lo = (h2 - hi.astype(F32)).astype(BF)
    rw2 = rw2_ref[...]
    lt = _dot_nt(rw2, hi) + _dot_nt(rw2, lo)
    idx, comb = _route_rows(lt[0:N_EXPERTS, :] + lt[N_EXPERTS:2 * N_EXPERTS, :], rb_ref)

    onehot = [(idx == g).astype(F32) for g in range(N_GROUPS)]
    oh8 = jnp.concatenate(onehot + [jnp.zeros((SUBLANES - N_GROUPS, tm), F32)], axis=0)
    prefix = _dot(oh8.astype(BF), upper_ref[...])
    slot = onehot[0] * prefix[0:1, :]
    for g in range(1, N_GROUPS):
        slot = slot + onehot[g] * prefix[g:g + 1, :]
    rt_buf[0:1, :] = idx
    rt_buf[1:2, :] = slot.astype(jnp.int32)
    for g in range(N_GROUPS):
        cnt_s[buf, g] = jnp.sum(onehot[g]).astype(jnp.int32)

    for k in range(EXPERTS_PER_GROUP):
        c_buf[k:k + 1, :] = comb[k]
    h_buf[...] = hi


def _moe_experts(x_ref, gt2, w1_ref, w3_ref, w2_ref, o_ref, h_buf, rt_buf, c_buf, cnt_s, buf,
                 tm, cap, overflow):
    def dispatch(g, j):
        want = lax.broadcasted_iota(jnp.int32, (cap, tm), 0) + j * cap
        hit = (rt_buf[1:2, :] == want) & (rt_buf[0:1, :] == g)
        return hit, jnp.where(hit, 1.0, 0.0).astype(BF)

    def run_experts(g, hit, pmat):
        hg = _dot(pmat, h_buf[...]).astype(BF)
        parts = []
        for e in range(EXPERTS_PER_GROUP):
            a = _dot(hg, w1_ref[EXPERTS_PER_GROUP * g + e])
            b = _dot(hg, w3_ref[EXPERTS_PER_GROUP * g + e])
            ce = jnp.sum(jnp.where(hit, c_buf[e:e + 1, :], 0.0), axis=-1, keepdims=True)
            parts.append(_silu(a) * b * ce)
        act = jnp.concatenate(parts, axis=1).astype(BF)
        return _dot(act, w2_ref[g]).astype(BF)

    def scatter(pmat, y):
        return lax.dot_general(pmat, y, (((0,), (0,)), ((), ())), preferred_element_type=F32)

    if not overflow:
        sel = [dispatch(g, 0) for g in range(N_GROUPS)]
        ys = [run_experts(g, *sel[g]) for g in range(N_GROUPS)]
        y_tile = scatter(jnp.concatenate([s[1] for s in sel], axis=0), jnp.concatenate(ys, axis=0))
        o_ref[...] = x_ref[...] + gt2 * y_tile
        return

    def group_body(g, carry):
        nblk = (cnt_s[buf, g] + (cap - 1)) // cap

        def block_body(j, carry2):
            hit, pmat = dispatch(g, j)
            o_ref[...] += gt2 * scatter(pmat, run_experts(g, hit, pmat))
            return carry2

        lax.fori_loop(1, nblk, block_body, 0)
        return carry

    lax.fori_loop(0, N_GROUPS, group_body, 0)


def _moe_kernel(x_ref, xn_ref, mod_ref, n2g_ref, rw2_ref, rb_ref, upper_ref,
                w1_ref, w3_ref, w2_ref, o_ref, h0_s, h1_s, rt0_s, rt1_s, c0_s, c1_s, cnt_s,
                *, tm, cap, tiles_per_row, n_tiles):
    route = functools.partial(_moe_route, n2g_ref=n2g_ref, rw2_ref=rw2_ref, rb_ref=rb_ref,
                              upper_ref=upper_ref, cnt_s=cnt_s, tm=tm)
    experts = functools.partial(_moe_experts, x_ref, w1_ref=w1_ref,
                                w3_ref=w3_ref, w2_ref=w2_ref, o_ref=o_ref, cnt_s=cnt_s, tm=tm, cap=cap)
    bufs = [dict(h_buf=h0_s, rt_buf=rt0_s, c_buf=c0_s, buf=0),
            dict(h_buf=h1_s, rt_buf=rt1_s, c_buf=c1_s, buf=1)]
    if tiles_per_row is None:
        mod = mod_ref[...]
        route(x_ref[...], mod, **bufs[0])
        for overflow in (False, True):
            experts(mod[:, 2 * D_MODEL:3 * D_MODEL], overflow=overflow, **bufs[0])
        return

    i = pl.program_id(0)
    mod_row = lambda tile: mod_ref[pl.ds(tile // tiles_per_row, 1), :]

    @pl.when(i == 0)
    def _():
        route(x_ref[...], mod_row(i), **bufs[0])

    def step(cur):
        gt2 = mod_row(i)[:, 2 * D_MODEL:3 * D_MODEL]
        experts(gt2, overflow=False, **bufs[cur])
        route(xn_ref[...], mod_row(jnp.minimum(i + 1, n_tiles - 1)), **bufs[1 - cur])
        experts(gt2, overflow=True, **bufs[cur])

    @pl.when(i % 2 == 0)
    def _():
        step(0)

    @pl.when(i % 2 == 1)
    def _():
        step(1)


def _moe_call(layer, x2d, mod, tiles_per_row, p, tm, name):
    n = x2d.shape[0]
    n_tiles = n // tm
    if tiles_per_row is None:
        assert n_tiles == 1
        mod_spec = pl.BlockSpec((None, n, 3 * D_MODEL), lambda i: (layer, 0, 1))
    else:
        mod_spec = pl.BlockSpec((None, SUBLANES, 3 * D_MODEL),
                                lambda i: (layer, PROMPT_MOD_ROW // SUBLANES, 1))
    lsel3 = lambda i: (layer, 0, 0)
    lsel4 = lambda i: (layer, 0, 0, 0)
    upper = np.triu(np.ones((tm, tm), np.float32), k=1)
    kernel = functools.partial(_moe_kernel, tm=tm, cap=MOE_CAP, tiles_per_row=tiles_per_row,
                               n_tiles=n_tiles)
    return pl.pallas_call(
        kernel,
        out_shape=jax.ShapeDtypeStruct((n, D_MODEL), F32),
        grid=(n_tiles,),
        in_specs=[
            pl.BlockSpec((tm, D_MODEL), lambda i: (i, 0)),
            pl.BlockSpec((tm, D_MODEL), lambda i: (jnp.minimum(i + 1, n_tiles - 1), 0),
                         pipeline_mode=pl.Buffered(1)),
            mod_spec,
            _const_spec((None, 1, D_MODEL), lsel3),
            _const_spec((2 * N_EXPERTS, D_MODEL), lambda i: (0, 0)),
            _const_spec((N_EXPERTS, 1), lambda i: (0, 0)),
            _const_spec((tm, tm), lambda i: (0, 0)),
            _const_spec((None, N_EXPERTS, D_MODEL, EXPERT_FF), lsel4),
            _const_spec((None, N_EXPERTS, D_MODEL, EXPERT_FF), lsel4),
            _const_spec((None, N_GROUPS, EXPERTS_PER_GROUP * EXPERT_FF, D_MODEL), lsel4),
        ],
        out_specs=pl.BlockSpec((tm, D_MODEL), lambda i: (i, 0)),
        scratch_shapes=[
            pltpu.VMEM((tm, D_MODEL), BF),
            pltpu.VMEM((tm, D_MODEL), BF),
            pltpu.VMEM((SUBLANES, tm), jnp.int32),
            pltpu.VMEM((SUBLANES, tm), jnp.int32),
            pltpu.VMEM((SUBLANES, tm), F32),
            pltpu.VMEM((SUBLANES, tm), F32),
            pltpu.SMEM((2, N_GROUPS), jnp.int32),
        ],
        compiler_params=pltpu.CompilerParams(
            dimension_semantics=("arbitrary",),
            vmem_limit_bytes=MOE_VMEM_LIMIT),
        name=name,
    )(x2d, x2d, mod, p["n2g"], p["rw2"], p["rb"], jnp.asarray(upper, dtype=BF),
      p["moe_w1"], p["moe_w3"], p["moe_w2"])


def _sample_proj_kernel(x_ref, mod_ref, n1g_ref, w_ref, z_ref):
    mod = mod_ref[...]
    h = _modulate(x_ref[...], n1g_ref[...], mod[:, 0:D_MODEL], mod[:, D_MODEL:2 * D_MODEL])
    z_ref[...] = _dot(h.astype(BF), w_ref[...])


def _sample_proj_call(layer, xs, mod, p):
    n = xs.shape[0]
    return pl.pallas_call(
        _sample_proj_kernel,
        out_shape=jax.ShapeDtypeStruct((n, IN_WIDTH), F32),
        grid=(IN_WIDTH // PROJ_BLOCK,),
        in_specs=[
            pl.BlockSpec((n, D_MODEL), lambda j: (0, 0)),
            pl.BlockSpec((None, n, 3 * D_MODEL), lambda j: (layer, 0, 0)),
            pl.BlockSpec((None, 1, D_MODEL), lambda j: (layer, 0, 0)),
            pl.BlockSpec((None, D_MODEL, PROJ_BLOCK), lambda j: (layer, 0, j)),
        ],
        out_specs=pl.BlockSpec((n, PROJ_BLOCK), lambda j: (0, j)),
        compiler_params=pltpu.CompilerParams(dimension_semantics=("arbitrary",)),
        name=f"sample_proj_l{layer}",
    )(xs, mod, p["n1g"], p["w_in"])


def _shift_in_column(cache_t, new_rows, bb):
    flat = cache_t.reshape(bb * HEAD_DIM, WINDOW)
    shifted = pltpu.roll(flat, WINDOW - 1, 1).reshape(bb, HEAD_DIM, WINDOW)
    padded = jnp.concatenate([new_rows, jnp.zeros((LANES - bb, LANES), F32)], axis=0)
    new_t = padded.T
    is_last = lax.broadcasted_iota(jnp.int32, (HEAD_DIM, WINDOW), 1) == WINDOW - 1
    out = []
    for b in range(bb):
        col = new_t[0:HEAD_DIM, b:b + 1]
        out.append(jnp.where(is_last, col, shifted[b]))
    return out


def _sample_mixer_kernel(z_ref, x_ref, mod_ref, kt_ref, vt_ref, st_ref,
                         gmg_ref, gmw_ref, gmb_ref, cdw_ref, cb_ref, clg_ref, clb_ref,
                         qn_ref, kn_ref, ropec_ref, ropes_ref, sink_ref, bd_ref, wb_ref, wout_ref,
                         _k_all, _v_all, _st_all,
                         x1_ref, kto_ref, vto_ref, sto_ref, gv_ref,
                         qf_s, of_s, oc_s, zg_s, *, bb):
    i = pl.program_id(0)
    r0 = pl.multiple_of(i * bb, bb)
    z = z_ref[...]
    zg_s[pl.ds(r0, bb), :] = z[:, OFF_GATE:OFF_GATE + N_BRANCH * D_MODEL]

    u = _gelu_half(z[:, OFF_GM_U:OFF_GM_U + GM_WIDTH])
    gv = _gelu_half(z[:, OFF_GM_V:OFF_GM_V + GM_WIDTH])
    v = gv * lax.rsqrt(jnp.mean(gv * gv, axis=-1, keepdims=True) + EPS) * gmg_ref[...]
    gv_ref[...] = v
    o_gm = u * (v * gmw_ref[...] + gmb_ref[...])

    a = _sigmoid_times(z[:, OFF_CV_G:OFF_CV_G + CONV_WIDTH], z[:, OFF_CV_A:OFF_CV_A + CONV_WIDTH])
    y = cb_ref[...] + cdw_ref[CONV_K - 1:CONV_K, :] * a
    for j in range(CONV_K - 1):
        y = y + cdw_ref[j:j + 1, :] * st_ref[j]
    o_cv = _layer_norm_silu(y, clg_ref[...], clb_ref[...])
    sto_ref[0:CONV_K - 2] = st_ref[1:CONV_K - 1]
    sto_ref[CONV_K - 2] = a

    rc = ropec_ref[...]
    rs = ropes_ref[...]
    bd = bd_ref[...]
    q = _rope(_head_norm(z[:, OFF_Q:OFF_Q + ATT_WIDTH], bd, qn_ref[...]), rc, rs) * (HEAD_DIM ** -0.5)
    knew = _rope(_head_norm(z[:, OFF_K:OFF_K + KV_WIDTH], bd[0:LANES, 0:LANES], kn_ref[...]), rc, rs)
    vnew = z[:, OFF_V:OFF_V + KV_WIDTH]
    lo = _low_half((bb, LANES))
    for h in range(N_HEADS):
        tile = q[:, (h // 2) * LANES:(h // 2 + 1) * LANES]
        if h % 2 == 1:
            tile = pltpu.roll(tile, HEAD_DIM, 1)
        qf_s[:, h, :] = jnp.where(lo, tile, 0.0)
    k_low = [knew, pltpu.roll(knew, HEAD_DIM, 1)]
    v_low = [vnew, pltpu.roll(vnew, HEAD_DIM, 1)]
    for g in range(N_KV_HEADS):
        qg = qf_s[:, Q_REP * g:Q_REP * (g + 1), :]
        kt = kt_ref[:, g]
        vt = vt_ref[:, g]
        s = jnp.einsum("brd,bdp->brp", qg[:, :, 0:HEAD_DIM].astype(BF), kt.astype(BF),
                       preferred_element_type=F32)
        s_new = jnp.sum(qg * k_low[g][:, None, :], axis=-1, keepdims=True)
        sink = sink_ref[Q_REP * g:Q_REP * (g + 1), 0:1][None]
        m = jnp.maximum(jnp.maximum(jnp.max(s, axis=-1, keepdims=True), s_new), sink)
        pr = jnp.exp(s - m)
        p_new = jnp.exp(s_new - m)
        den = jnp.sum(pr, axis=-1, keepdims=True) + p_new + jnp.exp(sink - m)
        o = jnp.einsum("brp,bdp->brd", pr.astype(BF), vt.astype(BF), preferred_element_type=F32)
        o = (o + p_new * v_low[g][:, None, 0:HEAD_DIM]) / den
        of_s[:, Q_REP * g:Q_REP * (g + 1), :] = jnp.concatenate([o, jnp.zeros(o.shape, F32)], axis=-1)
        for b, tile in enumerate(_shift_in_column(kt, k_low[g], bb)):
            kto_ref[b, g] = tile
        for b, tile in enumerate(_shift_in_column(vt, v_low[g], bb)):
            vto_ref[b, g] = tile
    att_tiles = []
    for j in range(N_HEADS // 2):
        second = pltpu.roll(of_s[:, 2 * j + 1, :], HEAD_DIM, 1)
        att_tiles.append(jnp.where(lo, of_s[:, 2 * j, :], second))
    o_att = jnp.concatenate(att_tiles, axis=1)

    oc_s[pl.ds(r0, bb), 0:GM_WIDTH] = o_gm.astype(BF)
    oc_s[pl.ds(r0, bb), GM_WIDTH:GM_WIDTH + CONV_WIDTH] = o_cv.astype(BF)
    oc_s[pl.ds(r0, bb), GM_WIDTH + CONV_WIDTH:GM_WIDTH + CONV_WIDTH + ATT_WIDTH] = o_att.astype(BF)

    @pl.when(i == pl.num_programs(0) - 1)
    def _():
        acc = _sigmoid_times(zg_s[:, 0:D_MODEL], _dot(oc_s[:, 0:GM_WIDTH], wb_ref[0]))
        acc = acc + _sigmoid_times(zg_s[:, D_MODEL:2 * D_MODEL], _dot(
            oc_s[:, GM_WIDTH:GM_WIDTH + CONV_WIDTH], wb_ref[1]))
        acc = acc + _sigmoid_times(zg_s[:, 2 * D_MODEL:3 * D_MODEL], _dot(
            oc_s[:, GM_WIDTH + CONV_WIDTH:GM_WIDTH + CONV_WIDTH + ATT_WIDTH], wb_ref[2]))
        gt1 = mod_ref[...][:, 2 * D_MODEL:3 * D_MODEL]
        x1_ref[...] = x_ref[...] + gt1 * _dot(acc.astype(BF), wout_ref[...])


def _sample_mixer_call(layer, z, xs, mod, kt, vt, st, new_k, new_v, new_st, p, bb):
    n = xs.shape[0]
    lsel3 = lambda i: (layer, 0, 0)
    cache_block = (bb, N_KV_HEADS, HEAD_DIM, WINDOW)
    kernel = functools.partial(_sample_mixer_kernel, bb=bb)
    args = (z, xs, mod, kt, vt, st, p["gm_g"], p["gm_w0"], p["gm_b0"], p["conv_dw"], p["conv_b"],
            p["conv_ln_g"], p["conv_ln_b"], p["qn"], p["kn"], p["rope_c1"], p["rope_s1"],
            p["sink_lanes"], p["blockdiag"], p["w_branch"], p["w_out"], new_k, new_v, new_st)
    n_in = len(args)
    return pl.pallas_call(
        kernel,
        out_shape=(
            jax.ShapeDtypeStruct((n, D_MODEL), F32),
            jax.ShapeDtypeStruct(new_k.shape, F32),
            jax.ShapeDtypeStruct(new_v.shape, F32),
            jax.ShapeDtypeStruct(new_st.shape, F32),
            jax.ShapeDtypeStruct((n, GM_WIDTH), F32),
        ),
        input_output_aliases={n_in - 3: 1, n_in - 2: 2, n_in - 1: 3},
        grid=(n // bb,),
        in_specs=[
            pl.BlockSpec((bb, IN_WIDTH), lambda i: (i, 0)),
            pl.BlockSpec((n, D_MODEL), lambda i: (0, 0)),
            pl.BlockSpec((None, n, 3 * D_MODEL), lsel3),
            pl.BlockSpec((None,) + cache_block, lambda i: (layer, i, 0, 0, 0)),
            pl.BlockSpec((None,) + cache_block, lambda i: (layer, i, 0, 0, 0)),
            pl.BlockSpec((None, CONV_K - 1, bb, CONV_WIDTH), lambda i: (layer, 0, i, 0)),
            pl.BlockSpec((None, 1, GM_WIDTH), lsel3),
            pl.BlockSpec((None, 1, GM_WIDTH), lsel3),
            pl.BlockSpec((None, 1, GM_WIDTH), lsel3),
            pl.BlockSpec((None, CONV_K, CONV_WIDTH), lsel3),
            pl.BlockSpec((None, 1, CONV_WIDTH), lsel3),
            pl.BlockSpec((None, 1, CONV_WIDTH), lsel3),
            pl.BlockSpec((None, 1, CONV_WIDTH), lsel3),
            pl.BlockSpec((None, 1, ATT_WIDTH), lsel3),
            pl.BlockSpec((None, 1, LANES), lsel3),
            pl.BlockSpec((1, LANES), lambda i: (0, 0)),
            pl.BlockSpec((1, LANES), lambda i: (0, 0)),
            pl.BlockSpec((None, N_HEADS, LANES), lsel3),
            pl.BlockSpec((ATT_WIDTH, ATT_WIDTH), lambda i: (0, 0)),
            pl.BlockSpec((None, N_BRANCH, GM_WIDTH, D_MODEL), lambda i: (layer, 0, 0, 0)),
            pl.BlockSpec((None, D_MODEL, D_MODEL), lsel3),
            pl.BlockSpec(memory_space=pl.ANY),
            pl.BlockSpec(memory_space=pl.ANY),
            pl.BlockSpec(memory_space=pl.ANY),
        ],
        out_specs=(
            pl.BlockSpec((n, D_MODEL), lambda i: (0, 0)),
            pl.BlockSpec((None,) + cache_block, lambda i: (layer, i, 0, 0, 0)),
            pl.BlockSpec((None,) + cache_block, lambda i: (layer, i, 0, 0, 0)),
            pl.BlockSpec((None, CONV_K - 1, bb, CONV_WIDTH), lambda i: (layer, 0, i, 0)),
            pl.BlockSpec((bb, GM_WIDTH), lambda i: (i, 0)),
        ),
        scratch_shapes=[
            pltpu.VMEM((bb, N_HEADS, LANES), F32),
            pltpu.VMEM((bb, N_HEADS, LANES), F32),
            pltpu.VMEM((n, GM_WIDTH + CONV_WIDTH + ATT_WIDTH), BF),
            pltpu.VMEM((n, N_BRANCH * D_MODEL), F32),
        ],
        compiler_params=pltpu.CompilerParams(
            dimension_semantics=("arbitrary",), vmem_limit_bytes=VMEM_LIMIT),
        name=f"sample_mixers_l{layer}",
    )(*args)


def _rope_lane_tables():
    half = ROT_DIM // 2
    freqs = jnp.exp(-math.log(ROPE_THETA) * jnp.arange(half, dtype=F32) * (2.0 / ROT_DIM))
    rest = jnp.zeros((HEAD_DIM - ROT_DIM,), F32)
    freq64 = jnp.concatenate([freqs, freqs, rest])
    sign64 = jnp.concatenate([-jnp.ones((half,), F32), jnp.ones((half,), F32), rest])
    reps = LANES // HEAD_DIM
    return jnp.tile(freq64, reps)[None, :], jnp.tile(sign64, reps)[None, :]


def _cos_sin(pos, lane_freq):
    ang = pos.astype(F32)[:, None] * lane_freq
    return jnp.cos(ang), jnp.sin(ang)


def _band_bias():
    i = np.arange(WINDOW)[:, None]
    j = np.arange(2 * WINDOW)[None, :]
    band = (j >= i) & (j <= i + WINDOW)
    first = band & (j >= WINDOW)
    out = np.where(np.stack([first, band]), 0.0, NEG_BIG).astype(np.float32)
    return jnp.asarray(out)


def _blockdiag():
    idx = np.arange(ATT_WIDTH) // HEAD_DIM
    return jnp.asarray((idx[:, None] == idx[None, :]).astype(np.float32), dtype=BF)


def _router_hi_lo(router_w):
    rw = router_w.astype(F32).T
    rw_hi = rw.astype(BF)
    rw_lo = (rw - rw_hi.astype(F32)).astype(BF)
    return jnp.concatenate([rw_hi, rw_lo], axis=0)


def _prepare(norm1_g, norm2_g, w_in, gm_norm_g, gm_ws, gm_b, conv_dw, conv_b, conv_ln_g,
             conv_ln_b, q_norm_g, k_norm_g, attn_sinks, w_branch, w_out, router_w, router_b,
             moe_w1, moe_w3, moe_w2, seq, tm):
    lane_freq, lane_sign = _rope_lane_tables()
    cb, sb = _cos_sin(jnp.arange(seq // tm, dtype=jnp.int32) * tm, lane_freq)
    co, so = _cos_sin(jnp.arange(tm, dtype=jnp.int32), lane_freq)
    c1, s1 = _cos_sin(PAST_LEN + jnp.arange(1, dtype=jnp.int32), lane_freq)
    row3 = lambda a: a.reshape(DEPTH, 1, a.shape[-1])
    col = np.arange(IN_WIDTH)
    col_scale = jnp.asarray(np.where((col >= OFF_Q) & (col < OFF_GATE), 1.0, 0.5), F32)
    return {
        "n1g": row3(norm1_g), "n2g": row3(norm2_g), "w_in": (w_in * col_scale).astype(BF),
        "gm_g": row3(gm_norm_g), "gm_ws": gm_ws,
        "gm_bias": jnp.repeat(jnp.swapaxes(gm_b, 1, 2), LANES, axis=2),
        "gm_w0": jnp.repeat(gm_ws[:, :, 0, 0], LANES, axis=1).reshape(DEPTH, 1, GM_WIDTH),
        "gm_b0": jnp.repeat(gm_b[:, :, 0], LANES, axis=1).reshape(DEPTH, 1, GM_WIDTH),
        "conv_dw": conv_dw, "conv_b": row3(conv_b), "conv_ln_g": row3(0.5 * conv_ln_g),
        "conv_ln_b": row3(0.5 * conv_ln_b),
        "qn": row3(jnp.tile(q_norm_g, (1, N_HEADS))),
        "kn": row3(jnp.tile(k_norm_g, (1, N_KV_HEADS))),
        "sinks": attn_sinks,
        "sink_lanes": jnp.broadcast_to(attn_sinks[:, :, None], (DEPTH, N_HEADS, LANES)),
        "rope_base": jnp.concatenate([cb, sb], axis=1)[:, None, :],
        "rope_off": jnp.concatenate([co, so], axis=1),
        "rope_sign": lane_sign, "rope_c1": c1, "rope_s1": s1 * lane_sign,
        "band_bias": _band_bias(), "blockdiag": _blockdiag(),
        "w_branch": (0.5 * w_branch).astype(BF), "w_out": w_out.astype(BF),
        "rw2": _router_hi_lo(router_w),
        "rb": router_b.astype(F32).reshape(N_EXPERTS, 1),
        "moe_w1": moe_w1.astype(BF), "moe_w3": moe_w3.astype(BF),
        "moe_w2": moe_w2.astype(BF).reshape(DEPTH, N_GROUPS, EXPERTS_PER_GROUP * EXPERT_FF, D_MODEL),
    }


def kernel(x_prompt, x_sample, cache_win_k, cache_win_v, state_conv, c_prompt, c_sample, norm1_g, norm2_g, w_ada, b_ada, w_in, gm_norm_g, gm_ws, gm_b, conv_dw, conv_b, conv_ln_g, conv_ln_b, q_norm_g, k_norm_g, attn_sinks, w_branch, w_out, router_w, router_b, moe_w1, moe_w3, moe_w2):
    nb, seq, _ = x_prompt.shape
    ns = x_sample.shape[0]
    tm = min(MIX_TILE, seq)
    tmoe = min(MOE_TILE, seq)
    p = _prepare(norm1_g, norm2_g, w_in, gm_norm_g, gm_ws, gm_b, conv_dw, conv_b, conv_ln_g,
                 conv_ln_b, q_norm_g, k_norm_g, attn_sinks, w_branch, w_out, router_w, router_b,
                 moe_w1, moe_w3, moe_w2, seq, tm)
    assert ns == PROMPT_MOD_ROW and nb <= ADA_ROWS - PROMPT_MOD_ROW
    c_all = jnp.concatenate(
        [c_sample, c_prompt, jnp.zeros((ADA_ROWS - nb - ns, D_MODEL), F32)], axis=0)
    mod = _ada_call(c_all, w_ada, b_ada)

    kt = jnp.transpose(cache_win_k, (0, 1, 3, 4, 2))
    vt = jnp.transpose(cache_win_v, (0, 1, 3, 4, 2))
    st = jnp.transpose(state_conv, (0, 2, 1, 3))

    xp = x_prompt
    xs = x_sample.reshape(ns, D_MODEL)
    kp_l, vp_l, cp_l, gs_l = [], [], [], []
    new_k = jnp.zeros(kt.shape, F32)
    new_v = jnp.zeros(vt.shape, F32)
    new_st = jnp.zeros(st.shape, F32)
    for l in range(DEPTH):
        xp, kp, vp, cp = _mixer_call(l, xp, mod, p, tm)
        xp = _moe_call(l, xp.reshape(nb * seq, D_MODEL), mod, seq // tmoe, p, tmoe,
                       f"prompt_moe_l{l}").reshape(nb, seq, D_MODEL)
        kp_l.append(kp)
        vp_l.append(vp)
        cp_l.append(cp)

        z = _sample_proj_call(l, xs, mod, p)
        xs, new_k, new_v, new_st, gv = _sample_mixer_call(l, z, xs, mod, kt, vt, st,
                                                          new_k, new_v, new_st, p, SAMPLE_BLOCK)
        xs = _moe_call(l, xs, mod, None, p, ns, f"sample_moe_l{l}")
        gs_l.append(gv)

    kv_shape = (DEPTH, nb, WINDOW, N_KV_HEADS, HEAD_DIM)
    return (xp, xs.reshape(ns, 1, D_MODEL),
            jnp.stack(kp_l).reshape(kv_shape), jnp.stack(vp_l).reshape(kv_shape),
            jnp.stack(cp_l),
            jnp.transpose(new_k, (0, 1, 4, 2, 3)),
            jnp.transpose(new_v, (0, 1, 4, 2, 3)),
            jnp.transpose(new_st, (0, 2, 1, 3)),
            jnp.stack(gs_l).reshape(DEPTH, ns, 1, GM_WIDTH))
```

```python
import functools
import math

import jax
import jax.numpy as jnp
import numpy as np
from jax import lax
from jax.experimental import pallas as pl
from jax.experimental.pallas import tpu as pltpu

F32 = jnp.float32
BF = jnp.bfloat16

D_MODEL = 1024
DEPTH = 2
PAST_LEN = 8192
CHUNK = 128
GM_GROUPS = 4
GM_WIDTH = 512
CONV_WIDTH = 512
CONV_K = 31
N_HEADS = 8
N_KV_HEADS = 2
Q_REP = N_HEADS // N_KV_HEADS
HEAD_DIM = 64
ATT_WIDTH = N_HEADS * HEAD_DIM
KV_WIDTH = N_KV_HEADS * HEAD_DIM
WINDOW = 128
ROPE_THETA = 500000.0
ROT_DIM = HEAD_DIM // 4
N_BRANCH = 3
N_EXPERTS = 16
EXPERTS_PER_GROUP = 4
N_GROUPS = N_EXPERTS // EXPERTS_PER_GROUP
EXPERT_FF = 256
EPS = 1e-6

OFF_GM_U = 0
OFF_GM_V = 512
OFF_CV_A = 1024
OFF_CV_G = 1536
OFF_Q = 2048
OFF_K = 2560
OFF_V = 2688
OFF_GATE = 2816
IN_WIDTH = OFF_GATE + N_BRANCH * D_MODEL

LANES = 128
SUBLANES = 8
MXU_COLS = 256
HALO = 32
MIX_TILE = 512
MOE_TILE = 512
MOE_CAP = 144
ADA_ROWS = 136
PROMPT_MOD_ROW = 128
ADA_BLOCK = 1536
PROJ_BLOCK = IN_WIDTH // 2
SAMPLE_BLOCK = 32
VMEM_LIMIT = 56 * 1024 * 1024
NEG_BIG = -1e30


def _dot(a, b):
    return jnp.dot(a, b, preferred_element_type=F32)


def _dot_nt(a, b):
    return lax.dot_general(a, b, (((1,), (1,)), ((), ())), preferred_element_type=F32)


def _sigmoid(x):
    return 0.5 * jnp.tanh(0.5 * x) + 0.5


def _silu(x):
    return x * _sigmoid(x)


GELU_C = 0.7978845608028654
GELU_K = 0.044715


def _gelu_half(xh):
    return xh + xh * jnp.tanh(xh * (2.0 * GELU_C + (8.0 * GELU_C * GELU_K) * (xh * xh)))


def _sigmoid_times(zh, dh):
    return jnp.tanh(zh) * dh + dh


def _modulate(x, g, shift, scale):
    ms = jnp.mean(x * x, axis=-1, keepdims=True)
    return (x * lax.rsqrt(ms + EPS)) * (g * (1.0 + scale)) + shift


def _head_norm(x, blockdiag, g):
    sq = (x * x).astype(BF)
    width = x.shape[-1]
    step = min(width, MXU_COLS)
    ssum = jnp.concatenate([_dot(sq[:, c:c + step], blockdiag[0:step, 0:step])
                            for c in range(0, width, step)], axis=-1)
    return x * lax.rsqrt(ssum * (1.0 / HEAD_DIM) + EPS) * g


def _rope(x, c, s):
    width = x.shape[-1]
    reps = width // LANES
    cc = jnp.concatenate([c] * reps, axis=-1)
    ss = jnp.concatenate([s] * reps, axis=-1)
    lane = lax.broadcasted_iota(jnp.int32, x.shape, x.ndim - 1) % HEAD_DIM
    partner = jnp.where(lane < ROT_DIM // 2,
                        pltpu.roll(x, width - ROT_DIM // 2, x.ndim - 1),
                        pltpu.roll(x, ROT_DIM // 2, x.ndim - 1))
    return x * cc + partner * ss


def _layer_norm_silu(y, gh, bh):
    mu = jnp.mean(y, axis=-1, keepdims=True)
    yc = y - mu
    var = jnp.mean(yc * yc, axis=-1, keepdims=True)
    h = yc * lax.rsqrt(var + EPS) * gh + bh
    return h * jnp.tanh(h) + h


def _low_half(shape):
    return lax.broadcasted_iota(jnp.int32, shape, len(shape) - 1) % LANES < HEAD_DIM


def _const_spec(shape, index_map):
    return pl.BlockSpec(shape, index_map, pipeline_mode=pl.Buffered(1))


def _ada_kernel(c_ref, w_ref, b_ref, o_ref):
    s = _silu(c_ref[...]).astype(BF)
    o_ref[...] = _dot(s, w_ref[...].astype(BF)) + b_ref[...]


def _ada_call(c_all, w_ada, b_ada):
    nb = (6 * D_MODEL) // ADA_BLOCK
    return pl.pallas_call(
        _ada_kernel,
        out_shape=jax.ShapeDtypeStruct((DEPTH, ADA_ROWS, 6 * D_MODEL), F32),
        grid=(DEPTH, nb),
        in_specs=[
            pl.BlockSpec((ADA_ROWS, D_MODEL), lambda l, j: (0, 0)),
            pl.BlockSpec((None, D_MODEL, ADA_BLOCK), lambda l, j: (l, 0, j)),
            pl.BlockSpec((None, 1, ADA_BLOCK), lambda l, j: (l, 0, j)),
        ],
        out_specs=pl.BlockSpec((None, ADA_ROWS, ADA_BLOCK), lambda l, j: (l, 0, j)),
        compiler_params=pltpu.CompilerParams(
            dimension_semantics=("arbitrary", "arbitrary")),
        name="ada_mod",
    )(c_all, w_ada, b_ada.reshape(DEPTH, 1, 6 * D_MODEL))


def _mixer_kernel(sinks_ref, x_ref, mod_ref, n1g_ref, win_ref, gmg_ref, gmws_ref, gmb_ref,
                  cdw_ref, cb_ref, clg_ref, clb_ref, qn_ref, kn_ref, ropeb_ref, ropeo_ref,
                  sign_ref, bias_ref, bd_ref, wb_ref, wout_ref,
                  x1_ref, kwin_ref, vwin_ref, cst_ref,
                  kd_s, vd_s, abuf, oatt_s, acc_s, *, layer, tm):
    t = pl.program_id(1)
    last = pl.num_programs(1) - 1
    nblk = tm // WINDOW

    @pl.when(t == 0)
    def _():
        kd_s[0:WINDOW, :] = jnp.zeros((WINDOW, 2 * LANES), BF)
        vd_s[0:WINDOW, :] = jnp.zeros((WINDOW, 2 * LANES), BF)
        abuf[0:HALO, :] = jnp.zeros((HALO, CONV_WIDTH), F32)
        abuf[HALO + tm:HALO + tm + SUBLANES, :] = jnp.zeros((SUBLANES, CONV_WIDTH), F32)

    x = x_ref[...]
    mod = mod_ref[pl.ds(pl.program_id(0), 1), :]
    sh1 = mod[:, 0:D_MODEL]
    sc1 = mod[:, D_MODEL:2 * D_MODEL]
    gt1 = mod[:, 2 * D_MODEL:3 * D_MODEL]
    hb = _modulate(x, n1g_ref[...], sh1, sc1).astype(BF)

    def proj(off, width):
        return _dot(hb, win_ref[:, off:off + width])

    z_cv = proj(OFF_CV_A, 2 * CONV_WIDTH)
    a = _sigmoid_times(z_cv[:, CONV_WIDTH:2 * CONV_WIDTH], z_cv[:, 0:CONV_WIDTH])
    abuf[HALO:HALO + tm, :] = a
    first_off = HALO - (CONV_K - 1)

    def conv_tile(ci):
        lanes = slice(ci * LANES, (ci + 1) * LANES)
        yt = cb_ref[:, lanes]
        for b in range(SUBLANES):
            part = None
            for off in range(b, HALO + 1, SUBLANES):
                if off < first_off:
                    continue
                term = (cdw_ref[off - first_off:off - first_off + 1, lanes]
                        * abuf[pl.ds(off - b, tm + SUBLANES), lanes])
                part = term if part is None else part + term
            yt = yt + part[b:b + tm, :]
        return yt

    z_gm = proj(OFF_GM_U, 2 * GM_WIDTH)
    z_att = proj(OFF_Q, ATT_WIDTH + 2 * KV_WIDTH)

    def gated_branch(i, o_branch, first):
        for n in range(D_MODEL // MXU_COLS):
            cols = slice(n * MXU_COLS, (n + 1) * MXU_COLS)
            g0 = OFF_GATE + i * D_MODEL + n * MXU_COLS
            piece = _sigmoid_times(_dot(hb, win_ref[:, g0:g0 + MXU_COLS]),
                                   _dot(o_branch, wb_ref[i, :, cols]))
            if first:
                acc_s[:, cols] = piece
            else:
                acc_s[:, cols] += piece

    y = jnp.concatenate([conv_tile(ci) for ci in range(CONV_WIDTH // LANES)], axis=1)
    abuf[0:HALO, :] = abuf[tm:tm + HALO, :]
    o_cv = _layer_norm_silu(y, clg_ref[...], clb_ref[...]).astype(BF)
    gated_branch(1, o_cv, True)

    u = _gelu_half(z_gm[:, 0:GM_WIDTH])
    gv = _gelu_half(z_gm[:, GM_WIDTH:2 * GM_WIDTH])
    v = gv * lax.rsqrt(jnp.mean(gv * gv, axis=-1, keepdims=True) + EPS) * gmg_ref[...]
    vb = v.astype(BF)
    row = lax.broadcasted_iota(jnp.int32, (CHUNK, CHUNK), 0)
    col = lax.broadcasted_iota(jnp.int32, (CHUNK, CHUNK), 1)
    ws = [jnp.where(row >= col, gmws_ref[g], 0.0).astype(BF) for g in range(GM_GROUPS)]
    gmb = gmb_ref[...]
    chunks = []
    for c in range(nblk):
        parts = [_dot(ws[g], vb[c * CHUNK:(c + 1) * CHUNK, g * LANES:(g + 1) * LANES])
                 for g in range(GM_GROUPS)]
        chunks.append(jnp.concatenate(parts, axis=1) + gmb)
    o_gm = (u * jnp.concatenate(chunks, axis=0)).astype(BF)
    gated_branch(0, o_gm, False)

    rbase = ropeb_ref[...]
    roff = ropeo_ref[...]
    cb_, sb_ = rbase[:, 0:LANES], rbase[:, LANES:2 * LANES]
    co_, so_ = roff[:, 0:LANES], roff[:, LANES:2 * LANES]
    rc = cb_ * co_ - sb_ * so_
    rs = (sb_ * co_ + cb_ * so_) * sign_ref[...]
    bd = bd_ref[...]
    qb = _rope(_head_norm(z_att[:, 0:ATT_WIDTH], bd, qn_ref[...]), rc, rs).astype(BF)
    k_nat = _rope(_head_norm(z_att[:, ATT_WIDTH:ATT_WIDTH + KV_WIDTH], bd[0:LANES, 0:LANES],
                             kn_ref[...]), rc, rs)
    v_nat = z_att[:, ATT_WIDTH + KV_WIDTH:ATT_WIDTH + 2 * KV_WIDTH]
    lo_kv = _low_half((tm, LANES))

    def doubled(x_nat):
        swapped = pltpu.roll(x_nat, HEAD_DIM, 1)
        return jnp.concatenate([jnp.where(lo_kv, x_nat, swapped), jnp.where(lo_kv, swapped, x_nat)],
                               axis=1)

    kd = doubled(k_nat)
    vd = doubled(v_nat)
    kd_s[WINDOW:WINDOW + tm, :] = kd.astype(BF)
    vd_s[WINDOW:WINDOW + tm, :] = vd.astype(BF)

    lo_q = _low_half((WINDOW, LANES))
    zero_q = jnp.zeros((WINDOW, LANES), BF)
    band = bias_ref[1]
    for bi in range(nblk):
        bias = jnp.where(t == 0, bias_ref[0], band) if bi == 0 else band
        bias4 = jnp.concatenate([bias] * Q_REP, axis=0)
        for g in range(N_KV_HEADS):
            tiles = [qb[bi * WINDOW:(bi + 1) * WINDOW, (2 * g + j) * LANES:(2 * g + j + 1) * LANES]
                     for j in range(2)]
            qs = jnp.concatenate([jnp.where(lo_q, tiles[0], zero_q), jnp.where(lo_q, zero_q, tiles[0]),
                                  jnp.where(lo_q, tiles[1], zero_q), jnp.where(lo_q, zero_q, tiles[1])],
                                 axis=0)
            keys = kd_s[bi * WINDOW:(bi + 2) * WINDOW, g * LANES:(g + 1) * LANES]
            vals = vd_s[bi * WINDOW:(bi + 2) * WINDOW, g * LANES:(g + 1) * LANES]
            s = _dot_nt(qs, keys) + bias4
            outs = []
            for hh in range(Q_REP):
                sink = sinks_ref[layer, Q_REP * g + hh]
                sh = s[hh * WINDOW:(hh + 1) * WINDOW, :]
                m = jnp.maximum(jnp.max(sh, axis=-1, keepdims=True), sink)
                p = jnp.exp(sh - m)
                den = jnp.sum(p, axis=-1, keepdims=True) + jnp.exp(sink - m)
                outs.append(_dot(p.astype(BF), vals) / den)
            for j in range(2):
                oatt_s[bi * WINDOW:(bi + 1) * WINDOW, (2 * g + j) * LANES:(2 * g + j + 1) * LANES] = (
                    jnp.where(lo_q, outs[2 * j], outs[2 * j + 1]).astype(BF))
    kd_s[0:WINDOW, :] = kd_s[tm:tm + WINDOW, :]
    vd_s[0:WINDOW, :] = vd_s[tm:tm + WINDOW, :]
    gated_branch(2, oatt_s[...], False)

    accb = acc_s[...].astype(BF)
    for n in range(D_MODEL // MXU_COLS):
        cols = slice(n * MXU_COLS, (n + 1) * MXU_COLS)
        x1_ref[:, cols] = x_ref[:, cols] + gt1[:, cols] * _dot(accb, wout_ref[:, cols])

    @pl.when(t == last)
    def _():
        kwin_ref[...] = k_nat[tm - WINDOW:tm, :]
        vwin_ref[...] = v_nat[tm - WINDOW:tm, :]
        cst_ref[...] = a[tm - (CONV_K - 1):tm, :]


def _mixer_call(layer, x, mod, p, tm):
    nb, seq, _ = x.shape
    nt = seq // tm
    lsel3 = lambda b, t: (layer, 0, 0)
    kernel = functools.partial(_mixer_kernel, layer=layer, tm=tm)
    return pl.pallas_call(
        kernel,
        out_shape=(
            jax.ShapeDtypeStruct((nb, seq, D_MODEL), F32),
            jax.ShapeDtypeStruct((nb, WINDOW, KV_WIDTH), F32),
            jax.ShapeDtypeStruct((nb, WINDOW, KV_WIDTH), F32),
            jax.ShapeDtypeStruct((nb, CONV_K - 1, CONV_WIDTH), F32),
        ),
        grid=(nb, nt),
        in_specs=[
            pl.BlockSpec(memory_space=pltpu.SMEM),
            pl.BlockSpec((None, tm, D_MODEL), lambda b, t: (b, t, 0)),
            pl.BlockSpec((None, SUBLANES, 3 * D_MODEL),
                         lambda b, t: (layer, PROMPT_MOD_ROW // SUBLANES, 0)),
            _const_spec((None, 1, D_MODEL), lsel3),
            _const_spec((None, D_MODEL, IN_WIDTH), lsel3),
            _const_spec((None, 1, GM_WIDTH), lsel3),
            _const_spec((None, GM_GROUPS, CHUNK, CHUNK), lambda b, t: (layer, 0, 0, 0)),
            _const_spec((None, CHUNK, GM_WIDTH), lsel3),
            _const_spec((None, CONV_K, CONV_WIDTH), lsel3),
            _const_spec((None, 1, CONV_WIDTH), lsel3),
            _const_spec((None, 1, CONV_WIDTH), lsel3),
            _const_spec((None, 1, CONV_WIDTH), lsel3),
            _const_spec((None, 1, ATT_WIDTH), lsel3),
            _const_spec((None, 1, LANES), lsel3),
            pl.BlockSpec((None, 1, 2 * LANES), lambda b, t: (t, 0, 0)),
            _const_spec((tm, 2 * LANES), lambda b, t: (0, 0)),
            _const_spec((1, LANES), lambda b, t: (0, 0)),
            _const_spec((2, WINDOW, 2 * WINDOW), lambda b, t: (0, 0, 0)),
            _const_spec((MXU_COLS, MXU_COLS), lambda b, t: (0, 0)),
            _const_spec((None, N_BRANCH, GM_WIDTH, D_MODEL), lambda b, t: (layer, 0, 0, 0)),
            _const_spec((None, D_MODEL, D_MODEL), lsel3),
        ],
        out_specs=(
            pl.BlockSpec((None, tm, D_MODEL), lambda b, t: (b, t, 0)),
            pl.BlockSpec((None, WINDOW, KV_WIDTH), lambda b, t: (b, 0, 0)),
            pl.BlockSpec((None, WINDOW, KV_WIDTH), lambda b, t: (b, 0, 0)),
            pl.BlockSpec((None, CONV_K - 1, CONV_WIDTH), lambda b, t: (b, 0, 0)),
        ),
        scratch_shapes=[
            pltpu.VMEM((WINDOW + tm, 2 * LANES), BF),
            pltpu.VMEM((WINDOW + tm, 2 * LANES), BF),
            pltpu.VMEM((HALO + tm + SUBLANES, CONV_WIDTH), F32),
            pltpu.VMEM((tm, ATT_WIDTH), BF),
            pltpu.VMEM((tm, D_MODEL), F32),
        ],
        compiler_params=pltpu.CompilerParams(
            dimension_semantics=("arbitrary", "arbitrary"),
            vmem_limit_bytes=VMEM_LIMIT),
        name=f"prompt_mixers_l{layer}",
    )(p["sinks"], x, mod, p["n1g"], p["w_in"], p["gm_g"], p["gm_ws"], p["gm_bias"],
      p["conv_dw"], p["conv_b"], p["conv_ln_g"], p["conv_ln_b"], p["qn"], p["kn"],
      p["rope_base"], p["rope_off"], p["rope_sign"], p["band_bias"], p["blockdiag"],
      p["w_branch"], p["w_out"])


def _route_rows(logits_t, rb_ref):
    scores = _sigmoid(logits_t)
    biased = scores + rb_ref[...]
    rows = lambda a, gi: [a[EXPERTS_PER_GROUP * gi + k:EXPERTS_PER_GROUP * gi + k + 1, :]
                          for k in range(EXPERTS_PER_GROUP)]
    best = None
    idx = None
    for gi in range(N_GROUPS):
        b = rows(biased, gi)
        hi1, lo1 = jnp.maximum(b[0], b[1]), jnp.minimum(b[0], b[1])
        hi2, lo2 = jnp.maximum(b[2], b[3]), jnp.minimum(b[2], b[3])
        gs = jnp.maximum(hi1, hi2) + jnp.maximum(jnp.minimum(hi1, hi2), jnp.maximum(lo1, lo2))
        if gi == 0:
            best, idx = gs, jnp.zeros(gs.shape, jnp.int32)
        else:
            better = gs > best
            idx = jnp.where(better, gi, idx)
            best = jnp.where(better, gs, best)
    bsel = rows(biased, 0)
    ssel = rows(scores, 0)
    for gi in range(1, N_GROUPS):
        bg, sg = rows(biased, gi), rows(scores, gi)
        pick = idx == gi
        bsel = [jnp.where(pick, bg[k], bsel[k]) for k in range(EXPERTS_PER_GROUP)]
        ssel = [jnp.where(pick, sg[k], ssel[k]) for k in range(EXPERTS_PER_GROUP)]
    chosen = []
    for k in range(EXPERTS_PER_GROUP):
        rank = jnp.zeros(idx.shape, jnp.int32)
        for k2 in range(EXPERTS_PER_GROUP):
            if k2 == k:
                continue
            beats = (bsel[k2] > bsel[k]) | ((bsel[k2] == bsel[k]) & (k2 < k))
            rank = rank + beats.astype(jnp.int32)
        chosen.append(jnp.where(rank < 2, ssel[k], 0.0))
    den = chosen[0] + chosen[1] + chosen[2] + chosen[3]
    return idx, [c / den for c in chosen]


def _moe_route(x, mod, n2g_ref, rw2_ref, rb_ref, upper_ref, h_buf, rt_buf, c_buf, cnt_s, buf, tm):
    h2 = _modulate(x, n2g_ref[...], mod[:, 0:D_MODEL], mod[:, D_MODEL:2 * D_MODEL])
    hi = h2.astype(BF)
    lo = (h2 - hi.astype(F32)).astype(BF)
    rw2 = rw2_ref[...]
    lt = _dot_nt(rw2, hi) + _dot_nt(rw2, lo)
    idx, comb = _route_rows(lt[0:N_EXPERTS, :] + lt[N_EXPERTS:2 * N_EXPERTS, :], rb_ref)

    onehot = [(idx == g).astype(F32) for g in range(N_GROUPS)]
    oh8 = jnp.concatenate(onehot + [jnp.zeros((SUBLANES - N_GROUPS, tm), F32)], axis=0)
    prefix = _dot(oh8.astype(BF), upper_ref[...])
    slot = onehot[0] * prefix[0:1, :]
    for g in range(1, N_GROUPS):
        slot = slot + onehot[g] * prefix[g:g + 1, :]
    rt_buf[0:1, :] = idx
    rt_buf[1:2, :] = slot.astype(jnp.int32)
    for g in range(N_GROUPS):
        cnt_s[buf, g] = jnp.sum(onehot[g]).astype(jnp.int32)

    for k in range(EXPERTS_PER_GROUP):
        c_buf[k:k + 1, :] = comb[k]
    h_buf[...] = hi


def _moe_experts(x_ref, gt2, w1_ref, w3_ref, w2_ref, o_ref, h_buf, rt_buf, c_buf, cnt_s, buf,
                 tm, cap, overflow):
    def dispatch(g, j):
        want = lax.broadcasted_iota(jnp.int32, (cap, tm), 0) + j * cap
        hit = (rt_buf[1:2, :] == want) & (rt_buf[0:1, :] == g)
        return hit, jnp.where(hit, 1.0, 0.0).astype(BF)

    def run_experts(g, hit, pmat):
        hg = _dot(pmat, h_buf[...]).astype(BF)
        parts = []
        for e in range(EXPERTS_PER_GROUP):
            a = _dot(hg, w1_ref[EXPERTS_PER_GROUP * g + e])
            b = _dot(hg, w3_ref[EXPERTS_PER_GROUP * g + e])
            ce = jnp.sum(jnp.where(hit, c_buf[e:e + 1, :], 0.0), axis=-1, keepdims=True)
            parts.append(_silu(a) * b * ce)
        act = jnp.concatenate(parts, axis=1).astype(BF)
        return _dot(act, w2_ref[g]).astype(BF)

    def scatter(pmat, y):
        return lax.dot_general(pmat, y, (((0,), (0,)), ((), ())), preferred_element_type=F32)

    if not overflow:
        sel = [dispatch(g, 0) for g in range(N_GROUPS)]
        ys = [run_experts(g, *sel[g]) for g in range(N_GROUPS)]
        y_tile = scatter(jnp.concatenate([s[1] for s in sel], axis=0), jnp.concatenate(ys, axis=0))
        o_ref[...] = x_ref[...] + gt2 * y_tile
        return

    def group_body(g, carry):
        nblk = (cnt_s[buf, g] + (cap - 1)) // cap

        def block_body(j, carry2):
            hit, pmat = dispatch(g, j)
            o_ref[...] += gt2 * scatter(pmat, run_experts(g, hit, pmat))
            return carry2

        lax.fori_loop(1, nblk, block_body, 0)
        return carry

    lax.fori_loop(0, N_GROUPS, group_body, 0)


def _moe_kernel(x_ref, xn_ref, mod_ref, n2g_ref, rw2_ref, rb_ref, upper_ref,
                w1_ref, w3_ref, w2_ref, o_ref, h0_s, h1_s, rt0_s, rt1_s, c0_s, c1_s, cnt_s,
                *, tm, cap, tiles_per_row, n_tiles):
    route = functools.partial(_moe_route, n2g_ref=n2g_ref, rw2_ref=rw2_ref, rb_ref=rb_ref,
                              upper_ref=upper_ref, cnt_s=cnt_s, tm=tm)
    experts = functools.partial(_moe_experts, x_ref, w1_ref=w1_ref,
                                w3_ref=w3_ref, w2_ref=w2_ref, o_ref=o_ref, cnt_s=cnt_s, tm=tm, cap=cap)
    bufs = [dict(h_buf=h0_s, rt_buf=rt0_s, c_buf=c0_s, buf=0),
            dict(h_buf=h1_s, rt_buf=rt1_s, c_buf=c1_s, buf=1)]
    if tiles_per_row is None:
        mod = mod_ref[...]
        route(x_ref[...], mod, **bufs[0])
        for overflow in (False, True):
            experts(mod[:, 2 * D_MODEL:3 * D_MODEL], overflow=overflow, **bufs[0])
        return

    i = pl.program_id(0)
    mod_row = lambda tile: mod_ref[pl.ds(tile // tiles_per_row, 1), :]

    @pl.when(i == 0)
    def _():
        route(x_ref[...], mod_row(i), **bufs[0])

    def step(cur):
        gt2 = mod_row(i)[:, 2 * D_MODEL:3 * D_MODEL]
        experts(gt2, overflow=False, **bufs[cur])
        route(xn_ref[...], mod_row(jnp.minimum(i + 1, n_tiles - 1)), **bufs[1 - cur])
        experts(gt2, overflow=True, **bufs[cur])

    @pl.when(i % 2 == 0)
    def _():
        step(0)

    @pl.when(i % 2 == 1)
    def _():
        step(1)


def _moe_call(layer, x2d, mod, tiles_per_row, p, tm, name):
    n = x2d.shape[0]
    n_tiles = n // tm
    if tiles_per_row is None:
        assert n_tiles == 1
        mod_spec = pl.BlockSpec((None, n, 3 * D_MODEL), lambda i: (layer, 0, 1))
    else:
        mod_spec = pl.BlockSpec((None, SUBLANES, 3 * D_MODEL),
                                lambda i: (layer, PROMPT_MOD_ROW // SUBLANES, 1))
    lsel3 = lambda i: (layer, 0, 0)
    lsel4 = lambda i: (layer, 0, 0, 0)
    upper = np.triu(np.ones((tm, tm), np.float32), k=1)
    kernel = functools.partial(_moe_kernel, tm=tm, cap=MOE_CAP, tiles_per_row=tiles_per_row,
                               n_tiles=n_tiles)
    return pl.pallas_call(
        kernel,
        out_shape=jax.ShapeDtypeStruct((n, D_MODEL), F32),
        grid=(n_tiles,),
        in_specs=[
            pl.BlockSpec((tm, D_MODEL), lambda i: (i, 0)),
            pl.BlockSpec((tm, D_MODEL), lambda i: (jnp.minimum(i + 1, n_tiles - 1), 0)),
            mod_spec,
            _const_spec((None, 1, D_MODEL), lsel3),
            _const_spec((2 * N_EXPERTS, D_MODEL), lambda i: (0, 0)),
            _const_spec((N_EXPERTS, 1), lambda i: (0, 0)),
            _const_spec((tm, tm), lambda i: (0, 0)),
            _const_spec((None, N_EXPERTS, D_MODEL, EXPERT_FF), lsel4),
            _const_spec((None, N_EXPERTS, D_MODEL, EXPERT_FF), lsel4),
            _const_spec((None, N_GROUPS, EXPERTS_PER_GROUP * EXPERT_FF, D_MODEL), lsel4),
        ],
        out_specs=pl.BlockSpec((tm, D_MODEL), lambda i: (i, 0)),
        scratch_shapes=[
            pltpu.VMEM((tm, D_MODEL), BF),
            pltpu.VMEM((tm, D_MODEL), BF),
            pltpu.VMEM((SUBLANES, tm), jnp.int32),
            pltpu.VMEM((SUBLANES, tm), jnp.int32),
            pltpu.VMEM((SUBLANES, tm), F32),
            pltpu.VMEM((SUBLANES, tm), F32),
            pltpu.SMEM((2, N_GROUPS), jnp.int32),
        ],
        compiler_params=pltpu.CompilerParams(
            dimension_semantics=("arbitrary",),
            vmem_limit_bytes=VMEM_LIMIT),
        name=name,
    )(x2d, x2d, mod, p["n2g"], p["rw2"], p["rb"], jnp.asarray(upper, dtype=BF),
      p["moe_w1"], p["moe_w3"], p["moe_w2"])


def _sample_proj_kernel(x_ref, mod_ref, n1g_ref, w_ref, z_ref):
    mod = mod_ref[...]
    h = _modulate(x_ref[...], n1g_ref[...], mod[:, 0:D_MODEL], mod[:, D_MODEL:2 * D_MODEL])
    z_ref[...] = _dot(h.astype(BF), w_ref[...])


def _sample_proj_call(layer, xs, mod, p):
    n = xs.shape[0]
    return pl.pallas_call(
        _sample_proj_kernel,
        out_shape=jax.ShapeDtypeStruct((n, IN_WIDTH), F32),
        grid=(IN_WIDTH // PROJ_BLOCK,),
        in_specs=[
            pl.BlockSpec((n, D_MODEL), lambda j: (0, 0)),
            pl.BlockSpec((None, n, 3 * D_MODEL), lambda j: (layer, 0, 0)),
            pl.BlockSpec((None, 1, D_MODEL), lambda j: (layer, 0, 0)),
            pl.BlockSpec((None, D_MODEL, PROJ_BLOCK), lambda j: (layer, 0, j)),
        ],
        out_specs=pl.BlockSpec((n, PROJ_BLOCK), lambda j: (0, j)),
        compiler_params=pltpu.CompilerParams(dimension_semantics=("arbitrary",)),
        name=f"sample_proj_l{layer}",
    )(xs, mod, p["n1g"], p["w_in"])


def _shift_in_column(cache_t, new_rows, bb):
    flat = cache_t.reshape(bb * HEAD_DIM, WINDOW)
    shifted = pltpu.roll(flat, WINDOW - 1, 1).reshape(bb, HEAD_DIM, WINDOW)
    padded = jnp.concatenate([new_rows, jnp.zeros((LANES - bb, LANES), F32)], axis=0)
    new_t = padded.T
    is_last = lax.broadcasted_iota(jnp.int32, (HEAD_DIM, WINDOW), 1) == WINDOW - 1
    out = []
    for b in range(bb):
        col = new_t[0:HEAD_DIM, b:b + 1]
        out.append(jnp.where(is_last, col, shifted[b]))
    return out


def _sample_mixer_kernel(z_ref, x_ref, mod_ref, kt_ref, vt_ref, st_ref,
                         gmg_ref, gmw_ref, gmb_ref, cdw_ref, cb_ref, clg_ref, clb_ref,
                         qn_ref, kn_ref, ropec_ref, ropes_ref, sink_ref, bd_ref, wb_ref, wout_ref,
                         _k_all, _v_all, _st_all,
                         x1_ref, kto_ref, vto_ref, sto_ref, gv_ref,
                         qf_s, of_s, oc_s, zg_s, *, bb):
    i = pl.program_id(0)
    r0 = pl.multiple_of(i * bb, bb)
    z = z_ref[...]
    zg_s[pl.ds(r0, bb), :] = z[:, OFF_GATE:OFF_GATE + N_BRANCH * D_MODEL]

    u = _gelu_half(z[:, OFF_GM_U:OFF_GM_U + GM_WIDTH])
    gv = _gelu_half(z[:, OFF_GM_V:OFF_GM_V + GM_WIDTH])
    v = gv * lax.rsqrt(jnp.mean(gv * gv, axis=-1, keepdims=True) + EPS) * gmg_ref[...]
    gv_ref[...] = v
    o_gm = u * (v * gmw_ref[...] + gmb_ref[...])

    a = _sigmoid_times(z[:, OFF_CV_G:OFF_CV_G + CONV_WIDTH], z[:, OFF_CV_A:OFF_CV_A + CONV_WIDTH])
    y = cb_ref[...] + cdw_ref[CONV_K - 1:CONV_K, :] * a
    for j in range(CONV_K - 1):
        y = y + cdw_ref[j:j + 1, :] * st_ref[j]
    o_cv = _layer_norm_silu(y, clg_ref[...], clb_ref[...])
    sto_ref[0:CONV_K - 2] = st_ref[1:CONV_K - 1]
    sto_ref[CONV_K - 2] = a

    rc = ropec_ref[...]
    rs = ropes_ref[...]
    bd = bd_ref[...]
    q = _rope(_head_norm(z[:, OFF_Q:OFF_Q + ATT_WIDTH], bd, qn_ref[...]), rc, rs)
    knew = _rope(_head_norm(z[:, OFF_K:OFF_K + KV_WIDTH], bd[0:LANES, 0:LANES], kn_ref[...]), rc, rs)
    vnew = z[:, OFF_V:OFF_V + KV_WIDTH]
    lo = _low_half((bb, LANES))
    for h in range(N_HEADS):
        tile = q[:, (h // 2) * LANES:(h // 2 + 1) * LANES]
        if h % 2 == 1:
            tile = pltpu.roll(tile, HEAD_DIM, 1)
        qf_s[:, h, :] = jnp.where(lo, tile, 0.0)
    k_low = [knew, pltpu.roll(knew, HEAD_DIM, 1)]
    v_low = [vnew, pltpu.roll(vnew, HEAD_DIM, 1)]
    for g in range(N_KV_HEADS):
        qg = qf_s[:, Q_REP * g:Q_REP * (g + 1), :]
        kt = kt_ref[:, g]
        vt = vt_ref[:, g]
        s = jnp.einsum("brd,bdp->brp", qg[:, :, 0:HEAD_DIM].astype(BF), kt.astype(BF),
                       preferred_element_type=F32)
        s_new = jnp.sum(qg * k_low[g][:, None, :], axis=-1, keepdims=True)
        sink = sink_ref[Q_REP * g:Q_REP * (g + 1), 0:1][None]
        m = jnp.maximum(jnp.maximum(jnp.max(s, axis=-1, keepdims=True), s_new), sink)
        pr = jnp.exp(s - m)
        p_new = jnp.exp(s_new - m)
        den = jnp.sum(pr, axis=-1, keepdims=True) + p_new + jnp.exp(sink - m)
        o = jnp.einsum("brp,bdp->brd", pr.astype(BF), vt.astype(BF), preferred_element_type=F32)
        o = (o + p_new * v_low[g][:, None, 0:HEAD_DIM]) / den
        of_s[:, Q_REP * g:Q_REP * (g + 1), :] = jnp.concatenate([o, jnp.zeros(o.shape, F32)], axis=-1)
        for b, tile in enumerate(_shift_in_column(kt, k_low[g], bb)):
            kto_ref[b, g] = tile
        for b, tile in enumerate(_shift_in_column(vt, v_low[g], bb)):
            vto_ref[b, g] = tile
    att_tiles = []
    for j in range(N_HEADS // 2):
        second = pltpu.roll(of_s[:, 2 * j + 1, :], HEAD_DIM, 1)
        att_tiles.append(jnp.where(lo, of_s[:, 2 * j, :], second))
    o_att = jnp.concatenate(att_tiles, axis=1)

    oc_s[pl.ds(r0, bb), 0:GM_WIDTH] = o_gm.astype(BF)
    oc_s[pl.ds(r0, bb), GM_WIDTH:GM_WIDTH + CONV_WIDTH] = o_cv.astype(BF)
    oc_s[pl.ds(r0, bb), GM_WIDTH + CONV_WIDTH:GM_WIDTH + CONV_WIDTH + ATT_WIDTH] = o_att.astype(BF)

    @pl.when(i == pl.num_programs(0) - 1)
    def _():
        acc = _sigmoid_times(zg_s[:, 0:D_MODEL], _dot(oc_s[:, 0:GM_WIDTH], wb_ref[0]))
        acc = acc + _sigmoid_times(zg_s[:, D_MODEL:2 * D_MODEL], _dot(
            oc_s[:, GM_WIDTH:GM_WIDTH + CONV_WIDTH], wb_ref[1]))
        acc = acc + _sigmoid_times(zg_s[:, 2 * D_MODEL:3 * D_MODEL], _dot(
            oc_s[:, GM_WIDTH + CONV_WIDTH:GM_WIDTH + CONV_WIDTH + ATT_WIDTH], wb_ref[2]))
        gt1 = mod_ref[...][:, 2 * D_MODEL:3 * D_MODEL]
        x1_ref[...] = x_ref[...] + gt1 * _dot(acc.astype(BF), wout_ref[...])


def _sample_mixer_call(layer, z, xs, mod, kt, vt, st, new_k, new_v, new_st, p, bb):
    n = xs.shape[0]
    lsel3 = lambda i: (layer, 0, 0)
    cache_block = (bb, N_KV_HEADS, HEAD_DIM, WINDOW)
    kernel = functools.partial(_sample_mixer_kernel, bb=bb)
    args = (z, xs, mod, kt, vt, st, p["gm_g"], p["gm_w0"], p["gm_b0"], p["conv_dw"], p["conv_b"],
            p["conv_ln_g"], p["conv_ln_b"], p["qn"], p["kn"], p["rope_c1"], p["rope_s1"],
            p["sink_lanes"], p["blockdiag"], p["w_branch"], p["w_out"], new_k, new_v, new_st)
    n_in = len(args)
    return pl.pallas_call(
        kernel,
        out_shape=(
            jax.ShapeDtypeStruct((n, D_MODEL), F32),
            jax.ShapeDtypeStruct(new_k.shape, F32),
            jax.ShapeDtypeStruct(new_v.shape, F32),
            jax.ShapeDtypeStruct(new_st.shape, F32),
            jax.ShapeDtypeStruct((n, GM_WIDTH), F32),
        ),
        input_output_aliases={n_in - 3: 1, n_in - 2: 2, n_in - 1: 3},
        grid=(n // bb,),
        in_specs=[
            pl.BlockSpec((bb, IN_WIDTH), lambda i: (i, 0)),
            pl.BlockSpec((n, D_MODEL), lambda i: (0, 0)),
            pl.BlockSpec((None, n, 3 * D_MODEL), lsel3),
            pl.BlockSpec((None,) + cache_block, lambda i: (layer, i, 0, 0, 0)),
            pl.BlockSpec((None,) + cache_block, lambda i: (layer, i, 0, 0, 0)),
            pl.BlockSpec((None, CONV_K - 1, bb, CONV_WIDTH), lambda i: (layer, 0, i, 0)),
            pl.BlockSpec((None, 1, GM_WIDTH), lsel3),
            pl.BlockSpec((None, 1, GM_WIDTH), lsel3),
            pl.BlockSpec((None, 1, GM_WIDTH), lsel3),
            pl.BlockSpec((None, CONV_K, CONV_WIDTH), lsel3),
            pl.BlockSpec((None, 1, CONV_WIDTH), lsel3),
            pl.BlockSpec((None, 1, CONV_WIDTH), lsel3),
            pl.BlockSpec((None, 1, CONV_WIDTH), lsel3),
            pl.BlockSpec((None, 1, ATT_WIDTH), lsel3),
            pl.BlockSpec((None, 1, LANES), lsel3),
            pl.BlockSpec((1, LANES), lambda i: (0, 0)),
            pl.BlockSpec((1, LANES), lambda i: (0, 0)),
            pl.BlockSpec((None, N_HEADS, LANES), lsel3),
            pl.BlockSpec((MXU_COLS, MXU_COLS), lambda i: (0, 0)),
            pl.BlockSpec((None, N_BRANCH, GM_WIDTH, D_MODEL), lambda i: (layer, 0, 0, 0)),
            pl.BlockSpec((None, D_MODEL, D_MODEL), lsel3),
            pl.BlockSpec(memory_space=pl.ANY),
            pl.BlockSpec(memory_space=pl.ANY),
            pl.BlockSpec(memory_space=pl.ANY),
        ],
        out_specs=(
            pl.BlockSpec((n, D_MODEL), lambda i: (0, 0)),
            pl.BlockSpec((None,) + cache_block, lambda i: (layer, i, 0, 0, 0)),
            pl.BlockSpec((None,) + cache_block, lambda i: (layer, i, 0, 0, 0)),
            pl.BlockSpec((None, CONV_K - 1, bb, CONV_WIDTH), lambda i: (layer, 0, i, 0)),
            pl.BlockSpec((bb, GM_WIDTH), lambda i: (i, 0)),
        ),
        scratch_shapes=[
            pltpu.VMEM((bb, N_HEADS, LANES), F32),
            pltpu.VMEM((bb, N_HEADS, LANES), F32),
            pltpu.VMEM((n, GM_WIDTH + CONV_WIDTH + ATT_WIDTH), BF),
            pltpu.VMEM((n, N_BRANCH * D_MODEL), F32),
        ],
        compiler_params=pltpu.CompilerParams(
            dimension_semantics=("arbitrary",), vmem_limit_bytes=VMEM_LIMIT),
        name=f"sample_mixers_l{layer}",
    )(*args)


def _rope_lane_tables():
    half = ROT_DIM // 2
    freqs = jnp.exp(-math.log(ROPE_THETA) * jnp.arange(half, dtype=F32) * (2.0 / ROT_DIM))
    rest = jnp.zeros((HEAD_DIM - ROT_DIM,), F32)
    freq64 = jnp.concatenate([freqs, freqs, rest])
    sign64 = jnp.concatenate([-jnp.ones((half,), F32), jnp.ones((half,), F32), rest])
    reps = LANES // HEAD_DIM
    return jnp.tile(freq64, reps)[None, :], jnp.tile(sign64, reps)[None, :]


def _cos_sin(pos, lane_freq):
    ang = pos.astype(F32)[:, None] * lane_freq
    return jnp.cos(ang), jnp.sin(ang)


def _band_bias():
    i = np.arange(WINDOW)[:, None]
    j = np.arange(2 * WINDOW)[None, :]
    band = (j >= i) & (j <= i + WINDOW)
    first = band & (j >= WINDOW)
    out = np.where(np.stack([first, band]), 0.0, NEG_BIG).astype(np.float32)
    return jnp.asarray(out)


def _blockdiag():
    idx = np.arange(MXU_COLS) // HEAD_DIM
    return jnp.asarray((idx[:, None] == idx[None, :]).astype(np.float32), dtype=BF)


def _router_hi_lo(router_w):
    rw = router_w.astype(F32).T
    rw_hi = rw.astype(BF)
    rw_lo = (rw - rw_hi.astype(F32)).astype(BF)
    return jnp.concatenate([rw_hi, rw_lo], axis=0)


def _prepare(norm1_g, norm2_g, w_in, gm_norm_g, gm_ws, gm_b, conv_dw, conv_b, conv_ln_g,
             conv_ln_b, q_norm_g, k_norm_g, attn_sinks, w_branch, w_out, router_w, router_b,
             moe_w1, moe_w3, moe_w2, seq, tm):
    lane_freq, lane_sign = _rope_lane_tables()
    cb, sb = _cos_sin(jnp.arange(seq // tm, dtype=jnp.int32) * tm, lane_freq)
    co, so = _cos_sin(jnp.arange(tm, dtype=jnp.int32), lane_freq)
    c1, s1 = _cos_sin(PAST_LEN + jnp.arange(1, dtype=jnp.int32), lane_freq)
    row3 = lambda a: a.reshape(DEPTH, 1, a.shape[-1])
    col = np.arange(IN_WIDTH)
    col_scale = jnp.asarray(np.where((col >= OFF_Q) & (col < OFF_GATE), 1.0, 0.5), F32)
    return {
        "n1g": row3(norm1_g), "n2g": row3(norm2_g), "w_in": (w_in * col_scale).astype(BF),
        "gm_g": row3(gm_norm_g), "gm_ws": gm_ws,
        "gm_bias": jnp.repeat(jnp.swapaxes(gm_b, 1, 2), LANES, axis=2),
        "gm_w0": jnp.repeat(gm_ws[:, :, 0, 0], LANES, axis=1).reshape(DEPTH, 1, GM_WIDTH),
        "gm_b0": jnp.repeat(gm_b[:, :, 0], LANES, axis=1).reshape(DEPTH, 1, GM_WIDTH),
        "conv_dw": conv_dw, "conv_b": row3(conv_b), "conv_ln_g": row3(0.5 * conv_ln_g),
        "conv_ln_b": row3(0.5 * conv_ln_b),
        "qn": row3(jnp.tile(q_norm_g, (1, N_HEADS)) * (HEAD_DIM ** -0.5)),
        "kn": row3(jnp.tile(k_norm_g, (1, N_KV_HEADS))),
        "sinks": attn_sinks,
        "sink_lanes": jnp.broadcast_to(attn_sinks[:, :, None], (DEPTH, N_HEADS, LANES)),
        "rope_base": jnp.concatenate([cb, sb], axis=1)[:, None, :],
        "rope_off": jnp.concatenate([co, so], axis=1),
        "rope_sign": lane_sign, "rope_c1": c1, "rope_s1": s1 * lane_sign,
        "band_bias": _band_bias(), "blockdiag": _blockdiag(),
        "w_branch": (0.5 * w_branch).astype(BF), "w_out": w_out.astype(BF),
        "rw2": _router_hi_lo(router_w),
        "rb": router_b.astype(F32).reshape(N_EXPERTS, 1),
        "moe_w1": moe_w1.astype(BF), "moe_w3": moe_w3.astype(BF),
        "moe_w2": moe_w2.astype(BF).reshape(DEPTH, N_GROUPS, EXPERTS_PER_GROUP * EXPERT_FF, D_MODEL),
    }


def kernel(x_prompt, x_sample, cache_win_k, cache_win_v, state_conv, c_prompt, c_sample, norm1_g, norm2_g, w_ada, b_ada, w_in, gm_norm_g, gm_ws, gm_b, conv_dw, conv_b, conv_ln_g, conv_ln_b, q_norm_g, k_norm_g, attn_sinks, w_branch, w_out, router_w, router_b, moe_w1, moe_w3, moe_w2):
    nb, seq, _ = x_prompt.shape
    ns = x_sample.shape[0]
    tm = min(MIX_TILE, seq)
    tmoe = min(MOE_TILE, seq)
    p = _prepare(norm1_g, norm2_g, w_in, gm_norm_g, gm_ws, gm_b, conv_dw, conv_b, conv_ln_g,
                 conv_ln_b, q_norm_g, k_norm_g, attn_sinks, w_branch, w_out, router_w, router_b,
                 moe_w1, moe_w3, moe_w2, seq, tm)
    assert ns == PROMPT_MOD_ROW and nb <= ADA_ROWS - PROMPT_MOD_ROW
    c_all = jnp.concatenate(
        [c_sample, c_prompt, jnp.zeros((ADA_ROWS - nb - ns, D_MODEL), F32)], axis=0)
    mod = _ada_call(c_all, w_ada, b_ada)

    kt = jnp.transpose(cache_win_k, (0, 1, 3, 4, 2))
    vt = jnp.transpose(cache_win_v, (0, 1, 3, 4, 2))
    st = jnp.transpose(state_conv, (0, 2, 1, 3))

    xp = x_prompt
    xs = x_sample.reshape(ns, D_MODEL)
    kp_l, vp_l, cp_l, gs_l = [], [], [], []
    new_k = jnp.zeros(kt.shape, F32)
    new_v = jnp.zeros(vt.shape, F32)
    new_st = jnp.zeros(st.shape, F32)
    for l in range(DEPTH):
        xp, kp, vp, cp = _mixer_call(l, xp, mod, p, tm)
        xp = _moe_call(l, xp.reshape(nb * seq, D_MODEL), mod, seq // tmoe, p, tmoe,
                       f"prompt_moe_l{l}").reshape(nb, seq, D_MODEL)
        kp_l.append(kp)
        vp_l.append(vp)
        cp_l.append(cp)

        z = _sample_proj_call(l, xs, mod, p)
        xs, new_k, new_v, new_st, gv = _sample_mixer_call(l, z, xs, mod, kt, vt, st,
                                                          new_k, new_v, new_st, p, SAMPLE_BLOCK)
        xs = _moe_call(l, xs, mod, None, p, ns, f"sample_moe_l{l}")
        gs_l.append(gv)

    kv_shape = (DEPTH, nb, WINDOW, N_KV_HEADS, HEAD_DIM)
    return (xp, xs.reshape(ns, 1, D_MODEL),
            jnp.stack(kp_l).reshape(kv_shape), jnp.stack(vp_l).reshape(kv_shape),
            jnp.stack(cp_l),
            jnp.transpose(new_k, (0, 1, 4, 2, 3)),
            jnp.transpose(new_v, (0, 1, 4, 2, 3)),
            jnp.transpose(new_st, (0, 2, 1, 3)),
            jnp.stack(gs_l).reshape(DEPTH, ns, 1, GM_WIDTH))
```

```python
import functools
import math

import jax
import jax.numpy as jnp
import numpy as np
from jax import lax
from jax.experimental import pallas as pl
from jax.experimental.pallas import tpu as pltpu

F32 = jnp.float32
BF = jnp.bfloat16

D_MODEL = 1024
DEPTH = 2
PAST_LEN = 8192
CHUNK = 128
GM_GROUPS = 4
GM_WIDTH = 512
CONV_WIDTH = 512
CONV_K = 31
N_HEADS = 8
N_KV_HEADS = 2
Q_REP = N_HEADS // N_KV_HEADS
HEAD_DIM = 64
ATT_WIDTH = N_HEADS * HEAD_DIM
KV_WIDTH = N_KV_HEADS * HEAD_DIM
WINDOW = 128
ROPE_THETA = 500000.0
ROT_DIM = HEAD_DIM // 4
N_BRANCH = 3
N_EXPERTS = 16
EXPERTS_PER_GROUP = 4
N_GROUPS = N_EXPERTS // EXPERTS_PER_GROUP
EXPERT_FF = 256
EPS = 1e-6

OFF_GM_U = 0
OFF_GM_V = 512
OFF_CV_A = 1024
OFF_CV_G = 1536
OFF_Q = 2048
OFF_K = 2560
OFF_V = 2688
OFF_GATE = 2816
IN_WIDTH = OFF_GATE + N_BRANCH * D_MODEL

LANES = 128
SUBLANES = 8
MXU_COLS = 256
HALO = 32
MIX_TILE = 512
MOE_TILE = 512
MOE_CAP = 160
ADA_ROWS = 136
PROMPT_MOD_ROW = 128
ADA_BLOCK = 1536
PROJ_BLOCK = IN_WIDTH // 2
SAMPLE_BLOCK = 32
VMEM_LIMIT = 56 * 1024 * 1024
NEG_BIG = -1e30


def _dot(a, b):
    return jnp.dot(a, b, preferred_element_type=F32)


def _dot_nt(a, b):
    return lax.dot_general(a, b, (((1,), (1,)), ((), ())), preferred_element_type=F32)


def _sigmoid(x):
    return 0.5 * jnp.tanh(0.5 * x) + 0.5


def _silu(x):
    return x * _sigmoid(x)


GELU_C = 0.7978845608028654
GELU_K = 0.044715


def _gelu_half(xh):
    return xh + xh * jnp.tanh(xh * (2.0 * GELU_C + (8.0 * GELU_C * GELU_K) * (xh * xh)))


def _sigmoid_times(zh, dh):
    return jnp.tanh(zh) * dh + dh


def _modulate(x, g, shift, scale):
    ms = jnp.mean(x * x, axis=-1, keepdims=True)
    return (x * lax.rsqrt(ms + EPS)) * (g * (1.0 + scale)) + shift


def _head_norm(x, blockdiag, g):
    sq = (x * x).astype(BF)
    width = x.shape[-1]
    step = min(width, MXU_COLS)
    ssum = jnp.concatenate([_dot(sq[:, c:c + step], blockdiag[0:step, 0:step])
                            for c in range(0, width, step)], axis=-1)
    return x * lax.rsqrt(ssum * (1.0 / HEAD_DIM) + EPS) * g


def _rope(x, c, s):
    width = x.shape[-1]
    reps = width // LANES
    cc = jnp.concatenate([c] * reps, axis=-1)
    ss = jnp.concatenate([s] * reps, axis=-1)
    lane = lax.broadcasted_iota(jnp.int32, x.shape, x.ndim - 1) % HEAD_DIM
    partner = jnp.where(lane < ROT_DIM // 2,
                        pltpu.roll(x, width - ROT_DIM // 2, x.ndim - 1),
                        pltpu.roll(x, ROT_DIM // 2, x.ndim - 1))
    return x * cc + partner * ss


def _layer_norm_silu(y, gh, bh):
    mu = jnp.mean(y, axis=-1, keepdims=True)
    yc = y - mu
    var = jnp.mean(yc * yc, axis=-1, keepdims=True)
    h = yc * lax.rsqrt(var + EPS) * gh + bh
    return h * jnp.tanh(h) + h


def _low_half(shape):
    return lax.broadcasted_iota(jnp.int32, shape, len(shape) - 1) % LANES < HEAD_DIM


def _const_spec(shape, index_map):
    return pl.BlockSpec(shape, index_map, pipeline_mode=pl.Buffered(1))


def _ada_kernel(c_ref, w_ref, b_ref, o_ref):
    s = _silu(c_ref[...]).astype(BF)
    o_ref[...] = _dot(s, w_ref[...].astype(BF)) + b_ref[...]


def _ada_call(c_all, w_ada, b_ada):
    nb = (6 * D_MODEL) // ADA_BLOCK
    return pl.pallas_call(
        _ada_kernel,
        out_shape=jax.ShapeDtypeStruct((DEPTH, ADA_ROWS, 6 * D_MODEL), F32),
        grid=(DEPTH, nb),
        in_specs=[
            pl.BlockSpec((ADA_ROWS, D_MODEL), lambda l, j: (0, 0)),
            pl.BlockSpec((None, D_MODEL, ADA_BLOCK), lambda l, j: (l, 0, j)),
            pl.BlockSpec((None, 1, ADA_BLOCK), lambda l, j: (l, 0, j)),
        ],
        out_specs=pl.BlockSpec((None, ADA_ROWS, ADA_BLOCK), lambda l, j: (l, 0, j)),
        compiler_params=pltpu.CompilerParams(
            dimension_semantics=("arbitrary", "arbitrary")),
        name="ada_mod",
    )(c_all, w_ada, b_ada.reshape(DEPTH, 1, 6 * D_MODEL))


def _mixer_kernel(sinks_ref, x_ref, mod_ref, n1g_ref, win_ref, gmg_ref, gmws_ref, gmb_ref,
                  cdw_ref, cb_ref, clg_ref, clb_ref, qn_ref, kn_ref, ropeb_ref, ropeo_ref,
                  sign_ref, bias_ref, bd_ref, wb_ref, wout_ref,
                  x1_ref, kwin_ref, vwin_ref, cst_ref,
                  kd_s, vd_s, abuf, oatt_s, acc_s, *, layer, tm):
    t = pl.program_id(1)
    last = pl.num_programs(1) - 1
    nblk = tm // WINDOW

    @pl.when(t == 0)
    def _():
        kd_s[0:WINDOW, :] = jnp.zeros((WINDOW, 2 * LANES), BF)
        vd_s[0:WINDOW, :] = jnp.zeros((WINDOW, 2 * LANES), BF)
        abuf[0:HALO, :] = jnp.zeros((HALO, CONV_WIDTH), F32)
        abuf[HALO + tm:HALO + tm + SUBLANES, :] = jnp.zeros((SUBLANES, CONV_WIDTH), F32)

    x = x_ref[...]
    mod = mod_ref[pl.ds(pl.program_id(0), 1), :]
    sh1 = mod[:, 0:D_MODEL]
    sc1 = mod[:, D_MODEL:2 * D_MODEL]
    gt1 = mod[:, 2 * D_MODEL:3 * D_MODEL]
    hb = _modulate(x, n1g_ref[...], sh1, sc1).astype(BF)

    def proj(off, width):
        return _dot(hb, win_ref[:, off:off + width])

    z_cv = proj(OFF_CV_A, 2 * CONV_WIDTH)
    a = _sigmoid_times(z_cv[:, CONV_WIDTH:2 * CONV_WIDTH], z_cv[:, 0:CONV_WIDTH])
    abuf[HALO:HALO + tm, :] = a
    first_off = HALO - (CONV_K - 1)

    def conv_tile(ci):
        lanes = slice(ci * LANES, (ci + 1) * LANES)
        yt = cb_ref[:, lanes]
        for b in range(SUBLANES):
            part = None
            for off in range(b, HALO + 1, SUBLANES):
                if off < first_off:
                    continue
                term = (cdw_ref[off - first_off:off - first_off + 1, lanes]
                        * abuf[pl.ds(off - b, tm + SUBLANES), lanes])
                part = term if part is None else part + term
            yt = yt + part[b:b + tm, :]
        return yt

    z_gm = proj(OFF_GM_U, 2 * GM_WIDTH)
    z_att = proj(OFF_Q, ATT_WIDTH + 2 * KV_WIDTH)

    def gated_branch(i, o_branch, first):
        for n in range(D_MODEL // MXU_COLS):
            cols = slice(n * MXU_COLS, (n + 1) * MXU_COLS)
            g0 = OFF_GATE + i * D_MODEL + n * MXU_COLS
            piece = _sigmoid_times(_dot(hb, win_ref[:, g0:g0 + MXU_COLS]),
                                   _dot(o_branch, wb_ref[i, :, cols]))
            if first:
                acc_s[:, cols] = piece
            else:
                acc_s[:, cols] += piece

    y = jnp.concatenate([conv_tile(ci) for ci in range(CONV_WIDTH // LANES)], axis=1)
    abuf[0:HALO, :] = abuf[tm:tm + HALO, :]
    o_cv = _layer_norm_silu(y, clg_ref[...], clb_ref[...]).astype(BF)
    gated_branch(1, o_cv, True)

    u = _gelu_half(z_gm[:, 0:GM_WIDTH])
    gv = _gelu_half(z_gm[:, GM_WIDTH:2 * GM_WIDTH])
    v = gv * lax.rsqrt(jnp.mean(gv * gv, axis=-1, keepdims=True) + EPS) * gmg_ref[...]
    vb = v.astype(BF)
    row = lax.broadcasted_iota(jnp.int32, (CHUNK, CHUNK), 0)
    col = lax.broadcasted_iota(jnp.int32, (CHUNK, CHUNK), 1)
    ws = [jnp.where(row >= col, gmws_ref[g], 0.0).astype(BF) for g in range(GM_GROUPS)]
    gmb = gmb_ref[...]
    chunks = []
    for c in range(nblk):
        parts = [_dot(ws[g], vb[c * CHUNK:(c + 1) * CHUNK, g * LANES:(g + 1) * LANES])
                 for g in range(GM_GROUPS)]
        chunks.append(jnp.concatenate(parts, axis=1) + gmb)
    o_gm = (u * jnp.concatenate(chunks, axis=0)).astype(BF)
    gated_branch(0, o_gm, False)

    rbase = ropeb_ref[...]
    roff = ropeo_ref[...]
    cb_, sb_ = rbase[:, 0:LANES], rbase[:, LANES:2 * LANES]
    co_, so_ = roff[:, 0:LANES], roff[:, LANES:2 * LANES]
    rc = cb_ * co_ - sb_ * so_
    rs = (sb_ * co_ + cb_ * so_) * sign_ref[...]
    bd = bd_ref[...]
    qb = _rope(_head_norm(z_att[:, 0:ATT_WIDTH], bd, qn_ref[...]), rc, rs).astype(BF)
    k_nat = _rope(_head_norm(z_att[:, ATT_WIDTH:ATT_WIDTH + KV_WIDTH], bd[0:LANES, 0:LANES],
                             kn_ref[...]), rc, rs)
    v_nat = z_att[:, ATT_WIDTH + KV_WIDTH:ATT_WIDTH + 2 * KV_WIDTH]
    lo_kv = _low_half((tm, LANES))

    def doubled(x_nat):
        swapped = pltpu.roll(x_nat, HEAD_DIM, 1)
        return jnp.concatenate([jnp.where(lo_kv, x_nat, swapped), jnp.where(lo_kv, swapped, x_nat)],
                               axis=1)

    kd = doubled(k_nat)
    vd = doubled(v_nat)
    kd_s[WINDOW:WINDOW + tm, :] = kd.astype(BF)
    vd_s[WINDOW:WINDOW + tm, :] = vd.astype(BF)

    lo_q = _low_half((WINDOW, LANES))
    zero_q = jnp.zeros((WINDOW, LANES), BF)
    band = bias_ref[1]
    for bi in range(nblk):
        bias = jnp.where(t == 0, bias_ref[0], band) if bi == 0 else band
        bias4 = jnp.concatenate([bias] * Q_REP, axis=0)
        for g in range(N_KV_HEADS):
            tiles = [qb[bi * WINDOW:(bi + 1) * WINDOW, (2 * g + j) * LANES:(2 * g + j + 1) * LANES]
                     for j in range(2)]
            qs = jnp.concatenate([jnp.where(lo_q, tiles[0], zero_q), jnp.where(lo_q, zero_q, tiles[0]),
                                  jnp.where(lo_q, tiles[1], zero_q), jnp.where(lo_q, zero_q, tiles[1])],
                                 axis=0)
            keys = kd_s[bi * WINDOW:(bi + 2) * WINDOW, g * LANES:(g + 1) * LANES]
            vals = vd_s[bi * WINDOW:(bi + 2) * WINDOW, g * LANES:(g + 1) * LANES]
            s = _dot_nt(qs, keys) + bias4
            outs = []
            for hh in range(Q_REP):
                sink = sinks_ref[layer, Q_REP * g + hh]
                sh = s[hh * WINDOW:(hh + 1) * WINDOW, :]
                m = jnp.maximum(jnp.max(sh, axis=-1, keepdims=True), sink)
                p = jnp.exp(sh - m)
                den = jnp.sum(p, axis=-1, keepdims=True) + jnp.exp(sink - m)
                outs.append(_dot(p.astype(BF), vals) / den)
            for j in range(2):
                oatt_s[bi * WINDOW:(bi + 1) * WINDOW, (2 * g + j) * LANES:(2 * g + j + 1) * LANES] = (
                    jnp.where(lo_q, outs[2 * j], outs[2 * j + 1]).astype(BF))
    kd_s[0:WINDOW, :] = kd_s[tm:tm + WINDOW, :]
    vd_s[0:WINDOW, :] = vd_s[tm:tm + WINDOW, :]
    gated_branch(2, oatt_s[...], False)

    accb = acc_s[...].astype(BF)
    for n in range(D_MODEL // MXU_COLS):
        cols = slice(n * MXU_COLS, (n + 1) * MXU_COLS)
        x1_ref[:, cols] = x_ref[:, cols] + gt1[:, cols] * _dot(accb, wout_ref[:, cols])

    @pl.when(t == last)
    def _():
        kwin_ref[...] = k_nat[tm - WINDOW:tm, :]
        vwin_ref[...] = v_nat[tm - WINDOW:tm, :]
        cst_ref[...] = a[tm - (CONV_K - 1):tm, :]


def _mixer_call(layer, x, mod, p, tm):
    nb, seq, _ = x.shape
    nt = seq // tm
    lsel3 = lambda b, t: (layer, 0, 0)
    kernel = functools.partial(_mixer_kernel, layer=layer, tm=tm)
    return pl.pallas_call(
        kernel,
        out_shape=(
            jax.ShapeDtypeStruct((nb, seq, D_MODEL), F32),
            jax.ShapeDtypeStruct((nb, WINDOW, KV_WIDTH), F32),
            jax.ShapeDtypeStruct((nb, WINDOW, KV_WIDTH), F32),
            jax.ShapeDtypeStruct((nb, CONV_K - 1, CONV_WIDTH), F32),
        ),
        grid=(nb, nt),
        in_specs=[
            pl.BlockSpec(memory_space=pltpu.SMEM),
            pl.BlockSpec((None, tm, D_MODEL), lambda b, t: (b, t, 0)),
            pl.BlockSpec((None, SUBLANES, 3 * D_MODEL),
                         lambda b, t: (layer, PROMPT_MOD_ROW // SUBLANES, 0)),
            _const_spec((None, 1, D_MODEL), lsel3),
            _const_spec((None, D_MODEL, IN_WIDTH), lsel3),
            _const_spec((None, 1, GM_WIDTH), lsel3),
            _const_spec((None, GM_GROUPS, CHUNK, CHUNK), lambda b, t: (layer, 0, 0, 0)),
            _const_spec((None, CHUNK, GM_WIDTH), lsel3),
            _const_spec((None, CONV_K, CONV_WIDTH), lsel3),
            _const_spec((None, 1, CONV_WIDTH), lsel3),
            _const_spec((None, 1, CONV_WIDTH), lsel3),
            _const_spec((None, 1, CONV_WIDTH), lsel3),
            _const_spec((None, 1, ATT_WIDTH), lsel3),
            _const_spec((None, 1, LANES), lsel3),
            pl.BlockSpec((None, 1, 2 * LANES), lambda b, t: (t, 0, 0)),
            _const_spec((tm, 2 * LANES), lambda b, t: (0, 0)),
            _const_spec((1, LANES), lambda b, t: (0, 0)),
            _const_spec((2, WINDOW, 2 * WINDOW), lambda b, t: (0, 0, 0)),
            _const_spec((MXU_COLS, MXU_COLS), lambda b, t: (0, 0)),
            _const_spec((None, N_BRANCH, GM_WIDTH, D_MODEL), lambda b, t: (layer, 0, 0, 0)),
            _const_spec((None, D_MODEL, D_MODEL), lsel3),
        ],
        out_specs=(
            pl.BlockSpec((None, tm, D_MODEL), lambda b, t: (b, t, 0)),
            pl.BlockSpec((None, WINDOW, KV_WIDTH), lambda b, t: (b, 0, 0)),
            pl.BlockSpec((None, WINDOW, KV_WIDTH), lambda b, t: (b, 0, 0)),
            pl.BlockSpec((None, CONV_K - 1, CONV_WIDTH), lambda b, t: (b, 0, 0)),
        ),
        scratch_shapes=[
            pltpu.VMEM((WINDOW + tm, 2 * LANES), BF),
            pltpu.VMEM((WINDOW + tm, 2 * LANES), BF),
            pltpu.VMEM((HALO + tm + SUBLANES, CONV_WIDTH), F32),
            pltpu.VMEM((tm, ATT_WIDTH), BF),
            pltpu.VMEM((tm, D_MODEL), F32),
        ],
        compiler_params=pltpu.CompilerParams(
            dimension_semantics=("arbitrary", "arbitrary"),
            vmem_limit_bytes=VMEM_LIMIT),
        name=f"prompt_mixers_l{layer}",
    )(p["sinks"], x, mod, p["n1g"], p["w_in"], p["gm_g"], p["gm_ws"], p["gm_bias"],
      p["conv_dw"], p["conv_b"], p["conv_ln_g"], p["conv_ln_b"], p["qn"], p["kn"],
      p["rope_base"], p["rope_off"], p["rope_sign"], p["band_bias"], p["blockdiag"],
      p["w_branch"], p["w_out"])


def _route_rows(logits_t, rb_ref):
    scores = _sigmoid(logits_t)
    biased = scores + rb_ref[...]
    rows = lambda a, gi: [a[EXPERTS_PER_GROUP * gi + k:EXPERTS_PER_GROUP * gi + k + 1, :]
                          for k in range(EXPERTS_PER_GROUP)]
    best = None
    idx = None
    for gi in range(N_GROUPS):
        b = rows(biased, gi)
        hi1, lo1 = jnp.maximum(b[0], b[1]), jnp.minimum(b[0], b[1])
        hi2, lo2 = jnp.maximum(b[2], b[3]), jnp.minimum(b[2], b[3])
        gs = jnp.maximum(hi1, hi2) + jnp.maximum(jnp.minimum(hi1, hi2), jnp.maximum(lo1, lo2))
        if gi == 0:
            best, idx = gs, jnp.zeros(gs.shape, jnp.int32)
        else:
            better = gs > best
            idx = jnp.where(better, gi, idx)
            best = jnp.where(better, gs, best)
    bsel = rows(biased, 0)
    ssel = rows(scores, 0)
    for gi in range(1, N_GROUPS):
        bg, sg = rows(biased, gi), rows(scores, gi)
        pick = idx == gi
        bsel = [jnp.where(pick, bg[k], bsel[k]) for k in range(EXPERTS_PER_GROUP)]
        ssel = [jnp.where(pick, sg[k], ssel[k]) for k in range(EXPERTS_PER_GROUP)]
    chosen = []
    for k in range(EXPERTS_PER_GROUP):
        rank = jnp.zeros(idx.shape, jnp.int32)
        for k2 in range(EXPERTS_PER_GROUP):
            if k2 == k:
                continue
            beats = (bsel[k2] > bsel[k]) | ((bsel[k2] == bsel[k]) & (k2 < k))
            rank = rank + beats.astype(jnp.int32)
        chosen.append(jnp.where(rank < 2, ssel[k], 0.0))
    den = chosen[0] + chosen[1] + chosen[2] + chosen[3]
    return idx, [c / den for c in chosen]


def _moe_route(x, mod, n2g_ref, rw2_ref, rb_ref, upper_ref, h_buf, rt_buf, c_buf, cnt_s, buf, tm):
    h2 = _modulate(x, n2g_ref[...], mod[:, 0:D_MODEL], mod[:, D_MODEL:2 * D_MODEL])
    hi = h2.astype(BF)
    lo = (h2 - hi.astype(F32)).astype(BF)
    rw2 = rw2_ref[...]
    lt = _dot_nt(rw2, hi) + _dot_nt(rw2, lo)
    idx, comb = _route_rows(lt[0:N_EXPERTS, :] + lt[N_EXPERTS:2 * N_EXPERTS, :], rb_ref)

    onehot = [(idx == g).astype(F32) for g in range(N_GROUPS)]
    oh8 = jnp.concatenate(onehot + [jnp.zeros((SUBLANES - N_GROUPS, tm), F32)], axis=0)
    prefix = _dot(oh8.astype(BF), upper_ref[...])
    slot = onehot[0] * prefix[0:1, :]
    for g in range(1, N_GROUPS):
        slot = slot + onehot[g] * prefix[g:g + 1, :]
    rt_buf[0:1, :] = idx
    rt_buf[1:2, :] = slot.astype(jnp.int32)
    for g in range(N_GROUPS):
        cnt_s[buf, g] = jnp.sum(onehot[g]).astype(jnp.int32)

    for k in range(EXPERTS_PER_GROUP):
        c_buf[k:k + 1, :] = comb[k]
    h_buf[...] = hi


def _moe_experts(x_ref, gt2, w1_ref, w3_ref, w2_ref, o_ref, h_buf, rt_buf, c_buf, cnt_s, buf,
                 tm, cap, overflow):
    def dispatch(g, j):
        want = lax.broadcasted_iota(jnp.int32, (cap, tm), 0) + j * cap
        hit = (rt_buf[1:2, :] == want) & (rt_buf[0:1, :] == g)
        return hit, jnp.where(hit, 1.0, 0.0).astype(BF)

    def run_experts(g, hit, pmat):
        hg = _dot(pmat, h_buf[...]).astype(BF)
        parts = []
        for e in range(EXPERTS_PER_GROUP):
            a = _dot(hg, w1_ref[EXPERTS_PER_GROUP * g + e])
            b = _dot(hg, w3_ref[EXPERTS_PER_GROUP * g + e])
            ce = jnp.sum(jnp.where(hit, c_buf[e:e + 1, :], 0.0), axis=-1, keepdims=True)
            parts.append(_silu(a) * b * ce)
        act = jnp.concatenate(parts, axis=1).astype(BF)
        return _dot(act, w2_ref[g]).astype(BF)

    def scatter(pmat, y):
        return lax.dot_general(pmat, y, (((0,), (0,)), ((), ())), preferred_element_type=F32)

    if not overflow:
        sel = [dispatch(g, 0) for g in range(N_GROUPS)]
        ys = [run_experts(g, *sel[g]) for g in range(N_GROUPS)]
        y_tile = scatter(jnp.concatenate([s[1] for s in sel], axis=0), jnp.concatenate(ys, axis=0))
        o_ref[...] = x_ref[...] + gt2 * y_tile
        return

    def group_body(g, carry):
        nblk = (cnt_s[buf, g] + (cap - 1)) // cap

        def block_body(j, carry2):
            hit, pmat = dispatch(g, j)
            o_ref[...] += gt2 * scatter(pmat, run_experts(g, hit, pmat))
            return carry2

        lax.fori_loop(1, nblk, block_body, 0)
        return carry

    lax.fori_loop(0, N_GROUPS, group_body, 0)


def _moe_kernel(x_ref, xn_ref, mod_ref, n2g_ref, rw2_ref, rb_ref, upper_ref,
                w1_ref, w3_ref, w2_ref, o_ref, h0_s, h1_s, rt0_s, rt1_s, c0_s, c1_s, cnt_s,
                *, tm, cap, tiles_per_row, n_tiles):
    route = functools.partial(_moe_route, n2g_ref=n2g_ref, rw2_ref=rw2_ref, rb_ref=rb_ref,
                              upper_ref=upper_ref, cnt_s=cnt_s, tm=tm)
    experts = functools.partial(_moe_experts, x_ref, w1_ref=w1_ref,
                                w3_ref=w3_ref, w2_ref=w2_ref, o_ref=o_ref, cnt_s=cnt_s, tm=tm, cap=cap)
    bufs = [dict(h_buf=h0_s, rt_buf=rt0_s, c_buf=c0_s, buf=0),
            dict(h_buf=h1_s, rt_buf=rt1_s, c_buf=c1_s, buf=1)]
    if tiles_per_row is None:
        mod = mod_ref[...]
        route(x_ref[...], mod, **bufs[0])
        for overflow in (False, True):
            experts(mod[:, 2 * D_MODEL:3 * D_MODEL], overflow=overflow, **bufs[0])
        return

    i = pl.program_id(0)
    mod_row = lambda tile: mod_ref[pl.ds(tile // tiles_per_row, 1), :]

    @pl.when(i == 0)
    def _():
        route(x_ref[...], mod_row(i), **bufs[0])

    def step(cur):
        gt2 = mod_row(i)[:, 2 * D_MODEL:3 * D_MODEL]
        experts(gt2, overflow=False, **bufs[cur])
        route(xn_ref[...], mod_row(jnp.minimum(i + 1, n_tiles - 1)), **bufs[1 - cur])
        experts(gt2, overflow=True, **bufs[cur])

    @pl.when(i % 2 == 0)
    def _():
        step(0)

    @pl.when(i % 2 == 1)
    def _():
        step(1)


def _moe_call(layer, x2d, mod, tiles_per_row, p, tm, name):
    n = x2d.shape[0]
    n_tiles = n // tm
    if tiles_per_row is None:
        assert n_tiles == 1
        mod_spec = pl.BlockSpec((None, n, 3 * D_MODEL), lambda i: (layer, 0, 1))
    else:
        mod_spec = pl.BlockSpec((None, SUBLANES, 3 * D_MODEL),
                                lambda i: (layer, PROMPT_MOD_ROW // SUBLANES, 1))
    lsel3 = lambda i: (layer, 0, 0)
    lsel4 = lambda i: (layer, 0, 0, 0)
    upper = np.triu(np.ones((tm, tm), np.float32), k=1)
    kernel = functools.partial(_moe_kernel, tm=tm, cap=MOE_CAP, tiles_per_row=tiles_per_row,
                               n_tiles=n_tiles)
    return pl.pallas_call(
        kernel,
        out_shape=jax.ShapeDtypeStruct((n, D_MODEL), F32),
        grid=(n_tiles,),
        in_specs=[
            pl.BlockSpec((tm, D_MODEL), lambda i: (i, 0)),
            pl.BlockSpec((tm, D_MODEL), lambda i: (jnp.minimum(i + 1, n_tiles - 1), 0)),
            mod_spec,
            _const_spec((None, 1, D_MODEL), lsel3),
            _const_spec((2 * N_EXPERTS, D_MODEL), lambda i: (0, 0)),
            _const_spec((N_EXPERTS, 1), lambda i: (0, 0)),
            _const_spec((tm, tm), lambda i: (0, 0)),
            _const_spec((None, N_EXPERTS, D_MODEL, EXPERT_FF), lsel4),
            _const_spec((None, N_EXPERTS, D_MODEL, EXPERT_FF), lsel4),
            _const_spec((None, N_GROUPS, EXPERTS_PER_GROUP * EXPERT_FF, D_MODEL), lsel4),
        ],
        out_specs=pl.BlockSpec((tm, D_MODEL), lambda i: (i, 0)),
        scratch_shapes=[
            pltpu.VMEM((tm, D_MODEL), BF),
            pltpu.VMEM((tm, D_MODEL), BF),
            pltpu.VMEM((SUBLANES, tm), jnp.int32),
            pltpu.VMEM((SUBLANES, tm), jnp.int32),
            pltpu.VMEM((SUBLANES, tm), F32),
            pltpu.VMEM((SUBLANES, tm), F32),
            pltpu.SMEM((2, N_GROUPS), jnp.int32),
        ],
        compiler_params=pltpu.CompilerParams(
            dimension_semantics=("arbitrary",),
            vmem_limit_bytes=VMEM_LIMIT),
        name=name,
    )(x2d, x2d, mod, p["n2g"], p["rw2"], p["rb"], jnp.asarray(upper, dtype=BF),
      p["moe_w1"], p["moe_w3"], p["moe_w2"])


def _sample_proj_kernel(x_ref, mod_ref, n1g_ref, w_ref, z_ref):
    mod = mod_ref[...]
    h = _modulate(x_ref[...], n1g_ref[...], mod[:, 0:D_MODEL], mod[:, D_MODEL:2 * D_MODEL])
    z_ref[...] = _dot(h.astype(BF), w_ref[...])


def _sample_proj_call(layer, xs, mod, p):
    n = xs.shape[0]
    return pl.pallas_call(
        _sample_proj_kernel,
        out_shape=jax.ShapeDtypeStruct((n, IN_WIDTH), F32),
        grid=(IN_WIDTH // PROJ_BLOCK,),
        in_specs=[
            pl.BlockSpec((n, D_MODEL), lambda j: (0, 0)),
            pl.BlockSpec((None, n, 3 * D_MODEL), lambda j: (layer, 0, 0)),
            pl.BlockSpec((None, 1, D_MODEL), lambda j: (layer, 0, 0)),
            pl.BlockSpec((None, D_MODEL, PROJ_BLOCK), lambda j: (layer, 0, j)),
        ],
        out_specs=pl.BlockSpec((n, PROJ_BLOCK), lambda j: (0, j)),
        compiler_params=pltpu.CompilerParams(dimension_semantics=("arbitrary",)),
        name=f"sample_proj_l{layer}",
    )(xs, mod, p["n1g"], p["w_in"])


def _shift_in_column(cache_t, new_rows, bb):
    flat = cache_t.reshape(bb * HEAD_DIM, WINDOW)
    shifted = pltpu.roll(flat, WINDOW - 1, 1).reshape(bb, HEAD_DIM, WINDOW)
    padded = jnp.concatenate([new_rows, jnp.zeros((LANES - bb, LANES), F32)], axis=0)
    new_t = padded.T
    is_last = lax.broadcasted_iota(jnp.int32, (HEAD_DIM, WINDOW), 1) == WINDOW - 1
    out = []
    for b in range(bb):
        col = new_t[0:HEAD_DIM, b:b + 1]
        out.append(jnp.where(is_last, col, shifted[b]))
    return out


def _sample_mixer_kernel(z_ref, x_ref, mod_ref, kt_ref, vt_ref, st_ref,
                         gmg_ref, gmw_ref, gmb_ref, cdw_ref, cb_ref, clg_ref, clb_ref,
                         qn_ref, kn_ref, ropec_ref, ropes_ref, sink_ref, bd_ref, wb_ref, wout_ref,
                         _k_all, _v_all, _st_all,
                         x1_ref, kto_ref, vto_ref, sto_ref, gv_ref,
                         qf_s, of_s, oc_s, zg_s, *, bb):
    i = pl.program_id(0)
    r0 = pl.multiple_of(i * bb, bb)
    z = z_ref[...]
    zg_s[pl.ds(r0, bb), :] = z[:, OFF_GATE:OFF_GATE + N_BRANCH * D_MODEL]

    u = _gelu_half(z[:, OFF_GM_U:OFF_GM_U + GM_WIDTH])
    gv = _gelu_half(z[:, OFF_GM_V:OFF_GM_V + GM_WIDTH])
    v = gv * lax.rsqrt(jnp.mean(gv * gv, axis=-1, keepdims=True) + EPS) * gmg_ref[...]
    gv_ref[...] = v
    o_gm = u * (v * gmw_ref[...] + gmb_ref[...])

    a = _sigmoid_times(z[:, OFF_CV_G:OFF_CV_G + CONV_WIDTH], z[:, OFF_CV_A:OFF_CV_A + CONV_WIDTH])
    y = cb_ref[...] + cdw_ref[CONV_K - 1:CONV_K, :] * a
    for j in range(CONV_K - 1):
        y = y + cdw_ref[j:j + 1, :] * st_ref[j]
    o_cv = _layer_norm_silu(y, clg_ref[...], clb_ref[...])
    sto_ref[0:CONV_K - 2] = st_ref[1:CONV_K - 1]
    sto_ref[CONV_K - 2] = a

    rc = ropec_ref[...]
    rs = ropes_ref[...]
    bd = bd_ref[...]
    q = _rope(_head_norm(z[:, OFF_Q:OFF_Q + ATT_WIDTH], bd, qn_ref[...]), rc, rs)
    knew = _rope(_head_norm(z[:, OFF_K:OFF_K + KV_WIDTH], bd[0:LANES, 0:LANES], kn_ref[...]), rc, rs)
    vnew = z[:, OFF_V:OFF_V + KV_WIDTH]
    lo = _low_half((bb, LANES))
    for h in range(N_HEADS):
        tile = q[:, (h // 2) * LANES:(h // 2 + 1) * LANES]
        if h % 2 == 1:
            tile = pltpu.roll(tile, HEAD_DIM, 1)
        qf_s[:, h, :] = jnp.where(lo, tile, 0.0)
    k_low = [knew, pltpu.roll(knew, HEAD_DIM, 1)]
    v_low = [vnew, pltpu.roll(vnew, HEAD_DIM, 1)]
    for g in range(N_KV_HEADS):
        qg = qf_s[:, Q_REP * g:Q_REP * (g + 1), :]
        kt = kt_ref[:, g]
        vt = vt_ref[:, g]
        s = jnp.einsum("brd,bdp->brp", qg[:, :, 0:HEAD_DIM].astype(BF), kt.astype(BF),
                       preferred_element_type=F32)
        s_new = jnp.sum(qg * k_low[g][:, None, :], axis=-1, keepdims=True)
        sink = sink_ref[Q_REP * g:Q_REP * (g + 1), 0:1][None]
        m = jnp.maximum(jnp.maximum(jnp.max(s, axis=-1, keepdims=True), s_new), sink)
        pr = jnp.exp(s - m)
        p_new = jnp.exp(s_new - m)
        den = jnp.sum(pr, axis=-1, keepdims=True) + p_new + jnp.exp(sink - m)
        o = jnp.einsum("brp,bdp->brd", pr.astype(BF), vt.astype(BF), preferred_element_type=F32)
        o = (o + p_new * v_low[g][:, None, 0:HEAD_DIM]) / den
        of_s[:, Q_REP * g:Q_REP * (g + 1), :] = jnp.concatenate([o, jnp.zeros(o.shape, F32)], axis=-1)
        for b, tile in enumerate(_shift_in_column(kt, k_low[g], bb)):
            kto_ref[b, g] = tile
        for b, tile in enumerate(_shift_in_column(vt, v_low[g], bb)):
            vto_ref[b, g] = tile
    att_tiles = []
    for j in range(N_HEADS // 2):
        second = pltpu.roll(of_s[:, 2 * j + 1, :], HEAD_DIM, 1)
        att_tiles.append(jnp.where(lo, of_s[:, 2 * j, :], second))
    o_att = jnp.concatenate(att_tiles, axis=1)

    oc_s[pl.ds(r0, bb), 0:GM_WIDTH] = o_gm.astype(BF)
    oc_s[pl.ds(r0, bb), GM_WIDTH:GM_WIDTH + CONV_WIDTH] = o_cv.astype(BF)
    oc_s[pl.ds(r0, bb), GM_WIDTH + CONV_WIDTH:GM_WIDTH + CONV_WIDTH + ATT_WIDTH] = o_att.astype(BF)

    @pl.when(i == pl.num_programs(0) - 1)
    def _():
        acc = _sigmoid_times(zg_s[:, 0:D_MODEL], _dot(oc_s[:, 0:GM_WIDTH], wb_ref[0]))
        acc = acc + _sigmoid_times(zg_s[:, D_MODEL:2 * D_MODEL], _dot(
            oc_s[:, GM_WIDTH:GM_WIDTH + CONV_WIDTH], wb_ref[1]))
        acc = acc + _sigmoid_times(zg_s[:, 2 * D_MODEL:3 * D_MODEL], _dot(
            oc_s[:, GM_WIDTH + CONV_WIDTH:GM_WIDTH + CONV_WIDTH + ATT_WIDTH], wb_ref[2]))
        gt1 = mod_ref[...][:, 2 * D_MODEL:3 * D_MODEL]
        x1_ref[...] = x_ref[...] + gt1 * _dot(acc.astype(BF), wout_ref[...])


def _sample_mixer_call(layer, z, xs, mod, kt, vt, st, new_k, new_v, new_st, p, bb):
    n = xs.shape[0]
    lsel3 = lambda i: (layer, 0, 0)
    cache_block = (bb, N_KV_HEADS, HEAD_DIM, WINDOW)
    kernel = functools.partial(_sample_mixer_kernel, bb=bb)
    args = (z, xs, mod, kt, vt, st, p["gm_g"], p["gm_w0"], p["gm_b0"], p["conv_dw"], p["conv_b"],
            p["conv_ln_g"], p["conv_ln_b"], p["qn"], p["kn"], p["rope_c1"], p["rope_s1"],
            p["sink_lanes"], p["blockdiag"], p["w_branch"], p["w_out"], new_k, new_v, new_st)
    n_in = len(args)
    return pl.pallas_call(
        kernel,
        out_shape=(
            jax.ShapeDtypeStruct((n, D_MODEL), F32),
            jax.ShapeDtypeStruct(new_k.shape, F32),
            jax.ShapeDtypeStruct(new_v.shape, F32),
            jax.ShapeDtypeStruct(new_st.shape, F32),
            jax.ShapeDtypeStruct((n, GM_WIDTH), F32),
        ),
        input_output_aliases={n_in - 3: 1, n_in - 2: 2, n_in - 1: 3},
        grid=(n // bb,),
        in_specs=[
            pl.BlockSpec((bb, IN_WIDTH), lambda i: (i, 0)),
            pl.BlockSpec((n, D_MODEL), lambda i: (0, 0)),
            pl.BlockSpec((None, n, 3 * D_MODEL), lsel3),
            pl.BlockSpec((None,) + cache_block, lambda i: (layer, i, 0, 0, 0)),
            pl.BlockSpec((None,) + cache_block, lambda i: (layer, i, 0, 0, 0)),
            pl.BlockSpec((None, CONV_K - 1, bb, CONV_WIDTH), lambda i: (layer, 0, i, 0)),
            pl.BlockSpec((None, 1, GM_WIDTH), lsel3),
            pl.BlockSpec((None, 1, GM_WIDTH), lsel3),
            pl.BlockSpec((None, 1, GM_WIDTH), lsel3),
            pl.BlockSpec((None, CONV_K, CONV_WIDTH), lsel3),
            pl.BlockSpec((None, 1, CONV_WIDTH), lsel3),
            pl.BlockSpec((None, 1, CONV_WIDTH), lsel3),
            pl.BlockSpec((None, 1, CONV_WIDTH), lsel3),
            pl.BlockSpec((None, 1, ATT_WIDTH), lsel3),
            pl.BlockSpec((None, 1, LANES), lsel3),
            pl.BlockSpec((1, LANES), lambda i: (0, 0)),
            pl.BlockSpec((1, LANES), lambda i: (0, 0)),
            pl.BlockSpec((None, N_HEADS, LANES), lsel3),
            pl.BlockSpec((MXU_COLS, MXU_COLS), lambda i: (0, 0)),
            pl.BlockSpec((None, N_BRANCH, GM_WIDTH, D_MODEL), lambda i: (layer, 0, 0, 0)),
            pl.BlockSpec((None, D_MODEL, D_MODEL), lsel3),
            pl.BlockSpec(memory_space=pl.ANY),
            pl.BlockSpec(memory_space=pl.ANY),
            pl.BlockSpec(memory_space=pl.ANY),
        ],
        out_specs=(
            pl.BlockSpec((n, D_MODEL), lambda i: (0, 0)),
            pl.BlockSpec((None,) + cache_block, lambda i: (layer, i, 0, 0, 0)),
            pl.BlockSpec((None,) + cache_block, lambda i: (layer, i, 0, 0, 0)),
            pl.BlockSpec((None, CONV_K - 1, bb, CONV_WIDTH), lambda i: (layer, 0, i, 0)),
            pl.BlockSpec((bb, GM_WIDTH), lambda i: (i, 0)),
        ),
        scratch_shapes=[
            pltpu.VMEM((bb, N_HEADS, LANES), F32),
            pltpu.VMEM((bb, N_HEADS, LANES), F32),
            pltpu.VMEM((n, GM_WIDTH + CONV_WIDTH + ATT_WIDTH), BF),
            pltpu.VMEM((n, N_BRANCH * D_MODEL), F32),
        ],
        compiler_params=pltpu.CompilerParams(
            dimension_semantics=("arbitrary",), vmem_limit_bytes=VMEM_LIMIT),
        name=f"sample_mixers_l{layer}",
    )(*args)


def _rope_lane_tables():
    half = ROT_DIM // 2
    freqs = jnp.exp(-math.log(ROPE_THETA) * jnp.arange(half, dtype=F32) * (2.0 / ROT_DIM))
    rest = jnp.zeros((HEAD_DIM - ROT_DIM,), F32)
    freq64 = jnp.concatenate([freqs, freqs, rest])
    sign64 = jnp.concatenate([-jnp.ones((half,), F32), jnp.ones((half,), F32), rest])
    reps = LANES // HEAD_DIM
    return jnp.tile(freq64, reps)[None, :], jnp.tile(sign64, reps)[None, :]


def _cos_sin(pos, lane_freq):
    ang = pos.astype(F32)[:, None] * lane_freq
    return jnp.cos(ang), jnp.sin(ang)


def _band_bias():
    i = np.arange(WINDOW)[:, None]
    j = np.arange(2 * WINDOW)[None, :]
    band = (j >= i) & (j <= i + WINDOW)
    first = band & (j >= WINDOW)
    out = np.where(np.stack([first, band]), 0.0, NEG_BIG).astype(np.float32)
    return jnp.asarray(out)


def _blockdiag():
    idx = np.arange(MXU_COLS) // HEAD_DIM
    return jnp.asarray((idx[:, None] == idx[None, :]).astype(np.float32), dtype=BF)


def _router_hi_lo(router_w):
    rw = router_w.astype(F32).T
    rw_hi = rw.astype(BF)
    rw_lo = (rw - rw_hi.astype(F32)).astype(BF)
    return jnp.concatenate([rw_hi, rw_lo], axis=0)


def _prepare(norm1_g, norm2_g, w_in, gm_norm_g, gm_ws, gm_b, conv_dw, conv_b, conv_ln_g,
             conv_ln_b, q_norm_g, k_norm_g, attn_sinks, w_branch, w_out, router_w, router_b,
             moe_w1, moe_w3, moe_w2, seq, tm):
    lane_freq, lane_sign = _rope_lane_tables()
    cb, sb = _cos_sin(jnp.arange(seq // tm, dtype=jnp.int32) * tm, lane_freq)
    co, so = _cos_sin(jnp.arange(tm, dtype=jnp.int32), lane_freq)
    c1, s1 = _cos_sin(PAST_LEN + jnp.arange(1, dtype=jnp.int32), lane_freq)
    row3 = lambda a: a.reshape(DEPTH, 1, a.shape[-1])
    col = np.arange(IN_WIDTH)
    col_scale = jnp.asarray(np.where((col >= OFF_Q) & (col < OFF_GATE), 1.0, 0.5), F32)
    return {
        "n1g": row3(norm1_g), "n2g": row3(norm2_g), "w_in": (w_in * col_scale).astype(BF),
        "gm_g": row3(gm_norm_g), "gm_ws": gm_ws,
        "gm_bias": jnp.repeat(jnp.swapaxes(gm_b, 1, 2), LANES, axis=2),
        "gm_w0": jnp.repeat(gm_ws[:, :, 0, 0], LANES, axis=1).reshape(DEPTH, 1, GM_WIDTH),
        "gm_b0": jnp.repeat(gm_b[:, :, 0], LANES, axis=1).reshape(DEPTH, 1, GM_WIDTH),
        "conv_dw": conv_dw, "conv_b": row3(conv_b), "conv_ln_g": row3(0.5 * conv_ln_g),
        "conv_ln_b": row3(0.5 * conv_ln_b),
        "qn": row3(jnp.tile(q_norm_g, (1, N_HEADS)) * (HEAD_DIM ** -0.5)),
        "kn": row3(jnp.tile(k_norm_g, (1, N_KV_HEADS))),
        "sinks": attn_sinks,
        "sink_lanes": jnp.broadcast_to(attn_sinks[:, :, None], (DEPTH, N_HEADS, LANES)),
        "rope_base": jnp.concatenate([cb, sb], axis=1)[:, None, :],
        "rope_off": jnp.concatenate([co, so], axis=1),
        "rope_sign": lane_sign, "rope_c1": c1, "rope_s1": s1 * lane_sign,
        "band_bias": _band_bias(), "blockdiag": _blockdiag(),
        "w_branch": (0.5 * w_branch).astype(BF), "w_out": w_out.astype(BF),
        "rw2": _router_hi_lo(router_w),
        "rb": router_b.astype(F32).reshape(N_EXPERTS, 1),
        "moe_w1": moe_w1.astype(BF), "moe_w3": moe_w3.astype(BF),
        "moe_w2": moe_w2.astype(BF).reshape(DEPTH, N_GROUPS, EXPERTS_PER_GROUP * EXPERT_FF, D_MODEL),
    }


def kernel(x_prompt, x_sample, cache_win_k, cache_win_v, state_conv, c_prompt, c_sample, norm1_g, norm2_g, w_ada, b_ada, w_in, gm_norm_g, gm_ws, gm_b, conv_dw, conv_b, conv_ln_g, conv_ln_b, q_norm_g, k_norm_g, attn_sinks, w_branch, w_out, router_w, router_b, moe_w1, moe_w3, moe_w2):
    nb, seq, _ = x_prompt.shape
    ns = x_sample.shape[0]
    tm = min(MIX_TILE, seq)
    tmoe = min(MOE_TILE, seq)
    p = _prepare(norm1_g, norm2_g, w_in, gm_norm_g, gm_ws, gm_b, conv_dw, conv_b, conv_ln_g,
                 conv_ln_b, q_norm_g, k_norm_g, attn_sinks, w_branch, w_out, router_w, router_b,
                 moe_w1, moe_w3, moe_w2, seq, tm)
    assert ns == PROMPT_MOD_ROW and nb <= ADA_ROWS - PROMPT_MOD_ROW
    c_all = jnp.concatenate(
        [c_sample, c_prompt, jnp.zeros((ADA_ROWS - nb - ns, D_MODEL), F32)], axis=0)
    mod = _ada_call(c_all, w_ada, b_ada)

    kt = jnp.transpose(cache_win_k, (0, 1, 3, 4, 2))
    vt = jnp.transpose(cache_win_v, (0, 1, 3, 4, 2))
    st = jnp.transpose(state_conv, (0, 2, 1, 3))

    xp = x_prompt
    xs = x_sample.reshape(ns, D_MODEL)
    kp_l, vp_l, cp_l, gs_l = [], [], [], []
    new_k = jnp.zeros(kt.shape, F32)
    new_v = jnp.zeros(vt.shape, F32)
    new_st = jnp.zeros(st.shape, F32)
    for l in range(DEPTH):
        xp, kp, vp, cp = _mixer_call(l, xp, mod, p, tm)
        xp = _moe_call(l, xp.reshape(nb * seq, D_MODEL), mod, seq // tmoe, p, tmoe,
                       f"prompt_moe_l{l}").reshape(nb, seq, D_MODEL)
        kp_l.append(kp)
        vp_l.append(vp)
        cp_l.append(cp)

        z = _sample_proj_call(l, xs, mod, p)
        xs, new_k, new_v, new_st, gv = _sample_mixer_call(l, z, xs, mod, kt, vt, st,
                                                          new_k, new_v, new_st, p, SAMPLE_BLOCK)
        xs = _moe_call(l, xs, mod, None, p, ns, f"sample_moe_l{l}")
        gs_l.append(gv)

    kv_shape = (DEPTH, nb, WINDOW, N_KV_HEADS, HEAD_DIM)
    return (xp, xs.reshape(ns, 1, D_MODEL),
            jnp.stack(kp_l).reshape(kv_shape), jnp.stack(vp_l).reshape(kv_shape),
            jnp.stack(cp_l),
            jnp.transpose(new_k, (0, 1, 4, 2, 3)),
            jnp.transpose(new_v, (0, 1, 4, 2, 3)),
            jnp.transpose(new_st, (0, 2, 1, 3)),
            jnp.stack(gs_l).reshape(DEPTH, ns, 1, GM_WIDTH))
```

```python
import functools
import math

import jax
import jax.numpy as jnp
import numpy as np
from jax import lax
from jax.experimental import pallas as pl
from jax.experimental.pallas import tpu as pltpu

F32 = jnp.float32
BF = jnp.bfloat16

D_MODEL = 1024
DEPTH = 2
PAST_LEN = 8192
CHUNK = 128
GM_GROUPS = 4
GM_WIDTH = 512
CONV_WIDTH = 512
CONV_K = 31
N_HEADS = 8
N_KV_HEADS = 2
Q_REP = N_HEADS // N_KV_HEADS
HEAD_DIM = 64
ATT_WIDTH = N_HEADS * HEAD_DIM
KV_WIDTH = N_KV_HEADS * HEAD_DIM
WINDOW = 128
ROPE_THETA = 500000.0
ROT_DIM = HEAD_DIM // 4
N_BRANCH = 3
N_EXPERTS = 16
EXPERTS_PER_GROUP = 4
N_GROUPS = N_EXPERTS // EXPERTS_PER_GROUP
EXPERT_FF = 256
EPS = 1e-6

OFF_GM_U = 0
OFF_GM_V = 512
OFF_CV_A = 1024
OFF_CV_G = 1536
OFF_Q = 2048
OFF_K = 2560
OFF_V = 2688
OFF_GATE = 2816
IN_WIDTH = OFF_GATE + N_BRANCH * D_MODEL

LANES = 128
SUBLANES = 8
MXU_COLS = 256
HALO = 32
MIX_TILE = 512
MOE_TILE = 512
MOE_CAP = 160
ADA_ROWS = 136
PROMPT_MOD_ROW = 128
ADA_BLOCK = 1536
PROJ_BLOCK = IN_WIDTH // 2
SAMPLE_BLOCK = 32
VMEM_LIMIT = 56 * 1024 * 1024
NEG_BIG = -1e30


def _dot(a, b):
    return jnp.dot(a, b, preferred_element_type=F32)


def _dot_nt(a, b):
    return lax.dot_general(a, b, (((1,), (1,)), ((), ())), preferred_element_type=F32)


def _sigmoid(x):
    return 0.5 * jnp.tanh(0.5 * x) + 0.5


def _silu(x):
    return x * _sigmoid(x)


GELU_C = 0.7978845608028654
GELU_K = 0.044715


def _gelu_half(xh):
    return xh + xh * jnp.tanh(xh * (2.0 * GELU_C + (8.0 * GELU_C * GELU_K) * (xh * xh)))


def _sigmoid_times(zh, dh):
    return jnp.tanh(zh) * dh + dh


def _modulate(x, g, shift, scale):
    ms = jnp.mean(x * x, axis=-1, keepdims=True)
    return (x * lax.rsqrt(ms + EPS)) * (g * (1.0 + scale)) + shift


def _head_norm(x, blockdiag, g):
    sq = (x * x).astype(BF)
    width = x.shape[-1]
    step = min(width, MXU_COLS)
    ssum = jnp.concatenate([_dot(sq[:, c:c + step], blockdiag[0:step, 0:step])
                            for c in range(0, width, step)], axis=-1)
    return x * lax.rsqrt(ssum * (1.0 / HEAD_DIM) + EPS) * g


def _rope(x, c, s):
    width = x.shape[-1]
    reps = width // LANES
    cc = jnp.concatenate([c] * reps, axis=-1)
    ss = jnp.concatenate([s] * reps, axis=-1)
    lane = lax.broadcasted_iota(jnp.int32, x.shape, x.ndim - 1) % HEAD_DIM
    partner = jnp.where(lane < ROT_DIM // 2,
                        pltpu.roll(x, width - ROT_DIM // 2, x.ndim - 1),
                        pltpu.roll(x, ROT_DIM // 2, x.ndim - 1))
    return x * cc + partner * ss


def _layer_norm_silu(y, gh, bh):
    mu = jnp.mean(y, axis=-1, keepdims=True)
    yc = y - mu
    var = jnp.mean(yc * yc, axis=-1, keepdims=True)
    h = yc * lax.rsqrt(var + EPS) * gh + bh
    return h * jnp.tanh(h) + h


def _low_half(shape):
    return lax.broadcasted_iota(jnp.int32, shape, len(shape) - 1) % LANES < HEAD_DIM


def _const_spec(shape, index_map):
    return pl.BlockSpec(shape, index_map, pipeline_mode=pl.Buffered(1))


def _ada_kernel(c_ref, w_ref, b_ref, o_ref):
    s = _silu(c_ref[...]).astype(BF)
    o_ref[...] = _dot(s, w_ref[...].astype(BF)) + b_ref[...]


def _ada_call(c_all, w_ada, b_ada):
    nb = (6 * D_MODEL) // ADA_BLOCK
    return pl.pallas_call(
        _ada_kernel,
        out_shape=jax.ShapeDtypeStruct((DEPTH, ADA_ROWS, 6 * D_MODEL), F32),
        grid=(DEPTH, nb),
        in_specs=[
            pl.BlockSpec((ADA_ROWS, D_MODEL), lambda l, j: (0, 0)),
            pl.BlockSpec((None, D_MODEL, ADA_BLOCK), lambda l, j: (l, 0, j)),
            pl.BlockSpec((None, 1, ADA_BLOCK), lambda l, j: (l, 0, j)),
        ],
        out_specs=pl.BlockSpec((None, ADA_ROWS, ADA_BLOCK), lambda l, j: (l, 0, j)),
        compiler_params=pltpu.CompilerParams(
            dimension_semantics=("arbitrary", "arbitrary")),
        name="ada_mod",
    )(c_all, w_ada, b_ada.reshape(DEPTH, 1, 6 * D_MODEL))


def _mixer_kernel(sinks_ref, x_ref, mod_ref, n1g_ref, win_ref, gmg_ref, gmws_ref, gmb_ref,
                  cdw_ref, cb_ref, clg_ref, clb_ref, qn_ref, kn_ref, ropeb_ref, ropeo_ref,
                  sign_ref, bias_ref, bd_ref, wb_ref, wout_ref,
                  x1_ref, kwin_ref, vwin_ref, cst_ref,
                  kd_s, vd_s, abuf, oatt_s, acc_s, *, layer, tm):
    t = pl.program_id(1)
    last = pl.num_programs(1) - 1
    nblk = tm // WINDOW

    @pl.when(t == 0)
    def _():
        kd_s[0:WINDOW, :] = jnp.zeros((WINDOW, 2 * LANES), BF)
        vd_s[0:WINDOW, :] = jnp.zeros((WINDOW, 2 * LANES), BF)
        abuf[0:HALO, :] = jnp.zeros((HALO, CONV_WIDTH), F32)
        abuf[HALO + tm:HALO + tm + SUBLANES, :] = jnp.zeros((SUBLANES, CONV_WIDTH), F32)

    x = x_ref[...]
    mod = mod_ref[pl.ds(pl.program_id(0), 1), :]
    sh1 = mod[:, 0:D_MODEL]
    sc1 = mod[:, D_MODEL:2 * D_MODEL]
    gt1 = mod[:, 2 * D_MODEL:3 * D_MODEL]
    hb = _modulate(x, n1g_ref[...], sh1, sc1).astype(BF)

    def proj(off, width):
        return _dot(hb, win_ref[:, off:off + width])

    z_cv = proj(OFF_CV_A, 2 * CONV_WIDTH)
    a = _sigmoid_times(z_cv[:, CONV_WIDTH:2 * CONV_WIDTH], z_cv[:, 0:CONV_WIDTH])
    abuf[HALO:HALO + tm, :] = a
    first_off = HALO - (CONV_K - 1)

    def conv_tile(ci):
        lanes = slice(ci * LANES, (ci + 1) * LANES)
        yt = cb_ref[:, lanes]
        for b in range(SUBLANES):
            part = None
            for off in range(b, HALO + 1, SUBLANES):
                if off < first_off:
                    continue
                term = (cdw_ref[off - first_off:off - first_off + 1, lanes]
                        * abuf[pl.ds(off - b, tm + SUBLANES), lanes])
                part = term if part is None else part + term
            yt = yt + part[b:b + tm, :]
        return yt

    z_gm = proj(OFF_GM_U, 2 * GM_WIDTH)
    z_att = proj(OFF_Q, ATT_WIDTH + 2 * KV_WIDTH)

    def gated_branch(i, o_branch, first):
        for n in range(D_MODEL // MXU_COLS):
            cols = slice(n * MXU_COLS, (n + 1) * MXU_COLS)
            g0 = OFF_GATE + i * D_MODEL + n * MXU_COLS
            piece = _sigmoid_times(_dot(hb, win_ref[:, g0:g0 + MXU_COLS]),
                                   _dot(o_branch, wb_ref[i, :, cols]))
            if first:
                acc_s[:, cols] = piece
            else:
                acc_s[:, cols] += piece

    y = jnp.concatenate([conv_tile(ci) for ci in range(CONV_WIDTH // LANES)], axis=1)
    abuf[0:HALO, :] = abuf[tm:tm + HALO, :]
    o_cv = _layer_norm_silu(y, clg_ref[...], clb_ref[...]).astype(BF)
    gated_branch(1, o_cv, True)

    u = _gelu_half(z_gm[:, 0:GM_WIDTH])
    gv = _gelu_half(z_gm[:, GM_WIDTH:2 * GM_WIDTH])
    v = gv * lax.rsqrt(jnp.mean(gv * gv, axis=-1, keepdims=True) + EPS) * gmg_ref[...]
    vb = v.astype(BF)
    row = lax.broadcasted_iota(jnp.int32, (CHUNK, CHUNK), 0)
    col = lax.broadcasted_iota(jnp.int32, (CHUNK, CHUNK), 1)
    ws = [jnp.where(row >= col, gmws_ref[g], 0.0).astype(BF) for g in range(GM_GROUPS)]
    gmb = gmb_ref[...]
    chunks = []
    for c in range(nblk):
        parts = [_dot(ws[g], vb[c * CHUNK:(c + 1) * CHUNK, g * LANES:(g + 1) * LANES])
                 for g in range(GM_GROUPS)]
        chunks.append(jnp.concatenate(parts, axis=1) + gmb)
    o_gm = (u * jnp.concatenate(chunks, axis=0)).astype(BF)
    gated_branch(0, o_gm, False)

    rbase = ropeb_ref[...]
    roff = ropeo_ref[...]
    cb_, sb_ = rbase[:, 0:LANES], rbase[:, LANES:2 * LANES]
    co_, so_ = roff[:, 0:LANES], roff[:, LANES:2 * LANES]
    rc = cb_ * co_ - sb_ * so_
    rs = (sb_ * co_ + cb_ * so_) * sign_ref[...]
    bd = bd_ref[...]
    qb = _rope(_head_norm(z_att[:, 0:ATT_WIDTH], bd, qn_ref[...]), rc, rs).astype(BF)
    k_nat = _rope(_head_norm(z_att[:, ATT_WIDTH:ATT_WIDTH + KV_WIDTH], bd[0:LANES, 0:LANES],
                             kn_ref[...]), rc, rs)
    v_nat = z_att[:, ATT_WIDTH + KV_WIDTH:ATT_WIDTH + 2 * KV_WIDTH]
    lo_kv = _low_half((tm, LANES))

    def doubled(x_nat):
        swapped = pltpu.roll(x_nat, HEAD_DIM, 1)
        return jnp.concatenate([jnp.where(lo_kv, x_nat, swapped), jnp.where(lo_kv, swapped, x_nat)],
                               axis=1)

    kd = doubled(k_nat)
    vd = doubled(v_nat)
    kd_s[WINDOW:WINDOW + tm, :] = kd.astype(BF)
    vd_s[WINDOW:WINDOW + tm, :] = vd.astype(BF)

    lo_q = _low_half((WINDOW, LANES))
    zero_q = jnp.zeros((WINDOW, LANES), BF)
    band = bias_ref[1]
    for bi in range(nblk):
        bias = jnp.where(t == 0, bias_ref[0], band) if bi == 0 else band
        bias4 = jnp.concatenate([bias] * Q_REP, axis=0)
        for g in range(N_KV_HEADS):
            tiles = [qb[bi * WINDOW:(bi + 1) * WINDOW, (2 * g + j) * LANES:(2 * g + j + 1) * LANES]
                     for j in range(2)]
            qs = jnp.concatenate([jnp.where(lo_q, tiles[0], zero_q), jnp.where(lo_q, zero_q, tiles[0]),
                                  jnp.where(lo_q, tiles[1], zero_q), jnp.where(lo_q, zero_q, tiles[1])],
                                 axis=0)
            keys = kd_s[bi * WINDOW:(bi + 2) * WINDOW, g * LANES:(g + 1) * LANES]
            vals = vd_s[bi * WINDOW:(bi + 2) * WINDOW, g * LANES:(g + 1) * LANES]
            s = _dot_nt(qs, keys) + bias4
            outs = []
            for hh in range(Q_REP):
                sink = sinks_ref[layer, Q_REP * g + hh]
                sh = s[hh * WINDOW:(hh + 1) * WINDOW, :]
                m = jnp.maximum(jnp.max(sh, axis=-1, keepdims=True), sink)
                p = jnp.exp(sh - m)
                den = jnp.sum(p, axis=-1, keepdims=True) + jnp.exp(sink - m)
                outs.append(_dot(p.astype(BF), vals) / den)
            for j in range(2):
                oatt_s[bi * WINDOW:(bi + 1) * WINDOW, (2 * g + j) * LANES:(2 * g + j + 1) * LANES] = (
                    jnp.where(lo_q, outs[2 * j], outs[2 * j + 1]).astype(BF))
    kd_s[0:WINDOW, :] = kd_s[tm:tm + WINDOW, :]
    vd_s[0:WINDOW, :] = vd_s[tm:tm + WINDOW, :]
    gated_branch(2, oatt_s[...], False)

    accb = acc_s[...].astype(BF)
    for n in range(D_MODEL // MXU_COLS):
        cols = slice(n * MXU_COLS, (n + 1) * MXU_COLS)
        x1_ref[:, cols] = x_ref[:, cols] + gt1[:, cols] * _dot(accb, wout_ref[:, cols])

    @pl.when(t == last)
    def _():
        kwin_ref[...] = k_nat[tm - WINDOW:tm, :]
        vwin_ref[...] = v_nat[tm - WINDOW:tm, :]
        cst_ref[...] = a[tm - (CONV_K - 1):tm, :]


def _mixer_call(layer, x, mod, p, tm):
    nb, seq, _ = x.shape
    nt = seq // tm
    lsel3 = lambda b, t: (layer, 0, 0)
    kernel = functools.partial(_mixer_kernel, layer=layer, tm=tm)
    return pl.pallas_call(
        kernel,
        out_shape=(
            jax.ShapeDtypeStruct((nb, seq, D_MODEL), F32),
            jax.ShapeDtypeStruct((nb, WINDOW, KV_WIDTH), F32),
            jax.ShapeDtypeStruct((nb, WINDOW, KV_WIDTH), F32),
            jax.ShapeDtypeStruct((nb, CONV_K - 1, CONV_WIDTH), F32),
        ),
        grid=(nb, nt),
        in_specs=[
            pl.BlockSpec(memory_space=pltpu.SMEM),
            pl.BlockSpec((None, tm, D_MODEL), lambda b, t: (b, t, 0)),
            pl.BlockSpec((None, SUBLANES, 3 * D_MODEL),
                         lambda b, t: (layer, PROMPT_MOD_ROW // SUBLANES, 0)),
            _const_spec((None, 1, D_MODEL), lsel3),
            _const_spec((None, D_MODEL, IN_WIDTH), lsel3),
            _const_spec((None, 1, GM_WIDTH), lsel3),
            _const_spec((None, GM_GROUPS, CHUNK, CHUNK), lambda b, t: (layer, 0, 0, 0)),
            _const_spec((None, CHUNK, GM_WIDTH), lsel3),
            _const_spec((None, CONV_K, CONV_WIDTH), lsel3),
            _const_spec((None, 1, CONV_WIDTH), lsel3),
            _const_spec((None, 1, CONV_WIDTH), lsel3),
            _const_spec((None, 1, CONV_WIDTH), lsel3),
            _const_spec((None, 1, ATT_WIDTH), lsel3),
            _const_spec((None, 1, LANES), lsel3),
            pl.BlockSpec((None, 1, 2 * LANES), lambda b, t: (t, 0, 0)),
            _const_spec((tm, 2 * LANES), lambda b, t: (0, 0)),
            _const_spec((1, LANES), lambda b, t: (0, 0)),
            _const_spec((2, WINDOW, 2 * WINDOW), lambda b, t: (0, 0, 0)),
            _const_spec((MXU_COLS, MXU_COLS), lambda b, t: (0, 0)),
            _const_spec((None, N_BRANCH, GM_WIDTH, D_MODEL), lambda b, t: (layer, 0, 0, 0)),
            _const_spec((None, D_MODEL, D_MODEL), lsel3),
        ],
        out_specs=(
            pl.BlockSpec((None, tm, D_MODEL), lambda b, t: (b, t, 0)),
            pl.BlockSpec((None, WINDOW, KV_WIDTH), lambda b, t: (b, 0, 0)),
            pl.BlockSpec((None, WINDOW, KV_WIDTH), lambda b, t: (b, 0, 0)),
            pl.BlockSpec((None, CONV_K - 1, CONV_WIDTH), lambda b, t: (b, 0, 0)),
        ),
        scratch_shapes=[
            pltpu.VMEM((WINDOW + tm, 2 * LANES), BF),
            pltpu.VMEM((WINDOW + tm, 2 * LANES), BF),
            pltpu.VMEM((HALO + tm + SUBLANES, CONV_WIDTH), F32),
            pltpu.VMEM((tm, ATT_WIDTH), BF),
            pltpu.VMEM((tm, D_MODEL), F32),
        ],
        compiler_params=pltpu.CompilerParams(
            dimension_semantics=("arbitrary", "arbitrary"),
            vmem_limit_bytes=VMEM_LIMIT),
        name=f"prompt_mixers_l{layer}",
    )(p["sinks"], x, mod, p["n1g"], p["w_in"], p["gm_g"], p["gm_ws"], p["gm_bias"],
      p["conv_dw"], p["conv_b"], p["conv_ln_g"], p["conv_ln_b"], p["qn"], p["kn"],
      p["rope_base"], p["rope_off"], p["rope_sign"], p["band_bias"], p["blockdiag"],
      p["w_branch"], p["w_out"])


def _route_rows(logits_t, rb_ref):
    scores = _sigmoid(logits_t)
    biased = scores + rb_ref[...]
    rows = lambda a, gi: [a[EXPERTS_PER_GROUP * gi + k:EXPERTS_PER_GROUP * gi + k + 1, :]
                          for k in range(EXPERTS_PER_GROUP)]
    best = None
    idx = None
    for gi in range(N_GROUPS):
        b = rows(biased, gi)
        hi1, lo1 = jnp.maximum(b[0], b[1]), jnp.minimum(b[0], b[1])
        hi2, lo2 = jnp.maximum(b[2], b[3]), jnp.minimum(b[2], b[3])
        gs = jnp.maximum(hi1, hi2) + jnp.maximum(jnp.minimum(hi1, hi2), jnp.maximum(lo1, lo2))
        if gi == 0:
            best, idx = gs, jnp.zeros(gs.shape, jnp.int32)
        else:
            better = gs > best
            idx = jnp.where(better, gi, idx)
            best = jnp.where(better, gs, best)
    bsel = rows(biased, 0)
    ssel = rows(scores, 0)
    for gi in range(1, N_GROUPS):
        bg, sg = rows(biased, gi), rows(scores, gi)
        pick = idx == gi
        bsel = [jnp.where(pick, bg[k], bsel[k]) for k in range(EXPERTS_PER_GROUP)]
        ssel = [jnp.where(pick, sg[k], ssel[k]) for k in range(EXPERTS_PER_GROUP)]
    chosen = []
    for k in range(EXPERTS_PER_GROUP):
        rank = jnp.zeros(idx.shape, jnp.int32)
        for k2 in range(EXPERTS_PER_GROUP):
            if k2 == k:
                continue
            beats = (bsel[k2] > bsel[k]) | ((bsel[k2] == bsel[k]) & (k2 < k))
            rank = rank + beats.astype(jnp.int32)
        chosen.append(jnp.where(rank < 2, ssel[k], 0.0))
    den = chosen[0] + chosen[1] + chosen[2] + chosen[3]
    return idx, [c / den for c in chosen]


def _moe_route(x, mod, n2g_ref, rw2_ref, rb_ref, upper_ref, h_buf, rt_buf, c_buf, cnt_s, buf, tm):
    h2 = _modulate(x, n2g_ref[...], mod[:, 0:D_MODEL], mod[:, D_MODEL:2 * D_MODEL])
    hi = h2.astype(BF)
    lo = (h2 - hi.astype(F32)).astype(BF)
    rw2 = rw2_ref[...]
    lt = _dot_nt(rw2, hi) + _dot_nt(rw2, lo)
    idx, comb = _route_rows(lt[0:N_EXPERTS, :] + lt[N_EXPERTS:2 * N_EXPERTS, :], rb_ref)

    onehot = [(idx == g).astype(F32) for g in range(N_GROUPS)]
    oh8 = jnp.concatenate(onehot + [jnp.zeros((SUBLANES - N_GROUPS, tm), F32)], axis=0)
    prefix = _dot(oh8.astype(BF), upper_ref[...])
    slot = onehot[0] * prefix[0:1, :]
    for g in range(1, N_GROUPS):
        slot = slot + onehot[g] * prefix[g:g + 1, :]
    rt_buf[0:1, :] = idx
    rt_buf[1:2, :] = slot.astype(jnp.int32)
    for g in range(N_GROUPS):
        cnt_s[buf, g] = jnp.sum(onehot[g]).astype(jnp.int32)

    for k in range(EXPERTS_PER_GROUP):
        c_buf[k:k + 1, :] = comb[k]
    h_buf[...] = hi


def _moe_experts(x_ref, gt2, w1_ref, w3_ref, w2_ref, o_ref, h_buf, rt_buf, c_buf, cnt_s, buf,
                 tm, cap, overflow):
    def dispatch(g, j):
        want = lax.broadcasted_iota(jnp.int32, (cap, tm), 0) + j * cap
        hit = (rt_buf[1:2, :] == want) & (rt_buf[0:1, :] == g)
        return hit, jnp.where(hit, 1.0, 0.0).astype(BF)

    def run_experts(g, hit, hg):
        parts = []
        for e in range(EXPERTS_PER_GROUP):
            a = _dot(hg, w1_ref[EXPERTS_PER_GROUP * g + e])
            b = _dot(hg, w3_ref[EXPERTS_PER_GROUP * g + e])
            ce = jnp.sum(jnp.where(hit, c_buf[e:e + 1, :], 0.0), axis=-1, keepdims=True)
            parts.append(_silu(a) * b * ce)
        act = jnp.concatenate(parts, axis=1).astype(BF)
        return _dot(act, w2_ref[g]).astype(BF)

    def scatter(pmat, y):
        return lax.dot_general(pmat, y, (((0,), (0,)), ((), ())), preferred_element_type=F32)

    def gather(pmat):
        return _dot(pmat, h_buf[...]).astype(BF)

    if not overflow:
        sel = [dispatch(g, 0) for g in range(N_GROUPS)]
        pmat_all = jnp.concatenate([s[1] for s in sel], axis=0)
        hg_all = gather(pmat_all)
        ys = [run_experts(g, sel[g][0], hg_all[g * cap:(g + 1) * cap, :]) for g in range(N_GROUPS)]
        y_tile = scatter(pmat_all, jnp.concatenate(ys, axis=0))
        o_ref[...] = x_ref[...] + gt2 * y_tile
        return

    def group_body(g, carry):
        nblk = (cnt_s[buf, g] + (cap - 1)) // cap

        def block_body(j, carry2):
            hit, pmat = dispatch(g, j)
            o_ref[...] += gt2 * scatter(pmat, run_experts(g, hit, gather(pmat)))
            return carry2

        lax.fori_loop(1, nblk, block_body, 0)
        return carry

    lax.fori_loop(0, N_GROUPS, group_body, 0)


def _moe_kernel(x_ref, xn_ref, mod_ref, n2g_ref, rw2_ref, rb_ref, upper_ref,
                w1_ref, w3_ref, w2_ref, o_ref, h0_s, h1_s, rt0_s, rt1_s, c0_s, c1_s, cnt_s,
                *, tm, cap, tiles_per_row, n_tiles):
    route = functools.partial(_moe_route, n2g_ref=n2g_ref, rw2_ref=rw2_ref, rb_ref=rb_ref,
                              upper_ref=upper_ref, cnt_s=cnt_s, tm=tm)
    experts = functools.partial(_moe_experts, x_ref, w1_ref=w1_ref,
                                w3_ref=w3_ref, w2_ref=w2_ref, o_ref=o_ref, cnt_s=cnt_s, tm=tm, cap=cap)
    bufs = [dict(h_buf=h0_s, rt_buf=rt0_s, c_buf=c0_s, buf=0),
            dict(h_buf=h1_s, rt_buf=rt1_s, c_buf=c1_s, buf=1)]
    if tiles_per_row is None:
        mod = mod_ref[...]
        route(x_ref[...], mod, **bufs[0])
        for overflow in (False, True):
            experts(mod[:, 2 * D_MODEL:3 * D_MODEL], overflow=overflow, **bufs[0])
        return

    i = pl.program_id(0)
    mod_row = lambda tile: mod_ref[pl.ds(tile // tiles_per_row, 1), :]

    @pl.when(i == 0)
    def _():
        route(x_ref[...], mod_row(i), **bufs[0])

    def step(cur):
        gt2 = mod_row(i)[:, 2 * D_MODEL:3 * D_MODEL]
        experts(gt2, overflow=False, **bufs[cur])
        route(xn_ref[...], mod_row(jnp.minimum(i + 1, n_tiles - 1)), **bufs[1 - cur])
        experts(gt2, overflow=True, **bufs[cur])

    @pl.when(i % 2 == 0)
    def _():
        step(0)

    @pl.when(i % 2 == 1)
    def _():
        step(1)


def _moe_call(layer, x2d, mod, tiles_per_row, p, tm, name):
    n = x2d.shape[0]
    n_tiles = n // tm
    if tiles_per_row is None:
        assert n_tiles == 1
        mod_spec = pl.BlockSpec((None, n, 3 * D_MODEL), lambda i: (layer, 0, 1))
    else:
        mod_spec = pl.BlockSpec((None, SUBLANES, 3 * D_MODEL),
                                lambda i: (layer, PROMPT_MOD_ROW // SUBLANES, 1))
    lsel3 = lambda i: (layer, 0, 0)
    lsel4 = lambda i: (layer, 0, 0, 0)
    upper = np.triu(np.ones((tm, tm), np.float32), k=1)
    kernel = functools.partial(_moe_kernel, tm=tm, cap=MOE_CAP, tiles_per_row=tiles_per_row,
                               n_tiles=n_tiles)
    return pl.pallas_call(
        kernel,
        out_shape=jax.ShapeDtypeStruct((n, D_MODEL), F32),
        grid=(n_tiles,),
        in_specs=[
            pl.BlockSpec((tm, D_MODEL), lambda i: (i, 0)),
            pl.BlockSpec((tm, D_MODEL), lambda i: (jnp.minimum(i + 1, n_tiles - 1), 0)),
            mod_spec,
            _const_spec((None, 1, D_MODEL), lsel3),
            _const_spec((2 * N_EXPERTS, D_MODEL), lambda i: (0, 0)),
            _const_spec((N_EXPERTS, 1), lambda i: (0, 0)),
            _const_spec((tm, tm), lambda i: (0, 0)),
            _const_spec((None, N_EXPERTS, D_MODEL, EXPERT_FF), lsel4),
            _const_spec((None, N_EXPERTS, D_MODEL, EXPERT_FF), lsel4),
            _const_spec((None, N_GROUPS, EXPERTS_PER_GROUP * EXPERT_FF, D_MODEL), lsel4),
        ],
        out_specs=pl.BlockSpec((tm, D_MODEL), lambda i: (i, 0)),
        scratch_shapes=[
            pltpu.VMEM((tm, D_MODEL), BF),
            pltpu.VMEM((tm, D_MODEL), BF),
            pltpu.VMEM((SUBLANES, tm), jnp.int32),
            pltpu.VMEM((SUBLANES, tm), jnp.int32),
            pltpu.VMEM((SUBLANES, tm), F32),
            pltpu.VMEM((SUBLANES, tm), F32),
            pltpu.SMEM((2, N_GROUPS), jnp.int32),
        ],
        compiler_params=pltpu.CompilerParams(
            dimension_semantics=("arbitrary",),
            vmem_limit_bytes=VMEM_LIMIT),
        name=name,
    )(x2d, x2d, mod, p["n2g"], p["rw2"], p["rb"], jnp.asarray(upper, dtype=BF),
      p["moe_w1"], p["moe_w3"], p["moe_w2"])


def _sample_proj_kernel(x_ref, mod_ref, n1g_ref, w_ref, z_ref):
    mod = mod_ref[...]
    h = _modulate(x_ref[...], n1g_ref[...], mod[:, 0:D_MODEL], mod[:, D_MODEL:2 * D_MODEL])
    z_ref[...] = _dot(h.astype(BF), w_ref[...])


def _sample_proj_call(layer, xs, mod, p):
    n = xs.shape[0]
    return pl.pallas_call(
        _sample_proj_kernel,
        out_shape=jax.ShapeDtypeStruct((n, IN_WIDTH), F32),
        grid=(IN_WIDTH // PROJ_BLOCK,),
        in_specs=[
            pl.BlockSpec((n, D_MODEL), lambda j: (0, 0)),
            pl.BlockSpec((None, n, 3 * D_MODEL), lambda j: (layer, 0, 0)),
            pl.BlockSpec((None, 1, D_MODEL), lambda j: (layer, 0, 0)),
            pl.BlockSpec((None, D_MODEL, PROJ_BLOCK), lambda j: (layer, 0, j)),
        ],
        out_specs=pl.BlockSpec((n, PROJ_BLOCK), lambda j: (0, j)),
        compiler_params=pltpu.CompilerParams(dimension_semantics=("arbitrary",)),
        name=f"sample_proj_l{layer}",
    )(xs, mod, p["n1g"], p["w_in"])


def _shift_in_column(cache_t, new_rows, bb):
    flat = cache_t.reshape(bb * HEAD_DIM, WINDOW)
    shifted = pltpu.roll(flat, WINDOW - 1, 1).reshape(bb, HEAD_DIM, WINDOW)
    padded = jnp.concatenate([new_rows, jnp.zeros((LANES - bb, LANES), F32)], axis=0)
    new_t = padded.T
    is_last = lax.broadcasted_iota(jnp.int32, (HEAD_DIM, WINDOW), 1) == WINDOW - 1
    out = []
    for b in range(bb):
        col = new_t[0:HEAD_DIM, b:b + 1]
        out.append(jnp.where(is_last, col, shifted[b]))
    return out


def _sample_mixer_kernel(z_ref, x_ref, mod_ref, kt_ref, vt_ref, st_ref,
                         gmg_ref, gmw_ref, gmb_ref, cdw_ref, cb_ref, clg_ref, clb_ref,
                         qn_ref, kn_ref, ropec_ref, ropes_ref, sink_ref, bd_ref, wb_ref, wout_ref,
                         _k_all, _v_all, _st_all,
                         x1_ref, kto_ref, vto_ref, sto_ref, gv_ref,
                         qf_s, of_s, oc_s, zg_s, *, bb):
    i = pl.program_id(0)
    r0 = pl.multiple_of(i * bb, bb)
    z = z_ref[...]
    zg_s[pl.ds(r0, bb), :] = z[:, OFF_GATE:OFF_GATE + N_BRANCH * D_MODEL]

    u = _gelu_half(z[:, OFF_GM_U:OFF_GM_U + GM_WIDTH])
    gv = _gelu_half(z[:, OFF_GM_V:OFF_GM_V + GM_WIDTH])
    v = gv * lax.rsqrt(jnp.mean(gv * gv, axis=-1, keepdims=True) + EPS) * gmg_ref[...]
    gv_ref[...] = v
    o_gm = u * (v * gmw_ref[...] + gmb_ref[...])

    a = _sigmoid_times(z[:, OFF_CV_G:OFF_CV_G + CONV_WIDTH], z[:, OFF_CV_A:OFF_CV_A + CONV_WIDTH])
    y = cb_ref[...] + cdw_ref[CONV_K - 1:CONV_K, :] * a
    for j in range(CONV_K - 1):
        y = y + cdw_ref[j:j + 1, :] * st_ref[j]
    o_cv = _layer_norm_silu(y, clg_ref[...], clb_ref[...])
    sto_ref[0:CONV_K - 2] = st_ref[1:CONV_K - 1]
    sto_ref[CONV_K - 2] = a

    rc = ropec_ref[...]
    rs = ropes_ref[...]
    bd = bd_ref[...]
    q = _rope(_head_norm(z[:, OFF_Q:OFF_Q + ATT_WIDTH], bd, qn_ref[...]), rc, rs)
    knew = _rope(_head_norm(z[:, OFF_K:OFF_K + KV_WIDTH], bd[0:LANES, 0:LANES], kn_ref[...]), rc, rs)
    vnew = z[:, OFF_V:OFF_V + KV_WIDTH]
    lo = _low_half((bb, LANES))
    for h in range(N_HEADS):
        tile = q[:, (h // 2) * LANES:(h // 2 + 1) * LANES]
        if h % 2 == 1:
            tile = pltpu.roll(tile, HEAD_DIM, 1)
        qf_s[:, h, :] = jnp.where(lo, tile, 0.0)
    k_low = [knew, pltpu.roll(knew, HEAD_DIM, 1)]
    v_low = [vnew, pltpu.roll(vnew, HEAD_DIM, 1)]
    for g in range(N_KV_HEADS):
        qg = qf_s[:, Q_REP * g:Q_REP * (g + 1), :]
        kt = kt_ref[:, g]
        vt = vt_ref[:, g]
        s = jnp.einsum("brd,bdp->brp", qg[:, :, 0:HEAD_DIM].astype(BF), kt.astype(BF),
                       preferred_element_type=F32)
        s_new = jnp.sum(qg * k_low[g][:, None, :], axis=-1, keepdims=True)
        sink = sink_ref[Q_REP * g:Q_REP * (g + 1), 0:1][None]
        m = jnp.maximum(jnp.maximum(jnp.max(s, axis=-1, keepdims=True), s_new), sink)
        pr = jnp.exp(s - m)
        p_new = jnp.exp(s_new - m)
        den = jnp.sum(pr, axis=-1, keepdims=True) + p_new + jnp.exp(sink - m)
        o = jnp.einsum("brp,bdp->brd", pr.astype(BF), vt.astype(BF), preferred_element_type=F32)
        o = (o + p_new * v_low[g][:, None, 0:HEAD_DIM]) / den
        of_s[:, Q_REP * g:Q_REP * (g + 1), :] = jnp.concatenate([o, jnp.zeros(o.shape, F32)], axis=-1)
        for b, tile in enumerate(_shift_in_column(kt, k_low[g], bb)):
            kto_ref[b, g] = tile
        for b, tile in enumerate(_shift_in_column(vt, v_low[g], bb)):
            vto_ref[b, g] = tile
    att_tiles = []
    for j in range(N_HEADS // 2):
        second = pltpu.roll(of_s[:, 2 * j + 1, :], HEAD_DIM, 1)
        att_tiles.append(jnp.where(lo, of_s[:, 2 * j, :], second))
    o_att = jnp.concatenate(att_tiles, axis=1)

    oc_s[pl.ds(r0, bb), 0:GM_WIDTH] = o_gm.astype(BF)
    oc_s[pl.ds(r0, bb), GM_WIDTH:GM_WIDTH + CONV_WIDTH] = o_cv.astype(BF)
    oc_s[pl.ds(r0, bb), GM_WIDTH + CONV_WIDTH:GM_WIDTH + CONV_WIDTH + ATT_WIDTH] = o_att.astype(BF)

    @pl.when(i == pl.num_programs(0) - 1)
    def _():
        acc = _sigmoid_times(zg_s[:, 0:D_MODEL], _dot(oc_s[:, 0:GM_WIDTH], wb_ref[0]))
        acc = acc + _sigmoid_times(zg_s[:, D_MODEL:2 * D_MODEL], _dot(
            oc_s[:, GM_WIDTH:GM_WIDTH + CONV_WIDTH], wb_ref[1]))
        acc = acc + _sigmoid_times(zg_s[:, 2 * D_MODEL:3 * D_MODEL], _dot(
            oc_s[:, GM_WIDTH + CONV_WIDTH:GM_WIDTH + CONV_WIDTH + ATT_WIDTH], wb_ref[2]))
        gt1 = mod_ref[...][:, 2 * D_MODEL:3 * D_MODEL]
        x1_ref[...] = x_ref[...] + gt1 * _dot(acc.astype(BF), wout_ref[...])


def _sample_mixer_call(layer, z, xs, mod, kt, vt, st, new_k, new_v, new_st, p, bb):
    n = xs.shape[0]
    lsel3 = lambda i: (layer, 0, 0)
    cache_block = (bb, N_KV_HEADS, HEAD_DIM, WINDOW)
    kernel = functools.partial(_sample_mixer_kernel, bb=bb)
    args = (z, xs, mod, kt, vt, st, p["gm_g"], p["gm_w0"], p["gm_b0"], p["conv_dw"], p["conv_b"],
            p["conv_ln_g"], p["conv_ln_b"], p["qn"], p["kn"], p["rope_c1"], p["rope_s1"],
            p["sink_lanes"], p["blockdiag"], p["w_branch"], p["w_out"], new_k, new_v, new_st)
    n_in = len(args)
    return pl.pallas_call(
        kernel,
        out_shape=(
            jax.ShapeDtypeStruct((n, D_MODEL), F32),
            jax.ShapeDtypeStruct(new_k.shape, F32),
            jax.ShapeDtypeStruct(new_v.shape, F32),
            jax.ShapeDtypeStruct(new_st.shape, F32),
            jax.ShapeDtypeStruct((n, GM_WIDTH), F32),
        ),
        input_output_aliases={n_in - 3: 1, n_in - 2: 2, n_in - 1: 3},
        grid=(n // bb,),
        in_specs=[
            pl.BlockSpec((bb, IN_WIDTH), lambda i: (i, 0)),
            pl.BlockSpec((n, D_MODEL), lambda i: (0, 0)),
            pl.BlockSpec((None, n, 3 * D_MODEL), lsel3),
            pl.BlockSpec((None,) + cache_block, lambda i: (layer, i, 0, 0, 0)),
            pl.BlockSpec((None,) + cache_block, lambda i: (layer, i, 0, 0, 0)),
            pl.BlockSpec((None, CONV_K - 1, bb, CONV_WIDTH), lambda i: (layer, 0, i, 0)),
            pl.BlockSpec((None, 1, GM_WIDTH), lsel3),
            pl.BlockSpec((None, 1, GM_WIDTH), lsel3),
            pl.BlockSpec((None, 1, GM_WIDTH), lsel3),
            pl.BlockSpec((None, CONV_K, CONV_WIDTH), lsel3),
            pl.BlockSpec((None, 1, CONV_WIDTH), lsel3),
            pl.BlockSpec((None, 1, CONV_WIDTH), lsel3),
            pl.BlockSpec((None, 1, CONV_WIDTH), lsel3),
            pl.BlockSpec((None, 1, ATT_WIDTH), lsel3),
            pl.BlockSpec((None, 1, LANES), lsel3),
            pl.BlockSpec((1, LANES), lambda i: (0, 0)),
            pl.BlockSpec((1, LANES), lambda i: (0, 0)),
            pl.BlockSpec((None, N_HEADS, LANES), lsel3),
            pl.BlockSpec((MXU_COLS, MXU_COLS), lambda i: (0, 0)),
            pl.BlockSpec((None, N_BRANCH, GM_WIDTH, D_MODEL), lambda i: (layer, 0, 0, 0)),
            pl.BlockSpec((None, D_MODEL, D_MODEL), lsel3),
            pl.BlockSpec(memory_space=pl.ANY),
            pl.BlockSpec(memory_space=pl.ANY),
            pl.BlockSpec(memory_space=pl.ANY),
        ],
        out_specs=(
            pl.BlockSpec((n, D_MODEL), lambda i: (0, 0)),
            pl.BlockSpec((None,) + cache_block, lambda i: (layer, i, 0, 0, 0)),
            pl.BlockSpec((None,) + cache_block, lambda i: (layer, i, 0, 0, 0)),
            pl.BlockSpec((None, CONV_K - 1, bb, CONV_WIDTH), lambda i: (layer, 0, i, 0)),
            pl.BlockSpec((bb, GM_WIDTH), lambda i: (i, 0)),
        ),
        scratch_shapes=[
            pltpu.VMEM((bb, N_HEADS, LANES), F32),
            pltpu.VMEM((bb, N_HEADS, LANES), F32),
            pltpu.VMEM((n, GM_WIDTH + CONV_WIDTH + ATT_WIDTH), BF),
            pltpu.VMEM((n, N_BRANCH * D_MODEL), F32),
        ],
        compiler_params=pltpu.CompilerParams(
            dimension_semantics=("arbitrary",), vmem_limit_bytes=VMEM_LIMIT),
        name=f"sample_mixers_l{layer}",
    )(*args)


def _rope_lane_tables():
    half = ROT_DIM // 2
    freqs = jnp.exp(-math.log(ROPE_THETA) * jnp.arange(half, dtype=F32) * (2.0 / ROT_DIM))
    rest = jnp.zeros((HEAD_DIM - ROT_DIM,), F32)
    freq64 = jnp.concatenate([freqs, freqs, rest])
    sign64 = jnp.concatenate([-jnp.ones((half,), F32), jnp.ones((half,), F32), rest])
    reps = LANES // HEAD_DIM
    return jnp.tile(freq64, reps)[None, :], jnp.tile(sign64, reps)[None, :]


def _cos_sin(pos, lane_freq):
    ang = pos.astype(F32)[:, None] * lane_freq
    return jnp.cos(ang), jnp.sin(ang)


def _band_bias():
    i = np.arange(WINDOW)[:, None]
    j = np.arange(2 * WINDOW)[None, :]
    band = (j >= i) & (j <= i + WINDOW)
    first = band & (j >= WINDOW)
    out = np.where(np.stack([first, band]), 0.0, NEG_BIG).astype(np.float32)
    return jnp.asarray(out)


def _blockdiag():
    idx = np.arange(MXU_COLS) // HEAD_DIM
    return jnp.asarray((idx[:, None] == idx[None, :]).astype(np.float32), dtype=BF)


def _router_hi_lo(router_w):
    rw = router_w.astype(F32).T
    rw_hi = rw.astype(BF)
    rw_lo = (rw - rw_hi.astype(F32)).astype(BF)
    return jnp.concatenate([rw_hi, rw_lo], axis=0)


def _prepare(norm1_g, norm2_g, w_in, gm_norm_g, gm_ws, gm_b, conv_dw, conv_b, conv_ln_g,
             conv_ln_b, q_norm_g, k_norm_g, attn_sinks, w_branch, w_out, router_w, router_b,
             moe_w1, moe_w3, moe_w2, seq, tm):
    lane_freq, lane_sign = _rope_lane_tables()
    cb, sb = _cos_sin(jnp.arange(seq // tm, dtype=jnp.int32) * tm, lane_freq)
    co, so = _cos_sin(jnp.arange(tm, dtype=jnp.int32), lane_freq)
    c1, s1 = _cos_sin(PAST_LEN + jnp.arange(1, dtype=jnp.int32), lane_freq)
    row3 = lambda a: a.reshape(DEPTH, 1, a.shape[-1])
    col = np.arange(IN_WIDTH)
    col_scale = jnp.asarray(np.where((col >= OFF_Q) & (col < OFF_GATE), 1.0, 0.5), F32)
    return {
        "n1g": row3(norm1_g), "n2g": row3(norm2_g), "w_in": (w_in * col_scale).astype(BF),
        "gm_g": row3(gm_norm_g), "gm_ws": gm_ws,
        "gm_bias": jnp.repeat(jnp.swapaxes(gm_b, 1, 2), LANES, axis=2),
        "gm_w0": jnp.repeat(gm_ws[:, :, 0, 0], LANES, axis=1).reshape(DEPTH, 1, GM_WIDTH),
        "gm_b0": jnp.repeat(gm_b[:, :, 0], LANES, axis=1).reshape(DEPTH, 1, GM_WIDTH),
        "conv_dw": conv_dw, "conv_b": row3(conv_b), "conv_ln_g": row3(0.5 * conv_ln_g),
        "conv_ln_b": row3(0.5 * conv_ln_b),
        "qn": row3(jnp.tile(q_norm_g, (1, N_HEADS)) * (HEAD_DIM ** -0.5)),
        "kn": row3(jnp.tile(k_norm_g, (1, N_KV_HEADS))),
        "sinks": attn_sinks,
        "sink_lanes": jnp.broadcast_to(attn_sinks[:, :, None], (DEPTH, N_HEADS, LANES)),
        "rope_base": jnp.concatenate([cb, sb], axis=1)[:, None, :],
        "rope_off": jnp.concatenate([co, so], axis=1),
        "rope_sign": lane_sign, "rope_c1": c1, "rope_s1": s1 * lane_sign,
        "band_bias": _band_bias(), "blockdiag": _blockdiag(),
        "w_branch": (0.5 * w_branch).astype(BF), "w_out": w_out.astype(BF),
        "rw2": _router_hi_lo(router_w),
        "rb": router_b.astype(F32).reshape(N_EXPERTS, 1),
        "moe_w1": moe_w1.astype(BF), "moe_w3": moe_w3.astype(BF),
        "moe_w2": moe_w2.astype(BF).reshape(DEPTH, N_GROUPS, EXPERTS_PER_GROUP * EXPERT_FF, D_MODEL),
    }


def kernel(x_prompt, x_sample, cache_win_k, cache_win_v, state_conv, c_prompt, c_sample, norm1_g, norm2_g, w_ada, b_ada, w_in, gm_norm_g, gm_ws, gm_b, conv_dw, conv_b, conv_ln_g, conv_ln_b, q_norm_g, k_norm_g, attn_sinks, w_branch, w_out, router_w, router_b, moe_w1, moe_w3, moe_w2):
    nb, seq, _ = x_prompt.shape
    ns = x_sample.shape[0]
    tm = min(MIX_TILE, seq)
    tmoe = min(MOE_TILE, seq)
    p = _prepare(norm1_g, norm2_g, w_in, gm_norm_g, gm_ws, gm_b, conv_dw, conv_b, conv_ln_g,
                 conv_ln_b, q_norm_g, k_norm_g, attn_sinks, w_branch, w_out, router_w, router_b,
                 moe_w1, moe_w3, moe_w2, seq, tm)
    assert ns == PROMPT_MOD_ROW and nb <= ADA_ROWS - PROMPT_MOD_ROW
    c_all = jnp.concatenate(
        [c_sample, c_prompt, jnp.zeros((ADA_ROWS - nb - ns, D_MODEL), F32)], axis=0)
    mod = _ada_call(c_all, w_ada, b_ada)

    kt = jnp.transpose(cache_win_k, (0, 1, 3, 4, 2))
    vt = jnp.transpose(cache_win_v, (0, 1, 3, 4, 2))
    st = jnp.transpose(state_conv, (0, 2, 1, 3))

    xp = x_prompt
    xs = x_sample.reshape(ns, D_MODEL)
    kp_l, vp_l, cp_l, gs_l = [], [], [], []
    new_k = jnp.zeros(kt.shape, F32)
    new_v = jnp.zeros(vt.shape, F32)
    new_st = jnp.zeros(st.shape, F32)
    for l in range(DEPTH):
        xp, kp, vp, cp = _mixer_call(l, xp, mod, p, tm)
        xp = _moe_call(l, xp.reshape(nb * seq, D_MODEL), mod, seq // tmoe, p, tmoe,
                       f"prompt_moe_l{l}").reshape(nb, seq, D_MODEL)
        kp_l.append(kp)
        vp_l.append(vp)
        cp_l.append(cp)

        z = _sample_proj_call(l, xs, mod, p)
        xs, new_k, new_v, new_st, gv = _sample_mixer_call(l, z, xs, mod, kt, vt, st,
                                                          new_k, new_v, new_st, p, SAMPLE_BLOCK)
        xs = _moe_call(l, xs, mod, None, p, ns, f"sample_moe_l{l}")
        gs_l.append(gv)

    kv_shape = (DEPTH, nb, WINDOW, N_KV_HEADS, HEAD_DIM)
    return (xp, xs.reshape(ns, 1, D_MODEL),
            jnp.stack(kp_l).reshape(kv_shape), jnp.stack(vp_l).reshape(kv_shape),
            jnp.stack(cp_l),
            jnp.transpose(new_k, (0, 1, 4, 2, 3)),
            jnp.transpose(new_v, (0, 1, 4, 2, 3)),
            jnp.transpose(new_st, (0, 2, 1, 3)),
            jnp.stack(gs_l).reshape(DEPTH, ns, 1, GM_WIDTH))
```

```python
import functools
import math

import jax
import jax.numpy as jnp
import numpy as np
from jax import lax
from jax.experimental import pallas as pl
from jax.experimental.pallas import tpu as pltpu

F32 = jnp.float32
BF = jnp.bfloat16

D_MODEL = 1024
DEPTH = 2
PAST_LEN = 8192
CHUNK = 128
GM_GROUPS = 4
GM_WIDTH = 512
CONV_WIDTH = 512
CONV_K = 31
N_HEADS = 8
N_KV_HEADS = 2
Q_REP = N_HEADS // N_KV_HEADS
HEAD_DIM = 64
ATT_WIDTH = N_HEADS * HEAD_DIM
KV_WIDTH = N_KV_HEADS * HEAD_DIM
WINDOW = 128
ROPE_THETA = 500000.0
ROT_DIM = HEAD_DIM // 4
N_BRANCH = 3
N_EXPERTS = 16
EXPERTS_PER_GROUP = 4
N_GROUPS = N_EXPERTS // EXPERTS_PER_GROUP
EXPERT_FF = 256
EPS = 1e-6

OFF_GM_U = 0
OFF_GM_V = 512
OFF_CV_A = 1024
OFF_CV_G = 1536
OFF_Q = 2048
OFF_K = 2560
OFF_V = 2688
OFF_GATE = 2816
IN_WIDTH = OFF_GATE + N_BRANCH * D_MODEL

LANES = 128
SUBLANES = 8
MXU_COLS = 256
HALO = 32
MIX_TILE = 512
MOE_TILE = 512
MOE_CAP = 208
ADA_ROWS = 136
PROMPT_MOD_ROW = 128
ADA_BLOCK = 1536
PROJ_BLOCK = IN_WIDTH // 2
SAMPLE_BLOCK = 32
VMEM_LIMIT = 56 * 1024 * 1024
NEG_BIG = -1e30


def _dot(a, b):
    return jnp.dot(a, b, preferred_element_type=F32)


def _dot_nt(a, b):
    return lax.dot_general(a, b, (((1,), (1,)), ((), ())), preferred_element_type=F32)


def _sigmoid(x):
    return 0.5 * jnp.tanh(0.5 * x) + 0.5


def _silu(x):
    return x * _sigmoid(x)


GELU_C = 0.7978845608028654
GELU_K = 0.044715


def _gelu_half(xh):
    return xh + xh * jnp.tanh(xh * (2.0 * GELU_C + (8.0 * GELU_C * GELU_K) * (xh * xh)))


def _sigmoid_times(zh, dh):
    return jnp.tanh(zh) * dh + dh


def _modulate(x, g, shift, scale):
    ms = jnp.mean(x * x, axis=-1, keepdims=True)
    return (x * lax.rsqrt(ms + EPS)) * (g * (1.0 + scale)) + shift


def _head_norm(x, blockdiag, g):
    sq = (x * x).astype(BF)
    width = x.shape[-1]
    step = min(width, MXU_COLS)
    ssum = jnp.concatenate([_dot(sq[:, c:c + step], blockdiag[0:step, 0:step])
                            for c in range(0, width, step)], axis=-1)
    return x * lax.rsqrt(ssum * (1.0 / HEAD_DIM) + EPS) * g


def _rope(x, c, s):
    width = x.shape[-1]
    reps = width // LANES
    cc = jnp.concatenate([c] * reps, axis=-1)
    ss = jnp.concatenate([s] * reps, axis=-1)
    lane = lax.broadcasted_iota(jnp.int32, x.shape, x.ndim - 1) % HEAD_DIM
    partner = jnp.where(lane < ROT_DIM // 2,
                        pltpu.roll(x, width - ROT_DIM // 2, x.ndim - 1),
                        pltpu.roll(x, ROT_DIM // 2, x.ndim - 1))
    return x * cc + partner * ss


def _layer_norm_silu(y, gh, bh):
    mu = jnp.mean(y, axis=-1, keepdims=True)
    yc = y - mu
    var = jnp.mean(yc * yc, axis=-1, keepdims=True)
    h = yc * lax.rsqrt(var + EPS) * gh + bh
    return h * jnp.tanh(h) + h


def _low_half(shape):
    return lax.broadcasted_iota(jnp.int32, shape, len(shape) - 1) % LANES < HEAD_DIM


def _const_spec(shape, index_map):
    return pl.BlockSpec(shape, index_map, pipeline_mode=pl.Buffered(1))


def _ada_kernel(c_ref, w_ref, b_ref, o_ref):
    s = _silu(c_ref[...]).astype(BF)
    o_ref[...] = _dot(s, w_ref[...].astype(BF)) + b_ref[...]


def _ada_call(c_all, w_ada, b_ada):
    nb = (6 * D_MODEL) // ADA_BLOCK
    return pl.pallas_call(
        _ada_kernel,
        out_shape=jax.ShapeDtypeStruct((DEPTH, ADA_ROWS, 6 * D_MODEL), F32),
        grid=(DEPTH, nb),
        in_specs=[
            pl.BlockSpec((ADA_ROWS, D_MODEL), lambda l, j: (0, 0)),
            pl.BlockSpec((None, D_MODEL, ADA_BLOCK), lambda l, j: (l, 0, j)),
            pl.BlockSpec((None, 1, ADA_BLOCK), lambda l, j: (l, 0, j)),
        ],
        out_specs=pl.BlockSpec((None, ADA_ROWS, ADA_BLOCK), lambda l, j: (l, 0, j)),
        compiler_params=pltpu.CompilerParams(
            dimension_semantics=("arbitrary", "arbitrary")),
        name="ada_mod",
    )(c_all, w_ada, b_ada.reshape(DEPTH, 1, 6 * D_MODEL))


def _mixer_kernel(sinks_ref, x_ref, mod_ref, n1g_ref, win_ref, gmg_ref, gmws_ref, gmb_ref,
                  cdw_ref, cb_ref, clg_ref, clb_ref, qn_ref, kn_ref, ropeb_ref, ropeo_ref,
                  sign_ref, bias_ref, bd_ref, wb_ref, wout_ref,
                  x1_ref, kwin_ref, vwin_ref, cst_ref,
                  kd_s, vd_s, abuf, oatt_s, acc_s, *, layer, tm):
    t = pl.program_id(1)
    last = pl.num_programs(1) - 1
    nblk = tm // WINDOW

    @pl.when(t == 0)
    def _():
        kd_s[0:WINDOW, :] = jnp.zeros((WINDOW, 2 * LANES), BF)
        vd_s[0:WINDOW, :] = jnp.zeros((WINDOW, 2 * LANES), BF)
        abuf[0:HALO, :] = jnp.zeros((HALO, CONV_WIDTH), F32)
        abuf[HALO + tm:HALO + tm + SUBLANES, :] = jnp.zeros((SUBLANES, CONV_WIDTH), F32)

    x = x_ref[...]
    mod = mod_ref[pl.ds(pl.program_id(0), 1), :]
    sh1 = mod[:, 0:D_MODEL]
    sc1 = mod[:, D_MODEL:2 * D_MODEL]
    gt1 = mod[:, 2 * D_MODEL:3 * D_MODEL]
    hb = _modulate(x, n1g_ref[...], sh1, sc1).astype(BF)

    def proj(off, width):
        return _dot(hb, win_ref[:, off:off + width])

    z_cv = proj(OFF_CV_A, 2 * CONV_WIDTH)
    a = _sigmoid_times(z_cv[:, CONV_WIDTH:2 * CONV_WIDTH], z_cv[:, 0:CONV_WIDTH])
    abuf[HALO:HALO + tm, :] = a
    first_off = HALO - (CONV_K - 1)

    def conv_tile(ci):
        lanes = slice(ci * LANES, (ci + 1) * LANES)
        yt = cb_ref[:, lanes]
        for b in range(SUBLANES):
            part = None
            for off in range(b, HALO + 1, SUBLANES):
                if off < first_off:
                    continue
                term = (cdw_ref[off - first_off:off - first_off + 1, lanes]
                        * abuf[pl.ds(off - b, tm + SUBLANES), lanes])
                part = term if part is None else part + term
            yt = yt + part[b:b + tm, :]
        return yt

    z_gm = proj(OFF_GM_U, 2 * GM_WIDTH)
    z_att = proj(OFF_Q, ATT_WIDTH + 2 * KV_WIDTH)

    def gated_branch(i, o_branch, first):
        for n in range(D_MODEL // MXU_COLS):
            cols = slice(n * MXU_COLS, (n + 1) * MXU_COLS)
            g0 = OFF_GATE + i * D_MODEL + n * MXU_COLS
            piece = _sigmoid_times(_dot(hb, win_ref[:, g0:g0 + MXU_COLS]),
                                   _dot(o_branch, wb_ref[i, :, cols]))
            if first:
                acc_s[:, cols] = piece
            else:
                acc_s[:, cols] += piece

    y = jnp.concatenate([conv_tile(ci) for ci in range(CONV_WIDTH // LANES)], axis=1)
    abuf[0:HALO, :] = abuf[tm:tm + HALO, :]
    o_cv = _layer_norm_silu(y, clg_ref[...], clb_ref[...]).astype(BF)
    gated_branch(1, o_cv, True)

    u = _gelu_half(z_gm[:, 0:GM_WIDTH])
    gv = _gelu_half(z_gm[:, GM_WIDTH:2 * GM_WIDTH])
    v = gv * lax.rsqrt(jnp.mean(gv * gv, axis=-1, keepdims=True) + EPS) * gmg_ref[...]
    vb = v.astype(BF)
    row = lax.broadcasted_iota(jnp.int32, (CHUNK, CHUNK), 0)
    col = lax.broadcasted_iota(jnp.int32, (CHUNK, CHUNK), 1)
    ws = [jnp.where(row >= col, gmws_ref[g], 0.0).astype(BF) for g in range(GM_GROUPS)]
    gmb = gmb_ref[...]
    chunks = []
    for c in range(nblk):
        parts = [_dot(ws[g], vb[c * CHUNK:(c + 1) * CHUNK, g * LANES:(g + 1) * LANES])
                 for g in range(GM_GROUPS)]
        chunks.append(jnp.concatenate(parts, axis=1) + gmb)
    o_gm = (u * jnp.concatenate(chunks, axis=0)).astype(BF)
    gated_branch(0, o_gm, False)

    rbase = ropeb_ref[...]
    roff = ropeo_ref[...]
    cb_, sb_ = rbase[:, 0:LANES], rbase[:, LANES:2 * LANES]
    co_, so_ = roff[:, 0:LANES], roff[:, LANES:2 * LANES]
    rc = cb_ * co_ - sb_ * so_
    rs = (sb_ * co_ + cb_ * so_) * sign_ref[...]
    bd = bd_ref[...]
    qb = _rope(_head_norm(z_att[:, 0:ATT_WIDTH], bd, qn_ref[...]), rc, rs).astype(BF)
    k_nat = _rope(_head_norm(z_att[:, ATT_WIDTH:ATT_WIDTH + KV_WIDTH], bd[0:LANES, 0:LANES],
                             kn_ref[...]), rc, rs)
    v_nat = z_att[:, ATT_WIDTH + KV_WIDTH:ATT_WIDTH + 2 * KV_WIDTH]
    lo_kv = _low_half((tm, LANES))

    def doubled(x_nat):
        swapped = pltpu.roll(x_nat, HEAD_DIM, 1)
        return jnp.concatenate([jnp.where(lo_kv, x_nat, swapped), jnp.where(lo_kv, swapped, x_nat)],
                               axis=1)

    kd = doubled(k_nat)
    vd = doubled(v_nat)
    kd_s[WINDOW:WINDOW + tm, :] = kd.astype(BF)
    vd_s[WINDOW:WINDOW + tm, :] = vd.astype(BF)

    lo_q = _low_half((WINDOW, LANES))
    zero_q = jnp.zeros((WINDOW, LANES), BF)
    band = bias_ref[1]
    for bi in range(nblk):
        bias = jnp.where(t == 0, bias_ref[0], band) if bi == 0 else band
        bias4 = jnp.concatenate([bias] * Q_REP, axis=0)
        for g in range(N_KV_HEADS):
            tiles = [qb[bi * WINDOW:(bi + 1) * WINDOW, (2 * g + j) * LANES:(2 * g + j + 1) * LANES]
                     for j in range(2)]
            qs = jnp.concatenate([jnp.where(lo_q, tiles[0], zero_q), jnp.where(lo_q, zero_q, tiles[0]),
                                  jnp.where(lo_q, tiles[1], zero_q), jnp.where(lo_q, zero_q, tiles[1])],
                                 axis=0)
            keys = kd_s[bi * WINDOW:(bi + 2) * WINDOW, g * LANES:(g + 1) * LANES]
            vals = vd_s[bi * WINDOW:(bi + 2) * WINDOW, g * LANES:(g + 1) * LANES]
            s = _dot_nt(qs, keys) + bias4
            outs = []
            for hh in range(Q_REP):
                sink = sinks_ref[layer, Q_REP * g + hh]
                sh = s[hh * WINDOW:(hh + 1) * WINDOW, :]
                m = jnp.maximum(jnp.max(sh, axis=-1, keepdims=True), sink)
                p = jnp.exp(sh - m)
                den = jnp.sum(p, axis=-1, keepdims=True) + jnp.exp(sink - m)
                outs.append(_dot(p.astype(BF), vals) / den)
            for j in range(2):
                oatt_s[bi * WINDOW:(bi + 1) * WINDOW, (2 * g + j) * LANES:(2 * g + j + 1) * LANES] = (
                    jnp.where(lo_q, outs[2 * j], outs[2 * j + 1]).astype(BF))
    kd_s[0:WINDOW, :] = kd_s[tm:tm + WINDOW, :]
    vd_s[0:WINDOW, :] = vd_s[tm:tm + WINDOW, :]
    gated_branch(2, oatt_s[...], False)

    accb = acc_s[...].astype(BF)
    for n in range(D_MODEL // MXU_COLS):
        cols = slice(n * MXU_COLS, (n + 1) * MXU_COLS)
        x1_ref[:, cols] = x_ref[:, cols] + gt1[:, cols] * _dot(accb, wout_ref[:, cols])

    @pl.when(t == last)
    def _():
        kwin_ref[...] = k_nat[tm - WINDOW:tm, :]
        vwin_ref[...] = v_nat[tm - WINDOW:tm, :]
        cst_ref[...] = a[tm - (CONV_K - 1):tm, :]


def _mixer_call(layer, x, mod, p, tm):
    nb, seq, _ = x.shape
    nt = seq // tm
    lsel3 = lambda b, t: (layer, 0, 0)
    kernel = functools.partial(_mixer_kernel, layer=layer, tm=tm)
    return pl.pallas_call(
        kernel,
        out_shape=(
            jax.ShapeDtypeStruct((nb, seq, D_MODEL), F32),
            jax.ShapeDtypeStruct((nb, WINDOW, KV_WIDTH), F32),
            jax.ShapeDtypeStruct((nb, WINDOW, KV_WIDTH), F32),
            jax.ShapeDtypeStruct((nb, CONV_K - 1, CONV_WIDTH), F32),
        ),
        grid=(nb, nt),
        in_specs=[
            pl.BlockSpec(memory_space=pltpu.SMEM),
            pl.BlockSpec((None, tm, D_MODEL), lambda b, t: (b, t, 0)),
            pl.BlockSpec((None, SUBLANES, 3 * D_MODEL),
                         lambda b, t: (layer, PROMPT_MOD_ROW // SUBLANES, 0)),
            _const_spec((None, 1, D_MODEL), lsel3),
            _const_spec((None, D_MODEL, IN_WIDTH), lsel3),
            _const_spec((None, 1, GM_WIDTH), lsel3),
            _const_spec((None, GM_GROUPS, CHUNK, CHUNK), lambda b, t: (layer, 0, 0, 0)),
            _const_spec((None, CHUNK, GM_WIDTH), lsel3),
            _const_spec((None, CONV_K, CONV_WIDTH), lsel3),
            _const_spec((None, 1, CONV_WIDTH), lsel3),
            _const_spec((None, 1, CONV_WIDTH), lsel3),
            _const_spec((None, 1, CONV_WIDTH), lsel3),
            _const_spec((None, 1, ATT_WIDTH), lsel3),
            _const_spec((None, 1, LANES), lsel3),
            pl.BlockSpec((None, 1, 2 * LANES), lambda b, t: (t, 0, 0)),
            _const_spec((tm, 2 * LANES), lambda b, t: (0, 0)),
            _const_spec((1, LANES), lambda b, t: (0, 0)),
            _const_spec((2, WINDOW, 2 * WINDOW), lambda b, t: (0, 0, 0)),
            _const_spec((MXU_COLS, MXU_COLS), lambda b, t: (0, 0)),
            _const_spec((None, N_BRANCH, GM_WIDTH, D_MODEL), lambda b, t: (layer, 0, 0, 0)),
            _const_spec((None, D_MODEL, D_MODEL), lsel3),
        ],
        out_specs=(
            pl.BlockSpec((None, tm, D_MODEL), lambda b, t: (b, t, 0)),
            pl.BlockSpec((None, WINDOW, KV_WIDTH), lambda b, t: (b, 0, 0)),
            pl.BlockSpec((None, WINDOW, KV_WIDTH), lambda b, t: (b, 0, 0)),
            pl.BlockSpec((None, CONV_K - 1, CONV_WIDTH), lambda b, t: (b, 0, 0)),
        ),
        scratch_shapes=[
            pltpu.VMEM((WINDOW + tm, 2 * LANES), BF),
            pltpu.VMEM((WINDOW + tm, 2 * LANES), BF),
            pltpu.VMEM((HALO + tm + SUBLANES, CONV_WIDTH), F32),
            pltpu.VMEM((tm, ATT_WIDTH), BF),
            pltpu.VMEM((tm, D_MODEL), F32),
        ],
        compiler_params=pltpu.CompilerParams(
            dimension_semantics=("arbitrary", "arbitrary"),
            vmem_limit_bytes=VMEM_LIMIT),
        name=f"prompt_mixers_l{layer}",
    )(p["sinks"], x, mod, p["n1g"], p["w_in"], p["gm_g"], p["gm_ws"], p["gm_bias"],
      p["conv_dw"], p["conv_b"], p["conv_ln_g"], p["conv_ln_b"], p["qn"], p["kn"],
      p["rope_base"], p["rope_off"], p["rope_sign"], p["band_bias"], p["blockdiag"],
      p["w_branch"], p["w_out"])


def _route_rows(logits_t, rb_ref):
    scores = _sigmoid(logits_t)
    biased = scores + rb_ref[...]
    rows = lambda a, gi: [a[EXPERTS_PER_GROUP * gi + k:EXPERTS_PER_GROUP * gi + k + 1, :]
                          for k in range(EXPERTS_PER_GROUP)]
    best = None
    idx = None
    for gi in range(N_GROUPS):
        b = rows(biased, gi)
        hi1, lo1 = jnp.maximum(b[0], b[1]), jnp.minimum(b[0], b[1])
        hi2, lo2 = jnp.maximum(b[2], b[3]), jnp.minimum(b[2], b[3])
        gs = jnp.maximum(hi1, hi2) + jnp.maximum(jnp.minimum(hi1, hi2), jnp.maximum(lo1, lo2))
        if gi == 0:
            best, idx = gs, jnp.zeros(gs.shape, jnp.int32)
        else:
            better = gs > best
            idx = jnp.where(better, gi, idx)
            best = jnp.where(better, gs, best)
    bsel = rows(biased, 0)
    ssel = rows(scores, 0)
    for gi in range(1, N_GROUPS):
        bg, sg = rows(biased, gi), rows(scores, gi)
        pick = idx == gi
        bsel = [jnp.where(pick, bg[k], bsel[k]) for k in range(EXPERTS_PER_GROUP)]
        ssel = [jnp.where(pick, sg[k], ssel[k]) for k in range(EXPERTS_PER_GROUP)]
    chosen = []
    for k in range(EXPERTS_PER_GROUP):
        rank = jnp.zeros(idx.shape, jnp.int32)
        for k2 in range(EXPERTS_PER_GROUP):
            if k2 == k:
                continue
            beats = (bsel[k2] > bsel[k]) | ((bsel[k2] == bsel[k]) & (k2 < k))
            rank = rank + beats.astype(jnp.int32)
        chosen.append(jnp.where(rank < 2, ssel[k], 0.0))
    den = chosen[0] + chosen[1] + chosen[2] + chosen[3]
    return idx, [c / den for c in chosen]


def _moe_route(x, mod, n2g_ref, rw2_ref, rb_ref, upper_ref, h_buf, rt_buf, c_buf, cnt_s, buf, tm):
    h2 = _modulate(x, n2g_ref[...], mod[:, 0:D_MODEL], mod[:, D_MODEL:2 * D_MODEL])
    hi = h2.astype(BF)
    lo = (h2 - hi.astype(F32)).astype(BF)
    rw2 = rw2_ref[...]
    lt = _dot_nt(rw2, hi) + _dot_nt(rw2, lo)
    idx, comb = _route_rows(lt[0:N_EXPERTS, :] + lt[N_EXPERTS:2 * N_EXPERTS, :], rb_ref)

    onehot = [(idx == g).astype(F32) for g in range(N_GROUPS)]
    oh8 = jnp.concatenate(onehot + [jnp.zeros((SUBLANES - N_GROUPS, tm), F32)], axis=0)
    prefix = _dot(oh8.astype(BF), upper_ref[...])
    slot = onehot[0] * prefix[0:1, :]
    for g in range(1, N_GROUPS):
        slot = slot + onehot[g] * prefix[g:g + 1, :]
    rt_buf[0:1, :] = idx
    rt_buf[1:2, :] = slot.astype(jnp.int32)
    for g in range(N_GROUPS):
        cnt_s[buf, g] = jnp.sum(onehot[g]).astype(jnp.int32)

    for k in range(EXPERTS_PER_GROUP):
        c_buf[k:k + 1, :] = comb[k]
    h_buf[...] = hi


def _moe_experts(x_ref, gt2, w1_ref, w3_ref, w2_ref, o_ref, h_buf, rt_buf, c_buf, cnt_s, buf,
                 tm, cap, overflow):
    def dispatch(g, j):
        want = lax.broadcasted_iota(jnp.int32, (cap, tm), 0) + j * cap
        hit = (rt_buf[1:2, :] == want) & (rt_buf[0:1, :] == g)
        return hit, jnp.where(hit, 1.0, 0.0).astype(BF)

    def run_experts(g, hit, hg):
        parts = []
        for e in range(EXPERTS_PER_GROUP):
            a = _dot(hg, w1_ref[EXPERTS_PER_GROUP * g + e])
            b = _dot(hg, w3_ref[EXPERTS_PER_GROUP * g + e])
            ce = jnp.sum(jnp.where(hit, c_buf[e:e + 1, :], 0.0), axis=-1, keepdims=True)
            parts.append(_silu(a) * b * ce)
        act = jnp.concatenate(parts, axis=1).astype(BF)
        return _dot(act, w2_ref[g]).astype(BF)

    def scatter(pmat, y):
        return lax.dot_general(pmat, y, (((0,), (0,)), ((), ())), preferred_element_type=F32)

    def gather(pmat):
        return _dot(pmat, h_buf[...]).astype(BF)

    if not overflow:
        sel = [dispatch(g, 0) for g in range(N_GROUPS)]
        pmat_all = jnp.concatenate([s[1] for s in sel], axis=0)
        hg_all = gather(pmat_all)
        ys = [run_experts(g, sel[g][0], hg_all[g * cap:(g + 1) * cap, :]) for g in range(N_GROUPS)]
        y_tile = scatter(pmat_all, jnp.concatenate(ys, axis=0))
        o_ref[...] = x_ref[...] + gt2 * y_tile
        return

    def group_body(g, carry):
        nblk = (cnt_s[buf, g] + (cap - 1)) // cap

        def block_body(j, carry2):
            hit, pmat = dispatch(g, j)
            o_ref[...] += gt2 * scatter(pmat, run_experts(g, hit, gather(pmat)))
            return carry2

        lax.fori_loop(1, nblk, block_body, 0)
        return carry

    lax.fori_loop(0, N_GROUPS, group_body, 0)


def _moe_kernel(x_ref, xn_ref, mod_ref, n2g_ref, rw2_ref, rb_ref, upper_ref,
                w1_ref, w3_ref, w2_ref, o_ref, h0_s, h1_s, rt0_s, rt1_s, c0_s, c1_s, cnt_s,
                *, tm, cap, tiles_per_row, n_tiles):
    route = functools.partial(_moe_route, n2g_ref=n2g_ref, rw2_ref=rw2_ref, rb_ref=rb_ref,
                              upper_ref=upper_ref, cnt_s=cnt_s, tm=tm)
    experts = functools.partial(_moe_experts, x_ref, w1_ref=w1_ref,
                                w3_ref=w3_ref, w2_ref=w2_ref, o_ref=o_ref, cnt_s=cnt_s, tm=tm, cap=cap)
    bufs = [dict(h_buf=h0_s, rt_buf=rt0_s, c_buf=c0_s, buf=0),
            dict(h_buf=h1_s, rt_buf=rt1_s, c_buf=c1_s, buf=1)]
    if tiles_per_row is None:
        mod = mod_ref[...]
        route(x_ref[...], mod, **bufs[0])
        for overflow in (False, True):
            experts(mod[:, 2 * D_MODEL:3 * D_MODEL], overflow=overflow, **bufs[0])
        return

    i = pl.program_id(0)
    mod_row = lambda tile: mod_ref[pl.ds(tile // tiles_per_row, 1), :]

    @pl.when(i == 0)
    def _():
        route(x_ref[...], mod_row(i), **bufs[0])

    def step(cur):
        gt2 = mod_row(i)[:, 2 * D_MODEL:3 * D_MODEL]
        experts(gt2, overflow=False, **bufs[cur])
        route(xn_ref[...], mod_row(jnp.minimum(i + 1, n_tiles - 1)), **bufs[1 - cur])
        experts(gt2, overflow=True, **bufs[cur])

    @pl.when(i % 2 == 0)
    def _():
        step(0)

    @pl.when(i % 2 == 1)
    def _():
        step(1)


def _moe_call(layer, x2d, mod, tiles_per_row, p, tm, name):
    n = x2d.shape[0]
    n_tiles = n // tm
    if tiles_per_row is None:
        assert n_tiles == 1
        mod_spec = pl.BlockSpec((None, n, 3 * D_MODEL), lambda i: (layer, 0, 1))
    else:
        mod_spec = pl.BlockSpec((None, SUBLANES, 3 * D_MODEL),
                                lambda i: (layer, PROMPT_MOD_ROW // SUBLANES, 1))
    lsel3 = lambda i: (layer, 0, 0)
    lsel4 = lambda i: (layer, 0, 0, 0)
    upper = np.triu(np.ones((tm, tm), np.float32), k=1)
    kernel = functools.partial(_moe_kernel, tm=tm, cap=min(MOE_CAP, tm), tiles_per_row=tiles_per_row,
                               n_tiles=n_tiles)
    return pl.pallas_call(
        kernel,
        out_shape=jax.ShapeDtypeStruct((n, D_MODEL), F32),
        grid=(n_tiles,),
        in_specs=[
            pl.BlockSpec((tm, D_MODEL), lambda i: (i, 0)),
            pl.BlockSpec((tm, D_MODEL), lambda i: (jnp.minimum(i + 1, n_tiles - 1), 0)),
            mod_spec,
            _const_spec((None, 1, D_MODEL), lsel3),
            _const_spec((2 * N_EXPERTS, D_MODEL), lambda i: (0, 0)),
            _const_spec((N_EXPERTS, 1), lambda i: (0, 0)),
            _const_spec((tm, tm), lambda i: (0, 0)),
            _const_spec((None, N_EXPERTS, D_MODEL, EXPERT_FF), lsel4),
            _const_spec((None, N_EXPERTS, D_MODEL, EXPERT_FF), lsel4),
            _const_spec((None, N_GROUPS, EXPERTS_PER_GROUP * EXPERT_FF, D_MODEL), lsel4),
        ],
        out_specs=pl.BlockSpec((tm, D_MODEL), lambda i: (i, 0)),
        scratch_shapes=[
            pltpu.VMEM((tm, D_MODEL), BF),
            pltpu.VMEM((tm, D_MODEL), BF),
            pltpu.VMEM((SUBLANES, tm), jnp.int32),
            pltpu.VMEM((SUBLANES, tm), jnp.int32),
            pltpu.VMEM((SUBLANES, tm), F32),
            pltpu.VMEM((SUBLANES, tm), F32),
            pltpu.SMEM((2, N_GROUPS), jnp.int32),
        ],
        compiler_params=pltpu.CompilerParams(
            dimension_semantics=("arbitrary",),
            vmem_limit_bytes=VMEM_LIMIT),
        name=name,
    )(x2d, x2d, mod, p["n2g"], p["rw2"], p["rb"], jnp.asarray(upper, dtype=BF),
      p["moe_w1"], p["moe_w3"], p["moe_w2"])


def _sample_proj_kernel(x_ref, mod_ref, n1g_ref, w_ref, z_ref):
    mod = mod_ref[...]
    h = _modulate(x_ref[...], n1g_ref[...], mod[:, 0:D_MODEL], mod[:, D_MODEL:2 * D_MODEL])
    z_ref[...] = _dot(h.astype(BF), w_ref[...])


def _sample_proj_call(layer, xs, mod, p):
    n = xs.shape[0]
    return pl.pallas_call(
        _sample_proj_kernel,
        out_shape=jax.ShapeDtypeStruct((n, IN_WIDTH), F32),
        grid=(IN_WIDTH // PROJ_BLOCK,),
        in_specs=[
            pl.BlockSpec((n, D_MODEL), lambda j: (0, 0)),
            pl.BlockSpec((None, n, 3 * D_MODEL), lambda j: (layer, 0, 0)),
            pl.BlockSpec((None, 1, D_MODEL), lambda j: (layer, 0, 0)),
            pl.BlockSpec((None, D_MODEL, PROJ_BLOCK), lambda j: (layer, 0, j)),
        ],
        out_specs=pl.BlockSpec((n, PROJ_BLOCK), lambda j: (0, j)),
        compiler_params=pltpu.CompilerParams(dimension_semantics=("arbitrary",)),
        name=f"sample_proj_l{layer}",
    )(xs, mod, p["n1g"], p["w_in"])


def _shift_in_column(cache_t, new_rows, bb):
    flat = cache_t.reshape(bb * HEAD_DIM, WINDOW)
    shifted = pltpu.roll(flat, WINDOW - 1, 1).reshape(bb, HEAD_DIM, WINDOW)
    padded = jnp.concatenate([new_rows, jnp.zeros((LANES - bb, LANES), F32)], axis=0)
    new_t = padded.T
    is_last = lax.broadcasted_iota(jnp.int32, (HEAD_DIM, WINDOW), 1) == WINDOW - 1
    out = []
    for b in range(bb):
        col = new_t[0:HEAD_DIM, b:b + 1]
        out.append(jnp.where(is_last, col, shifted[b]))
    return out


def _sample_mixer_kernel(z_ref, x_ref, mod_ref, kt_ref, vt_ref, st_ref,
                         gmg_ref, gmw_ref, gmb_ref, cdw_ref, cb_ref, clg_ref, clb_ref,
                         qn_ref, kn_ref, ropec_ref, ropes_ref, sink_ref, bd_ref, wb_ref, wout_ref,
                         _k_all, _v_all, _st_all,
                         x1_ref, kto_ref, vto_ref, sto_ref, gv_ref,
                         qf_s, of_s, oc_s, zg_s, *, bb):
    i = pl.program_id(0)
    r0 = pl.multiple_of(i * bb, bb)
    z = z_ref[...]
    zg_s[pl.ds(r0, bb), :] = z[:, OFF_GATE:OFF_GATE + N_BRANCH * D_MODEL]

    u = _gelu_half(z[:, OFF_GM_U:OFF_GM_U + GM_WIDTH])
    gv = _gelu_half(z[:, OFF_GM_V:OFF_GM_V + GM_WIDTH])
    v = gv * lax.rsqrt(jnp.mean(gv * gv, axis=-1, keepdims=True) + EPS) * gmg_ref[...]
    gv_ref[...] = v
    o_gm = u * (v * gmw_ref[...] + gmb_ref[...])

    a = _sigmoid_times(z[:, OFF_CV_G:OFF_CV_G + CONV_WIDTH], z[:, OFF_CV_A:OFF_CV_A + CONV_WIDTH])
    y = cb_ref[...] + cdw_ref[CONV_K - 1:CONV_K, :] * a
    for j in range(CONV_K - 1):
        y = y + cdw_ref[j:j + 1, :] * st_ref[j]
    o_cv = _layer_norm_silu(y, clg_ref[...], clb_ref[...])
    sto_ref[0:CONV_K - 2] = st_ref[1:CONV_K - 1]
    sto_ref[CONV_K - 2] = a

    rc = ropec_ref[...]
    rs = ropes_ref[...]
    bd = bd_ref[...]
    q = _rope(_head_norm(z[:, OFF_Q:OFF_Q + ATT_WIDTH], bd, qn_ref[...]), rc, rs)
    knew = _rope(_head_norm(z[:, OFF_K:OFF_K + KV_WIDTH], bd[0:LANES, 0:LANES], kn_ref[...]), rc, rs)
    vnew = z[:, OFF_V:OFF_V + KV_WIDTH]
    lo = _low_half((bb, LANES))
    for h in range(N_HEADS):
        tile = q[:, (h // 2) * LANES:(h // 2 + 1) * LANES]
        if h % 2 == 1:
            tile = pltpu.roll(tile, HEAD_DIM, 1)
        qf_s[:, h, :] = jnp.where(lo, tile, 0.0)
    k_low = [knew, pltpu.roll(knew, HEAD_DIM, 1)]
    v_low = [vnew, pltpu.roll(vnew, HEAD_DIM, 1)]
    for g in range(N_KV_HEADS):
        qg = qf_s[:, Q_REP * g:Q_REP * (g + 1), :]
        kt = kt_ref[:, g]
        vt = vt_ref[:, g]
        s = jnp.einsum("brd,bdp->brp", qg[:, :, 0:HEAD_DIM].astype(BF), kt.astype(BF),
                       preferred_element_type=F32)
        s_new = jnp.sum(qg * k_low[g][:, None, :], axis=-1, keepdims=True)
        sink = sink_ref[Q_REP * g:Q_REP * (g + 1), 0:1][None]
        m = jnp.maximum(jnp.maximum(jnp.max(s, axis=-1, keepdims=True), s_new), sink)
        pr = jnp.exp(s - m)
        p_new = jnp.exp(s_new - m)
        den = jnp.sum(pr, axis=-1, keepdims=True) + p_new + jnp.exp(sink - m)
        o = jnp.einsum("brp,bdp->brd", pr.astype(BF), vt.astype(BF), preferred_element_type=F32)
        o = (o + p_new * v_low[g][:, None, 0:HEAD_DIM]) / den
        of_s[:, Q_REP * g:Q_REP * (g + 1), :] = jnp.concatenate([o, jnp.zeros(o.shape, F32)], axis=-1)
        for b, tile in enumerate(_shift_in_column(kt, k_low[g], bb)):
            kto_ref[b, g] = tile
        for b, tile in enumerate(_shift_in_column(vt, v_low[g], bb)):
            vto_ref[b, g] = tile
    att_tiles = []
    for j in range(N_HEADS // 2):
        second = pltpu.roll(of_s[:, 2 * j + 1, :], HEAD_DIM, 1)
        att_tiles.append(jnp.where(lo, of_s[:, 2 * j, :], second))
    o_att = jnp.concatenate(att_tiles, axis=1)

    oc_s[pl.ds(r0, bb), 0:GM_WIDTH] = o_gm.astype(BF)
    oc_s[pl.ds(r0, bb), GM_WIDTH:GM_WIDTH + CONV_WIDTH] = o_cv.astype(BF)
    oc_s[pl.ds(r0, bb), GM_WIDTH + CONV_WIDTH:GM_WIDTH + CONV_WIDTH + ATT_WIDTH] = o_att.astype(BF)

    @pl.when(i == pl.num_programs(0) - 1)
    def _():
        acc = _sigmoid_times(zg_s[:, 0:D_MODEL], _dot(oc_s[:, 0:GM_WIDTH], wb_ref[0]))
        acc = acc + _sigmoid_times(zg_s[:, D_MODEL:2 * D_MODEL], _dot(
            oc_s[:, GM_WIDTH:GM_WIDTH + CONV_WIDTH], wb_ref[1]))
        acc = acc + _sigmoid_times(zg_s[:, 2 * D_MODEL:3 * D_MODEL], _dot(
            oc_s[:, GM_WIDTH + CONV_WIDTH:GM_WIDTH + CONV_WIDTH + ATT_WIDTH], wb_ref[2]))
        gt1 = mod_ref[...][:, 2 * D_MODEL:3 * D_MODEL]
        x1_ref[...] = x_ref[...] + gt1 * _dot(acc.astype(BF), wout_ref[...])


def _sample_mixer_call(layer, z, xs, mod, kt, vt, st, new_k, new_v, new_st, p, bb):
    n = xs.shape[0]
    lsel3 = lambda i: (layer, 0, 0)
    cache_block = (bb, N_KV_HEADS, HEAD_DIM, WINDOW)
    kernel = functools.partial(_sample_mixer_kernel, bb=bb)
    args = (z, xs, mod, kt, vt, st, p["gm_g"], p["gm_w0"], p["gm_b0"], p["conv_dw"], p["conv_b"],
            p["conv_ln_g"], p["conv_ln_b"], p["qn"], p["kn"], p["rope_c1"], p["rope_s1"],
            p["sink_lanes"], p["blockdiag"], p["w_branch"], p["w_out"], new_k, new_v, new_st)
    n_in = len(args)
    return pl.pallas_call(
        kernel,
        out_shape=(
            jax.ShapeDtypeStruct((n, D_MODEL), F32),
            jax.ShapeDtypeStruct(new_k.shape, F32),
            jax.ShapeDtypeStruct(new_v.shape, F32),
            jax.ShapeDtypeStruct(new_st.shape, F32),
            jax.ShapeDtypeStruct((n, GM_WIDTH), F32),
        ),
        input_output_aliases={n_in - 3: 1, n_in - 2: 2, n_in - 1: 3},
        grid=(n // bb,),
        in_specs=[
            pl.BlockSpec((bb, IN_WIDTH), lambda i: (i, 0)),
            pl.BlockSpec((n, D_MODEL), lambda i: (0, 0)),
            pl.BlockSpec((None, n, 3 * D_MODEL), lsel3),
            pl.BlockSpec((None,) + cache_block, lambda i: (layer, i, 0, 0, 0)),
            pl.BlockSpec((None,) + cache_block, lambda i: (layer, i, 0, 0, 0)),
            pl.BlockSpec((None, CONV_K - 1, bb, CONV_WIDTH), lambda i: (layer, 0, i, 0)),
            pl.BlockSpec((None, 1, GM_WIDTH), lsel3),
            pl.BlockSpec((None, 1, GM_WIDTH), lsel3),
            pl.BlockSpec((None, 1, GM_WIDTH), lsel3),
            pl.BlockSpec((None, CONV_K, CONV_WIDTH), lsel3),
            pl.BlockSpec((None, 1, CONV_WIDTH), lsel3),
            pl.BlockSpec((None, 1, CONV_WIDTH), lsel3),
            pl.BlockSpec((None, 1, CONV_WIDTH), lsel3),
            pl.BlockSpec((None, 1, ATT_WIDTH), lsel3),
            pl.BlockSpec((None, 1, LANES), lsel3),
            pl.BlockSpec((1, LANES), lambda i: (0, 0)),
            pl.BlockSpec((1, LANES), lambda i: (0, 0)),
            pl.BlockSpec((None, N_HEADS, LANES), lsel3),
            pl.BlockSpec((MXU_COLS, MXU_COLS), lambda i: (0, 0)),
            pl.BlockSpec((None, N_BRANCH, GM_WIDTH, D_MODEL), lambda i: (layer, 0, 0, 0)),
            pl.BlockSpec((None, D_MODEL, D_MODEL), lsel3),
            pl.BlockSpec(memory_space=pl.ANY),
            pl.BlockSpec(memory_space=pl.ANY),
            pl.BlockSpec(memory_space=pl.ANY),
        ],
        out_specs=(
            pl.BlockSpec((n, D_MODEL), lambda i: (0, 0)),
            pl.BlockSpec((None,) + cache_block, lambda i: (layer, i, 0, 0, 0)),
            pl.BlockSpec((None,) + cache_block, lambda i: (layer, i, 0, 0, 0)),
            pl.BlockSpec((None, CONV_K - 1, bb, CONV_WIDTH), lambda i: (layer, 0, i, 0)),
            pl.BlockSpec((bb, GM_WIDTH), lambda i: (i, 0)),
        ),
        scratch_shapes=[
            pltpu.VMEM((bb, N_HEADS, LANES), F32),
            pltpu.VMEM((bb, N_HEADS, LANES), F32),
            pltpu.VMEM((n, GM_WIDTH + CONV_WIDTH + ATT_WIDTH), BF),
            pltpu.VMEM((n, N_BRANCH * D_MODEL), F32),
        ],
        compiler_params=pltpu.CompilerParams(
            dimension_semantics=("arbitrary",), vmem_limit_bytes=VMEM_LIMIT),
        name=f"sample_mixers_l{layer}",
    )(*args)


def _rope_lane_tables():
    half = ROT_DIM // 2
    freqs = jnp.exp(-math.log(ROPE_THETA) * jnp.arange(half, dtype=F32) * (2.0 / ROT_DIM))
    rest = jnp.zeros((HEAD_DIM - ROT_DIM,), F32)
    freq64 = jnp.concatenate([freqs, freqs, rest])
    sign64 = jnp.concatenate([-jnp.ones((half,), F32), jnp.ones((half,), F32), rest])
    reps = LANES // HEAD_DIM
    return jnp.tile(freq64, reps)[None, :], jnp.tile(sign64, reps)[None, :]


def _cos_sin(pos, lane_freq):
    ang = pos.astype(F32)[:, None] * lane_freq
    return jnp.cos(ang), jnp.sin(ang)


def _band_bias():
    i = np.arange(WINDOW)[:, None]
    j = np.arange(2 * WINDOW)[None, :]
    band = (j >= i) & (j <= i + WINDOW)
    first = band & (j >= WINDOW)
    out = np.where(np.stack([first, band]), 0.0, NEG_BIG).astype(np.float32)
    return jnp.asarray(out)


def _blockdiag():
    idx = np.arange(MXU_COLS) // HEAD_DIM
    return jnp.asarray((idx[:, None] == idx[None, :]).astype(np.float32), dtype=BF)


def _router_hi_lo(router_w):
    rw = router_w.astype(F32).T
    rw_hi = rw.astype(BF)
    rw_lo = (rw - rw_hi.astype(F32)).astype(BF)
    return jnp.concatenate([rw_hi, rw_lo], axis=0)


def _prepare(norm1_g, norm2_g, w_in, gm_norm_g, gm_ws, gm_b, conv_dw, conv_b, conv_ln_g,
             conv_ln_b, q_norm_g, k_norm_g, attn_sinks, w_branch, w_out, router_w, router_b,
             moe_w1, moe_w3, moe_w2, seq, tm):
    lane_freq, lane_sign = _rope_lane_tables()
    cb, sb = _cos_sin(jnp.arange(seq // tm, dtype=jnp.int32) * tm, lane_freq)
    co, so = _cos_sin(jnp.arange(tm, dtype=jnp.int32), lane_freq)
    c1, s1 = _cos_sin(PAST_LEN + jnp.arange(1, dtype=jnp.int32), lane_freq)
    row3 = lambda a: a.reshape(DEPTH, 1, a.shape[-1])
    col = np.arange(IN_WIDTH)
    col_scale = jnp.asarray(np.where((col >= OFF_Q) & (col < OFF_GATE), 1.0, 0.5), F32)
    return {
        "n1g": row3(norm1_g), "n2g": row3(norm2_g), "w_in": (w_in * col_scale).astype(BF),
        "gm_g": row3(gm_norm_g), "gm_ws": gm_ws,
        "gm_bias": jnp.repeat(jnp.swapaxes(gm_b, 1, 2), LANES, axis=2),
        "gm_w0": jnp.repeat(gm_ws[:, :, 0, 0], LANES, axis=1).reshape(DEPTH, 1, GM_WIDTH),
        "gm_b0": jnp.repeat(gm_b[:, :, 0], LANES, axis=1).reshape(DEPTH, 1, GM_WIDTH),
        "conv_dw": conv_dw, "conv_b": row3(conv_b), "conv_ln_g": row3(0.5 * conv_ln_g),
        "conv_ln_b": row3(0.5 * conv_ln_b),
        "qn": row3(jnp.tile(q_norm_g, (1, N_HEADS)) * (HEAD_DIM ** -0.5)),
        "kn": row3(jnp.tile(k_norm_g, (1, N_KV_HEADS))),
        "sinks": attn_sinks,
        "sink_lanes": jnp.broadcast_to(attn_sinks[:, :, None], (DEPTH, N_HEADS, LANES)),
        "rope_base": jnp.concatenate([cb, sb], axis=1)[:, None, :],
        "rope_off": jnp.concatenate([co, so], axis=1),
        "rope_sign": lane_sign, "rope_c1": c1, "rope_s1": s1 * lane_sign,
        "band_bias": _band_bias(), "blockdiag": _blockdiag(),
        "w_branch": (0.5 * w_branch).astype(BF), "w_out": w_out.astype(BF),
        "rw2": _router_hi_lo(router_w),
        "rb": router_b.astype(F32).reshape(N_EXPERTS, 1),
        "moe_w1": moe_w1.astype(BF), "moe_w3": moe_w3.astype(BF),
        "moe_w2": moe_w2.astype(BF).reshape(DEPTH, N_GROUPS, EXPERTS_PER_GROUP * EXPERT_FF, D_MODEL),
    }


def kernel(x_prompt, x_sample, cache_win_k, cache_win_v, state_conv, c_prompt, c_sample, norm1_g, norm2_g, w_ada, b_ada, w_in, gm_norm_g, gm_ws, gm_b, conv_dw, conv_b, conv_ln_g, conv_ln_b, q_norm_g, k_norm_g, attn_sinks, w_branch, w_out, router_w, router_b, moe_w1, moe_w3, moe_w2):
    nb, seq, _ = x_prompt.shape
    ns = x_sample.shape[0]
    tm = min(MIX_TILE, seq)
    tmoe = min(MOE_TILE, seq)
    p = _prepare(norm1_g, norm2_g, w_in, gm_norm_g, gm_ws, gm_b, conv_dw, conv_b, conv_ln_g,
                 conv_ln_b, q_norm_g, k_norm_g, attn_sinks, w_branch, w_out, router_w, router_b,
                 moe_w1, moe_w3, moe_w2, seq, tm)
    assert ns == PROMPT_MOD_ROW and nb <= ADA_ROWS - PROMPT_MOD_ROW
    c_all = jnp.concatenate(
        [c_sample, c_prompt, jnp.zeros((ADA_ROWS - nb - ns, D_MODEL), F32)], axis=0)
    mod = _ada_call(c_all, w_ada, b_ada)

    kt = jnp.transpose(cache_win_k, (0, 1, 3, 4, 2))
    vt = jnp.transpose(cache_win_v, (0, 1, 3, 4, 2))
    st = jnp.transpose(state_conv, (0, 2, 1, 3))

    xp = x_prompt
    xs = x_sample.reshape(ns, D_MODEL)
    kp_l, vp_l, cp_l, gs_l = [], [], [], []
    new_k = jnp.zeros(kt.shape, F32)
    new_v = jnp.zeros(vt.shape, F32)
    new_st = jnp.zeros(st.shape, F32)
    for l in range(DEPTH):
        xp, kp, vp, cp = _mixer_call(l, xp, mod, p, tm)
        xp = _moe_call(l, xp.reshape(nb * seq, D_MODEL), mod, seq // tmoe, p, tmoe,
                       f"prompt_moe_l{l}").reshape(nb, seq, D_MODEL)
        kp_l.append(kp)
        vp_l.append(vp)
        cp_l.append(cp)

        z = _sample_proj_call(l, xs, mod, p)
        xs, new_k, new_v, new_st, gv = _sample_mixer_call(l, z, xs, mod, kt, vt, st,
                                                          new_k, new_v, new_st, p, SAMPLE_BLOCK)
        xs = _moe_call(l, xs, mod, None, p, ns, f"sample_moe_l{l}")
        gs_l.append(gv)

    kv_shape = (DEPTH, nb, WINDOW, N_KV_HEADS, HEAD_DIM)
    return (xp, xs.reshape(ns, 1, D_MODEL),
            jnp.stack(kp_l).reshape(kv_shape), jnp.stack(vp_l).reshape(kv_shape),
            jnp.stack(cp_l),
            jnp.transpose(new_k, (0, 1, 4, 2, 3)),
            jnp.transpose(new_v, (0, 1, 4, 2, 3)),
            jnp.transpose(new_st, (0, 2, 1, 3)),
            jnp.stack(gs_l).reshape(DEPTH, ns, 1, GM_WIDTH))
```

```python
import functools
import math

import jax
import jax.numpy as jnp
import numpy as np
from jax import lax
from jax.experimental import pallas as pl
from jax.experimental.pallas import tpu as pltpu

F32 = jnp.float32
BF = jnp.bfloat16

D_MODEL = 1024
DEPTH = 2
PAST_LEN = 8192
CHUNK = 128
GM_GROUPS = 4
GM_WIDTH = 512
CONV_WIDTH = 512
CONV_K = 31
N_HEADS = 8
N_KV_HEADS = 2
Q_REP = N_HEADS // N_KV_HEADS
HEAD_DIM = 64
ATT_WIDTH = N_HEADS * HEAD_DIM
KV_WIDTH = N_KV_HEADS * HEAD_DIM
WINDOW = 128
ROPE_THETA = 500000.0
ROT_DIM = HEAD_DIM // 4
N_BRANCH = 3
N_EXPERTS = 16
EXPERTS_PER_GROUP = 4
N_GROUPS = N_EXPERTS // EXPERTS_PER_GROUP
EXPERT_FF = 256
EPS = 1e-6

OFF_GM_U = 0
OFF_GM_V = 512
OFF_CV_A = 1024
OFF_CV_G = 1536
OFF_Q = 2048
OFF_K = 2560
OFF_V = 2688
OFF_GATE = 2816
IN_WIDTH = OFF_GATE + N_BRANCH * D_MODEL

LANES = 128
SUBLANES = 8
MXU_COLS = 256
HALO = 32
MIX_TILE = 512
MOE_TILE = 512
MOE_CAP = 160
ADA_ROWS = 136
PROMPT_MOD_ROW = 128
ADA_BLOCK = 3072
PROJ_BLOCK = IN_WIDTH // 2
SAMPLE_BLOCK = 32
VMEM_LIMIT = 56 * 1024 * 1024
NEG_BIG = -1e30


def _dot(a, b):
    return jnp.dot(a, b, preferred_element_type=F32)


def _dot_nt(a, b):
    return lax.dot_general(a, b, (((1,), (1,)), ((), ())), preferred_element_type=F32)


def _sigmoid(x):
    return 0.5 * jnp.tanh(0.5 * x) + 0.5


def _silu(x):
    return x * _sigmoid(x)


GELU_C = 0.7978845608028654
GELU_K = 0.044715


def _gelu_half(xh):
    return xh + xh * jnp.tanh(xh * (2.0 * GELU_C + (8.0 * GELU_C * GELU_K) * (xh * xh)))


def _sigmoid_times(zh, dh):
    return jnp.tanh(zh) * dh + dh


def _modulate(x, g, shift, scale):
    ms = jnp.mean(x * x, axis=-1, keepdims=True)
    return (x * lax.rsqrt(ms + EPS)) * (g * (1.0 + scale)) + shift


def _head_norm(x, blockdiag, g):
    sq = (x * x).astype(BF)
    width = x.shape[-1]
    step = min(width, MXU_COLS)
    ssum = jnp.concatenate([_dot(sq[:, c:c + step], blockdiag[0:step, 0:step])
                            for c in range(0, width, step)], axis=-1)
    return x * lax.rsqrt(ssum * (1.0 / HEAD_DIM) + EPS) * g


def _rope(x, c, s):
    width = x.shape[-1]
    reps = width // LANES
    cc = jnp.concatenate([c] * reps, axis=-1)
    ss = jnp.concatenate([s] * reps, axis=-1)
    lane = lax.broadcasted_iota(jnp.int32, x.shape, x.ndim - 1) % HEAD_DIM
    partner = jnp.where(lane < ROT_DIM // 2,
                        pltpu.roll(x, width - ROT_DIM // 2, x.ndim - 1),
                        pltpu.roll(x, ROT_DIM // 2, x.ndim - 1))
    return x * cc + partner * ss


def _layer_norm_silu(y, gh, bh):
    mu = jnp.mean(y, axis=-1, keepdims=True)
    yc = y - mu
    var = jnp.mean(yc * yc, axis=-1, keepdims=True)
    h = yc * lax.rsqrt(var + EPS) * gh + bh
    return h * jnp.tanh(h) + h


def _low_half(shape):
    return lax.broadcasted_iota(jnp.int32, shape, len(shape) - 1) % LANES < HEAD_DIM


def _const_spec(shape, index_map):
    return pl.BlockSpec(shape, index_map, pipeline_mode=pl.Buffered(1))


def _ada_kernel(c_ref, w_ref, b_ref, o_ref):
    s = _silu(c_ref[...]).astype(BF)
    o_ref[...] = _dot(s, w_ref[...].astype(BF)) + b_ref[...]


def _ada_call(c_all, w_ada, b_ada):
    nb = (6 * D_MODEL) // ADA_BLOCK
    return pl.pallas_call(
        _ada_kernel,
        out_shape=jax.ShapeDtypeStruct((DEPTH, ADA_ROWS, 6 * D_MODEL), F32),
        grid=(DEPTH, nb),
        in_specs=[
            pl.BlockSpec((ADA_ROWS, D_MODEL), lambda l, j: (0, 0)),
            pl.BlockSpec((None, D_MODEL, ADA_BLOCK), lambda l, j: (l, 0, j)),
            pl.BlockSpec((None, 1, ADA_BLOCK), lambda l, j: (l, 0, j)),
        ],
        out_specs=pl.BlockSpec((None, ADA_ROWS, ADA_BLOCK), lambda l, j: (l, 0, j)),
        compiler_params=pltpu.CompilerParams(
            dimension_semantics=("arbitrary", "arbitrary")),
        name="ada_mod",
    )(c_all, w_ada, b_ada.reshape(DEPTH, 1, 6 * D_MODEL))


def _mixer_kernel(sinks_ref, x_ref, mod_ref, n1g_ref, win_ref, gmg_ref, gmws_ref, gmb_ref,
                  cdw_ref, cb_ref, clg_ref, clb_ref, qn_ref, kn_ref, ropeb_ref, ropeo_ref,
                  sign_ref, bias_ref, bd_ref, wb_ref, wout_ref,
                  x1_ref, kwin_ref, vwin_ref, cst_ref,
                  kd_s, vd_s, abuf, oatt_s, acc_s, *, layer, tm):
    t = pl.program_id(1)
    last = pl.num_programs(1) - 1
    nblk = tm // WINDOW

    @pl.when(t == 0)
    def _():
        kd_s[0:WINDOW, :] = jnp.zeros((WINDOW, 2 * LANES), BF)
        vd_s[0:WINDOW, :] = jnp.zeros((WINDOW, 2 * LANES), BF)
        abuf[0:HALO, :] = jnp.zeros((HALO, CONV_WIDTH), F32)
        abuf[HALO + tm:HALO + tm + SUBLANES, :] = jnp.zeros((SUBLANES, CONV_WIDTH), F32)

    x = x_ref[...]
    mod = mod_ref[pl.ds(pl.program_id(0), 1), :]
    sh1 = mod[:, 0:D_MODEL]
    sc1 = mod[:, D_MODEL:2 * D_MODEL]
    gt1 = mod[:, 2 * D_MODEL:3 * D_MODEL]
    hb = _modulate(x, n1g_ref[...], sh1, sc1).astype(BF)

    def proj(off, width):
        return _dot(hb, win_ref[:, off:off + width])

    z_cv = proj(OFF_CV_A, 2 * CONV_WIDTH)
    a = _sigmoid_times(z_cv[:, CONV_WIDTH:2 * CONV_WIDTH], z_cv[:, 0:CONV_WIDTH])
    abuf[HALO:HALO + tm, :] = a
    first_off = HALO - (CONV_K - 1)

    def conv_tile(ci):
        lanes = slice(ci * LANES, (ci + 1) * LANES)
        yt = cb_ref[:, lanes]
        for b in range(SUBLANES):
            part = None
            for off in range(b, HALO + 1, SUBLANES):
                if off < first_off:
                    continue
                term = (cdw_ref[off - first_off:off - first_off + 1, lanes]
                        * abuf[pl.ds(off - b, tm + SUBLANES), lanes])
                part = term if part is None else part + term
            yt = yt + part[b:b + tm, :]
        return yt

    z_gm = proj(OFF_GM_U, 2 * GM_WIDTH)
    z_att = proj(OFF_Q, ATT_WIDTH + 2 * KV_WIDTH)

    def gated_branch(i, o_branch, first):
        for n in range(D_MODEL // MXU_COLS):
            cols = slice(n * MXU_COLS, (n + 1) * MXU_COLS)
            g0 = OFF_GATE + i * D_MODEL + n * MXU_COLS
            piece = _sigmoid_times(_dot(hb, win_ref[:, g0:g0 + MXU_COLS]),
                                   _dot(o_branch, wb_ref[i, :, cols]))
            if first:
                acc_s[:, cols] = piece
            else:
                acc_s[:, cols] += piece

    y = jnp.concatenate([conv_tile(ci) for ci in range(CONV_WIDTH // LANES)], axis=1)
    abuf[0:HALO, :] = abuf[tm:tm + HALO, :]
    o_cv = _layer_norm_silu(y, clg_ref[...], clb_ref[...]).astype(BF)
    gated_branch(1, o_cv, True)

    u = _gelu_half(z_gm[:, 0:GM_WIDTH])
    gv = _gelu_half(z_gm[:, GM_WIDTH:2 * GM_WIDTH])
    v = gv * lax.rsqrt(jnp.mean(gv * gv, axis=-1, keepdims=True) + EPS) * gmg_ref[...]
    vb = v.astype(BF)
    row = lax.broadcasted_iota(jnp.int32, (CHUNK, CHUNK), 0)
    col = lax.broadcasted_iota(jnp.int32, (CHUNK, CHUNK), 1)
    ws = [jnp.where(row >= col, gmws_ref[g], 0.0).astype(BF) for g in range(GM_GROUPS)]
    gmb = gmb_ref[...]
    chunks = []
    for c in range(nblk):
        parts = [_dot(ws[g], vb[c * CHUNK:(c + 1) * CHUNK, g * LANES:(g + 1) * LANES])
                 for g in range(GM_GROUPS)]
        chunks.append(jnp.concatenate(parts, axis=1) + gmb)
    o_gm = (u * jnp.concatenate(chunks, axis=0)).astype(BF)
    gated_branch(0, o_gm, False)

    rbase = ropeb_ref[...]
    roff = ropeo_ref[...]
    cb_, sb_ = rbase[:, 0:LANES], rbase[:, LANES:2 * LANES]
    co_, so_ = roff[:, 0:LANES], roff[:, LANES:2 * LANES]
    rc = cb_ * co_ - sb_ * so_
    rs = (sb_ * co_ + cb_ * so_) * sign_ref[...]
    bd = bd_ref[...]
    qb = _rope(_head_norm(z_att[:, 0:ATT_WIDTH], bd, qn_ref[...]), rc, rs).astype(BF)
    k_nat = _rope(_head_norm(z_att[:, ATT_WIDTH:ATT_WIDTH + KV_WIDTH], bd[0:LANES, 0:LANES],
                             kn_ref[...]), rc, rs)
    v_nat = z_att[:, ATT_WIDTH + KV_WIDTH:ATT_WIDTH + 2 * KV_WIDTH]
    lo_kv = _low_half((tm, LANES))

    def doubled(x_nat):
        swapped = pltpu.roll(x_nat, HEAD_DIM, 1)
        return jnp.concatenate([jnp.where(lo_kv, x_nat, swapped), jnp.where(lo_kv, swapped, x_nat)],
                               axis=1)

    kd = doubled(k_nat)
    vd = doubled(v_nat)
    kd_s[WINDOW:WINDOW + tm, :] = kd.astype(BF)
    vd_s[WINDOW:WINDOW + tm, :] = vd.astype(BF)

    lo_q = _low_half((WINDOW, LANES))
    zero_q = jnp.zeros((WINDOW, LANES), BF)
    band = bias_ref[1]
    for bi in range(nblk):
        bias = jnp.where(t == 0, bias_ref[0], band) if bi == 0 else band
        bias4 = jnp.concatenate([bias] * Q_REP, axis=0)
        for g in range(N_KV_HEADS):
            tiles = [qb[bi * WINDOW:(bi + 1) * WINDOW, (2 * g + j) * LANES:(2 * g + j + 1) * LANES]
                     for j in range(2)]
            qs = jnp.concatenate([jnp.where(lo_q, tiles[0], zero_q), jnp.where(lo_q, zero_q, tiles[0]),
                                  jnp.where(lo_q, tiles[1], zero_q), jnp.where(lo_q, zero_q, tiles[1])],
                                 axis=0)
            keys = kd_s[bi * WINDOW:(bi + 2) * WINDOW, g * LANES:(g + 1) * LANES]
            vals = vd_s[bi * WINDOW:(bi + 2) * WINDOW, g * LANES:(g + 1) * LANES]
            s = _dot_nt(qs, keys) + bias4
            outs = []
            for hh in range(Q_REP):
                sink = sinks_ref[layer, Q_REP * g + hh]
                sh = s[hh * WINDOW:(hh + 1) * WINDOW, :]
                m = jnp.maximum(jnp.max(sh, axis=-1, keepdims=True), sink)
                p = jnp.exp(sh - m)
                den = jnp.sum(p, axis=-1, keepdims=True) + jnp.exp(sink - m)
                outs.append(_dot(p.astype(BF), vals) / den)
            for j in range(2):
                oatt_s[bi * WINDOW:(bi + 1) * WINDOW, (2 * g + j) * LANES:(2 * g + j + 1) * LANES] = (
                    jnp.where(lo_q, outs[2 * j], outs[2 * j + 1]).astype(BF))
    kd_s[0:WINDOW, :] = kd_s[tm:tm + WINDOW, :]
    vd_s[0:WINDOW, :] = vd_s[tm:tm + WINDOW, :]
    gated_branch(2, oatt_s[...], False)

    accb = acc_s[...].astype(BF)
    for n in range(D_MODEL // MXU_COLS):
        cols = slice(n * MXU_COLS, (n + 1) * MXU_COLS)
        x1_ref[:, cols] = x_ref[:, cols] + gt1[:, cols] * _dot(accb, wout_ref[:, cols])

    @pl.when(t == last)
    def _():
        kwin_ref[...] = k_nat[tm - WINDOW:tm, :]
        vwin_ref[...] = v_nat[tm - WINDOW:tm, :]
        cst_ref[...] = a[tm - (CONV_K - 1):tm, :]


def _mixer_call(layer, x, mod, p, tm):
    nb, seq, _ = x.shape
    nt = seq // tm
    lsel3 = lambda b, t: (layer, 0, 0)
    kernel = functools.partial(_mixer_kernel, layer=layer, tm=tm)
    return pl.pallas_call(
        kernel,
        out_shape=(
            jax.ShapeDtypeStruct((nb, seq, D_MODEL), F32),
            jax.ShapeDtypeStruct((nb, WINDOW, KV_WIDTH), F32),
            jax.ShapeDtypeStruct((nb, WINDOW, KV_WIDTH), F32),
            jax.ShapeDtypeStruct((nb, CONV_K - 1, CONV_WIDTH), F32),
        ),
        grid=(nb, nt),
        in_specs=[
            pl.BlockSpec(memory_space=pltpu.SMEM),
            pl.BlockSpec((None, tm, D_MODEL), lambda b, t: (b, t, 0)),
            pl.BlockSpec((None, SUBLANES, 3 * D_MODEL),
                         lambda b, t: (layer, PROMPT_MOD_ROW // SUBLANES, 0)),
            _const_spec((None, 1, D_MODEL), lsel3),
            _const_spec((None, D_MODEL, IN_WIDTH), lsel3),
            _const_spec((None, 1, GM_WIDTH), lsel3),
            _const_spec((None, GM_GROUPS, CHUNK, CHUNK), lambda b, t: (layer, 0, 0, 0)),
            _const_spec((None, CHUNK, GM_WIDTH), lsel3),
            _const_spec((None, CONV_K, CONV_WIDTH), lsel3),
            _const_spec((None, 1, CONV_WIDTH), lsel3),
            _const_spec((None, 1, CONV_WIDTH), lsel3),
            _const_spec((None, 1, CONV_WIDTH), lsel3),
            _const_spec((None, 1, ATT_WIDTH), lsel3),
            _const_spec((None, 1, LANES), lsel3),
            pl.BlockSpec((None, 1, 2 * LANES), lambda b, t: (t, 0, 0)),
            _const_spec((tm, 2 * LANES), lambda b, t: (0, 0)),
            _const_spec((1, LANES), lambda b, t: (0, 0)),
            _const_spec((2, WINDOW, 2 * WINDOW), lambda b, t: (0, 0, 0)),
            _const_spec((MXU_COLS, MXU_COLS), lambda b, t: (0, 0)),
            _const_spec((None, N_BRANCH, GM_WIDTH, D_MODEL), lambda b, t: (layer, 0, 0, 0)),
            _const_spec((None, D_MODEL, D_MODEL), lsel3),
        ],
        out_specs=(
            pl.BlockSpec((None, tm, D_MODEL), lambda b, t: (b, t, 0)),
            pl.BlockSpec((None, WINDOW, KV_WIDTH), lambda b, t: (b, 0, 0)),
            pl.BlockSpec((None, WINDOW, KV_WIDTH), lambda b, t: (b, 0, 0)),
            pl.BlockSpec((None, CONV_K - 1, CONV_WIDTH), lambda b, t: (b, 0, 0)),
        ),
        scratch_shapes=[
            pltpu.VMEM((WINDOW + tm, 2 * LANES), BF),
            pltpu.VMEM((WINDOW + tm, 2 * LANES), BF),
            pltpu.VMEM((HALO + tm + SUBLANES, CONV_WIDTH), F32),
            pltpu.VMEM((tm, ATT_WIDTH), BF),
            pltpu.VMEM((tm, D_MODEL), F32),
        ],
        compiler_params=pltpu.CompilerParams(
            dimension_semantics=("arbitrary", "arbitrary"),
            vmem_limit_bytes=VMEM_LIMIT),
        name=f"prompt_mixers_l{layer}",
    )(p["sinks"], x, mod, p["n1g"], p["w_in"], p["gm_g"], p["gm_ws"], p["gm_bias"],
      p["conv_dw"], p["conv_b"], p["conv_ln_g"], p["conv_ln_b"], p["qn"], p["kn"],
      p["rope_base"], p["rope_off"], p["rope_sign"], p["band_bias"], p["blockdiag"],
      p["w_branch"], p["w_out"])


def _route_rows(logits_t, rb_ref):
    scores = _sigmoid(logits_t)
    biased = scores + rb_ref[...]
    rows = lambda a, gi: [a[EXPERTS_PER_GROUP * gi + k:EXPERTS_PER_GROUP * gi + k + 1, :]
                          for k in range(EXPERTS_PER_GROUP)]
    best = None
    idx = None
    for gi in range(N_GROUPS):
        b = rows(biased, gi)
        hi1, lo1 = jnp.maximum(b[0], b[1]), jnp.minimum(b[0], b[1])
        hi2, lo2 = jnp.maximum(b[2], b[3]), jnp.minimum(b[2], b[3])
        gs = jnp.maximum(hi1, hi2) + jnp.maximum(jnp.minimum(hi1, hi2), jnp.maximum(lo1, lo2))
        if gi == 0:
            best, idx = gs, jnp.zeros(gs.shape, jnp.int32)
        else:
            better = gs > best
            idx = jnp.where(better, gi, idx)
            best = jnp.where(better, gs, best)
    bsel = rows(biased, 0)
    ssel = rows(scores, 0)
    for gi in range(1, N_GROUPS):
        bg, sg = rows(biased, gi), rows(scores, gi)
        pick = idx == gi
        bsel = [jnp.where(pick, bg[k], bsel[k]) for k in range(EXPERTS_PER_GROUP)]
        ssel = [jnp.where(pick, sg[k], ssel[k]) for k in range(EXPERTS_PER_GROUP)]
    chosen = []
    for k in range(EXPERTS_PER_GROUP):
        rank = jnp.zeros(idx.shape, jnp.int32)
        for k2 in range(EXPERTS_PER_GROUP):
            if k2 == k:
                continue
            beats = (bsel[k2] > bsel[k]) | ((bsel[k2] == bsel[k]) & (k2 < k))
            rank = rank + beats.astype(jnp.int32)
        chosen.append(jnp.where(rank < 2, ssel[k], 0.0))
    den = chosen[0] + chosen[1] + chosen[2] + chosen[3]
    return idx, [c / den for c in chosen]


def _moe_route(x, mod, n2g_ref, rw2_ref, rb_ref, upper_ref, h_buf, rt_buf, c_buf, cnt_s, buf, tm):
    h2 = _modulate(x, n2g_ref[...], mod[:, 0:D_MODEL], mod[:, D_MODEL:2 * D_MODEL])
    hi = h2.astype(BF)
    lo = (h2 - hi.astype(F32)).astype(BF)
    rw2 = rw2_ref[...]
    lt = _dot_nt(rw2, hi) + _dot_nt(rw2, lo)
    idx, comb = _route_rows(lt[0:N_EXPERTS, :] + lt[N_EXPERTS:2 * N_EXPERTS, :], rb_ref)

    onehot = [(idx == g).astype(F32) for g in range(N_GROUPS)]
    oh8 = jnp.concatenate(onehot + [jnp.zeros((SUBLANES - N_GROUPS, tm), F32)], axis=0)
    prefix = _dot(oh8.astype(BF), upper_ref[...])
    slot = onehot[0] * prefix[0:1, :]
    for g in range(1, N_GROUPS):
        slot = slot + onehot[g] * prefix[g:g + 1, :]
    rt_buf[0:1, :] = idx
    rt_buf[1:2, :] = slot.astype(jnp.int32)
    for g in range(N_GROUPS):
        cnt_s[buf, g] = jnp.sum(onehot[g]).astype(jnp.int32)

    for k in range(EXPERTS_PER_GROUP):
        c_buf[k:k + 1, :] = comb[k]
    h_buf[...] = hi


def _moe_experts(x_ref, gt2, w1_ref, w3_ref, w2_ref, o_ref, h_buf, rt_buf, c_buf, cnt_s, buf,
                 tm, cap, overflow):
    def dispatch(g, j):
        want = lax.broadcasted_iota(jnp.int32, (cap, tm), 0) + j * cap
        hit = (rt_buf[1:2, :] == want) & (rt_buf[0:1, :] == g)
        return hit, jnp.where(hit, 1.0, 0.0).astype(BF)

    def run_experts(g, hit, hg):
        parts = []
        for e in range(EXPERTS_PER_GROUP):
            a = _dot(hg, w1_ref[EXPERTS_PER_GROUP * g + e])
            b = _dot(hg, w3_ref[EXPERTS_PER_GROUP * g + e])
            ce = jnp.sum(jnp.where(hit, c_buf[e:e + 1, :], 0.0), axis=-1, keepdims=True)
            parts.append(_silu(a) * b * ce)
        act = jnp.concatenate(parts, axis=1).astype(BF)
        return _dot(act, w2_ref[g]).astype(BF)

    def scatter(pmat, y):
        return lax.dot_general(pmat, y, (((0,), (0,)), ((), ())), preferred_element_type=F32)

    def gather(pmat):
        return _dot(pmat, h_buf[...]).astype(BF)

    if not overflow:
        sel = [dispatch(g, 0) for g in range(N_GROUPS)]
        pmat_all = jnp.concatenate([s[1] for s in sel], axis=0)
        hg_all = gather(pmat_all)
        ys = [run_experts(g, sel[g][0], hg_all[g * cap:(g + 1) * cap, :]) for g in range(N_GROUPS)]
        y_tile = scatter(pmat_all, jnp.concatenate(ys, axis=0))
        o_ref[...] = x_ref[...] + gt2 * y_tile
        return

    def group_body(g, carry):
        nblk = (cnt_s[buf, g] + (cap - 1)) // cap

        def block_body(j, carry2):
            hit, pmat = dispatch(g, j)
            o_ref[...] += gt2 * scatter(pmat, run_experts(g, hit, gather(pmat)))
            return carry2

        lax.fori_loop(1, nblk, block_body, 0)
        return carry

    lax.fori_loop(0, N_GROUPS, group_body, 0)


def _moe_kernel(x_ref, xn_ref, mod_ref, n2g_ref, rw2_ref, rb_ref, upper_ref,
                w1_ref, w3_ref, w2_ref, o_ref, h0_s, h1_s, rt0_s, rt1_s, c0_s, c1_s, cnt_s,
                *, tm, cap, tiles_per_row, n_tiles):
    route = functools.partial(_moe_route, n2g_ref=n2g_ref, rw2_ref=rw2_ref, rb_ref=rb_ref,
                              upper_ref=upper_ref, cnt_s=cnt_s, tm=tm)
    experts = functools.partial(_moe_experts, x_ref, w1_ref=w1_ref,
                                w3_ref=w3_ref, w2_ref=w2_ref, o_ref=o_ref, cnt_s=cnt_s, tm=tm, cap=cap)
    bufs = [dict(h_buf=h0_s, rt_buf=rt0_s, c_buf=c0_s, buf=0),
            dict(h_buf=h1_s, rt_buf=rt1_s, c_buf=c1_s, buf=1)]
    if tiles_per_row is None:
        mod = mod_ref[...]
        route(x_ref[...], mod, **bufs[0])
        for overflow in (False, True):
            experts(mod[:, 2 * D_MODEL:3 * D_MODEL], overflow=overflow, **bufs[0])
        return

    i = pl.program_id(0)
    mod_row = lambda tile: mod_ref[pl.ds(tile // tiles_per_row, 1), :]

    @pl.when(i == 0)
    def _():
        route(x_ref[...], mod_row(i), **bufs[0])

    def step(cur):
        gt2 = mod_row(i)[:, 2 * D_MODEL:3 * D_MODEL]
        experts(gt2, overflow=False, **bufs[cur])
        route(xn_ref[...], mod_row(jnp.minimum(i + 1, n_tiles - 1)), **bufs[1 - cur])
        experts(gt2, overflow=True, **bufs[cur])

    @pl.when(i % 2 == 0)
    def _():
        step(0)

    @pl.when(i % 2 == 1)
    def _():
        step(1)


def _moe_call(layer, x2d, mod, tiles_per_row, p, tm, name):
    n = x2d.shape[0]
    n_tiles = n // tm
    if tiles_per_row is None:
        assert n_tiles == 1
        mod_spec = pl.BlockSpec((None, n, 3 * D_MODEL), lambda i: (layer, 0, 1))
    else:
        mod_spec = pl.BlockSpec((None, SUBLANES, 3 * D_MODEL),
                                lambda i: (layer, PROMPT_MOD_ROW // SUBLANES, 1))
    lsel3 = lambda i: (layer, 0, 0)
    lsel4 = lambda i: (layer, 0, 0, 0)
    upper = np.triu(np.ones((tm, tm), np.float32), k=1)
    kernel = functools.partial(_moe_kernel, tm=tm, cap=min(MOE_CAP, tm), tiles_per_row=tiles_per_row,
                               n_tiles=n_tiles)
    return pl.pallas_call(
        kernel,
        out_shape=jax.ShapeDtypeStruct((n, D_MODEL), F32),
        grid=(n_tiles,),
        in_specs=[
            pl.BlockSpec((tm, D_MODEL), lambda i: (i, 0)),
            pl.BlockSpec((tm, D_MODEL), lambda i: (jnp.minimum(i + 1, n_tiles - 1), 0)),
            mod_spec,
            _const_spec((None, 1, D_MODEL), lsel3),
            _const_spec((2 * N_EXPERTS, D_MODEL), lambda i: (0, 0)),
            _const_spec((N_EXPERTS, 1), lambda i: (0, 0)),
            _const_spec((tm, tm), lambda i: (0, 0)),
            _const_spec((None, N_EXPERTS, D_MODEL, EXPERT_FF), lsel4),
            _const_spec((None, N_EXPERTS, D_MODEL, EXPERT_FF), lsel4),
            _const_spec((None, N_GROUPS, EXPERTS_PER_GROUP * EXPERT_FF, D_MODEL), lsel4),
        ],
        out_specs=pl.BlockSpec((tm, D_MODEL), lambda i: (i, 0)),
        scratch_shapes=[
            pltpu.VMEM((tm, D_MODEL), BF),
            pltpu.VMEM((tm, D_MODEL), BF),
            pltpu.VMEM((SUBLANES, tm), jnp.int32),
            pltpu.VMEM((SUBLANES, tm), jnp.int32),
            pltpu.VMEM((SUBLANES, tm), F32),
            pltpu.VMEM((SUBLANES, tm), F32),
            pltpu.SMEM((2, N_GROUPS), jnp.int32),
        ],
        compiler_params=pltpu.CompilerParams(
            dimension_semantics=("arbitrary",),
            vmem_limit_bytes=VMEM_LIMIT),
        name=name,
    )(x2d, x2d, mod, p["n2g"], p["rw2"], p["rb"], jnp.asarray(upper, dtype=BF),
      p["moe_w1"], p["moe_w3"], p["moe_w2"])


def _sample_proj_kernel(x_ref, mod_ref, n1g_ref, w_ref, z_ref):
    mod = mod_ref[...]
    h = _modulate(x_ref[...], n1g_ref[...], mod[:, 0:D_MODEL], mod[:, D_MODEL:2 * D_MODEL])
    z_ref[...] = _dot(h.astype(BF), w_ref[...])


def _sample_proj_call(layer, xs, mod, p):
    n = xs.shape[0]
    return pl.pallas_call(
        _sample_proj_kernel,
        out_shape=jax.ShapeDtypeStruct((n, IN_WIDTH), F32),
        grid=(IN_WIDTH // PROJ_BLOCK,),
        in_specs=[
            pl.BlockSpec((n, D_MODEL), lambda j: (0, 0)),
            pl.BlockSpec((None, n, 3 * D_MODEL), lambda j: (layer, 0, 0)),
            pl.BlockSpec((None, 1, D_MODEL), lambda j: (layer, 0, 0)),
            pl.BlockSpec((None, D_MODEL, PROJ_BLOCK), lambda j: (layer, 0, j)),
        ],
        out_specs=pl.BlockSpec((n, PROJ_BLOCK), lambda j: (0, j)),
        compiler_params=pltpu.CompilerParams(dimension_semantics=("arbitrary",)),
        name=f"sample_proj_l{layer}",
    )(xs, mod, p["n1g"], p["w_in"])


def _shift_in_column(cache_t, new_rows, bb):
    flat = cache_t.reshape(bb * HEAD_DIM, WINDOW)
    shifted = pltpu.roll(flat, WINDOW - 1, 1).reshape(bb, HEAD_DIM, WINDOW)
    padded = jnp.concatenate([new_rows, jnp.zeros((LANES - bb, LANES), F32)], axis=0)
    new_t = padded.T
    is_last = lax.broadcasted_iota(jnp.int32, (HEAD_DIM, WINDOW), 1) == WINDOW - 1
    out = []
    for b in range(bb):
        col = new_t[0:HEAD_DIM, b:b + 1]
        out.append(jnp.where(is_last, col, shifted[b]))
    return out


def _sample_mixer_kernel(z_ref, x_ref, mod_ref, kt_ref, vt_ref, st_ref,
                         gmg_ref, gmw_ref, gmb_ref, cdw_ref, cb_ref, clg_ref, clb_ref,
                         qn_ref, kn_ref, ropec_ref, ropes_ref, sink_ref, bd_ref, wb_ref, wout_ref,
                         _k_all, _v_all, _st_all,
                         x1_ref, kto_ref, vto_ref, sto_ref, gv_ref,
                         qf_s, of_s, oc_s, zg_s, *, bb):
    i = pl.program_id(0)
    r0 = pl.multiple_of(i * bb, bb)
    z = z_ref[...]
    zg_s[pl.ds(r0, bb), :] = z[:, OFF_GATE:OFF_GATE + N_BRANCH * D_MODEL]

    u = _gelu_half(z[:, OFF_GM_U:OFF_GM_U + GM_WIDTH])
    gv = _gelu_half(z[:, OFF_GM_V:OFF_GM_V + GM_WIDTH])
    v = gv * lax.rsqrt(jnp.mean(gv * gv, axis=-1, keepdims=True) + EPS) * gmg_ref[...]
    gv_ref[...] = v
    o_gm = u * (v * gmw_ref[...] + gmb_ref[...])

    a = _sigmoid_times(z[:, OFF_CV_G:OFF_CV_G + CONV_WIDTH], z[:, OFF_CV_A:OFF_CV_A + CONV_WIDTH])
    y = cb_ref[...] + cdw_ref[CONV_K - 1:CONV_K, :] * a
    for j in range(CONV_K - 1):
        y = y + cdw_ref[j:j + 1, :] * st_ref[j]
    o_cv = _layer_norm_silu(y, clg_ref[...], clb_ref[...])
    sto_ref[0:CONV_K - 2] = st_ref[1:CONV_K - 1]
    sto_ref[CONV_K - 2] = a

    rc = ropec_ref[...]
    rs = ropes_ref[...]
    bd = bd_ref[...]
    q = _rope(_head_norm(z[:, OFF_Q:OFF_Q + ATT_WIDTH], bd, qn_ref[...]), rc, rs)
    knew = _rope(_head_norm(z[:, OFF_K:OFF_K + KV_WIDTH], bd[0:LANES, 0:LANES], kn_ref[...]), rc, rs)
    vnew = z[:, OFF_V:OFF_V + KV_WIDTH]
    lo = _low_half((bb, LANES))
    for h in range(N_HEADS):
        tile = q[:, (h // 2) * LANES:(h // 2 + 1) * LANES]
        if h % 2 == 1:
            tile = pltpu.roll(tile, HEAD_DIM, 1)
        qf_s[:, h, :] = jnp.where(lo, tile, 0.0)
    k_low = [knew, pltpu.roll(knew, HEAD_DIM, 1)]
    v_low = [vnew, pltpu.roll(vnew, HEAD_DIM, 1)]
    for g in range(N_KV_HEADS):
        qg = qf_s[:, Q_REP * g:Q_REP * (g + 1), :]
        kt = kt_ref[:, g]
        vt = vt_ref[:, g]
        s = jnp.einsum("brd,bdp->brp", qg[:, :, 0:HEAD_DIM].astype(BF), kt.astype(BF),
                       preferred_element_type=F32)
        s_new = jnp.sum(qg * k_low[g][:, None, :], axis=-1, keepdims=True)
        sink = sink_ref[Q_REP * g:Q_REP * (g + 1), 0:1][None]
        m = jnp.maximum(jnp.maximum(jnp.max(s, axis=-1, keepdims=True), s_new), sink)
        pr = jnp.exp(s - m)
        p_new = jnp.exp(s_new - m)
        den = jnp.sum(pr, axis=-1, keepdims=True) + p_new + jnp.exp(sink - m)
        o = jnp.einsum("brp,bdp->brd", pr.astype(BF), vt.astype(BF), preferred_element_type=F32)
        o = (o + p_new * v_low[g][:, None, 0:HEAD_DIM]) / den
        of_s[:, Q_REP * g:Q_REP * (g + 1), :] = jnp.concatenate([o, jnp.zeros(o.shape, F32)], axis=-1)
        for b, tile in enumerate(_shift_in_column(kt, k_low[g], bb)):
            kto_ref[b, g] = tile
        for b, tile in enumerate(_shift_in_column(vt, v_low[g], bb)):
            vto_ref[b, g] = tile
    att_tiles = []
    for j in range(N_HEADS // 2):
        second = pltpu.roll(of_s[:, 2 * j + 1, :], HEAD_DIM, 1)
        att_tiles.append(jnp.where(lo, of_s[:, 2 * j, :], second))
    o_att = jnp.concatenate(att_tiles, axis=1)

    oc_s[pl.ds(r0, bb), 0:GM_WIDTH] = o_gm.astype(BF)
    oc_s[pl.ds(r0, bb), GM_WIDTH:GM_WIDTH + CONV_WIDTH] = o_cv.astype(BF)
    oc_s[pl.ds(r0, bb), GM_WIDTH + CONV_WIDTH:GM_WIDTH + CONV_WIDTH + ATT_WIDTH] = o_att.astype(BF)

    @pl.when(i == pl.num_programs(0) - 1)
    def _():
        acc = _sigmoid_times(zg_s[:, 0:D_MODEL], _dot(oc_s[:, 0:GM_WIDTH], wb_ref[0]))
        acc = acc + _sigmoid_times(zg_s[:, D_MODEL:2 * D_MODEL], _dot(
            oc_s[:, GM_WIDTH:GM_WIDTH + CONV_WIDTH], wb_ref[1]))
        acc = acc + _sigmoid_times(zg_s[:, 2 * D_MODEL:3 * D_MODEL], _dot(
            oc_s[:, GM_WIDTH + CONV_WIDTH:GM_WIDTH + CONV_WIDTH + ATT_WIDTH], wb_ref[2]))
        gt1 = mod_ref[...][:, 2 * D_MODEL:3 * D_MODEL]
        x1_ref[...] = x_ref[...] + gt1 * _dot(acc.astype(BF), wout_ref[...])


def _sample_mixer_call(layer, z, xs, mod, kt, vt, st, new_k, new_v, new_st, p, bb):
    n = xs.shape[0]
    lsel3 = lambda i: (layer, 0, 0)
    cache_block = (bb, N_KV_HEADS, HEAD_DIM, WINDOW)
    kernel = functools.partial(_sample_mixer_kernel, bb=bb)
    args = (z, xs, mod, kt, vt, st, p["gm_g"], p["gm_w0"], p["gm_b0"], p["conv_dw"], p["conv_b"],
            p["conv_ln_g"], p["conv_ln_b"], p["qn"], p["kn"], p["rope_c1"], p["rope_s1"],
            p["sink_lanes"], p["blockdiag"], p["w_branch"], p["w_out"], new_k, new_v, new_st)
    n_in = len(args)
    return pl.pallas_call(
        kernel,
        out_shape=(
            jax.ShapeDtypeStruct((n, D_MODEL), F32),
            jax.ShapeDtypeStruct(new_k.shape, F32),
            jax.ShapeDtypeStruct(new_v.shape, F32),
            jax.ShapeDtypeStruct(new_st.shape, F32),
            jax.ShapeDtypeStruct((n, GM_WIDTH), F32),
        ),
        input_output_aliases={n_in - 3: 1, n_in - 2: 2, n_in - 1: 3},
        grid=(n // bb,),
        in_specs=[
            pl.BlockSpec((bb, IN_WIDTH), lambda i: (i, 0)),
            pl.BlockSpec((n, D_MODEL), lambda i: (0, 0)),
            pl.BlockSpec((None, n, 3 * D_MODEL), lsel3),
            pl.BlockSpec((None,) + cache_block, lambda i: (layer, i, 0, 0, 0)),
            pl.BlockSpec((None,) + cache_block, lambda i: (layer, i, 0, 0, 0)),
            pl.BlockSpec((None, CONV_K - 1, bb, CONV_WIDTH), lambda i: (layer, 0, i, 0)),
            pl.BlockSpec((None, 1, GM_WIDTH), lsel3),
            pl.BlockSpec((None, 1, GM_WIDTH), lsel3),
            pl.BlockSpec((None, 1, GM_WIDTH), lsel3),
            pl.BlockSpec((None, CONV_K, CONV_WIDTH), lsel3),
            pl.BlockSpec((None, 1, CONV_WIDTH), lsel3),
            pl.BlockSpec((None, 1, CONV_WIDTH), lsel3),
            pl.BlockSpec((None, 1, CONV_WIDTH), lsel3),
            pl.BlockSpec((None, 1, ATT_WIDTH), lsel3),
            pl.BlockSpec((None, 1, LANES), lsel3),
            pl.BlockSpec((1, LANES), lambda i: (0, 0)),
            pl.BlockSpec((1, LANES), lambda i: (0, 0)),
            pl.BlockSpec((None, N_HEADS, LANES), lsel3),
            pl.BlockSpec((MXU_COLS, MXU_COLS), lambda i: (0, 0)),
            pl.BlockSpec((None, N_BRANCH, GM_WIDTH, D_MODEL), lambda i: (layer, 0, 0, 0)),
            pl.BlockSpec((None, D_MODEL, D_MODEL), lsel3),
            pl.BlockSpec(memory_space=pl.ANY),
            pl.BlockSpec(memory_space=pl.ANY),
            pl.BlockSpec(memory_space=pl.ANY),
        ],
        out_specs=(
            pl.BlockSpec((n, D_MODEL), lambda i: (0, 0)),
            pl.BlockSpec((None,) + cache_block, lambda i: (layer, i, 0, 0, 0)),
            pl.BlockSpec((None,) + cache_block, lambda i: (layer, i, 0, 0, 0)),
            pl.BlockSpec((None, CONV_K - 1, bb, CONV_WIDTH), lambda i: (layer, 0, i, 0)),
            pl.BlockSpec((bb, GM_WIDTH), lambda i: (i, 0)),
        ),
        scratch_shapes=[
            pltpu.VMEM((bb, N_HEADS, LANES), F32),
            pltpu.VMEM((bb, N_HEADS, LANES), F32),
            pltpu.VMEM((n, GM_WIDTH + CONV_WIDTH + ATT_WIDTH), BF),
            pltpu.VMEM((n, N_BRANCH * D_MODEL), F32),
        ],
        compiler_params=pltpu.CompilerParams(
            dimension_semantics=("arbitrary",), vmem_limit_bytes=VMEM_LIMIT),
        name=f"sample_mixers_l{layer}",
    )(*args)


def _rope_lane_tables():
    half = ROT_DIM // 2
    freqs = jnp.exp(-math.log(ROPE_THETA) * jnp.arange(half, dtype=F32) * (2.0 / ROT_DIM))
    rest = jnp.zeros((HEAD_DIM - ROT_DIM,), F32)
    freq64 = jnp.concatenate([freqs, freqs, rest])
    sign64 = jnp.concatenate([-jnp.ones((half,), F32), jnp.ones((half,), F32), rest])
    reps = LANES // HEAD_DIM
    return jnp.tile(freq64, reps)[None, :], jnp.tile(sign64, reps)[None, :]


def _cos_sin(pos, lane_freq):
    ang = pos.astype(F32)[:, None] * lane_freq
    return jnp.cos(ang), jnp.sin(ang)


def _band_bias():
    i = np.arange(WINDOW)[:, None]
    j = np.arange(2 * WINDOW)[None, :]
    band = (j >= i) & (j <= i + WINDOW)
    first = band & (j >= WINDOW)
    out = np.where(np.stack([first, band]), 0.0, NEG_BIG).astype(np.float32)
    return jnp.asarray(out)


def _blockdiag():
    idx = np.arange(MXU_COLS) // HEAD_DIM
    return jnp.asarray((idx[:, None] == idx[None, :]).astype(np.float32), dtype=BF)


def _router_hi_lo(router_w):
    rw = router_w.astype(F32).T
    rw_hi = rw.astype(BF)
    rw_lo = (rw - rw_hi.astype(F32)).astype(BF)
    return jnp.concatenate([rw_hi, rw_lo], axis=0)


def _prepare(norm1_g, norm2_g, w_in, gm_norm_g, gm_ws, gm_b, conv_dw, conv_b, conv_ln_g,
             conv_ln_b, q_norm_g, k_norm_g, attn_sinks, w_branch, w_out, router_w, router_b,
             moe_w1, moe_w3, moe_w2, seq, tm):
    lane_freq, lane_sign = _rope_lane_tables()
    cb, sb = _cos_sin(jnp.arange(seq // tm, dtype=jnp.int32) * tm, lane_freq)
    co, so = _cos_sin(jnp.arange(tm, dtype=jnp.int32), lane_freq)
    c1, s1 = _cos_sin(PAST_LEN + jnp.arange(1, dtype=jnp.int32), lane_freq)
    row3 = lambda a: a.reshape(DEPTH, 1, a.shape[-1])
    col = np.arange(IN_WIDTH)
    col_scale = jnp.asarray(np.where((col >= OFF_Q) & (col < OFF_GATE), 1.0, 0.5), F32)
    return {
        "n1g": row3(norm1_g), "n2g": row3(norm2_g), "w_in": (w_in * col_scale).astype(BF),
        "gm_g": row3(gm_norm_g), "gm_ws": gm_ws,
        "gm_bias": jnp.repeat(jnp.swapaxes(gm_b, 1, 2), LANES, axis=2),
        "gm_w0": jnp.repeat(gm_ws[:, :, 0, 0], LANES, axis=1).reshape(DEPTH, 1, GM_WIDTH),
        "gm_b0": jnp.repeat(gm_b[:, :, 0], LANES, axis=1).reshape(DEPTH, 1, GM_WIDTH),
        "conv_dw": conv_dw, "conv_b": row3(conv_b), "conv_ln_g": row3(0.5 * conv_ln_g),
        "conv_ln_b": row3(0.5 * conv_ln_b),
        "qn": row3(jnp.tile(q_norm_g, (1, N_HEADS)) * (HEAD_DIM ** -0.5)),
        "kn": row3(jnp.tile(k_norm_g, (1, N_KV_HEADS))),
        "sinks": attn_sinks,
        "sink_lanes": jnp.broadcast_to(attn_sinks[:, :, None], (DEPTH, N_HEADS, LANES)),
        "rope_base": jnp.concatenate([cb, sb], axis=1)[:, None, :],
        "rope_off": jnp.concatenate([co, so], axis=1),
        "rope_sign": lane_sign, "rope_c1": c1, "rope_s1": s1 * lane_sign,
        "band_bias": _band_bias(), "blockdiag": _blockdiag(),
        "w_branch": (0.5 * w_branch).astype(BF), "w_out": w_out.astype(BF),
        "rw2": _router_hi_lo(router_w),
        "rb": router_b.astype(F32).reshape(N_EXPERTS, 1),
        "moe_w1": moe_w1.astype(BF), "moe_w3": moe_w3.astype(BF),
        "moe_w2": moe_w2.astype(BF).reshape(DEPTH, N_GROUPS, EXPERTS_PER_GROUP * EXPERT_FF, D_MODEL),
    }


def kernel(x_prompt, x_sample, cache_win_k, cache_win_v, state_conv, c_prompt, c_sample, norm1_g, norm2_g, w_ada, b_ada, w_in, gm_norm_g, gm_ws, gm_b, conv_dw, conv_b, conv_ln_g, conv_ln_b, q_norm_g, k_norm_g, attn_sinks, w_branch, w_out, router_w, router_b, moe_w1, moe_w3, moe_w2):
    nb, seq, _ = x_prompt.shape
    ns = x_sample.shape[0]
    tm = min(MIX_TILE, seq)
    tmoe = min(MOE_TILE, seq)
    p = _prepare(norm1_g, norm2_g, w_in, gm_norm_g, gm_ws, gm_b, conv_dw, conv_b, conv_ln_g,
                 conv_ln_b, q_norm_g, k_norm_g, attn_sinks, w_branch, w_out, router_w, router_b,
                 moe_w1, moe_w3, moe_w2, seq, tm)
    assert ns == PROMPT_MOD_ROW and nb <= ADA_ROWS - PROMPT_MOD_ROW
    c_all = jnp.concatenate(
        [c_sample, c_prompt, jnp.zeros((ADA_ROWS - nb - ns, D_MODEL), F32)], axis=0)
    mod = _ada_call(c_all, w_ada, b_ada)

    kt = jnp.transpose(cache_win_k, (0, 1, 3, 4, 2))
    vt = jnp.transpose(cache_win_v, (0, 1, 3, 4, 2))
    st = jnp.transpose(state_conv, (0, 2, 1, 3))

    xp = x_prompt
    xs = x_sample.reshape(ns, D_MODEL)
    kp_l, vp_l, cp_l, gs_l = [], [], [], []
    new_k = jnp.zeros(kt.shape, F32)
    new_v = jnp.zeros(vt.shape, F32)
    new_st = jnp.zeros(st.shape, F32)
    for l in range(DEPTH):
        xp, kp, vp, cp = _mixer_call(l, xp, mod, p, tm)
        xp = _moe_call(l, xp.reshape(nb * seq, D_MODEL), mod, seq // tmoe, p, tmoe,
                       f"prompt_moe_l{l}").reshape(nb, seq, D_MODEL)
        kp_l.append(kp)
        vp_l.append(vp)
        cp_l.append(cp)

        z = _sample_proj_call(l, xs, mod, p)
        xs, new_k, new_v, new_st, gv = _sample_mixer_call(l, z, xs, mod, kt, vt, st,
                                                          new_k, new_v, new_st, p, SAMPLE_BLOCK)
        xs = _moe_call(l, xs, mod, None, p, ns, f"sample_moe_l{l}")
        gs_l.append(gv)

    kv_shape = (DEPTH, nb, WINDOW, N_KV_HEADS, HEAD_DIM)
    return (xp, xs.reshape(ns, 1, D_MODEL),
            jnp.stack(kp_l).reshape(kv_shape), jnp.stack(vp_l).reshape(kv_shape),
            jnp.stack(cp_l),
            jnp.transpose(new_k, (0, 1, 4, 2, 3)),
            jnp.transpose(new_v, (0, 1, 4, 2, 3)),
            jnp.transpose(new_st, (0, 2, 1, 3)),
            jnp.stack(gs_l).reshape(DEPTH, ns, 1, GM_WIDTH))
```

```python
import functools
import math

import jax
import jax.numpy as jnp
import numpy as np
from jax import lax
from jax.experimental import pallas as pl
from jax.experimental.pallas import tpu as pltpu

F32 = jnp.float32
BF = jnp.bfloat16

D_MODEL = 1024
DEPTH = 2
PAST_LEN = 8192
CHUNK = 128
GM_GROUPS = 4
GM_WIDTH = 512
CONV_WIDTH = 512
CONV_K = 31
N_HEADS = 8
N_KV_HEADS = 2
Q_REP = N_HEADS // N_KV_HEADS
HEAD_DIM = 64
ATT_WIDTH = N_HEADS * HEAD_DIM
KV_WIDTH = N_KV_HEADS * HEAD_DIM
WINDOW = 128
ROPE_THETA = 500000.0
ROT_DIM = HEAD_DIM // 4
N_BRANCH = 3
N_EXPERTS = 16
EXPERTS_PER_GROUP = 4
N_GROUPS = N_EXPERTS // EXPERTS_PER_GROUP
EXPERT_FF = 256
EPS = 1e-6

OFF_GM_U = 0
OFF_GM_V = 512
OFF_CV_A = 1024
OFF_CV_G = 1536
OFF_Q = 2048
OFF_K = 2560
OFF_V = 2688
OFF_GATE = 2816
IN_WIDTH = OFF_GATE + N_BRANCH * D_MODEL

LANES = 128
SUBLANES = 8
MXU_COLS = 256
HALO = 32
MIX_TILE = 512
MOE_TILE = 512
MOE_CAP = 160
ADA_ROWS = 136
PROMPT_MOD_ROW = 128
ADA_BLOCK = 1536
PROJ_BLOCK = IN_WIDTH // 2
SAMPLE_BLOCK = 32
VMEM_LIMIT = 56 * 1024 * 1024
NEG_BIG = -1e30


def _dot(a, b):
    return jnp.dot(a, b, preferred_element_type=F32)


def _dot_nt(a, b):
    return lax.dot_general(a, b, (((1,), (1,)), ((), ())), preferred_element_type=F32)


def _sigmoid(x):
    return 0.5 * jnp.tanh(0.5 * x) + 0.5


def _silu(x):
    return x * _sigmoid(x)


GELU_C = 0.7978845608028654
GELU_K = 0.044715


def _gelu_half(xh):
    return xh + xh * jnp.tanh(xh * (2.0 * GELU_C + (8.0 * GELU_C * GELU_K) * (xh * xh)))


def _sigmoid_times(zh, dh):
    return jnp.tanh(zh) * dh + dh


def _modulate(x, g, shift, scale):
    ms = jnp.mean(x * x, axis=-1, keepdims=True)
    return (x * lax.rsqrt(ms + EPS)) * (g * (1.0 + scale)) + shift


def _head_norm(x, blockdiag, g):
    sq = (x * x).astype(BF)
    width = x.shape[-1]
    step = min(width, MXU_COLS)
    ssum = jnp.concatenate([_dot(sq[:, c:c + step], blockdiag[0:step, 0:step])
                            for c in range(0, width, step)], axis=-1)
    return x * lax.rsqrt(ssum * (1.0 / HEAD_DIM) + EPS) * g


def _rope(x, c, s):
    width = x.shape[-1]
    reps = width // LANES
    cc = jnp.concatenate([c] * reps, axis=-1)
    ss = jnp.concatenate([s] * reps, axis=-1)
    lane = lax.broadcasted_iota(jnp.int32, x.shape, x.ndim - 1) % HEAD_DIM
    partner = jnp.where(lane < ROT_DIM // 2,
                        pltpu.roll(x, width - ROT_DIM // 2, x.ndim - 1),
                        pltpu.roll(x, ROT_DIM // 2, x.ndim - 1))
    return x * cc + partner * ss


def _layer_norm_silu(y, gh, bh):
    mu = jnp.mean(y, axis=-1, keepdims=True)
    yc = y - mu
    var = jnp.mean(yc * yc, axis=-1, keepdims=True)
    h = yc * lax.rsqrt(var + EPS) * gh + bh
    return h * jnp.tanh(h) + h


def _low_half(shape):
    return lax.broadcasted_iota(jnp.int32, shape, len(shape) - 1) % LANES < HEAD_DIM


def _const_spec(shape, index_map):
    return pl.BlockSpec(shape, index_map, pipeline_mode=pl.Buffered(1))


def _ada_kernel(c_ref, w_ref, b_ref, o_ref):
    s = _silu(c_ref[...]).astype(BF)
    o_ref[...] = _dot(s, w_ref[...].astype(BF)) + b_ref[...]


def _ada_call(c_all, w_ada, b_ada):
    nb = (6 * D_MODEL) // ADA_BLOCK
    return pl.pallas_call(
        _ada_kernel,
        out_shape=jax.ShapeDtypeStruct((DEPTH, ADA_ROWS, 6 * D_MODEL), F32),
        grid=(DEPTH, nb),
        in_specs=[
            pl.BlockSpec((ADA_ROWS, D_MODEL), lambda l, j: (0, 0)),
            pl.BlockSpec((None, D_MODEL, ADA_BLOCK), lambda l, j: (l, 0, j)),
            pl.BlockSpec((None, 1, ADA_BLOCK), lambda l, j: (l, 0, j)),
        ],
        out_specs=pl.BlockSpec((None, ADA_ROWS, ADA_BLOCK), lambda l, j: (l, 0, j)),
        compiler_params=pltpu.CompilerParams(
            dimension_semantics=("arbitrary", "arbitrary")),
        name="ada_mod",
    )(c_all, w_ada, b_ada.reshape(DEPTH, 1, 6 * D_MODEL))


def _mixer_kernel(sinks_ref, x_ref, mod_ref, n1g_ref, win_ref, gmg_ref, gmws_ref, gmb_ref,
                  cdw_ref, cb_ref, clg_ref, clb_ref, qn_ref, kn_ref, ropeb_ref, ropeo_ref,
                  sign_ref, bias_ref, bd_ref, wb_ref, wout_ref,
                  x1_ref, kwin_ref, vwin_ref, cst_ref,
                  kd_s, vd_s, abuf, oatt_s, acc_s, *, layer, tm):
    t = pl.program_id(1)
    last = pl.num_programs(1) - 1
    nblk = tm // WINDOW

    @pl.when(t == 0)
    def _():
        kd_s[0:WINDOW, :] = jnp.zeros((WINDOW, 2 * LANES), BF)
        vd_s[0:WINDOW, :] = jnp.zeros((WINDOW, 2 * LANES), BF)
        abuf[0:HALO, :] = jnp.zeros((HALO, CONV_WIDTH), F32)
        abuf[HALO + tm:HALO + tm + SUBLANES, :] = jnp.zeros((SUBLANES, CONV_WIDTH), F32)

    x = x_ref[...]
    mod = mod_ref[pl.ds(pl.program_id(0), 1), :]
    sh1 = mod[:, 0:D_MODEL]
    sc1 = mod[:, D_MODEL:2 * D_MODEL]
    gt1 = mod[:, 2 * D_MODEL:3 * D_MODEL]
    hb = _modulate(x, n1g_ref[...], sh1, sc1).astype(BF)

    def proj(off, width):
        return _dot(hb, win_ref[:, off:off + width])

    z_cv = proj(OFF_CV_A, 2 * CONV_WIDTH)
    a = _sigmoid_times(z_cv[:, CONV_WIDTH:2 * CONV_WIDTH], z_cv[:, 0:CONV_WIDTH])
    abuf[HALO:HALO + tm, :] = a
    first_off = HALO - (CONV_K - 1)

    def conv_tile(ci):
        lanes = slice(ci * LANES, (ci + 1) * LANES)
        yt = cb_ref[:, lanes]
        for b in range(SUBLANES):
            part = None
            for off in range(b, HALO + 1, SUBLANES):
                if off < first_off:
                    continue
                term = (cdw_ref[off - first_off:off - first_off + 1, lanes]
                        * abuf[pl.ds(off - b, tm + SUBLANES), lanes])
                part = term if part is None else part + term
            yt = yt + part[b:b + tm, :]
        return yt

    z_gm = proj(OFF_GM_U, 2 * GM_WIDTH)
    z_att = proj(OFF_Q, ATT_WIDTH + 2 * KV_WIDTH)

    def gated_branch(i, o_branch, first):
        for n in range(D_MODEL // MXU_COLS):
            cols = slice(n * MXU_COLS, (n + 1) * MXU_COLS)
            g0 = OFF_GATE + i * D_MODEL + n * MXU_COLS
            piece = _sigmoid_times(_dot(hb, win_ref[:, g0:g0 + MXU_COLS]),
                                   _dot(o_branch, wb_ref[i, :, cols]))
            if first:
                acc_s[:, cols] = piece
            else:
                acc_s[:, cols] += piece

    y = jnp.concatenate([conv_tile(ci) for ci in range(CONV_WIDTH // LANES)], axis=1)
    abuf[0:HALO, :] = abuf[tm:tm + HALO, :]
    o_cv = _layer_norm_silu(y, clg_ref[...], clb_ref[...]).astype(BF)
    gated_branch(1, o_cv, True)

    gv = _gelu_half(z_gm[:, GM_WIDTH:2 * GM_WIDTH])
    v = gv * lax.rsqrt(jnp.mean(gv * gv, axis=-1, keepdims=True) + EPS) * gmg_ref[...]
    vb = v.astype(BF)
    row = lax.broadcasted_iota(jnp.int32, (CHUNK, CHUNK), 0)
    col = lax.broadcasted_iota(jnp.int32, (CHUNK, CHUNK), 1)
    ws = [jnp.where(row >= col, gmws_ref[g], 0.0).astype(BF) for g in range(GM_GROUPS)]
    gmb = gmb_ref[...]
    chunks = []
    for c in range(nblk):
        rows = slice(c * CHUNK, (c + 1) * CHUNK)
        parts = []
        for g in range(GM_GROUPS):
            lanes = slice(g * LANES, (g + 1) * LANES)
            mixed = _dot(ws[g], vb[rows, lanes]) + gmb[:, lanes]
            parts.append(_gelu_half(z_gm[rows, lanes]) * mixed)
        chunks.append(jnp.concatenate(parts, axis=1))
    o_gm = jnp.concatenate(chunks, axis=0).astype(BF)
    gated_branch(0, o_gm, False)

    rbase = ropeb_ref[...]
    roff = ropeo_ref[...]
    cb_, sb_ = rbase[:, 0:LANES], rbase[:, LANES:2 * LANES]
    co_, so_ = roff[:, 0:LANES], roff[:, LANES:2 * LANES]
    rc = cb_ * co_ - sb_ * so_
    rs = (sb_ * co_ + cb_ * so_) * sign_ref[...]
    bd = bd_ref[...]
    qb = _rope(_head_norm(z_att[:, 0:ATT_WIDTH], bd, qn_ref[...]), rc, rs).astype(BF)
    k_nat = _rope(_head_norm(z_att[:, ATT_WIDTH:ATT_WIDTH + KV_WIDTH], bd[0:LANES, 0:LANES],
                             kn_ref[...]), rc, rs)
    v_nat = z_att[:, ATT_WIDTH + KV_WIDTH:ATT_WIDTH + 2 * KV_WIDTH]
    lo_kv = _low_half((tm, LANES))

    def doubled(x_nat):
        swapped = pltpu.roll(x_nat, HEAD_DIM, 1)
        return jnp.concatenate([jnp.where(lo_kv, x_nat, swapped), jnp.where(lo_kv, swapped, x_nat)],
                               axis=1)

    kd = doubled(k_nat)
    vd = doubled(v_nat)
    kd_s[WINDOW:WINDOW + tm, :] = kd.astype(BF)
    vd_s[WINDOW:WINDOW + tm, :] = vd.astype(BF)

    lo_q = _low_half((WINDOW, LANES))
    zero_q = jnp.zeros((WINDOW, LANES), BF)
    band = bias_ref[1]
    for bi in range(nblk):
        bias = jnp.where(t == 0, bias_ref[0], band) if bi == 0 else band
        bias4 = jnp.concatenate([bias] * Q_REP, axis=0)
        for g in range(N_KV_HEADS):
            tiles = [qb[bi * WINDOW:(bi + 1) * WINDOW, (2 * g + j) * LANES:(2 * g + j + 1) * LANES]
                     for j in range(2)]
            qs = jnp.concatenate([jnp.where(lo_q, tiles[0], zero_q), jnp.where(lo_q, zero_q, tiles[0]),
                                  jnp.where(lo_q, tiles[1], zero_q), jnp.where(lo_q, zero_q, tiles[1])],
                                 axis=0)
            keys = kd_s[bi * WINDOW:(bi + 2) * WINDOW, g * LANES:(g + 1) * LANES]
            vals = vd_s[bi * WINDOW:(bi + 2) * WINDOW, g * LANES:(g + 1) * LANES]
            s = _dot_nt(qs, keys) + bias4
            outs = []
            for hh in range(Q_REP):
                sink = sinks_ref[layer, Q_REP * g + hh]
                sh = s[hh * WINDOW:(hh + 1) * WINDOW, :]
                m = jnp.maximum(jnp.max(sh, axis=-1, keepdims=True), sink)
                p = jnp.exp(sh - m)
                den = jnp.sum(p, axis=-1, keepdims=True) + jnp.exp(sink - m)
                outs.append(_dot(p.astype(BF), vals) / den)
            for j in range(2):
                oatt_s[bi * WINDOW:(bi + 1) * WINDOW, (2 * g + j) * LANES:(2 * g + j + 1) * LANES] = (
                    jnp.where(lo_q, outs[2 * j], outs[2 * j + 1]).astype(BF))
    kd_s[0:WINDOW, :] = kd_s[tm:tm + WINDOW, :]
    vd_s[0:WINDOW, :] = vd_s[tm:tm + WINDOW, :]
    gated_branch(2, oatt_s[...], False)

    accb = acc_s[...].astype(BF)
    for n in range(D_MODEL // MXU_COLS):
        cols = slice(n * MXU_COLS, (n + 1) * MXU_COLS)
        x1_ref[:, cols] = x_ref[:, cols] + gt1[:, cols] * _dot(accb, wout_ref[:, cols])

    @pl.when(t == last)
    def _():
        kwin_ref[...] = k_nat[tm - WINDOW:tm, :]
        vwin_ref[...] = v_nat[tm - WINDOW:tm, :]
        cst_ref[...] = a[tm - (CONV_K - 1):tm, :]


def _mixer_call(layer, x, mod, p, tm):
    nb, seq, _ = x.shape
    nt = seq // tm
    lsel3 = lambda b, t: (layer, 0, 0)
    kernel = functools.partial(_mixer_kernel, layer=layer, tm=tm)
    return pl.pallas_call(
        kernel,
        out_shape=(
            jax.ShapeDtypeStruct((nb, seq, D_MODEL), F32),
            jax.ShapeDtypeStruct((nb, WINDOW, KV_WIDTH), F32),
            jax.ShapeDtypeStruct((nb, WINDOW, KV_WIDTH), F32),
            jax.ShapeDtypeStruct((nb, CONV_K - 1, CONV_WIDTH), F32),
        ),
        grid=(nb, nt),
        in_specs=[
            pl.BlockSpec(memory_space=pltpu.SMEM),
            pl.BlockSpec((None, tm, D_MODEL), lambda b, t: (b, t, 0)),
            pl.BlockSpec((None, SUBLANES, 3 * D_MODEL),
                         lambda b, t: (layer, PROMPT_MOD_ROW // SUBLANES, 0)),
            _const_spec((None, 1, D_MODEL), lsel3),
            _const_spec((None, D_MODEL, IN_WIDTH), lsel3),
            _const_spec((None, 1, GM_WIDTH), lsel3),
            _const_spec((None, GM_GROUPS, CHUNK, CHUNK), lambda b, t: (layer, 0, 0, 0)),
            _const_spec((None, CHUNK, GM_WIDTH), lsel3),
            _const_spec((None, CONV_K, CONV_WIDTH), lsel3),
            _const_spec((None, 1, CONV_WIDTH), lsel3),
            _const_spec((None, 1, CONV_WIDTH), lsel3),
            _const_spec((None, 1, CONV_WIDTH), lsel3),
            _const_spec((None, 1, ATT_WIDTH), lsel3),
            _const_spec((None, 1, LANES), lsel3),
            pl.BlockSpec((None, 1, 2 * LANES), lambda b, t: (t, 0, 0)),
            _const_spec((tm, 2 * LANES), lambda b, t: (0, 0)),
            _const_spec((1, LANES), lambda b, t: (0, 0)),
            _const_spec((2, WINDOW, 2 * WINDOW), lambda b, t: (0, 0, 0)),
            _const_spec((MXU_COLS, MXU_COLS), lambda b, t: (0, 0)),
            _const_spec((None, N_BRANCH, GM_WIDTH, D_MODEL), lambda b, t: (layer, 0, 0, 0)),
            _const_spec((None, D_MODEL, D_MODEL), lsel3),
        ],
        out_specs=(
            pl.BlockSpec((None, tm, D_MODEL), lambda b, t: (b, t, 0)),
            pl.BlockSpec((None, WINDOW, KV_WIDTH), lambda b, t: (b, 0, 0)),
            pl.BlockSpec((None, WINDOW, KV_WIDTH), lambda b, t: (b, 0, 0)),
            pl.BlockSpec((None, CONV_K - 1, CONV_WIDTH), lambda b, t: (b, 0, 0)),
        ),
        scratch_shapes=[
            pltpu.VMEM((WINDOW + tm, 2 * LANES), BF),
            pltpu.VMEM((WINDOW + tm, 2 * LANES), BF),
            pltpu.VMEM((HALO + tm + SUBLANES, CONV_WIDTH), F32),
            pltpu.VMEM((tm, ATT_WIDTH), BF),
            pltpu.VMEM((tm, D_MODEL), F32),
        ],
        compiler_params=pltpu.CompilerParams(
            dimension_semantics=("arbitrary", "arbitrary"),
            vmem_limit_bytes=VMEM_LIMIT),
        name=f"prompt_mixers_l{layer}",
    )(p["sinks"], x, mod, p["n1g"], p["w_in"], p["gm_g"], p["gm_ws"], p["gm_bias"],
      p["conv_dw"], p["conv_b"], p["conv_ln_g"], p["conv_ln_b"], p["qn"], p["kn"],
      p["rope_base"], p["rope_off"], p["rope_sign"], p["band_bias"], p["blockdiag"],
      p["w_branch"], p["w_out"])


def _route_rows(logits_t, rb_ref):
    scores = _sigmoid(logits_t)
    biased = scores + rb_ref[...]
    rows = lambda a, gi: [a[EXPERTS_PER_GROUP * gi + k:EXPERTS_PER_GROUP * gi + k + 1, :]
                          for k in range(EXPERTS_PER_GROUP)]
    best = None
    idx = None
    for gi in range(N_GROUPS):
        b = rows(biased, gi)
        hi1, lo1 = jnp.maximum(b[0], b[1]), jnp.minimum(b[0], b[1])
        hi2, lo2 = jnp.maximum(b[2], b[3]), jnp.minimum(b[2], b[3])
        gs = jnp.maximum(hi1, hi2) + jnp.maximum(jnp.minimum(hi1, hi2), jnp.maximum(lo1, lo2))
        if gi == 0:
            best, idx = gs, jnp.zeros(gs.shape, jnp.int32)
        else:
            better = gs > best
            idx = jnp.where(better, gi, idx)
            best = jnp.where(better, gs, best)
    bsel = rows(biased, 0)
    ssel = rows(scores, 0)
    for gi in range(1, N_GROUPS):
        bg, sg = rows(biased, gi), rows(scores, gi)
        pick = idx == gi
        bsel = [jnp.where(pick, bg[k], bsel[k]) for k in range(EXPERTS_PER_GROUP)]
        ssel = [jnp.where(pick, sg[k], ssel[k]) for k in range(EXPERTS_PER_GROUP)]
    chosen = []
    for k in range(EXPERTS_PER_GROUP):
        rank = jnp.zeros(idx.shape, jnp.int32)
        for k2 in range(EXPERTS_PER_GROUP):
            if k2 == k:
                continue
            beats = (bsel[k2] > bsel[k]) | ((bsel[k2] == bsel[k]) & (k2 < k))
            rank = rank + beats.astype(jnp.int32)
        chosen.append(jnp.where(rank < 2, ssel[k], 0.0))
    den = chosen[0] + chosen[1] + chosen[2] + chosen[3]
    return idx, [c / den for c in chosen]


def _moe_route(x, mod, n2g_ref, rw2_ref, rb_ref, upper_ref, h_buf, rt_buf, c_buf, cnt_s, buf, tm):
    h2 = _modulate(x, n2g_ref[...], mod[:, 0:D_MODEL], mod[:, D_MODEL:2 * D_MODEL])
    hi = h2.astype(BF)
    lo = (h2 - hi.astype(F32)).astype(BF)
    rw2 = rw2_ref[...]
    lt = _dot_nt(rw2, hi) + _dot_nt(rw2, lo)
    idx, comb = _route_rows(lt[0:N_EXPERTS, :] + lt[N_EXPERTS:2 * N_EXPERTS, :], rb_ref)

    onehot = [(idx == g).astype(F32) for g in range(N_GROUPS)]
    oh8 = jnp.concatenate(onehot + [jnp.zeros((SUBLANES - N_GROUPS, tm), F32)], axis=0)
    prefix = _dot(oh8.astype(BF), upper_ref[...])
    slot = onehot[0] * prefix[0:1, :]
    for g in range(1, N_GROUPS):
        slot = slot + onehot[g] * prefix[g:g + 1, :]
    rt_buf[0:1, :] = idx
    rt_buf[1:2, :] = slot.astype(jnp.int32)
    for g in range(N_GROUPS):
        cnt_s[buf, g] = jnp.sum(onehot[g]).astype(jnp.int32)

    for k in range(EXPERTS_PER_GROUP):
        c_buf[k:k + 1, :] = comb[k]
    h_buf[...] = hi


def _moe_experts(x_ref, gt2, w1_ref, w3_ref, w2_ref, o_ref, h_buf, rt_buf, c_buf, cnt_s, buf,
                 tm, cap, overflow):
    def dispatch(g, j):
        want = lax.broadcasted_iota(jnp.int32, (cap, tm), 0) + j * cap
        hit = (rt_buf[1:2, :] == want) & (rt_buf[0:1, :] == g)
        return hit, jnp.where(hit, 1.0, 0.0).astype(BF)

    def run_experts(g, hit, hg):
        parts = []
        for e in range(EXPERTS_PER_GROUP):
            a = _dot(hg, w1_ref[EXPERTS_PER_GROUP * g + e])
            b = _dot(hg, w3_ref[EXPERTS_PER_GROUP * g + e])
            ce = jnp.sum(jnp.where(hit, c_buf[e:e + 1, :], 0.0), axis=-1, keepdims=True)
            parts.append(_silu(a) * b * ce)
        act = jnp.concatenate(parts, axis=1).astype(BF)
        return _dot(act, w2_ref[g]).astype(BF)

    def scatter(pmat, y):
        return lax.dot_general(pmat, y, (((0,), (0,)), ((), ())), preferred_element_type=F32)

    def gather(pmat):
        return _dot(pmat, h_buf[...]).astype(BF)

    if not overflow:
        sel = [dispatch(g, 0) for g in range(N_GROUPS)]
        pmat_all = jnp.concatenate([s[1] for s in sel], axis=0)
        hg_all = gather(pmat_all)
        ys = [run_experts(g, sel[g][0], hg_all[g * cap:(g + 1) * cap, :]) for g in range(N_GROUPS)]
        y_tile = scatter(pmat_all, jnp.concatenate(ys, axis=0))
        o_ref[...] = x_ref[...] + gt2 * y_tile
        return

    def group_body(g, carry):
        nblk = (cnt_s[buf, g] + (cap - 1)) // cap

        def block_body(j, carry2):
            hit, pmat = dispatch(g, j)
            o_ref[...] += gt2 * scatter(pmat, run_experts(g, hit, gather(pmat)))
            return carry2

        lax.fori_loop(1, nblk, block_body, 0)
        return carry

    lax.fori_loop(0, N_GROUPS, group_body, 0)


def _moe_kernel(x_ref, xn_ref, mod_ref, n2g_ref, rw2_ref, rb_ref, upper_ref,
                w1_ref, w3_ref, w2_ref, o_ref, h0_s, h1_s, rt0_s, rt1_s, c0_s, c1_s, cnt_s,
                *, tm, cap, tiles_per_row, n_tiles):
    route = functools.partial(_moe_route, n2g_ref=n2g_ref, rw2_ref=rw2_ref, rb_ref=rb_ref,
                              upper_ref=upper_ref, cnt_s=cnt_s, tm=tm)
    experts = functools.partial(_moe_experts, x_ref, w1_ref=w1_ref,
                                w3_ref=w3_ref, w2_ref=w2_ref, o_ref=o_ref, cnt_s=cnt_s, tm=tm, cap=cap)
    bufs = [dict(h_buf=h0_s, rt_buf=rt0_s, c_buf=c0_s, buf=0),
            dict(h_buf=h1_s, rt_buf=rt1_s, c_buf=c1_s, buf=1)]
    if tiles_per_row is None:
        mod = mod_ref[...]
        route(x_ref[...], mod, **bufs[0])
        for overflow in (False, True):
            experts(mod[:, 2 * D_MODEL:3 * D_MODEL], overflow=overflow, **bufs[0])
        return

    i = pl.program_id(0)
    mod_row = lambda tile: mod_ref[pl.ds(tile // tiles_per_row, 1), :]

    @pl.when(i == 0)
    def _():
        route(x_ref[...], mod_row(i), **bufs[0])

    def step(cur):
        gt2 = mod_row(i)[:, 2 * D_MODEL:3 * D_MODEL]
        experts(gt2, overflow=False, **bufs[cur])
        route(xn_ref[...], mod_row(jnp.minimum(i + 1, n_tiles - 1)), **bufs[1 - cur])
        experts(gt2, overflow=True, **bufs[cur])

    @pl.when(i % 2 == 0)
    def _():
        step(0)

    @pl.when(i % 2 == 1)
    def _():
        step(1)


def _moe_call(layer, x2d, mod, tiles_per_row, p, tm, name):
    n = x2d.shape[0]
    n_tiles = n // tm
    if tiles_per_row is None:
        assert n_tiles == 1
        mod_spec = pl.BlockSpec((None, n, 3 * D_MODEL), lambda i: (layer, 0, 1))
    else:
        mod_spec = pl.BlockSpec((None, SUBLANES, 3 * D_MODEL),
                                lambda i: (layer, PROMPT_MOD_ROW // SUBLANES, 1))
    lsel3 = lambda i: (layer, 0, 0)
    lsel4 = lambda i: (layer, 0, 0, 0)
    upper = np.triu(np.ones((tm, tm), np.float32), k=1)
    kernel = functools.partial(_moe_kernel, tm=tm, cap=MOE_CAP, tiles_per_row=tiles_per_row,
                               n_tiles=n_tiles)
    return pl.pallas_call(
        kernel,
        out_shape=jax.ShapeDtypeStruct((n, D_MODEL), F32),
        grid=(n_tiles,),
        in_specs=[
            pl.BlockSpec((tm, D_MODEL), lambda i: (i, 0)),
            pl.BlockSpec((tm, D_MODEL), lambda i: (jnp.minimum(i + 1, n_tiles - 1), 0)),
            mod_spec,
            _const_spec((None, 1, D_MODEL), lsel3),
            _const_spec((2 * N_EXPERTS, D_MODEL), lambda i: (0, 0)),
            _const_spec((N_EXPERTS, 1), lambda i: (0, 0)),
            _const_spec((tm, tm), lambda i: (0, 0)),
            _const_spec((None, N_EXPERTS, D_MODEL, EXPERT_FF), lsel4),
            _const_spec((None, N_EXPERTS, D_MODEL, EXPERT_FF), lsel4),
            _const_spec((None, N_GROUPS, EXPERTS_PER_GROUP * EXPERT_FF, D_MODEL), lsel4),
        ],
        out_specs=pl.BlockSpec((tm, D_MODEL), lambda i: (i, 0)),
        scratch_shapes=[
            pltpu.VMEM((tm, D_MODEL), BF),
            pltpu.VMEM((tm, D_MODEL), BF),
            pltpu.VMEM((SUBLANES, tm), jnp.int32),
            pltpu.VMEM((SUBLANES, tm), jnp.int32),
            pltpu.VMEM((SUBLANES, tm), F32),
            pltpu.VMEM((SUBLANES, tm), F32),
            pltpu.SMEM((2, N_GROUPS), jnp.int32),
        ],
        compiler_params=pltpu.CompilerParams(
            dimension_semantics=("arbitrary",),
            vmem_limit_bytes=VMEM_LIMIT),
        name=name,
    )(x2d, x2d, mod, p["n2g"], p["rw2"], p["rb"], jnp.asarray(upper, dtype=BF),
      p["moe_w1"], p["moe_w3"], p["moe_w2"])


def _sample_proj_kernel(x_ref, mod_ref, n1g_ref, w_ref, z_ref):
    mod = mod_ref[...]
    h = _modulate(x_ref[...], n1g_ref[...], mod[:, 0:D_MODEL], mod[:, D_MODEL:2 * D_MODEL])
    z_ref[...] = _dot(h.astype(BF), w_ref[...])


def _sample_proj_call(layer, xs, mod, p):
    n = xs.shape[0]
    return pl.pallas_call(
        _sample_proj_kernel,
        out_shape=jax.ShapeDtypeStruct((n, IN_WIDTH), F32),
        grid=(IN_WIDTH // PROJ_BLOCK,),
        in_specs=[
            pl.BlockSpec((n, D_MODEL), lambda j: (0, 0)),
            pl.BlockSpec((None, n, 3 * D_MODEL), lambda j: (layer, 0, 0)),
            pl.BlockSpec((None, 1, D_MODEL), lambda j: (layer, 0, 0)),
            pl.BlockSpec((None, D_MODEL, PROJ_BLOCK), lambda j: (layer, 0, j)),
        ],
        out_specs=pl.BlockSpec((n, PROJ_BLOCK), lambda j: (0, j)),
        compiler_params=pltpu.CompilerParams(dimension_semantics=("arbitrary",)),
        name=f"sample_proj_l{layer}",
    )(xs, mod, p["n1g"], p["w_in"])


def _shift_in_column(cache_t, new_rows, bb):
    flat = cache_t.reshape(bb * HEAD_DIM, WINDOW)
    shifted = pltpu.roll(flat, WINDOW - 1, 1).reshape(bb, HEAD_DIM, WINDOW)
    padded = jnp.concatenate([new_rows, jnp.zeros((LANES - bb, LANES), F32)], axis=0)
    new_t = padded.T
    is_last = lax.broadcasted_iota(jnp.int32, (HEAD_DIM, WINDOW), 1) == WINDOW - 1
    out = []
    for b in range(bb):
        col = new_t[0:HEAD_DIM, b:b + 1]
        out.append(jnp.where(is_last, col, shifted[b]))
    return out


def _sample_mixer_kernel(z_ref, x_ref, mod_ref, kt_ref, vt_ref, st_ref,
                         gmg_ref, gmw_ref, gmb_ref, cdw_ref, cb_ref, clg_ref, clb_ref,
                         qn_ref, kn_ref, ropec_ref, ropes_ref, sink_ref, bd_ref, wb_ref, wout_ref,
                         _k_all, _v_all, _st_all,
                         x1_ref, kto_ref, vto_ref, sto_ref, gv_ref,
                         qf_s, of_s, oc_s, zg_s, *, bb):
    i = pl.program_id(0)
    r0 = pl.multiple_of(i * bb, bb)
    z = z_ref[...]
    zg_s[pl.ds(r0, bb), :] = z[:, OFF_GATE:OFF_GATE + N_BRANCH * D_MODEL]

    u = _gelu_half(z[:, OFF_GM_U:OFF_GM_U + GM_WIDTH])
    gv = _gelu_half(z[:, OFF_GM_V:OFF_GM_V + GM_WIDTH])
    v = gv * lax.rsqrt(jnp.mean(gv * gv, axis=-1, keepdims=True) + EPS) * gmg_ref[...]
    gv_ref[...] = v
    o_gm = u * (v * gmw_ref[...] + gmb_ref[...])

    a = _sigmoid_times(z[:, OFF_CV_G:OFF_CV_G + CONV_WIDTH], z[:, OFF_CV_A:OFF_CV_A + CONV_WIDTH])
    y = cb_ref[...] + cdw_ref[CONV_K - 1:CONV_K, :] * a
    for j in range(CONV_K - 1):
        y = y + cdw_ref[j:j + 1, :] * st_ref[j]
    o_cv = _layer_norm_silu(y, clg_ref[...], clb_ref[...])
    sto_ref[0:CONV_K - 2] = st_ref[1:CONV_K - 1]
    sto_ref[CONV_K - 2] = a

    rc = ropec_ref[...]
    rs = ropes_ref[...]
    bd = bd_ref[...]
    q = _rope(_head_norm(z[:, OFF_Q:OFF_Q + ATT_WIDTH], bd, qn_ref[...]), rc, rs)
    knew = _rope(_head_norm(z[:, OFF_K:OFF_K + KV_WIDTH], bd[0:LANES, 0:LANES], kn_ref[...]), rc, rs)
    vnew = z[:, OFF_V:OFF_V + KV_WIDTH]
    lo = _low_half((bb, LANES))
    for h in range(N_HEADS):
        tile = q[:, (h // 2) * LANES:(h // 2 + 1) * LANES]
        if h % 2 == 1:
            tile = pltpu.roll(tile, HEAD_DIM, 1)
        qf_s[:, h, :] = jnp.where(lo, tile, 0.0)
    k_low = [knew, pltpu.roll(knew, HEAD_DIM, 1)]
    v_low = [vnew, pltpu.roll(vnew, HEAD_DIM, 1)]
    for g in range(N_KV_HEADS):
        qg = qf_s[:, Q_REP * g:Q_REP * (g + 1), :]
        kt = kt_ref[:, g]
        vt = vt_ref[:, g]
        s = jnp.einsum("brd,bdp->brp", qg[:, :, 0:HEAD_DIM].astype(BF), kt.astype(BF),
                       preferred_element_type=F32)
        s_new = jnp.sum(qg * k_low[g][:, None, :], axis=-1, keepdims=True)
        sink = sink_ref[Q_REP * g:Q_REP * (g + 1), 0:1][None]
        m = jnp.maximum(jnp.maximum(jnp.max(s, axis=-1, keepdims=True), s_new), sink)
        pr = jnp.exp(s - m)
        p_new = jnp.exp(s_new - m)
        den = jnp.sum(pr, axis=-1, keepdims=True) + p_new + jnp.exp(sink - m)
        o = jnp.einsum("brp,bdp->brd", pr.astype(BF), vt.astype(BF), preferred_element_type=F32)
        o = (o + p_new * v_low[g][:, None, 0:HEAD_DIM]) / den
        of_s[:, Q_REP * g:Q_REP * (g + 1), :] = jnp.concatenate([o, jnp.zeros(o.shape, F32)], axis=-1)
        for b, tile in enumerate(_shift_in_column(kt, k_low[g], bb)):
            kto_ref[b, g] = tile
        for b, tile in enumerate(_shift_in_column(vt, v_low[g], bb)):
            vto_ref[b, g] = tile
    att_tiles = []
    for j in range(N_HEADS // 2):
        second = pltpu.roll(of_s[:, 2 * j + 1, :], HEAD_DIM, 1)
        att_tiles.append(jnp.where(lo, of_s[:, 2 * j, :], second))
    o_att = jnp.concatenate(att_tiles, axis=1)

    oc_s[pl.ds(r0, bb), 0:GM_WIDTH] = o_gm.astype(BF)
    oc_s[pl.ds(r0, bb), GM_WIDTH:GM_WIDTH + CONV_WIDTH] = o_cv.astype(BF)
    oc_s[pl.ds(r0, bb), GM_WIDTH + CONV_WIDTH:GM_WIDTH + CONV_WIDTH + ATT_WIDTH] = o_att.astype(BF)

    @pl.when(i == pl.num_programs(0) - 1)
    def _():
        acc = _sigmoid_times(zg_s[:, 0:D_MODEL], _dot(oc_s[:, 0:GM_WIDTH], wb_ref[0]))
        acc = acc + _sigmoid_times(zg_s[:, D_MODEL:2 * D_MODEL], _dot(
            oc_s[:, GM_WIDTH:GM_WIDTH + CONV_WIDTH], wb_ref[1]))
        acc = acc + _sigmoid_times(zg_s[:, 2 * D_MODEL:3 * D_MODEL], _dot(
            oc_s[:, GM_WIDTH + CONV_WIDTH:GM_WIDTH + CONV_WIDTH + ATT_WIDTH], wb_ref[2]))
        gt1 = mod_ref[...][:, 2 * D_MODEL:3 * D_MODEL]
        x1_ref[...] = x_ref[...] + gt1 * _dot(acc.astype(BF), wout_ref[...])


def _sample_mixer_call(layer, z, xs, mod, kt, vt, st, new_k, new_v, new_st, p, bb):
    n = xs.shape[0]
    lsel3 = lambda i: (layer, 0, 0)
    cache_block = (bb, N_KV_HEADS, HEAD_DIM, WINDOW)
    kernel = functools.partial(_sample_mixer_kernel, bb=bb)
    args = (z, xs, mod, kt, vt, st, p["gm_g"], p["gm_w0"], p["gm_b0"], p["conv_dw"], p["conv_b"],
            p["conv_ln_g"], p["conv_ln_b"], p["qn"], p["kn"], p["rope_c1"], p["rope_s1"],
            p["sink_lanes"], p["blockdiag"], p["w_branch"], p["w_out"], new_k, new_v, new_st)
    n_in = len(args)
    return pl.pallas_call(
        kernel,
        out_shape=(
            jax.ShapeDtypeStruct((n, D_MODEL), F32),
            jax.ShapeDtypeStruct(new_k.shape, F32),
            jax.ShapeDtypeStruct(new_v.shape, F32),
            jax.ShapeDtypeStruct(new_st.shape, F32),
            jax.ShapeDtypeStruct((n, GM_WIDTH), F32),
        ),
        input_output_aliases={n_in - 3: 1, n_in - 2: 2, n_in - 1: 3},
        grid=(n // bb,),
        in_specs=[
            pl.BlockSpec((bb, IN_WIDTH), lambda i: (i, 0)),
            pl.BlockSpec((n, D_MODEL), lambda i: (0, 0)),
            pl.BlockSpec((None, n, 3 * D_MODEL), lsel3),
            pl.BlockSpec((None,) + cache_block, lambda i: (layer, i, 0, 0, 0)),
            pl.BlockSpec((None,) + cache_block, lambda i: (layer, i, 0, 0, 0)),
            pl.BlockSpec((None, CONV_K - 1, bb, CONV_WIDTH), lambda i: (layer, 0, i, 0)),
            pl.BlockSpec((None, 1, GM_WIDTH), lsel3),
            pl.BlockSpec((None, 1, GM_WIDTH), lsel3),
            pl.BlockSpec((None, 1, GM_WIDTH), lsel3),
            pl.BlockSpec((None, CONV_K, CONV_WIDTH), lsel3),
            pl.BlockSpec((None, 1, CONV_WIDTH), lsel3),
            pl.BlockSpec((None, 1, CONV_WIDTH), lsel3),
            pl.BlockSpec((None, 1, CONV_WIDTH), lsel3),
            pl.BlockSpec((None, 1, ATT_WIDTH), lsel3),
            pl.BlockSpec((None, 1, LANES), lsel3),
            pl.BlockSpec((1, LANES), lambda i: (0, 0)),
            pl.BlockSpec((1, LANES), lambda i: (0, 0)),
            pl.BlockSpec((None, N_HEADS, LANES), lsel3),
            pl.BlockSpec((MXU_COLS, MXU_COLS), lambda i: (0, 0)),
            pl.BlockSpec((None, N_BRANCH, GM_WIDTH, D_MODEL), lambda i: (layer, 0, 0, 0)),
            pl.BlockSpec((None, D_MODEL, D_MODEL), lsel3),
            pl.BlockSpec(memory_space=pl.ANY),
            pl.BlockSpec(memory_space=pl.ANY),
            pl.BlockSpec(memory_space=pl.ANY),
        ],
        out_specs=(
            pl.BlockSpec((n, D_MODEL), lambda i: (0, 0)),
            pl.BlockSpec((None,) + cache_block, lambda i: (layer, i, 0, 0, 0)),
            pl.BlockSpec((None,) + cache_block, lambda i: (layer, i, 0, 0, 0)),
            pl.BlockSpec((None, CONV_K - 1, bb, CONV_WIDTH), lambda i: (layer, 0, i, 0)),
            pl.BlockSpec((bb, GM_WIDTH), lambda i: (i, 0)),
        ),
        scratch_shapes=[
            pltpu.VMEM((bb, N_HEADS, LANES), F32),
            pltpu.VMEM((bb, N_HEADS, LANES), F32),
            pltpu.VMEM((n, GM_WIDTH + CONV_WIDTH + ATT_WIDTH), BF),
            pltpu.VMEM((n, N_BRANCH * D_MODEL), F32),
        ],
        compiler_params=pltpu.CompilerParams(
            dimension_semantics=("arbitrary",), vmem_limit_bytes=VMEM_LIMIT),
        name=f"sample_mixers_l{layer}",
    )(*args)


def _rope_lane_tables():
    half = ROT_DIM // 2
    freqs = jnp.exp(-math.log(ROPE_THETA) * jnp.arange(half, dtype=F32) * (2.0 / ROT_DIM))
    rest = jnp.zeros((HEAD_DIM - ROT_DIM,), F32)
    freq64 = jnp.concatenate([freqs, freqs, rest])
    sign64 = jnp.concatenate([-jnp.ones((half,), F32), jnp.ones((half,), F32), rest])
    reps = LANES // HEAD_DIM
    return jnp.tile(freq64, reps)[None, :], jnp.tile(sign64, reps)[None, :]


def _cos_sin(pos, lane_freq):
    ang = pos.astype(F32)[:, None] * lane_freq
    return jnp.cos(ang), jnp.sin(ang)


def _band_bias():
    i = np.arange(WINDOW)[:, None]
    j = np.arange(2 * WINDOW)[None, :]
    band = (j >= i) & (j <= i + WINDOW)
    first = band & (j >= WINDOW)
    out = np.where(np.stack([first, band]), 0.0, NEG_BIG).astype(np.float32)
    return jnp.asarray(out)


def _blockdiag():
    idx = np.arange(MXU_COLS) // HEAD_DIM
    return jnp.asarray((idx[:, None] == idx[None, :]).astype(np.float32), dtype=BF)


def _router_hi_lo(router_w):
    rw = router_w.astype(F32).T
    rw_hi = rw.astype(BF)
    rw_lo = (rw - rw_hi.astype(F32)).astype(BF)
    return jnp.concatenate([rw_hi, rw_lo], axis=0)


def _prepare(norm1_g, norm2_g, w_in, gm_norm_g, gm_ws, gm_b, conv_dw, conv_b, conv_ln_g,
             conv_ln_b, q_norm_g, k_norm_g, attn_sinks, w_branch, w_out, router_w, router_b,
             moe_w1, moe_w3, moe_w2, seq, tm):
    lane_freq, lane_sign = _rope_lane_tables()
    cb, sb = _cos_sin(jnp.arange(seq // tm, dtype=jnp.int32) * tm, lane_freq)
    co, so = _cos_sin(jnp.arange(tm, dtype=jnp.int32), lane_freq)
    c1, s1 = _cos_sin(PAST_LEN + jnp.arange(1, dtype=jnp.int32), lane_freq)
    row3 = lambda a: a.reshape(DEPTH, 1, a.shape[-1])
    col = np.arange(IN_WIDTH)
    col_scale = jnp.asarray(np.where((col >= OFF_Q) & (col < OFF_GATE), 1.0, 0.5), F32)
    return {
        "n1g": row3(norm1_g), "n2g": row3(norm2_g), "w_in": (w_in * col_scale).astype(BF),
        "gm_g": row3(gm_norm_g), "gm_ws": gm_ws,
        "gm_bias": jnp.repeat(jnp.swapaxes(gm_b, 1, 2), LANES, axis=2),
        "gm_w0": jnp.repeat(gm_ws[:, :, 0, 0], LANES, axis=1).reshape(DEPTH, 1, GM_WIDTH),
        "gm_b0": jnp.repeat(gm_b[:, :, 0], LANES, axis=1).reshape(DEPTH, 1, GM_WIDTH),
        "conv_dw": conv_dw, "conv_b": row3(conv_b), "conv_ln_g": row3(0.5 * conv_ln_g),
        "conv_ln_b": row3(0.5 * conv_ln_b),
        "qn": row3(jnp.tile(q_norm_g, (1, N_HEADS)) * (HEAD_DIM ** -0.5)),
        "kn": row3(jnp.tile(k_norm_g, (1, N_KV_HEADS))),
        "sinks": attn_sinks,
        "sink_lanes": jnp.broadcast_to(attn_sinks[:, :, None], (DEPTH, N_HEADS, LANES)),
        "rope_base": jnp.concatenate([cb, sb], axis=1)[:, None, :],
        "rope_off": jnp.concatenate([co, so], axis=1),
        "rope_sign": lane_sign, "rope_c1": c1, "rope_s1": s1 * lane_sign,
        "band_bias": _band_bias(), "blockdiag": _blockdiag(),
        "w_branch": (0.5 * w_branch).astype(BF), "w_out": w_out.astype(BF),
        "rw2": _router_hi_lo(router_w),
        "rb": router_b.astype(F32).reshape(N_EXPERTS, 1),
        "moe_w1": moe_w1.astype(BF), "moe_w3": moe_w3.astype(BF),
        "moe_w2": moe_w2.astype(BF).reshape(DEPTH, N_GROUPS, EXPERTS_PER_GROUP * EXPERT_FF, D_MODEL),
    }


def kernel(x_prompt, x_sample, cache_win_k, cache_win_v, state_conv, c_prompt, c_sample, norm1_g, norm2_g, w_ada, b_ada, w_in, gm_norm_g, gm_ws, gm_b, conv_dw, conv_b, conv_ln_g, conv_ln_b, q_norm_g, k_norm_g, attn_sinks, w_branch, w_out, router_w, router_b, moe_w1, moe_w3, moe_w2):
    nb, seq, _ = x_prompt.shape
    ns = x_sample.shape[0]
    tm = min(MIX_TILE, seq)
    tmoe = min(MOE_TILE, seq)
    p = _prepare(norm1_g, norm2_g, w_in, gm_norm_g, gm_ws, gm_b, conv_dw, conv_b, conv_ln_g,
                 conv_ln_b, q_norm_g, k_norm_g, attn_sinks, w_branch, w_out, router_w, router_b,
                 moe_w1, moe_w3, moe_w2, seq, tm)
    assert ns == PROMPT_MOD_ROW and nb <= ADA_ROWS - PROMPT_MOD_ROW
    c_all = jnp.concatenate(
        [c_sample, c_prompt, jnp.zeros((ADA_ROWS - nb - ns, D_MODEL), F32)], axis=0)
    mod = _ada_call(c_all, w_ada, b_ada)

    kt = jnp.transpose(cache_win_k, (0, 1, 3, 4, 2))
    vt = jnp.transpose(cache_win_v, (0, 1, 3, 4, 2))
    st = jnp.transpose(state_conv, (0, 2, 1, 3))

    xp = x_prompt
    xs = x_sample.reshape(ns, D_MODEL)
    kp_l, vp_l, cp_l, gs_l = [], [], [], []
    new_k = jnp.zeros(kt.shape, F32)
    new_v = jnp.zeros(vt.shape, F32)
    new_st = jnp.zeros(st.shape, F32)
    for l in range(DEPTH):
        xp, kp, vp, cp = _mixer_call(l, xp, mod, p, tm)
        xp = _moe_call(l, xp.reshape(nb * seq, D_MODEL), mod, seq // tmoe, p, tmoe,
                       f"prompt_moe_l{l}").reshape(nb, seq, D_MODEL)
        kp_l.append(kp)
        vp_l.append(vp)
        cp_l.append(cp)

        z = _sample_proj_call(l, xs, mod, p)
        xs, new_k, new_v, new_st, gv = _sample_mixer_call(l, z, xs, mod, kt, vt, st,
                                                          new_k, new_v, new_st, p, SAMPLE_BLOCK)
        xs = _moe_call(l, xs, mod, None, p, ns, f"sample_moe_l{l}")
        gs_l.append(gv)

    kv_shape = (DEPTH, nb, WINDOW, N_KV_HEADS, HEAD_DIM)
    return (xp, xs.reshape(ns, 1, D_MODEL),
            jnp.stack(kp_l).reshape(kv_shape), jnp.stack(vp_l).reshape(kv_shape),
            jnp.stack(cp_l),
            jnp.transpose(new_k, (0, 1, 4, 2, 3)),
            jnp.transpose(new_v, (0, 1, 4, 2, 3)),
            jnp.transpose(new_st, (0, 2, 1, 3)),
            jnp.stack(gs_l).reshape(DEPTH, ns, 1, GM_WIDTH))
```
